```python
import functools
import jax, jax.numpy as jnp
from jax import lax
import numpy as np

D_MODEL = 1024
BATCH = 8
SEQ = 2048
DEPTH = 1
DEC_BATCH = 32
DEC_SEQ = 4
PAST_LEN = 8192
PAGE_SIZE = 128

H_A = 8
G_A = 2
HPG = H_A // G_A
DH = 64
ROPE_DIM = DH // 4
ROPE_THETA = 500000.0
L_CMP = 32
L_SEL = 64
N_SEL = 8
WINDOW = 512
Q_BLOCK = 128
FORCE_BONUS = 1e4
H_R = 4
DK_R = 128
DV_R = 128
RET_THETA = 10000.0
RET_CHUNK = 128
D_FF = 2816
CONV_W = 3
EPS = 1e-6
NEG = -1e30
IN_SIZES = (H_A * DH, 6 * G_A * DH, 3 * H_A, H_R * DK_R, H_R * DK_R, H_R * DV_R, H_R * DV_R, 2 * D_MODEL)
D_IN = H_A * DH + 6 * G_A * DH + 3 * H_A + 2 * H_R * DK_R + 2 * H_R * DV_R + 2 * D_MODEL

kernel_name = 'hybrid_nsa_retention_convffn_step'


def rmsnorm(x, w):
    xf = x.astype(jnp.float32)
    y = xf * lax.rsqrt(jnp.mean(xf * xf, axis=-1, keepdims=True) + EPS)
    return (y * w.astype(jnp.float32)).astype(x.dtype)


def rotate(x, pos, rot_dim, theta):
    half = rot_dim // 2
    inv = jnp.power(jnp.float32(theta), -jnp.arange(half, dtype=jnp.float32) * (2.0 / rot_dim))
    ang = pos.astype(jnp.float32)[:, None, None] * inv
    cos, sin = jnp.cos(ang), jnp.sin(ang)
    xr = x[..., :rot_dim].astype(jnp.float32)
    x1, x2 = xr[..., :half], xr[..., half:]
    rot = jnp.concatenate([x1 * cos - x2 * sin, x1 * sin + x2 * cos], axis=-1).astype(x.dtype)
    return jnp.concatenate([rot, x[..., rot_dim:]], axis=-1)


def masked_softmax(s, mask):
    s = jnp.where(mask, s.astype(jnp.float32), NEG)
    p = jnp.exp(s - jnp.max(s, axis=-1, keepdims=True)) * mask
    return p / jnp.maximum(jnp.sum(p, axis=-1, keepdims=True), 1e-30)


def compress(rows, pos_emb, w1, w2):
    B, L, G, D = rows.shape
    blk = rows.reshape(B, L // L_CMP, L_CMP, G, D) + pos_emb[None, None, :, None, :]
    blk = blk.transpose(0, 3, 1, 2, 4).reshape(B, G, L // L_CMP, L_CMP * D)
    return jax.nn.gelu(blk @ w1) @ w2


def block_summaries(cmp_kv, slc_kv, phi_pos_k, phi_k1, phi_k2, phi_pos_v, phi_v1, phi_v2):
    B, L, _, G, D = cmp_kv.shape
    Lp = -(-L // L_SEL) * L_SEL
    padw = ((0, 0), (0, Lp - L), (0, 0), (0, 0), (0, 0))
    cmp_kv = jnp.pad(cmp_kv, padw)
    slc_kv = jnp.pad(slc_kv, padw)
    kc = compress(cmp_kv[:, :, 0], phi_pos_k, phi_k1, phi_k2)
    vc = compress(cmp_kv[:, :, 1], phi_pos_v, phi_v1, phi_v2)
    blk = slc_kv.reshape(B, Lp // L_SEL, L_SEL, 2, G, D).transpose(3, 0, 4, 1, 2, 5)
    return kc, vc, blk[0], blk[1]


def nsa_cmp_slc(q, q_rot, q_pos, kc, vc, ks_blk, vs_blk):
    B, G, R, Q, D = q.shape
    NC, NS = kc.shape[2], ks_blk.shape[2]
    scale = D ** -0.5
    cmask = (jnp.arange(NC) * L_CMP + (L_CMP - 1))[None, :] <= q_pos[:, None]
    p_cmp = masked_softmax(jnp.einsum('bgrqd,bgnd->bgrqn', q, kc) * scale, cmask)
    o_cmp = jnp.einsum('bgrqn,bgnd->bgrqd', p_cmp.astype(vc.dtype), vc)
    imp = p_cmp.sum(axis=2).reshape(B, G, Q, NS, L_SEL // L_CMP).sum(axis=-1)
    blk = jnp.arange(NS)
    valid = (blk * L_SEL)[None, :] <= q_pos[:, None]
    forced = (blk[None, :] == 0) | (blk[None, :] == q_pos[:, None] // L_SEL)
    score = jnp.where(valid, imp + jnp.where(forced, FORCE_BONUS, 0.0), NEG)
    n_top = min(N_SEL, NS)
    top_s, top_i = lax.top_k(score, n_top)
    gather = jax.vmap(jax.vmap(lambda kb, ib: kb[ib]))
    ks = gather(ks_blk, top_i)
    vs = gather(vs_blk, top_i)
    kpos = top_i[..., None] * L_SEL + jnp.arange(L_SEL)
    smask = (top_s[..., None] > 0.5 * NEG) & (kpos <= q_pos[:, None, None])
    ss = jnp.einsum('bgrqd,bgqnld->bgrqnl', q_rot, ks) * scale
    K = n_top * L_SEL
    p = masked_softmax(ss.reshape(B, G, R, Q, K), smask.reshape(B, G, 1, Q, K))
    o_slc = jnp.einsum('bgrqk,bgqkd->bgrqd', p.astype(vs.dtype), vs.reshape(B, G, Q, K, D))
    return o_cmp, o_slc


def nsa_window(q_rot, q_pos, kw, vw, k_pos):
    diff = q_pos[:, :, None] - k_pos[:, None, :]
    mask = (diff >= 0) & (diff < WINDOW) & (k_pos[:, None, :] >= 0)
    s = jnp.einsum('bgrnqd,bgnkd->bgrnqk', q_rot, kw) * (q_rot.shape[-1] ** -0.5)
    p = masked_softmax(s, mask)
    return jnp.einsum('bgrnqk,bgnkd->bgrnqd', p.astype(vw.dtype), vw)


def attend_prompt(qg, qg_rot, cmp_kv, slc_kv, win_kv, pos, phi):
    B, G, R, T, D = qg.shape
    kc, vc, ks, vs = block_summaries(cmp_kv, slc_kv, *phi)
    qb = Q_BLOCK if T % Q_BLOCK == 0 else T
    nqb = T // qb
    blocks = lambda t: jnp.moveaxis(t.reshape(B, G, R, nqb, qb, D), 3, 0)
    unblock = lambda t: jnp.moveaxis(t, 0, 3).reshape(B, G, R, T, D)
    o_cmp, o_slc = lax.map(lambda a: nsa_cmp_slc(a[0], a[1], a[2], kc, vc, ks, vs),
                           (blocks(qg), blocks(qg_rot), pos.reshape(nqb, qb)))
    kvp = jnp.pad(win_kv, ((0, 0), (WINDOW, 0), (0, 0), (0, 0), (0, 0)))
    idx = (jnp.arange(nqb) * qb)[:, None] + jnp.arange(qb + WINDOW)[None, :]
    band = kvp[:, idx].transpose(3, 0, 4, 1, 2, 5)
    o_win = nsa_window(qg_rot.reshape(B, G, R, nqb, qb, D), pos.reshape(nqb, qb),
                       band[0], band[1], pos[0] + idx - WINDOW).reshape(B, G, R, T, D)
    return unblock(o_cmp), unblock(o_slc), o_win, win_kv[:, -min(WINDOW, T):]


def attend_sample(qg, qg_rot, cmp_kv, slc_kv, win_kv, pos, phi, past_cmp, past_slc, win_buf):
    B, G, R, S, D = qg.shape
    kc, vc, ks, vs = block_summaries(jnp.concatenate([past_cmp, cmp_kv], axis=1),
                                     jnp.concatenate([past_slc, slc_kv], axis=1), *phi)
    o_cmp, o_slc = nsa_cmp_slc(qg, qg_rot, pos, kc, vc, ks, vs)
    kv = jnp.concatenate([win_buf, win_kv], axis=1)
    wb = win_buf.shape[1]
    k_pos = pos[0] - wb + jnp.arange(wb + S)
    kvt = kv.transpose(2, 0, 3, 1, 4)[:, :, :, None]
    o_win = nsa_window(qg_rot[:, :, :, None], pos[None], kvt[0], kvt[1], k_pos[None])[:, :, :, 0]
    return o_cmp, o_slc, o_win, kv[:, -wb:]


def retention_chunk(S, q, k, v, log_g):
    C = q.shape[2]
    i = jnp.arange(C, dtype=jnp.float32)
    diff = i[:, None] - i[None, :]
    dmask = jnp.where(diff >= 0, jnp.exp(log_g[:, None, None] * jnp.maximum(diff, 0.0)), 0.0)
    inner = jnp.einsum('bhid,bhjd->bhij', q, k) * dmask[None]
    o = jnp.einsum('bhij,bhje->bhie', inner, v) + \
        jnp.einsum('bhid,bhde->bhie', q, S) * jnp.exp(log_g[:, None] * (i + 1.0))[None, :, :, None]
    w = jnp.exp(log_g[:, None] * (C - 1.0 - i))
    S_new = jnp.exp(log_g * C)[None, :, None, None] * S + jnp.einsum('bhjd,bhje->bhde', k * w[None, :, :, None], v)
    return S_new, o


def retention(q, k, v, state):
    B, T, H, _ = q.shape
    C = RET_CHUNK if T % RET_CHUNK == 0 else T
    nC = T // C
    to_chunks = lambda t: t.astype(jnp.float32).reshape(B, nC, C, H, -1).transpose(1, 0, 3, 2, 4)
    log_g = jnp.log1p(-jnp.exp2(-5.0 - jnp.arange(H, dtype=jnp.float32)))
    S, o = lax.scan(lambda s, a: retention_chunk(s, a[0], a[1], a[2], log_g),
                    state.astype(jnp.float32), (to_chunks(q), to_chunks(k), to_chunks(v)))
    return o.transpose(1, 0, 3, 2, 4).reshape(B, T, H, -1), S


def token_mixer(h, pos, attend, ret_state, w_in, w_up_a, ret_gn_w, w_up_r, w_out):
    B, T, _ = h.shape
    splits = np.cumsum(IN_SIZES)[:-1].tolist()
    q_a, kv_a, g_a, q_r, k_r, v_r, g_r, g_m = jnp.split(h @ w_in, splits, axis=-1)
    q = q_a.reshape(B, T, H_A, DH)
    q_rot = rotate(q, pos, ROPE_DIM, ROPE_THETA)
    kv6 = kv_a.reshape(B, T, 6, G_A, DH)
    cmp_kv = kv6[:, :, 0:2]
    slc_kv = jnp.stack([rotate(kv6[:, :, 2], pos, ROPE_DIM, ROPE_THETA), kv6[:, :, 3]], axis=2)
    win_kv = jnp.stack([rotate(kv6[:, :, 4], pos, ROPE_DIM, ROPE_THETA), kv6[:, :, 5]], axis=2)
    grp = lambda t: t.reshape(B, T, G_A, HPG, DH).transpose(0, 2, 3, 1, 4)
    ungrp = lambda t: t.transpose(0, 3, 1, 2, 4).reshape(B, T, H_A, DH)
    o_cmp, o_slc, o_win, win_state = attend(grp(q), grp(q_rot), cmp_kv, slc_kv, win_kv, pos)
    gates = jax.nn.sigmoid(g_a).reshape(B, T, 3, H_A, 1)
    o_a = gates[:, :, 0] * ungrp(o_cmp) + gates[:, :, 1] * ungrp(o_slc) + gates[:, :, 2] * ungrp(o_win)
    y_a = o_a.reshape(B, T, H_A * DH) @ w_up_a
    qr = rotate(q_r.reshape(B, T, H_R, DK_R), pos, DK_R, RET_THETA)
    kr = rotate(k_r.reshape(B, T, H_R, DK_R), pos, DK_R, RET_THETA) * (DK_R ** -0.5)
    o_r, ret_new = retention(qr, kr, v_r.reshape(B, T, H_R, DV_R), ret_state)
    mu = jnp.mean(o_r, axis=-1, keepdims=True)
    var = jnp.mean(jnp.square(o_r - mu), axis=-1, keepdims=True)
    o_r = ((o_r - mu) * lax.rsqrt(var + EPS) * ret_gn_w.astype(jnp.float32).reshape(H_R, DV_R)).astype(h.dtype)
    y_r = (jax.nn.silu(g_r) * o_r.reshape(B, T, H_R * DV_R)) @ w_up_r
    gm = jax.nn.sigmoid(g_m)
    merged = gm[..., :D_MODEL] * y_a + gm[..., D_MODEL:] * y_r
    return merged @ w_out, (cmp_kv, slc_kv, win_state, ret_new.astype(ret_state.dtype))


def conv_ffn(h, conv_state, w_up, conv_w, conv_b, w_down):
    T = h.shape[1]
    a, b = jnp.split(h @ w_up, 2, axis=-1)
    ext = jnp.concatenate([conv_state.astype(a.dtype), a], axis=1)
    u = conv_b + sum(ext[:, j:j + T] * conv_w[j] for j in range(CONV_W))
    return (jax.nn.gelu(u) * b) @ w_down, ext[:, T:]


def layer(x, c, pos, attend, ret_state, conv_state, norm1_w, norm2_w, w_ada, b_ada, w_in, w_up_a,
          ret_gn_w, w_up_r, w_out, w_ffn_up, ffn_conv_w, ffn_conv_b, w_ffn_down):
    mods = jax.nn.silu(c) @ w_ada + b_ada
    sh1, sc1, gt1, sh2, sc2, gt2 = [m[:, None, :] for m in jnp.split(mods, 6, axis=-1)]
    h = rmsnorm(x, norm1_w) * (1.0 + sc1) + sh1
    mix, (cmp_kv, slc_kv, win_state, ret_new) = token_mixer(h, pos, attend, ret_state, w_in, w_up_a,
                                                             ret_gn_w, w_up_r, w_out)
    x = x + gt1 * mix
    h = rmsnorm(x, norm2_w) * (1.0 + sc2) + sh2
    ff, conv_new = conv_ffn(h, conv_state, w_ffn_up, ffn_conv_w, ffn_conv_b, w_ffn_down)
    x = x + gt2 * ff
    return x, (cmp_kv, slc_kv, win_state, ret_new, conv_new)


def setup_inputs(seed: int = 0) -> dict:
    key = jax.random.key(seed)
    ks = iter(jax.random.split(key, 40))
    nrm = lambda shape, scale: jax.random.normal(next(ks), shape, jnp.float32) * scale
    n_pages = PAST_LEN // PAGE_SIZE
    n_pool = (5 * DEC_BATCH * n_pages + 3) // 4
    wb = min(WINDOW, PAST_LEN)
    perm = jax.random.permutation(next(ks), n_pool)
    page_table = perm[:DEC_BATCH * n_pages].reshape(DEC_BATCH, n_pages).astype(jnp.int32)
    return {
        'x_prompt': nrm((BATCH, SEQ, D_MODEL), 1.0),
        'x_sample': nrm((DEC_BATCH, DEC_SEQ, D_MODEL), 1.0),
        'cache_cmp_kv': nrm((DEPTH, n_pool, PAGE_SIZE, 2, G_A, DH), 1.0),
        'cache_slc_kv': nrm((DEPTH, n_pool, PAGE_SIZE, 2, G_A, DH), 1.0),
        'state_win_kv': nrm((DEPTH, DEC_BATCH, wb, 2, G_A, DH), 1.0),
        'state_ret': nrm((DEPTH, DEC_BATCH, H_R, DK_R, DV_R), 0.5),
        'state_conv': nrm((DEPTH, DEC_BATCH, CONV_W - 1, D_FF), 1.0),
        'page_table': page_table,
        'c_prompt': nrm((BATCH, D_MODEL), 1.0),
        'c_sample': nrm((DEC_BATCH, D_MODEL), 1.0),
        'norm1_w': 1.0 + nrm((DEPTH, D_MODEL), 0.01),
        'norm2_w': 1.0 + nrm((DEPTH, D_MODEL), 0.01),
        'w_ada': nrm((DEPTH, D_MODEL, 6 * D_MODEL), 0.5 * D_MODEL ** -0.5),
        'b_ada': nrm((DEPTH, 6 * D_MODEL), 0.01),
        'w_in': nrm((DEPTH, D_MODEL, D_IN), D_MODEL ** -0.5),
        'phi_pos_k': nrm((DEPTH, L_CMP, DH), 0.1),
        'phi_k1': nrm((DEPTH, L_CMP * DH, DH), (L_CMP * DH) ** -0.5),
        'phi_k2': nrm((DEPTH, DH, DH), DH ** -0.5),
        'phi_pos_v': nrm((DEPTH, L_CMP, DH), 0.1),
        'phi_v1': nrm((DEPTH, L_CMP * DH, DH), (L_CMP * DH) ** -0.5),
        'phi_v2': nrm((DEPTH, DH, DH), DH ** -0.5),
        'w_up_a': nrm((DEPTH, H_A * DH, D_MODEL), (H_A * DH) ** -0.5),
        'ret_gn_w': 1.0 + nrm((DEPTH, H_R * DV_R), 0.01),
        'w_up_r': nrm((DEPTH, H_R * DV_R, D_MODEL), (H_R * DV_R) ** -0.5),
        'w_out': nrm((DEPTH, D_MODEL, D_MODEL), D_MODEL ** -0.5),
        'w_ffn_up': nrm((DEPTH, D_MODEL, 2 * D_FF), D_MODEL ** -0.5),
        'ffn_conv_w': nrm((DEPTH, CONV_W, D_FF), CONV_W ** -0.5),
        'ffn_conv_b': nrm((DEPTH, D_FF), 0.01),
        'w_ffn_down': nrm((DEPTH, D_FF, D_MODEL), D_FF ** -0.5),
        'normf_w': 1.0 + nrm((D_MODEL,), 0.01),
    }


def reference(x_prompt, x_sample, cache_cmp_kv, cache_slc_kv, state_win_kv, state_ret, state_conv, page_table,
              c_prompt, c_sample, norm1_w, norm2_w, w_ada, b_ada, w_in, phi_pos_k, phi_k1, phi_k2, phi_pos_v,
              phi_v1, phi_v2, w_up_a, ret_gn_w, w_up_r, w_out, w_ffn_up, ffn_conv_w, ffn_conv_b, w_ffn_down, normf_w):
    B, T, _ = x_prompt.shape
    DB, S, _ = x_sample.shape
    P = page_table.shape[1] * cache_cmp_kv.shape[2]
    pos_p = jnp.arange(T)
    pos_s = P + jnp.arange(S)
    xp, xs = x_prompt, x_sample
    st_p = [[] for _ in range(5)]
    st_s = [[] for _ in range(5)]
    for l in range(DEPTH):
        lw = (norm1_w[l], norm2_w[l], w_ada[l], b_ada[l], w_in[l], w_up_a[l], ret_gn_w[l], w_up_r[l],
              w_out[l], w_ffn_up[l], ffn_conv_w[l], ffn_conv_b[l], w_ffn_down[l])
        phi = (phi_pos_k[l], phi_k1[l], phi_k2[l], phi_pos_v[l], phi_v1[l], phi_v2[l])
        attend_p = functools.partial(attend_prompt, phi=phi)
        xp, new_p = layer(xp, c_prompt, pos_p, attend_p,
                          jnp.zeros((B, H_R, DK_R, DV_R), xp.dtype),
                          jnp.zeros((B, CONV_W - 1, D_FF), xp.dtype), *lw)
        past_cmp = cache_cmp_kv[l][page_table].reshape(DB, P, 2, G_A, DH)
        past_slc = cache_slc_kv[l][page_table].reshape(DB, P, 2, G_A, DH)
        attend_s = functools.partial(attend_sample, phi=phi, past_cmp=past_cmp, past_slc=past_slc,
                                     win_buf=state_win_kv[l])
        xs, new_s = layer(xs, c_sample, pos_s, attend_s, state_ret[l], state_conv[l], *lw)
        for i in range(5):
            st_p[i].append(new_p[i])
            st_s[i].append(new_s[i])
    y_prompt = rmsnorm(xp, normf_w)
    y_sample = rmsnorm(xs, normf_w)
    cmp_p, slc_p, win_p, ret_p, conv_p = [jnp.stack(s) for s in st_p]
    cmp_s, slc_s, win_s, ret_s, conv_s = [jnp.stack(s) for s in st_s]
    return (y_prompt, y_sample, cmp_p, cmp_s, slc_p, slc_s, win_p, win_s, ret_p, ret_s, conv_p, conv_s)
```

```python
import functools

import numpy as np
import jax
import jax.numpy as jnp
from jax import lax
from jax.experimental import pallas as pl
from jax.experimental.pallas import tpu as pltpu

BF = jnp.bfloat16
F32 = jnp.float32

D_MODEL = 1024
H_A, G_A, DH = 8, 2, 64
HPG = H_A // G_A
ROPE_DIM = DH // 4
ROPE_THETA = 500000.0
L_CMP, L_SEL, N_SEL = 32, 64, 8
WINDOW = 512
Q_BLOCK = 128
FORCE_BONUS = 1e4
H_R, DK_R, DV_R = 4, 128, 128
RET_THETA = 10000.0
D_FF = 2816
CONV_W = 3
EPS = 1e-6
NEG = -1e30
LANE = 128
VMEM_LIMIT = 56 * 1024 * 1024


def _cparams(sem):
    return pltpu.CompilerParams(dimension_semantics=sem, vmem_limit_bytes=VMEM_LIMIT)


def _sigmoid(x):
    return 1.0 / (1.0 + jnp.exp(-x))


def _gelu(x):
    return 0.5 * x * (1.0 + jnp.tanh(0.7978845608028654 * (x + 0.044715 * (x * x * x))))


def _nt(a, b):
    return lax.dot_general(a, b, (((1,), (1,)), ((), ())), preferred_element_type=F32)


def _tn(a, b):
    return lax.dot_general(a, b, (((0,), (0,)), ((), ())), preferred_element_type=F32)


def _nn(a, b):
    return jnp.dot(a, b, preferred_element_type=F32)


def _rope_tab(pos, scale):
    half = ROPE_DIM // 2
    inv = ROPE_THETA ** (-np.arange(half, dtype=np.float64) * (2.0 / ROPE_DIM))
    ang = pos.astype(np.float64)[:, None] * inv
    cos, sin = np.cos(ang), np.sin(ang)
    n = pos.shape[0]
    c = np.ones((n, DH)); s_lo = np.zeros((n, DH)); s_hi = np.zeros((n, DH))
    c[:, :half] = cos; c[:, half:ROPE_DIM] = cos
    s_lo[:, half:ROPE_DIM] = sin
    s_hi[:, :half] = -sin
    tab = np.concatenate([np.tile(t, (1, 2)) for t in (c, s_lo, s_hi)], axis=1) * scale
    return jnp.asarray(tab, F32)


def _ret_tab(pos):
    half = DK_R // 2
    inv = RET_THETA ** (-np.arange(half, dtype=np.float64) * (2.0 / DK_R))
    ang = pos.astype(np.float64)[:, None] * inv
    cos, sin = np.cos(ang), np.sin(ang)
    c = np.concatenate([cos, cos], axis=1)
    s = np.concatenate([-sin, sin], axis=1)
    ks = DK_R ** -0.5
    return jnp.asarray(np.concatenate([c, s, c * ks, s * ks], axis=1), F32)


def _ret_decay(C, c_true):
    h = np.arange(H_R, dtype=np.float64)
    log_g = np.log1p(-np.exp2(-5.0 - h))
    i = np.arange(C, dtype=np.float64)
    diff = i[:, None] - i[None, :]
    dm = np.where(diff >= 0, np.exp(log_g[:, None, None] * np.maximum(diff, 0.0)), 0.0)
    dq = np.exp(log_g[:, None] * (i + 1.0))[:, :, None] * np.ones((1, 1, LANE))
    wk = np.exp(log_g[:, None] * (c_true - 1.0 - i))[:, :, None] * np.ones((1, 1, LANE))
    wk = np.where(i[None, :, None] < c_true, wk, 0.0)
    gc = np.exp(log_g * c_true)[:, None, None] * np.ones((1, 8, LANE))
    return (jnp.asarray(dm, F32), jnp.asarray(dq, F32), jnp.asarray(wk, F32), jnp.asarray(gc, F32))


def _mods_body(c_ref, w_ref, b_ref, o_ref):
    c = c_ref[...]
    s = c * _sigmoid(c)
    o_ref[...] = _nn(s.astype(BF), w_ref[...].astype(BF)) + b_ref[...]


def _mods(c_all, w_ada, b_ada):
    n = c_all.shape[0]
    nout = w_ada.shape[1]
    tn = 1024
    return pl.pallas_call(
        _mods_body,
        grid=(nout // tn,),
        in_specs=[pl.BlockSpec((n, D_MODEL), lambda j: (0, 0)),
                  pl.BlockSpec((D_MODEL, tn), lambda j: (0, j)),
                  pl.BlockSpec((1, tn), lambda j: (0, j))],
        out_specs=pl.BlockSpec((n, tn), lambda j: (0, j)),
        out_shape=jax.ShapeDtypeStruct((n, nout), F32),
        compiler_params=_cparams(("arbitrary",)),
        name="mods",
    )(c_all, w_ada, b_ada.reshape(1, nout))


_C_Q, _C_KV, _C_QR, _C_KR, _C_VR, _C_GR, _C_GM, _C_GA, _C_END = (
    0, 512, 1280, 1792, 2304, 2816, 3328, 5376, 5632)


def _pack_w_in(w_in):
    o = np.cumsum((0, 512, 768, 24, 512, 512, 512, 512, 2048))
    q, kv, ga, qr, kr, vr, gr, gm = [w_in[:, o[i]:o[i + 1]] for i in range(8)]
    ga = ga.reshape(D_MODEL, 3, G_A, HPG).transpose(0, 2, 1, 3).reshape(D_MODEL, G_A, 3 * HPG)
    ga = jnp.pad(ga, ((0, 0), (0, 0), (0, LANE - 3 * HPG))).reshape(D_MODEL, G_A * LANE)
    return jnp.concatenate([q, kv, qr, kr, vr, gr, gm, ga], axis=1).astype(BF)


def _inproj_body(x_ref, nw_ref, sh_ref, sc_ref, w_ref, rq_ref, rk_ref, rr_ref,
                 q_ref, qr_ref, kvc_ref, kvs_ref, kvw_ref, kvb_ref, ret_ref, gr_ref, gm_ref, ga_ref):
    x = x_ref[...]
    tm = x.shape[0]
    ms = jnp.mean(x * x, axis=-1, keepdims=True)
    h = (x * lax.rsqrt(ms + EPS)) * nw_ref[...]
    h = h * (1.0 + sc_ref[...]) + sh_ref[...]
    hb = h.astype(BF)
    lo64 = lax.broadcasted_iota(jnp.int32, (tm, LANE), 1) < DH

    def mm(lo, hi):
        return _nn(hb, w_ref[:, lo:hi])

    def rope(xc, tab_ref):
        return (xc * tab_ref[:, 0:LANE] + pltpu.roll(xc, 8, 1) * tab_ref[:, LANE:2 * LANE]
                + pltpu.roll(xc, LANE - 8, 1) * tab_ref[:, 2 * LANE:3 * LANE])

    qa = mm(_C_Q, _C_KV)
    q_ref[...] = (qa * (DH ** -0.5)).astype(BF)
    for c in range(4):
        qr_ref[:, c * LANE:(c + 1) * LANE] = rope(qa[:, c * LANE:(c + 1) * LANE], rq_ref).astype(BF)

    kv = mm(_C_KV, _C_QR)
    kvc_ref[...] = kv[:, 0:256]
    for kind, out_ref in ((0, kvs_ref), (1, kvw_ref)):
        base = 256 + kind * 256
        k = rope(kv[:, base:base + LANE], rk_ref)
        v = kv[:, base + LANE:base + 2 * LANE]
        out_ref[:, 0:LANE] = k
        out_ref[:, LANE:2 * LANE] = v
        kr_ = pltpu.roll(k, DH, 1)
        vr_ = pltpu.roll(v, DH, 1)
        kvb_ref[kind, 0, 0] = jnp.where(lo64, k, vr_).astype(BF)
        kvb_ref[kind, 0, 1] = jnp.where(lo64, v, kr_).astype(BF)
        kvb_ref[kind, 1, 0] = jnp.where(lo64, kr_, v).astype(BF)
        kvb_ref[kind, 1, 1] = jnp.where(lo64, vr_, k).astype(BF)

    qr = mm(_C_QR, _C_KR)
    kr = mm(_C_KR, _C_VR)
    for hh in range(H_R):
        sl = slice(hh * LANE, (hh + 1) * LANE)
        xq = qr[:, sl]
        ret_ref[0, :, sl] = (xq * rr_ref[:, 0:LANE] + pltpu.roll(xq, DK_R // 2, 1) * rr_ref[:, LANE:2 * LANE]).astype(BF)
        xk = kr[:, sl]
        ret_ref[1, :, sl] = (xk * rr_ref[:, 2 * LANE:3 * LANE]
                             + pltpu.roll(xk, DK_R // 2, 1) * rr_ref[:, 3 * LANE:4 * LANE]).astype(BF)
    ret_ref[2] = mm(_C_VR, _C_GR).astype(BF)
    g = mm(_C_GR, _C_GM)
    gr_ref[...] = (g * _sigmoid(g)).astype(BF)
    gm_ref[...] = _sigmoid(mm(_C_GM, _C_GA)).astype(BF)
    ga_ref[...] = _sigmoid(mm(_C_GA, _C_END))


def _inproj(x2d, nw, sh, sc, w_pack, rq, rk, rr, tm, tab_blocks, per_row_mods):
    rows = x2d.shape[0]
    nb = rows // tm
    if per_row_mods:
        mod_spec = pl.BlockSpec((tm, D_MODEL), lambda i: (i, 0))
    else:
        mod_spec = pl.BlockSpec((None, 1, D_MODEL), lambda i: (i // tab_blocks, 0, 0))
    tab = lambda w: pl.BlockSpec((tm, w), lambda i: (i % tab_blocks, 0))
    row = lambda w: pl.BlockSpec((tm, w), lambda i: (i, 0))
    out_shapes = [
        jax.ShapeDtypeStruct((rows, 512), BF),
        jax.ShapeDtypeStruct((rows, 512), BF),
        jax.ShapeDtypeStruct((rows, 256), F32),
        jax.ShapeDtypeStruct((rows, 256), F32),
        jax.ShapeDtypeStruct((rows, 256), F32),
        jax.ShapeDtypeStruct((2, G_A, 2, rows, LANE), BF),
        jax.ShapeDtypeStruct((3, rows, 512), BF),
        jax.ShapeDtypeStruct((rows, 512), BF),
        jax.ShapeDtypeStruct((rows, 2048), BF),
        jax.ShapeDtypeStruct((rows, 256), F32),
    ]
    out_specs = [row(512), row(512), row(256), row(256), row(256),
                 pl.BlockSpec((2, G_A, 2, tm, LANE), lambda i: (0, 0, 0, i, 0)),
                 pl.BlockSpec((3, tm, 512), lambda i: (0, i, 0)),
                 row(512), row(2048), row(256)]
    return pl.pallas_call(
        _inproj_body,
        grid=(nb,),
        in_specs=[row(D_MODEL),
                  pl.BlockSpec((1, D_MODEL), lambda i: (0, 0)),
                  mod_spec, mod_spec,
                  pl.BlockSpec((D_MODEL, _C_END), lambda i: (0, 0), pipeline_mode=pl.Buffered(1)),
                  tab(384), tab(384), tab(512)],
        out_specs=out_specs,
        out_shape=out_shapes,
        compiler_params=_cparams(("arbitrary",)),
        name="inproj",
    )(x2d, nw, sh, sc, w_pack, rq, rk, rr)


def _pack_phi(phi_pos_k, phi_k1, phi_k2, phi_pos_v, phi_v1, phi_v2):
    w1 = jnp.zeros((L_CMP, 2, G_A, DH, 2, G_A, DH), F32)
    for kv, w in ((0, phi_k1), (1, phi_v1)):
        wl = w.reshape(L_CMP, DH, DH)
        for g in range(G_A):
            w1 = w1.at[:, kv, g, :, kv, g, :].set(wl)
    w1 = w1.reshape(L_CMP * 2 * G_A * DH, 2 * G_A * DH).astype(BF)
    w2 = jnp.zeros((2, G_A, DH, 2, G_A, DH), F32)
    for kv, w in ((0, phi_k2), (1, phi_v2)):
        for g in range(G_A):
            w2 = w2.at[kv, g, :, kv, g, :].set(w)
    w2 = w2.reshape(2 * G_A * DH, 2 * G_A * DH).astype(BF)
    pos = jnp.stack([phi_pos_k, phi_pos_v], axis=1)
    pos = jnp.broadcast_to(pos[:, :, None, :], (L_CMP, 2, G_A, DH)).reshape(1, L_CMP * 2 * G_A * DH)
    return pos, w1, w2


def _compress_body(x_ref, pos_ref, w1_ref, w2_ref, o_ref):
    xb = (x_ref[...] + pos_ref[...]).astype(BF)
    hmid = _gelu(_nn(xb, w1_ref[...]))
    o_ref[...] = _nn(hmid.astype(BF), w2_ref[...])


def _compress(blocks, pos, w1, w2):
    nblk, feat = blocks.shape
    tb = min(nblk, 128)
    return pl.pallas_call(
        _compress_body,
        grid=(nblk // tb,),
        in_specs=[pl.BlockSpec((tb, feat), lambda i: (i, 0)),
                  pl.BlockSpec((1, feat), lambda i: (0, 0)),
                  pl.BlockSpec((feat, 256), lambda i: (0, 0)),
                  pl.BlockSpec((256, 256), lambda i: (0, 0))],
        out_specs=pl.BlockSpec((tb, 256), lambda i: (i, 0)),
        out_shape=jax.ShapeDtypeStruct((nblk, 256), F32),
        compiler_params=_cparams(("arbitrary",)),
        name="compress",
    )(blocks, pos, w1, w2)


_KT = 512


def _nsa_prompt_body(q_ref, qr_ref, skv_ref, svk_ref, wkv_ref, wvk_ref, kc_ref, vc_ref, ga_ref, e_ref, o_ref,
                     *, nc, ns):
    i = pl.program_id(2)
    QB = Q_BLOCK
    lane = lax.broadcasted_iota(jnp.int32, (QB, LANE), 1)
    lo64 = lane < DH
    zero_b = jnp.zeros((QB, LANE), BF)

    def split_heads(ref):
        ev, od = [], []
        for c in range(2):
            xc = ref[:, c * LANE:(c + 1) * LANE]
            ev.append(jnp.where(lo64, xc, zero_b))
            od.append(jnp.where(lo64, zero_b, xc))
        return jnp.concatenate(ev, axis=0), jnp.concatenate(od, axis=0)

    q_e, q_o = split_heads(q_ref)
    kc = kc_ref[...]
    vc = vc_ref[...]
    tq_l = i * QB + lax.broadcasted_iota(jnp.int32, (nc, QB), 1)
    r_c = lax.broadcasted_iota(jnp.int32, (nc, QB), 0)
    half = nc // 2
    blk_c = jnp.where(r_c < half, 2 * r_c, 2 * (r_c - half) + 1)
    cmask = (blk_c * L_CMP + (L_CMP - 1)) <= tq_l
    cmask_f = cmask.astype(F32)
    imp = jnp.zeros((ns, QB), F32)
    o_cmp = {}
    for stack, qs in ((0, q_e), (1, q_o)):
        st = _nt(kc, qs)
        for c in range(2):
            s = jnp.where(cmask, st[:, c * QB:(c + 1) * QB], NEG)
            p = jnp.exp(s - jnp.max(s, axis=0, keepdims=True)) * cmask_f
            p = p / jnp.maximum(jnp.sum(p, axis=0, keepdims=True), 1e-30)
            imp = imp + p[0:half] + p[half:nc]
            o_cmp[(c, stack)] = _tn(p.astype(BF), vc)
    tq_s = i * QB + lax.broadcasted_iota(jnp.int32, (ns, QB), 1)
    blk_s = lax.broadcasted_iota(jnp.int32, (ns, QB), 0)
    valid = (blk_s * L_SEL) <= tq_s
    forced = (blk_s == 0) | (blk_s == tq_s // L_SEL)
    score = jnp.where(valid, imp + jnp.where(forced, FORCE_BONUS, 0.0), NEG)
    sel = jnp.zeros((ns, QB), F32)
    for _ in range(min(N_SEL, ns)):
        m = jnp.max(score, axis=0, keepdims=True)
        idx = jnp.min(jnp.where(score == m, blk_s, ns), axis=0, keepdims=True)
        hit = blk_s == idx
        sel = jnp.where(hit & (m > 0.5 * NEG), 1.0, sel)
        score = jnp.where(hit, -jnp.inf, score)
    sel_b = sel.astype(BF)

    qr_e, qr_o = split_heads(qr_ref)
    tq_r = i * QB + lax.broadcasted_iota(jnp.int32, (QB, _KT), 0)
    kcol = lax.broadcasted_iota(jnp.int32, (QB, _KT), 1)

    def tile_step(j, carry):
        m_e, l_e, a_e, m_o, l_o, a_o = carry
        off = pl.multiple_of(j * _KT, _KT)
        kv = skv_ref[pl.ds(off, _KT), :]
        vk = svk_ref[pl.ds(off, _KT), :]
        picked = _tn(sel_b, e_ref[:, pl.ds(off, _KT)])
        bias = jnp.where((picked > 0.5) & ((kcol + off) <= tq_r), 0.0, NEG)
        bias2 = jnp.concatenate([bias, bias], axis=0)

        def upd(qs, kmat, vmat, m, l, a):
            s = _nt(qs, kmat) + bias2
            m_new = jnp.maximum(m, jnp.max(s, axis=1, keepdims=True))
            alpha = jnp.exp(m - m_new)
            p = jnp.exp(s - m_new)
            l_new = alpha * l + jnp.sum(p, axis=1, keepdims=True)
            a_new = alpha * a + _nn(p.astype(BF), vmat)
            return m_new, l_new, a_new

        m_e, l_e, a_e = upd(qr_e, kv, vk, m_e, l_e, a_e)
        m_o, l_o, a_o = upd(qr_o, vk, kv, m_o, l_o, a_o)
        return m_e, l_e, a_e, m_o, l_o, a_o

    init = (jnp.full((2 * QB, 1), -jnp.inf, F32), jnp.zeros((2 * QB, 1), F32), jnp.zeros((2 * QB, LANE), F32)) * 2
    n_tiles = (i * QB + QB + _KT - 1) // _KT
    m_e, l_e, a_e, m_o, l_o, a_o = lax.fori_loop(0, n_tiles, tile_step, init)
    os_e = a_e / l_e
    os_o = a_o / l_o

    WK = WINDOW + QB
    start = pl.multiple_of(jnp.maximum(i * QB - WINDOW, 0), QB)
    wkv = wkv_ref[pl.ds(start, WK), :]
    wvk = wvk_ref[pl.ds(start, WK), :]
    diff = (i * QB + lax.broadcasted_iota(jnp.int32, (QB, WK), 0)) - (start + lax.broadcasted_iota(jnp.int32, (QB, WK), 1))
    wb = jnp.where((diff >= 0) & (diff < WINDOW), 0.0, NEG)
    wb2 = jnp.concatenate([wb, wb], axis=0)

    def win(qs, kmat, vmat):
        s = _nt(qs, kmat) + wb2
        p = jnp.exp(s - jnp.max(s, axis=1, keepdims=True))
        return _nn(p.astype(BF), vmat) / jnp.sum(p, axis=1, keepdims=True)

    ow_e = win(qr_e, wkv, wvk)
    ow_o = win(qr_o, wvk, wkv)

    ga = ga_ref[...]
    for c in range(2):
        rows = slice(c * QB, (c + 1) * QB)
        branches = (
            jnp.where(lo64, o_cmp[(c, 0)], o_cmp[(c, 1)]),
            jnp.where(lo64, os_e[rows], os_o[rows]),
            jnp.where(lo64, ow_e[rows], ow_o[rows]),
        )
        acc = jnp.zeros((QB, LANE), F32)
        for br in range(3):
            col = br * HPG + 2 * c
            gate = jnp.where(lo64, ga[:, col:col + 1], ga[:, col + 1:col + 2])
            acc = acc + gate * branches[br]
        o_ref[:, c * LANE:(c + 1) * LANE] = acc.astype(BF)


def _sel_expand(ns, nkeys):
    e = (np.arange(nkeys)[None, :] // L_SEL) == np.arange(ns)[:, None]
    return jnp.asarray(e, BF)


def _nsa_prompt(q, qr, kvb, kc2, vc2, ga, B, T):
    nqb = T // Q_BLOCK
    nc, ns = T // L_CMP, T // L_SEL
    qspec = pl.BlockSpec((Q_BLOCK, 256), lambda b, g, i: (b * nqb + i, g))
    kvspec = lambda kind, lay: pl.BlockSpec((None, None, None, T, LANE),
                                            lambda b, g, i: (kind, g, lay, b, 0))
    cspec = pl.BlockSpec((None, None, nc, LANE), lambda b, g, i: (b, g, 0, 0))
    return pl.pallas_call(
        functools.partial(_nsa_prompt_body, nc=nc, ns=ns),
        grid=(B, G_A, nqb),
        in_specs=[qspec, qspec, kvspec(0, 0), kvspec(0, 1), kvspec(1, 0), kvspec(1, 1), cspec, cspec,
                  pl.BlockSpec((Q_BLOCK, LANE), lambda b, g, i: (b * nqb + i, g)),
                  pl.BlockSpec((ns, T), lambda b, g, i: (0, 0))],
        out_specs=qspec,
        out_shape=jax.ShapeDtypeStruct((B * T, 512), BF),
        compiler_params=_cparams(("arbitrary", "arbitrary", "arbitrary")),
        name="nsa_prompt",
    )(q, qr, kvb, kvb, kvb, kvb, kc2, vc2, ga, _sel_expand(ns, T))


def _ret_body(q_ref, k_ref, v_ref, gr_ref, s0_ref, dm_ref, dq_ref, wk_ref, gc_ref, gnw_ref, z_ref, sout_ref, s_scr):
    c = pl.program_id(2)

    @pl.when(c == 0)
    def _():
        s_scr[...] = s0_ref[...]

    q = q_ref[...]
    k = k_ref[...]
    v = v_ref[...]
    s_old = s_scr[...]
    inner = _nt(q, k) * dm_ref[...]
    o = _nn(inner.astype(BF), v) + _nn(q, s_old.astype(BF)) * dq_ref[...]
    kw = (k.astype(F32) * wk_ref[...]).astype(BF)
    s_new = gc_ref[0:1, :] * s_old + _tn(kw, v)
    s_scr[...] = s_new
    sout_ref[...] = s_new
    mu = jnp.mean(o, axis=-1, keepdims=True)
    d = o - mu
    var = jnp.mean(d * d, axis=-1, keepdims=True)
    on = d * lax.rsqrt(var + EPS) * gnw_ref[...]
    z_ref[...] = (gr_ref[...].astype(F32) * on).astype(BF)


def _retention(ret3, gr, s0, gnw, nseq, rows_per_seq, C, c_true):
    nC = rows_per_seq // C
    rows = nseq * rows_per_seq
    dm, dq, wk, gc = _ret_decay(C, c_true)
    qkv = lambda j: pl.BlockSpec((None, C, LANE), lambda b, h, c: (j, b * nC + c, h))
    hd = lambda r: pl.BlockSpec((None, r, LANE), lambda b, h, c: (h, 0, 0))
    return pl.pallas_call(
        _ret_body,
        grid=(nseq, H_R, nC),
        in_specs=[qkv(0), qkv(1), qkv(2),
                  pl.BlockSpec((C, LANE), lambda b, h, c: (b * nC + c, h)),
                  pl.BlockSpec((None, None, DK_R, DV_R), lambda b, h, c: (b, h, 0, 0)),
                  pl.BlockSpec((None, C, C), lambda b, h, c: (h, 0, 0)),
                  hd(C), hd(C), hd(8),
                  pl.BlockSpec((1, LANE), lambda b, h, c: (0, h))],
        out_specs=[pl.BlockSpec((C, LANE), lambda b, h, c: (b * nC + c, h)),
                   pl.BlockSpec((None, None, DK_R, DV_R), lambda b, h, c: (b, h, 0, 0))],
        out_shape=[jax.ShapeDtypeStruct((rows, 512), BF),
                   jax.ShapeDtypeStruct((nseq, H_R, DK_R, DV_R), F32)],
        scratch_shapes=[pltpu.VMEM((DK_R, DV_R), F32)],
        compiler_params=_cparams(("arbitrary", "arbitrary", "arbitrary")),
        name="retention",
    )(ret3, ret3, ret3, gr, s0, dm, dq, wk, gc, gnw)


def _merge_body(x_ref, oa_ref, zr_ref, gm_ref, gt_ref, wa_ref, wr_ref, wo_ref, o_ref):
    ya = _nn(oa_ref[...], wa_ref[...])
    yr = _nn(zr_ref[...], wr_ref[...])
    gm = gm_ref[...].astype(F32)
    merged = gm[:, 0:D_MODEL] * ya + gm[:, D_MODEL:2 * D_MODEL] * yr
    mix = _nn(merged.astype(BF), wo_ref[...])
    o_ref[...] = x_ref[...] + gt_ref[...] * mix


def _merge(x2d, oa, zr, gm, gt, wa, wr, wo, tm, blocks_per_seq, per_row_mods):
    rows = x2d.shape[0]
    if per_row_mods:
        mod_spec = pl.BlockSpec((tm, D_MODEL), lambda i: (i, 0))
    else:
        mod_spec = pl.BlockSpec((None, 1, D_MODEL), lambda i: (i // blocks_per_seq, 0, 0))
    row = lambda w: pl.BlockSpec((tm, w), lambda i: (i, 0))
    full = lambda a, b: pl.BlockSpec((a, b), lambda i: (0, 0))
    return pl.pallas_call(
        _merge_body,
        grid=(rows // tm,),
        in_specs=[row(D_MODEL), row(512), row(512), row(2048), mod_spec,
                  full(512, D_MODEL), full(512, D_MODEL), full(D_MODEL, D_MODEL)],
        out_specs=row(D_MODEL),
        out_shape=jax.ShapeDtypeStruct((rows, D_MODEL), F32),
        compiler_params=_cparams(("arbitrary",)),
        name="merge",
    )(x2d, oa, zr, gm, gt, wa, wr, wo)


def _ffn_body(x_ref, nw_ref, sh_ref, sc_ref, gt_ref, wu_ref, cw_ref, cb_ref, wd_ref, nf_ref, p1_ref, p2_ref,
              y_ref, a_ref, carry, *, blocks_per_seq, seq_rows):
    i = pl.program_id(0)
    x = x_ref[...]
    tm = x.shape[0]
    ms = jnp.mean(x * x, axis=-1, keepdims=True)
    h = (x * lax.rsqrt(ms + EPS)) * nw_ref[...]
    h = (h * (1.0 + sc_ref[...]) + sh_ref[...]).astype(BF)
    a = _nn(h, wu_ref[:, 0:D_FF])
    b = _nn(h, wu_ref[:, D_FF:2 * D_FF])
    a_ref[...] = a
    rid = lax.broadcasted_iota(jnp.int32, (tm, D_FF), 0)
    if seq_rows is None:
        first = (i % blocks_per_seq) == 0
        prev = jnp.where(first, p1_ref[...], carry[...])
        carry[...] = a[tm - 8:tm, :]
        am1 = jnp.where(rid == 0, prev[7:8, :], pltpu.roll(a, 1, 0))
        am2 = jnp.where(rid == 0, prev[6:7, :], jnp.where(rid == 1, prev[7:8, :], pltpu.roll(a, 2, 0)))
    else:
        s = rid % seq_rows
        am1 = jnp.where(s == 0, p1_ref[...], pltpu.roll(a, 1, 0))
        am2 = jnp.where(s <= 1, p2_ref[...], pltpu.roll(a, 2, 0))
    u = cb_ref[...] + am2 * cw_ref[0:1, :] + am1 * cw_ref[1:2, :] + a * cw_ref[2:3, :]
    ff = _nn((_gelu(u) * b).astype(BF), wd_ref[...])
    x2 = x + gt_ref[...] * ff
    ms2 = jnp.mean(x2 * x2, axis=-1, keepdims=True)
    y_ref[...] = (x2 * lax.rsqrt(ms2 + EPS)) * nf_ref[...]


def _ffn(x2d, nw, sh, sc, gt, wu, cw, cb, wd, nf, p1, p2, tm, blocks_per_seq, per_row_mods, seq_rows):
    rows = x2d.shape[0]
    if per_row_mods:
        mod_spec = pl.BlockSpec((tm, D_MODEL), lambda i: (i, 0))
    else:
        mod_spec = pl.BlockSpec((None, 1, D_MODEL), lambda i: (i // blocks_per_seq, 0, 0))
    row = lambda w: pl.BlockSpec((tm, w), lambda i: (i, 0))
    full = lambda a, b: pl.BlockSpec((a, b), lambda i: (0, 0), pipeline_mode=pl.Buffered(1))
    vec = lambda w: pl.BlockSpec((1, w), lambda i: (0, 0))
    pspec = pl.BlockSpec(p1.shape, lambda i: (0, 0))
    return pl.pallas_call(
        functools.partial(_ffn_body, blocks_per_seq=blocks_per_seq, seq_rows=seq_rows),
        grid=(rows // tm,),
        in_specs=[row(D_MODEL), vec(D_MODEL), mod_spec, mod_spec, mod_spec,
                  full(D_MODEL, 2 * D_FF), pl.BlockSpec((CONV_W, D_FF), lambda i: (0, 0)), vec(D_FF),
                  full(D_FF, D_MODEL), vec(D_MODEL), pspec, pspec],
        out_specs=[row(D_MODEL), row(D_FF)],
        out_shape=[jax.ShapeDtypeStruct((rows, D_MODEL), F32), jax.ShapeDtypeStruct((rows, D_FF), F32)],
        scratch_shapes=[pltpu.VMEM((8, D_FF), F32)],
        compiler_params=_cparams(("arbitrary",)),
        name="ffn",
    )(x2d, nw, sh, sc, gt, wu, cw, cb, wd, nf, p1, p2)


def _prompt_path(x_prompt, mods_p, W):
    B, T, _ = x_prompt.shape
    rows = B * T
    x2d = x_prompt.reshape(rows, D_MODEL)
    pos = np.arange(T)
    tm = 256
    sh1, sc1, gt1, sh2, sc2, gt2 = [mods_p[:, None, j * D_MODEL:(j + 1) * D_MODEL] for j in range(6)]
    (q, qr, kvc, kvs, kvw, kvb, ret3, gr, gm, ga) = _inproj(
        x2d, W["norm1"], sh1, sc1, W["w_in"], _rope_tab(pos, DH ** -0.5), _rope_tab(pos, 1.0), _ret_tab(pos),
        tm, T // tm, False)
    nc = T // L_CMP
    comp = _compress(kvc.reshape(B * nc, L_CMP * 256), W["phi_pos"], W["phi_w1"], W["phi_w2"])
    comp = comp.reshape(B, nc // 2, 2, 2, G_A, DH).transpose(3, 0, 4, 2, 1, 5)
    comp = comp.reshape(2, B, G_A, nc, DH)
    comp2 = jnp.concatenate([comp, comp], axis=-1).astype(BF)
    oa = _nsa_prompt(q, qr, kvb, comp2[0], comp2[1], ga, B, T)
    C = 256 if T % 256 == 0 else T
    zr, ret_new = _retention(ret3, gr, jnp.zeros((B, H_R, DK_R, DV_R), F32), W["gnw"], B, T, C, C)
    x1 = _merge(x2d, oa, zr, gm, gt1, W["w_up_a"], W["w_up_r"], W["w_out"], tm, T // tm, False)
    zeros8 = jnp.zeros((8, D_FF), F32)
    y, a_up = _ffn(x1, W["norm2"], sh2, sc2, gt2, W["w_ffn_up"], W["conv_w"], W["conv_b"], W["w_ffn_down"],
                   W["normf"], zeros8, zeros8, tm, T // tm, False, None)
    shp = (1, B, T, 2, G_A, DH)
    wsz = min(WINDOW, T)
    outs = dict(
        y=y.reshape(B, T, D_MODEL),
        cmp=kvc.reshape(shp), slc=kvs.reshape(shp),
        win=kvw.reshape(B, T, 2, G_A, DH)[None, :, T - wsz:],
        ret=ret_new[None],
        conv=a_up.reshape(B, T, D_FF)[None, :, T - (CONV_W - 1):],
    )
    return outs


def _cmp_sample_body(pt_ref, cache_ref, q_ref, pos_ref, w1_ref, w2_ref, ocmp_ref, topi_ref, xbuf, sem, kcs,
                     *, ppg, page_rows, n_pick, n_q):
    b, hf = pl.program_id(0), pl.program_id(1)
    nh = pl.num_programs(1)
    t = b * nh + hf
    n_steps = pl.num_programs(0) * nh
    slot = t % 2

    def copies(bb, hh, sl, p):
        page = pt_ref[bb, hh * ppg + p]
        return [pltpu.make_async_copy(cache_ref.at[page, :, kv, g, :],
                                      xbuf.at[sl, kv, g, pl.ds(p * page_rows, page_rows), :], sem.at[sl])
                for kv in range(2) for g in range(G_A)]

    def issue(bb, hh, sl):
        def f(p, c):
            for cp in copies(bb, hh, sl, p):
                cp.start()
            return c
        lax.fori_loop(0, ppg, f, 0)

    @pl.when(t == 0)
    def _():
        issue(b, hf, slot)

    @pl.when(t + 1 < n_steps)
    def _():
        issue((t + 1) // nh, (t + 1) % nh, 1 - slot)

    def wait(p, c):
        for cp in copies(b, hf, slot, p):
            cp.wait()
        return c
    lax.fori_loop(0, ppg, wait, 0)

    nblk = ppg * page_rows // L_CMP
    hb = nblk // 2
    for kv in range(2):
        for g in range(G_A):
            acc = jnp.zeros((nblk, DH), F32)
            for l in range(L_CMP):
                xe = xbuf[slot, kv, g, pl.ds(l, hb, stride=2 * L_CMP), :]
                xo = xbuf[slot, kv, g, pl.ds(l + L_CMP, hb, stride=2 * L_CMP), :]
                xl = jnp.concatenate([xe, xo], axis=0) + pos_ref[kv, l]
                acc = acc + _nn(xl.astype(BF), w1_ref[kv, l])
            kcs[kv, g, pl.ds(pl.multiple_of(hf * nblk, nblk), nblk), :] = _nn(_gelu(acc).astype(BF), w2_ref[kv])

    @pl.when(hf == nh - 1)
    def _():
        nhs = kcs.shape[2] // nblk
        for g in range(G_A):
            kc = kcs[0, g].astype(BF)
            vc = kcs[1, g].astype(BF)
            st = _nt(kc, q_ref[g])
            p = jnp.exp(st - jnp.max(st, axis=0, keepdims=True))
            p = p / jnp.sum(p, axis=0, keepdims=True)
            ocmp_ref[g] = _tn(p.astype(BF), vc)
            pair = jnp.concatenate([p[h * nblk:h * nblk + hb] + p[h * nblk + hb:(h + 1) * nblk]
                                    for h in range(nhs)], axis=0)
            imp = pair
            for r in range(1, HPG):
                imp = imp + pltpu.roll(pair, LANE - r * n_q, 1)
            nsel = pair.shape[0]
            blk = lax.broadcasted_iota(jnp.int32, (nsel, LANE), 0)
            score = imp + jnp.where(blk == 0, FORCE_BONUS, 0.0)
            topi_ref[g] = jnp.zeros((8, LANE), jnp.int32)
            for k in range(n_pick):
                m = jnp.max(score, axis=0, keepdims=True)
                idx = jnp.min(jnp.where(score == m, blk, nsel), axis=0, keepdims=True)
                topi_ref[g, k:k + 1, :] = idx
                score = jnp.where(blk == idx, -jnp.inf, score)


def _cmp_sample(page_table, cache, q_cmp, pos, w1, w2, n_q):
    DB, n_pages = page_table.shape
    page_rows = cache.shape[1]
    nh = 2
    ppg = n_pages // nh
    ncb = n_pages * page_rows // L_CMP
    kern = functools.partial(_cmp_sample_body, ppg=ppg, page_rows=page_rows, n_pick=N_SEL - 1, n_q=n_q)
    return pl.pallas_call(
        kern,
        grid_spec=pltpu.PrefetchScalarGridSpec(
            num_scalar_prefetch=1, grid=(DB, nh),
            in_specs=[pl.BlockSpec(memory_space=pl.ANY),
                      pl.BlockSpec((None, G_A, LANE, DH), lambda b, h, pt: (b, 0, 0, 0)),
                      pl.BlockSpec((2, L_CMP, 1, DH), lambda b, h, pt: (0, 0, 0, 0)),
                      pl.BlockSpec((2, L_CMP, DH, DH), lambda b, h, pt: (0, 0, 0, 0)),
                      pl.BlockSpec((2, DH, DH), lambda b, h, pt: (0, 0, 0))],
            out_specs=[pl.BlockSpec((None, G_A, LANE, DH), lambda b, h, pt: (b, 0, 0, 0)),
                       pl.BlockSpec((None, G_A, 8, LANE), lambda b, h, pt: (b, 0, 0, 0))],
            scratch_shapes=[pltpu.VMEM((2, 2, G_A, ppg * page_rows, DH), F32),
                            pltpu.SemaphoreType.DMA((2,)),
                            pltpu.VMEM((2, G_A, ncb, DH), F32)]),
        out_shape=[jax.ShapeDtypeStruct((DB, G_A, LANE, DH), F32),
                   jax.ShapeDtypeStruct((DB, G_A, 8, LANE), jnp.int32)],
        compiler_params=_cparams(("arbitrary", "arbitrary")),
        name="cmp_sample",
    )(page_table, cache, q_cmp, pos, w1, w2)


def _slc_win_sample_body(pt_ref, ti_ref, cslc_ref, swin_ref, q_ref, ns_ref, nw_ref, oslc_ref, owin_ref,
                         sbuf, wbuf, sem, *, n_q, n_pick, blocks_per_page):
    b = pl.program_id(0)
    nb = pl.num_programs(0)
    slot = b % 2
    wb = wbuf.shape[3]

    def copies(bb, sl):
        cps = []
        for g in range(G_A):
            for s in range(n_q):
                for k in range(n_pick):
                    blk = ti_ref[bb, (g * n_q + s) * n_pick + k]
                    page = pt_ref[bb, blk // blocks_per_page]
                    off = (blk % blocks_per_page) * L_SEL
                    for kv in range(2):
                        cps.append(pltpu.make_async_copy(
                            cslc_ref.at[page, pl.ds(off, L_SEL), kv, g, :],
                            sbuf.at[sl, g, kv, pl.ds((s * n_pick + k) * L_SEL, L_SEL), :], sem.at[sl]))
            for kv in range(2):
                cps.append(pltpu.make_async_copy(swin_ref.at[bb, :, kv, g, :], wbuf.at[sl, g, kv], sem.at[sl]))
        return cps

    @pl.when(b == 0)
    def _():
        for cp in copies(b, slot):
            cp.start()

    @pl.when(b + 1 < nb)
    def _():
        for cp in copies(b + 1, 1 - slot):
            cp.start()

    for cp in copies(b, slot):
        cp.wait()

    nq_rows = q_ref.shape[1]
    per_q = n_pick * L_SEL
    nk = n_q * per_q
    row_s = lax.broadcasted_iota(jnp.int32, (nq_rows, nk), 0) % n_q
    own = row_s == (lax.broadcasted_iota(jnp.int32, (nq_rows, nk), 1) // per_q)
    nnew = ns_ref.shape[2]
    new_ok = lax.broadcasted_iota(jnp.int32, (nq_rows, nnew), 1) <= (lax.broadcasted_iota(jnp.int32, (nq_rows, nnew), 0) % n_q)
    dwin = wb + (lax.broadcasted_iota(jnp.int32, (nq_rows, wb), 0) % n_q) - lax.broadcasted_iota(jnp.int32, (nq_rows, wb), 1)
    win_ok = (dwin >= 0) & (dwin < WINDOW)

    def attend(q, k_old, v_old, ok_old, k_new, v_new):
        s_o = jnp.where(ok_old, _nt(q, k_old), NEG)
        s_n = jnp.where(new_ok, _nt(q, k_new), NEG)
        m = jnp.maximum(jnp.max(s_o, axis=1, keepdims=True), jnp.max(s_n, axis=1, keepdims=True))
        p_o = jnp.exp(s_o - m)
        p_n = jnp.exp(s_n - m)
        den = jnp.sum(p_o, axis=1, keepdims=True) + jnp.sum(p_n, axis=1, keepdims=True)
        return (_nn(p_o.astype(BF), v_old) + _nn(p_n.astype(BF), v_new)) / den

    for g in range(G_A):
        q = q_ref[g]
        oslc_ref[g] = attend(q, sbuf[slot, g, 0].astype(BF), sbuf[slot, g, 1].astype(BF), own,
                             ns_ref[g, 0].astype(BF), ns_ref[g, 1].astype(BF))
        owin_ref[g] = attend(q, wbuf[slot, g, 0].astype(BF), wbuf[slot, g, 1].astype(BF), win_ok,
                             nw_ref[g, 0].astype(BF), nw_ref[g, 1].astype(BF))


def _slc_win_sample(page_table, topi, cache_slc, state_win, q_rot, new_slc, new_win, n_q):
    DB = page_table.shape[0]
    n_pick = N_SEL - 1
    wb = state_win.shape[1]
    nq_rows = q_rot.shape[2]
    kern = functools.partial(_slc_win_sample_body, n_q=n_q, n_pick=n_pick,
                             blocks_per_page=cache_slc.shape[1] // L_SEL)
    bspec = lambda shp: pl.BlockSpec((None,) + shp, lambda b, pt, ti: (b,) + (0,) * len(shp))
    return pl.pallas_call(
        kern,
        grid_spec=pltpu.PrefetchScalarGridSpec(
            num_scalar_prefetch=2, grid=(DB,),
            in_specs=[pl.BlockSpec(memory_space=pl.ANY), pl.BlockSpec(memory_space=pl.ANY),
                      bspec((G_A, nq_rows, DH)), bspec((G_A, 2, 8, DH)), bspec((G_A, 2, 8, DH))],
            out_specs=[bspec((G_A, nq_rows, DH)), bspec((G_A, nq_rows, DH))],
            scratch_shapes=[pltpu.VMEM((2, G_A, 2, n_q * n_pick * L_SEL, DH), F32),
                            pltpu.VMEM((2, G_A, 2, wb, DH), F32),
                            pltpu.SemaphoreType.DMA((2,))]),
        out_shape=[jax.ShapeDtypeStruct((DB, G_A, nq_rows, DH), F32)] * 2,
        compiler_params=_cparams(("arbitrary",)),
        name="slc_win_sample",
    )(page_table, topi, cache_slc, state_win, q_rot, new_slc, new_win)


def _gate_sample_body(oc_ref, os_ref, ow_ref, ga_ref, o_ref):
    rows = o_ref.shape[0]
    lo64 = lax.broadcasted_iota(jnp.int32, (rows, LANE), 1) < DH
    for c4 in range(H_A // 2):
        g, c = c4 // 2, c4 % 2
        sl = slice(c4 * LANE, (c4 + 1) * LANE)
        acc = jnp.zeros((rows, LANE), F32)
        for br, ref in enumerate((oc_ref, os_ref, ow_ref)):
            col = g * LANE + br * HPG + 2 * c
            gate = jnp.where(lo64, ga_ref[:, col:col + 1], ga_ref[:, col + 1:col + 2])
            acc = acc + gate * ref[:, sl]
        o_ref[:, sl] = acc.astype(BF)


def _gate_sample(oc, osl, ow, ga):
    rows = oc.shape[0]
    full = lambda w: pl.BlockSpec((rows, w), lambda i: (0, 0))
    return pl.pallas_call(
        _gate_sample_body, grid=(1,),
        in_specs=[full(512), full(512), full(512), full(256)],
        out_specs=full(512),
        out_shape=jax.ShapeDtypeStruct((rows, 512), BF),
        compiler_params=_cparams(("arbitrary",)),
        name="gate_sample",
    )(oc, osl, ow, ga)


def _sample_path(x_sample, mods_s, W, cache_cmp, cache_slc, state_win, state_ret, state_conv, page_table):
    DB, S, _ = x_sample.shape
    rows = DB * S
    page_rows = cache_cmp.shape[1]
    P = page_table.shape[1] * page_rows
    wb = state_win.shape[1]
    assert P % L_SEL == 0 and S < L_CMP and S <= 8 and wb == WINDOW and page_rows % L_SEL == 0
    assert P // L_SEL >= N_SEL and CONV_W == 3 and S >= CONV_W - 1
    pos = P + np.arange(S)
    pos_rows = np.tile(pos, DB)
    x2d = x_sample.reshape(rows, D_MODEL)
    modr = jnp.repeat(mods_s, S, axis=0)
    sh1, sc1, gt1, sh2, sc2, gt2 = [modr[:, j * D_MODEL:(j + 1) * D_MODEL] for j in range(6)]
    (q, qr, kvc, kvs, kvw, _, ret3, gr, gm, ga) = _inproj(
        x2d, W["norm1"], sh1, sc1, W["w_in"], _rope_tab(pos_rows, DH ** -0.5), _rope_tab(pos_rows, 1.0),
        _ret_tab(pos_rows), rows, 1, True)

    def to_heads(t):
        return t.reshape(DB, S, G_A, HPG, DH).transpose(0, 2, 3, 1, 4).reshape(DB, G_A, HPG * S, DH)

    def from_heads(t):
        return t.reshape(DB, G_A, HPG, S, DH).transpose(0, 3, 1, 2, 4).reshape(rows, H_A * DH)

    q_cmp = jnp.pad(to_heads(q), ((0, 0), (0, 0), (0, LANE - HPG * S), (0, 0)))
    o_cmp, topi = _cmp_sample(page_table, cache_cmp, q_cmp, W["phi_pos4"], W["phi_w1_4"], W["phi_w2_4"], S)
    topi = topi[:, :, :N_SEL - 1, :S].transpose(0, 1, 3, 2).reshape(DB, G_A * S * (N_SEL - 1))

    def new_rows(t):
        t = t.reshape(DB, S, 2, G_A, DH).transpose(0, 3, 2, 1, 4)
        return jnp.pad(t, ((0, 0), (0, 0), (0, 0), (0, 8 - S), (0, 0)))

    o_slc, o_win = _slc_win_sample(page_table, topi, cache_slc, state_win, to_heads(qr), new_rows(kvs),
                                   new_rows(kvw), S)
    oa = _gate_sample(from_heads(o_cmp[:, :, :HPG * S]), from_heads(o_slc), from_heads(o_win), ga)

    RP = 16
    padr = lambda t: jnp.pad(t.reshape(t.shape[:-2] + (DB, S, 512)),
                             ((0, 0),) * (t.ndim - 1) + ((0, RP - S), (0, 0))).reshape(t.shape[:-2] + (DB * RP, 512))
    zr, ret_new = _retention(padr(ret3), padr(gr), state_ret, W["gnw"], DB, RP, RP, S)
    zr = zr.reshape(DB, RP, 512)[:, :S].reshape(rows, 512)
    x1 = _merge(x2d, oa, zr, gm, gt1, W["w_up_a"], W["w_up_r"], W["w_out"], rows, 1, True)
    zs = jnp.zeros((DB, S, D_FF), F32)
    p1 = zs.at[:, 0].set(state_conv[:, 1]).reshape(rows, D_FF)
    p2 = zs.at[:, 0].set(state_conv[:, 0]).at[:, 1].set(state_conv[:, 1]).reshape(rows, D_FF)
    y, a_up = _ffn(x1, W["norm2"], sh2, sc2, gt2, W["w_ffn_up"], W["conv_w"], W["conv_b"], W["w_ffn_down"],
                   W["normf"], p1, p2, rows, 1, True, S)
    shp = (1, DB, S, 2, G_A, DH)
    return dict(
        y=y.reshape(DB, S, D_MODEL),
        cmp=kvc.reshape(shp), slc=kvs.reshape(shp),
        win=jnp.concatenate([state_win[:, S:], kvw.reshape(DB, S, 2, G_A, DH)], axis=1)[None],
        ret=ret_new[None],
        conv=a_up.reshape(DB, S, D_FF)[None, :, S - (CONV_W - 1):],
    )


def kernel(x_prompt, x_sample, cache_cmp_kv, cache_slc_kv, state_win_kv, state_ret, state_conv, page_table,
           c_prompt, c_sample, norm1_w, norm2_w, w_ada, b_ada, w_in, phi_pos_k, phi_k1, phi_k2, phi_pos_v,
           phi_v1, phi_v2, w_up_a, ret_gn_w, w_up_r, w_out, w_ffn_up, ffn_conv_w, ffn_conv_b, w_ffn_down, normf_w):
    B = x_prompt.shape[0]
    l = 0
    pos, w1, w2 = _pack_phi(phi_pos_k[l], phi_k1[l], phi_k2[l], phi_pos_v[l], phi_v1[l], phi_v2[l])
    W = dict(
        norm1=norm1_w[l].reshape(1, D_MODEL), norm2=norm2_w[l].reshape(1, D_MODEL), normf=normf_w.reshape(1, D_MODEL),
        w_in=_pack_w_in(w_in[l]), phi_pos=pos, phi_w1=w1, phi_w2=w2,
        w_up_a=w_up_a[l].astype(BF), w_up_r=w_up_r[l].astype(BF), w_out=w_out[l].astype(BF),
        gnw=ret_gn_w[l].reshape(1, H_R * DV_R),
        w_ffn_up=w_ffn_up[l].astype(BF), conv_w=ffn_conv_w[l], conv_b=ffn_conv_b[l].reshape(1, D_FF),
        w_ffn_down=w_ffn_down[l].astype(BF),
        phi_pos4=jnp.stack([phi_pos_k[l], phi_pos_v[l]])[:, :, None, :],
        phi_w1_4=jnp.stack([phi_k1[l].reshape(L_CMP, DH, DH), phi_v1[l].reshape(L_CMP, DH, DH)]).astype(BF),
        phi_w2_4=jnp.stack([phi_k2[l], phi_v2[l]]).astype(BF),
    )
    mods = _mods(jnp.concatenate([c_prompt, c_sample], axis=0), w_ada[l], b_ada[l])
    p = _prompt_path(x_prompt, mods[:B], W)
    s = _sample_path(x_sample, mods[B:], W, cache_cmp_kv[l], cache_slc_kv[l], state_win_kv[l], state_ret[l],
                     state_conv[l], page_table)
    return (p["y"], s["y"], p["cmp"], s["cmp"], p["slc"], s["slc"], p["win"], s["win"],
            p["ret"], s["ret"], p["conv"], s["conv"])
```

```python
import functools

import numpy as np
import jax
import jax.numpy as jnp
from jax import lax
from jax.experimental import pallas as pl
from jax.experimental.pallas import tpu as pltpu

BF = jnp.bfloat16
F32 = jnp.float32

D_MODEL = 1024
H_A, G_A, DH = 8, 2, 64
HPG = H_A // G_A
ROPE_DIM = DH // 4
ROPE_THETA = 500000.0
L_CMP, L_SEL, N_SEL = 32, 64, 8
WINDOW = 512
Q_BLOCK = 128
FORCE_BONUS = 1e4
H_R, DK_R, DV_R = 4, 128, 128
RET_THETA = 10000.0
D_FF = 2816
CONV_W = 3
EPS = 1e-6
NEG = -1e30
LANE = 128
VMEM_LIMIT = 56 * 1024 * 1024


def _cparams(sem):
    return pltpu.CompilerParams(dimension_semantics=sem, vmem_limit_bytes=VMEM_LIMIT)


def _sigmoid(x):
    return 1.0 / (1.0 + jnp.exp(-x))


def _gelu(x):
    return 0.5 * x * (1.0 + jnp.tanh(0.7978845608028654 * (x + 0.044715 * (x * x * x))))


def _nt(a, b):
    return lax.dot_general(a, b, (((1,), (1,)), ((), ())), preferred_element_type=F32)


def _tn(a, b):
    return lax.dot_general(a, b, (((0,), (0,)), ((), ())), preferred_element_type=F32)


def _nn(a, b):
    return jnp.dot(a, b, preferred_element_type=F32)


def _rope_tab(pos, scale):
    half = ROPE_DIM // 2
    inv = ROPE_THETA ** (-np.arange(half, dtype=np.float64) * (2.0 / ROPE_DIM))
    ang = pos.astype(np.float64)[:, None] * inv
    cos, sin = np.cos(ang), np.sin(ang)
    n = pos.shape[0]
    c = np.ones((n, DH)); s_lo = np.zeros((n, DH)); s_hi = np.zeros((n, DH))
    c[:, :half] = cos; c[:, half:ROPE_DIM] = cos
    s_lo[:, half:ROPE_DIM] = sin
    s_hi[:, :half] = -sin
    tab = np.concatenate([np.tile(t, (1, 2)) for t in (c, s_lo, s_hi)], axis=1) * scale
    return jnp.asarray(tab, F32)


def _ret_tab(pos):
    half = DK_R // 2
    inv = RET_THETA ** (-np.arange(half, dtype=np.float64) * (2.0 / DK_R))
    ang = pos.astype(np.float64)[:, None] * inv
    cos, sin = np.cos(ang), np.sin(ang)
    c = np.concatenate([cos, cos], axis=1)
    s = np.concatenate([-sin, sin], axis=1)
    ks = DK_R ** -0.5
    return jnp.asarray(np.concatenate([c, s, c * ks, s * ks], axis=1), F32)


def _ret_decay(C, c_true):
    h = np.arange(H_R, dtype=np.float64)
    log_g = np.log1p(-np.exp2(-5.0 - h))
    i = np.arange(C, dtype=np.float64)
    diff = i[:, None] - i[None, :]
    dm = np.where(diff >= 0, np.exp(log_g[:, None, None] * np.maximum(diff, 0.0)), 0.0)
    dq = np.exp(log_g[:, None] * (i + 1.0))[:, :, None] * np.ones((1, 1, LANE))
    wk = np.exp(log_g[:, None] * (c_true - 1.0 - i))[:, :, None] * np.ones((1, 1, LANE))
    wk = np.where(i[None, :, None] < c_true, wk, 0.0)
    gc = np.exp(log_g * c_true)[:, None, None] * np.ones((1, 8, LANE))
    return (jnp.asarray(dm, F32), jnp.asarray(dq, F32), jnp.asarray(wk, F32), jnp.asarray(gc, F32))


def _mods_body(c_ref, w_ref, b_ref, o_ref):
    c = c_ref[...]
    s = c * _sigmoid(c)
    o_ref[...] = _nn(s.astype(BF), w_ref[...].astype(BF)) + b_ref[...]


def _mods(c_all, w_ada, b_ada):
    n = c_all.shape[0]
    nout = w_ada.shape[1]
    tn = 1024
    return pl.pallas_call(
        _mods_body,
        grid=(nout // tn,),
        in_specs=[pl.BlockSpec((n, D_MODEL), lambda j: (0, 0)),
                  pl.BlockSpec((D_MODEL, tn), lambda j: (0, j)),
                  pl.BlockSpec((1, tn), lambda j: (0, j))],
        out_specs=pl.BlockSpec((n, tn), lambda j: (0, j)),
        out_shape=jax.ShapeDtypeStruct((n, nout), F32),
        compiler_params=_cparams(("arbitrary",)),
        name="mods",
    )(c_all, w_ada, b_ada.reshape(1, nout))


_C_Q, _C_KV, _C_QR, _C_KR, _C_VR, _C_GR, _C_GM, _C_GA, _C_END = (
    0, 512, 1280, 1792, 2304, 2816, 3328, 5376, 5632)


def _pack_w_in(w_in):
    o = np.cumsum((0, 512, 768, 24, 512, 512, 512, 512, 2048))
    q, kv, ga, qr, kr, vr, gr, gm = [w_in[:, o[i]:o[i + 1]] for i in range(8)]
    ga = ga.reshape(D_MODEL, 3, G_A, HPG).transpose(0, 2, 1, 3).reshape(D_MODEL, G_A, 3 * HPG)
    ga = jnp.pad(ga, ((0, 0), (0, 0), (0, LANE - 3 * HPG))).reshape(D_MODEL, G_A * LANE)
    return jnp.concatenate([q, kv, qr, kr, vr, gr, gm, ga], axis=1).astype(BF)


def _inproj_body(x_ref, nw_ref, sh_ref, sc_ref, w_ref, rq_ref, rk_ref, rr_ref,
                 q_ref, qr_ref, kvc_ref, kvs_ref, kvw_ref, kvb_ref, ret_ref, gr_ref, gm_ref, ga_ref):
    x = x_ref[...]
    tm = x.shape[0]
    ms = jnp.mean(x * x, axis=-1, keepdims=True)
    h = (x * lax.rsqrt(ms + EPS)) * nw_ref[...]
    h = h * (1.0 + sc_ref[...]) + sh_ref[...]
    hb = h.astype(BF)
    lo64 = lax.broadcasted_iota(jnp.int32, (tm, LANE), 1) < DH

    def mm(lo, hi):
        return _nn(hb, w_ref[:, lo:hi])

    def rope(xc, tab_ref):
        return (xc * tab_ref[:, 0:LANE] + pltpu.roll(xc, 8, 1) * tab_ref[:, LANE:2 * LANE]
                + pltpu.roll(xc, LANE - 8, 1) * tab_ref[:, 2 * LANE:3 * LANE])

    qa = mm(_C_Q, _C_KV)
    q_ref[...] = (qa * (DH ** -0.5)).astype(BF)
    for c in range(4):
        qr_ref[:, c * LANE:(c + 1) * LANE] = rope(qa[:, c * LANE:(c + 1) * LANE], rq_ref).astype(BF)

    kv = mm(_C_KV, _C_QR)
    kvc_ref[...] = kv[:, 0:256]
    for kind, out_ref in ((0, kvs_ref), (1, kvw_ref)):
        base = 256 + kind * 256
        k = rope(kv[:, base:base + LANE], rk_ref)
        v = kv[:, base + LANE:base + 2 * LANE]
        out_ref[:, 0:LANE] = k
        out_ref[:, LANE:2 * LANE] = v
        kr_ = pltpu.roll(k, DH, 1)
        vr_ = pltpu.roll(v, DH, 1)
        kvb_ref[kind, 0, 0] = jnp.where(lo64, k, vr_).astype(BF)
        kvb_ref[kind, 0, 1] = jnp.where(lo64, v, kr_).astype(BF)
        kvb_ref[kind, 1, 0] = jnp.where(lo64, kr_, v).astype(BF)
        kvb_ref[kind, 1, 1] = jnp.where(lo64, vr_, k).astype(BF)

    qr = mm(_C_QR, _C_KR)
    kr = mm(_C_KR, _C_VR)
    for hh in range(H_R):
        sl = slice(hh * LANE, (hh + 1) * LANE)
        xq = qr[:, sl]
        ret_ref[0, :, sl] = (xq * rr_ref[:, 0:LANE] + pltpu.roll(xq, DK_R // 2, 1) * rr_ref[:, LANE:2 * LANE]).astype(BF)
        xk = kr[:, sl]
        ret_ref[1, :, sl] = (xk * rr_ref[:, 2 * LANE:3 * LANE]
                             + pltpu.roll(xk, DK_R // 2, 1) * rr_ref[:, 3 * LANE:4 * LANE]).astype(BF)
    ret_ref[2] = mm(_C_VR, _C_GR).astype(BF)
    g = mm(_C_GR, _C_GM)
    gr_ref[...] = (g * _sigmoid(g)).astype(BF)
    gm_ref[...] = _sigmoid(mm(_C_GM, _C_GA)).astype(BF)
    ga_ref[...] = _sigmoid(mm(_C_GA, _C_END))


def _inproj(x2d, nw, sh, sc, w_pack, rq, rk, rr, tm, tab_blocks, per_row_mods):
    rows = x2d.shape[0]
    nb = rows // tm
    if per_row_mods:
        mod_spec = pl.BlockSpec((tm, D_MODEL), lambda i: (i, 0))
    else:
        mod_spec = pl.BlockSpec((None, 1, D_MODEL), lambda i: (i // tab_blocks, 0, 0))
    tab = lambda w: pl.BlockSpec((tm, w), lambda i: (i % tab_blocks, 0))
    row = lambda w: pl.BlockSpec((tm, w), lambda i: (i, 0))
    out_shapes = [
        jax.ShapeDtypeStruct((rows, 512), BF),
        jax.ShapeDtypeStruct((rows, 512), BF),
        jax.ShapeDtypeStruct((rows, 256), F32),
        jax.ShapeDtypeStruct((rows, 256), F32),
        jax.ShapeDtypeStruct((rows, 256), F32),
        jax.ShapeDtypeStruct((2, G_A, 2, rows, LANE), BF),
        jax.ShapeDtypeStruct((3, rows, 512), BF),
        jax.ShapeDtypeStruct((rows, 512), BF),
        jax.ShapeDtypeStruct((rows, 2048), BF),
        jax.ShapeDtypeStruct((rows, 256), F32),
    ]
    out_specs = [row(512), row(512), row(256), row(256), row(256),
                 pl.BlockSpec((2, G_A, 2, tm, LANE), lambda i: (0, 0, 0, i, 0)),
                 pl.BlockSpec((3, tm, 512), lambda i: (0, i, 0)),
                 row(512), row(2048), row(256)]
    return pl.pallas_call(
        _inproj_body,
        grid=(nb,),
        in_specs=[row(D_MODEL),
                  pl.BlockSpec((1, D_MODEL), lambda i: (0, 0)),
                  mod_spec, mod_spec,
                  pl.BlockSpec((D_MODEL, _C_END), lambda i: (0, 0), pipeline_mode=pl.Buffered(1)),
                  tab(384), tab(384), tab(512)],
        out_specs=out_specs,
        out_shape=out_shapes,
        compiler_params=_cparams(("arbitrary",)),
        name="inproj",
    )(x2d, nw, sh, sc, w_pack, rq, rk, rr)


def _pack_phi(phi_pos_k, phi_k1, phi_k2, phi_pos_v, phi_v1, phi_v2):
    w1 = jnp.zeros((L_CMP, 2, G_A, DH, 2, G_A, DH), F32)
    for kv, w in ((0, phi_k1), (1, phi_v1)):
        wl = w.reshape(L_CMP, DH, DH)
        for g in range(G_A):
            w1 = w1.at[:, kv, g, :, kv, g, :].set(wl)
    w1 = w1.reshape(L_CMP * 2 * G_A * DH, 2 * G_A * DH).astype(BF)
    w2 = jnp.zeros((2, G_A, DH, 2, G_A, DH), F32)
    for kv, w in ((0, phi_k2), (1, phi_v2)):
        for g in range(G_A):
            w2 = w2.at[kv, g, :, kv, g, :].set(w)
    w2 = w2.reshape(2 * G_A * DH, 2 * G_A * DH).astype(BF)
    pos = jnp.stack([phi_pos_k, phi_pos_v], axis=1)
    pos = jnp.broadcast_to(pos[:, :, None, :], (L_CMP, 2, G_A, DH)).reshape(1, L_CMP * 2 * G_A * DH)
    return pos, w1, w2


def _compress_body(x_ref, pos_ref, w1_ref, w2_ref, o_ref):
    xb = (x_ref[...] + pos_ref[...]).astype(BF)
    hmid = _gelu(_nn(xb, w1_ref[...]))
    o_ref[...] = _nn(hmid.astype(BF), w2_ref[...])


def _compress(blocks, pos, w1, w2):
    nblk, feat = blocks.shape
    tb = min(nblk, 128)
    return pl.pallas_call(
        _compress_body,
        grid=(nblk // tb,),
        in_specs=[pl.BlockSpec((tb, feat), lambda i: (i, 0)),
                  pl.BlockSpec((1, feat), lambda i: (0, 0)),
                  pl.BlockSpec((feat, 256), lambda i: (0, 0)),
                  pl.BlockSpec((256, 256), lambda i: (0, 0))],
        out_specs=pl.BlockSpec((tb, 256), lambda i: (i, 0)),
        out_shape=jax.ShapeDtypeStruct((nblk, 256), F32),
        compiler_params=_cparams(("arbitrary",)),
        name="compress",
    )(blocks, pos, w1, w2)


_KT = 512


def _nsa_prompt_body(q_ref, qr_ref, skv_ref, svk_ref, wkv_ref, wvk_ref, kc_ref, vc_ref, ga_ref, e_ref, o_ref,
                     *, nc, ns):
    i = pl.program_id(2)
    QB = Q_BLOCK
    lane = lax.broadcasted_iota(jnp.int32, (QB, LANE), 1)
    lo64 = lane < DH
    zero_b = jnp.zeros((QB, LANE), BF)

    def split_heads(ref):
        ev, od = [], []
        for c in range(2):
            xc = ref[:, c * LANE:(c + 1) * LANE]
            ev.append(jnp.where(lo64, xc, zero_b))
            od.append(jnp.where(lo64, zero_b, xc))
        return jnp.concatenate(ev, axis=0), jnp.concatenate(od, axis=0)

    q_e, q_o = split_heads(q_ref)
    kc = kc_ref[...]
    vc = vc_ref[...]
    tq_l = i * QB + lax.broadcasted_iota(jnp.int32, (nc, QB), 1)
    r_c = lax.broadcasted_iota(jnp.int32, (nc, QB), 0)
    half = nc // 2
    blk_c = jnp.where(r_c < half, 2 * r_c, 2 * (r_c - half) + 1)
    cmask = (blk_c * L_CMP + (L_CMP - 1)) <= tq_l
    cmask_f = cmask.astype(F32)
    imp = jnp.zeros((ns, QB), F32)
    o_cmp = {}
    for stack, qs in ((0, q_e), (1, q_o)):
        st = _nt(kc, qs)
        for c in range(2):
            s = jnp.where(cmask, st[:, c * QB:(c + 1) * QB], NEG)
            p = jnp.exp(s - jnp.max(s, axis=0, keepdims=True)) * cmask_f
            p = p / jnp.maximum(jnp.sum(p, axis=0, keepdims=True), 1e-30)
            imp = imp + p[0:half] + p[half:nc]
            o_cmp[(c, stack)] = _tn(p.astype(BF), vc)
    tq_s = i * QB + lax.broadcasted_iota(jnp.int32, (ns, QB), 1)
    blk_s = lax.broadcasted_iota(jnp.int32, (ns, QB), 0)
    valid = (blk_s * L_SEL) <= tq_s
    forced = (blk_s == 0) | (blk_s == tq_s // L_SEL)
    score = jnp.where(valid, imp + jnp.where(forced, FORCE_BONUS, 0.0), NEG)
    sel = jnp.zeros((ns, QB), F32)
    for _ in range(min(N_SEL, ns)):
        m = jnp.max(score, axis=0, keepdims=True)
        idx = jnp.min(jnp.where(score == m, blk_s, ns), axis=0, keepdims=True)
        hit = blk_s == idx
        sel = jnp.where(hit & (m > 0.5 * NEG), 1.0, sel)
        score = jnp.where(hit, -jnp.inf, score)
    sel_b = sel.astype(BF)

    qr_e, qr_o = split_heads(qr_ref)
    tq_r = i * QB + lax.broadcasted_iota(jnp.int32, (QB, _KT), 0)
    kcol = lax.broadcasted_iota(jnp.int32, (QB, _KT), 1)

    def tile_step(j, carry):
        m_e, l_e, a_e, m_o, l_o, a_o = carry
        off = pl.multiple_of(j * _KT, _KT)
        kv = skv_ref[pl.ds(off, _KT), :]
        vk = svk_ref[pl.ds(off, _KT), :]
        picked = _tn(sel_b, e_ref[:, pl.ds(off, _KT)])
        bias = jnp.where((picked > 0.5) & ((kcol + off) <= tq_r), 0.0, NEG)
        bias2 = jnp.concatenate([bias, bias], axis=0)

        def upd(qs, kmat, vmat, m, l, a):
            s = _nt(qs, kmat) + bias2
            m_new = jnp.maximum(m, jnp.max(s, axis=1, keepdims=True))
            alpha = jnp.exp(m - m_new)
            p = jnp.exp(s - m_new)
            l_new = alpha * l + jnp.sum(p, axis=1, keepdims=True)
            a_new = alpha * a + _nn(p.astype(BF), vmat)
            return m_new, l_new, a_new

        m_e, l_e, a_e = upd(qr_e, kv, vk, m_e, l_e, a_e)
        m_o, l_o, a_o = upd(qr_o, vk, kv, m_o, l_o, a_o)
        return m_e, l_e, a_e, m_o, l_o, a_o

    init = (jnp.full((2 * QB, 1), -jnp.inf, F32), jnp.zeros((2 * QB, 1), F32), jnp.zeros((2 * QB, LANE), F32)) * 2
    n_tiles = (i * QB + QB + _KT - 1) // _KT
    m_e, l_e, a_e, m_o, l_o, a_o = lax.fori_loop(0, n_tiles, tile_step, init)
    os_e = a_e / l_e
    os_o = a_o / l_o

    WK = WINDOW + QB
    start = pl.multiple_of(jnp.maximum(i * QB - WINDOW, 0), QB)
    wkv = wkv_ref[pl.ds(start, WK), :]
    wvk = wvk_ref[pl.ds(start, WK), :]
    diff = (i * QB + lax.broadcasted_iota(jnp.int32, (QB, WK), 0)) - (start + lax.broadcasted_iota(jnp.int32, (QB, WK), 1))
    wb = jnp.where((diff >= 0) & (diff < WINDOW), 0.0, NEG)
    wb2 = jnp.concatenate([wb, wb], axis=0)

    def win(qs, kmat, vmat):
        s = _nt(qs, kmat) + wb2
        p = jnp.exp(s - jnp.max(s, axis=1, keepdims=True))
        return _nn(p.astype(BF), vmat) / jnp.sum(p, axis=1, keepdims=True)

    ow_e = win(qr_e, wkv, wvk)
    ow_o = win(qr_o, wvk, wkv)

    ga = ga_ref[...]
    for c in range(2):
        rows = slice(c * QB, (c + 1) * QB)
        branches = (
            jnp.where(lo64, o_cmp[(c, 0)], o_cmp[(c, 1)]),
            jnp.where(lo64, os_e[rows], os_o[rows]),
            jnp.where(lo64, ow_e[rows], ow_o[rows]),
        )
        acc = jnp.zeros((QB, LANE), F32)
        for br in range(3):
            col = br * HPG + 2 * c
            gate = jnp.where(lo64, ga[:, col:col + 1], ga[:, col + 1:col + 2])
            acc = acc + gate * branches[br]
        o_ref[:, c * LANE:(c + 1) * LANE] = acc.astype(BF)


def _sel_expand(ns, nkeys):
    e = (np.arange(nkeys)[None, :] // L_SEL) == np.arange(ns)[:, None]
    return jnp.asarray(e, BF)


def _nsa_prompt(q, qr, kvb, kc2, vc2, ga, B, T):
    nqb = T // Q_BLOCK
    nc, ns = T // L_CMP, T // L_SEL
    qspec = pl.BlockSpec((Q_BLOCK, 256), lambda b, g, i: (b * nqb + i, g))
    kvspec = lambda kind, lay: pl.BlockSpec((None, None, None, T, LANE),
                                            lambda b, g, i: (kind, g, lay, b, 0))
    cspec = pl.BlockSpec((None, None, nc, LANE), lambda b, g, i: (b, g, 0, 0))
    return pl.pallas_call(
        functools.partial(_nsa_prompt_body, nc=nc, ns=ns),
        grid=(B, G_A, nqb),
        in_specs=[qspec, qspec, kvspec(0, 0), kvspec(0, 1), kvspec(1, 0), kvspec(1, 1), cspec, cspec,
                  pl.BlockSpec((Q_BLOCK, LANE), lambda b, g, i: (b * nqb + i, g)),
                  pl.BlockSpec((ns, T), lambda b, g, i: (0, 0))],
        out_specs=qspec,
        out_shape=jax.ShapeDtypeStruct((B * T, 512), BF),
        compiler_params=_cparams(("arbitrary", "arbitrary", "arbitrary")),
        name="nsa_prompt",
    )(q, qr, kvb, kvb, kvb, kvb, kc2, vc2, ga, _sel_expand(ns, T))


def _ret_body(q_ref, k_ref, v_ref, gr_ref, s0_ref, dm_ref, dq_ref, wk_ref, gc_ref, gnw_ref, z_ref, sout_ref, s_scr):
    c = pl.program_id(2)

    @pl.when(c == 0)
    def _():
        s_scr[...] = s0_ref[...]

    q = q_ref[...]
    k = k_ref[...]
    v = v_ref[...]
    s_old = s_scr[...]
    inner = _nt(q, k) * dm_ref[...]
    o = _nn(inner.astype(BF), v) + _nn(q, s_old.astype(BF)) * dq_ref[...]
    kw = (k.astype(F32) * wk_ref[...]).astype(BF)
    s_new = gc_ref[0:1, :] * s_old + _tn(kw, v)
    s_scr[...] = s_new
    sout_ref[...] = s_new
    mu = jnp.mean(o, axis=-1, keepdims=True)
    d = o - mu
    var = jnp.mean(d * d, axis=-1, keepdims=True)
    on = d * lax.rsqrt(var + EPS) * gnw_ref[...]
    z_ref[...] = (gr_ref[...].astype(F32) * on).astype(BF)


def _retention(ret3, gr, s0, gnw, nseq, rows_per_seq, C, c_true):
    nC = rows_per_seq // C
    rows = nseq * rows_per_seq
    dm, dq, wk, gc = _ret_decay(C, c_true)
    qkv = lambda j: pl.BlockSpec((None, C, LANE), lambda b, h, c: (j, b * nC + c, h))
    hd = lambda r: pl.BlockSpec((None, r, LANE), lambda b, h, c: (h, 0, 0))
    return pl.pallas_call(
        _ret_body,
        grid=(nseq, H_R, nC),
        in_specs=[qkv(0), qkv(1), qkv(2),
                  pl.BlockSpec((C, LANE), lambda b, h, c: (b * nC + c, h)),
                  pl.BlockSpec((None, None, DK_R, DV_R), lambda b, h, c: (b, h, 0, 0)),
                  pl.BlockSpec((None, C, C), lambda b, h, c: (h, 0, 0)),
                  hd(C), hd(C), hd(8),
                  pl.BlockSpec((1, LANE), lambda b, h, c: (0, h))],
        out_specs=[pl.BlockSpec((C, LANE), lambda b, h, c: (b * nC + c, h)),
                   pl.BlockSpec((None, None, DK_R, DV_R), lambda b, h, c: (b, h, 0, 0))],
        out_shape=[jax.ShapeDtypeStruct((rows, 512), BF),
                   jax.ShapeDtypeStruct((nseq, H_R, DK_R, DV_R), F32)],
        scratch_shapes=[pltpu.VMEM((DK_R, DV_R), F32)],
        compiler_params=_cparams(("arbitrary", "arbitrary", "arbitrary")),
        name="retention",
    )(ret3, ret3, ret3, gr, s0, dm, dq, wk, gc, gnw)


def _merge_body(x_ref, oa_ref, zr_ref, gm_ref, gt_ref, wa_ref, wr_ref, wo_ref, o_ref):
    ya = _nn(oa_ref[...], wa_ref[...])
    yr = _nn(zr_ref[...], wr_ref[...])
    gm = gm_ref[...].astype(F32)
    merged = gm[:, 0:D_MODEL] * ya + gm[:, D_MODEL:2 * D_MODEL] * yr
    mix = _nn(merged.astype(BF), wo_ref[...])
    o_ref[...] = x_ref[...] + gt_ref[...] * mix


def _merge(x2d, oa, zr, gm, gt, wa, wr, wo, tm, blocks_per_seq, per_row_mods):
    rows = x2d.shape[0]
    if per_row_mods:
        mod_spec = pl.BlockSpec((tm, D_MODEL), lambda i: (i, 0))
    else:
        mod_spec = pl.BlockSpec((None, 1, D_MODEL), lambda i: (i // blocks_per_seq, 0, 0))
    row = lambda w: pl.BlockSpec((tm, w), lambda i: (i, 0))
    full = lambda a, b: pl.BlockSpec((a, b), lambda i: (0, 0))
    return pl.pallas_call(
        _merge_body,
        grid=(rows // tm,),
        in_specs=[row(D_MODEL), row(512), row(512), row(2048), mod_spec,
                  full(512, D_MODEL), full(512, D_MODEL), full(D_MODEL, D_MODEL)],
        out_specs=row(D_MODEL),
        out_shape=jax.ShapeDtypeStruct((rows, D_MODEL), F32),
        compiler_params=_cparams(("arbitrary",)),
        name="merge",
    )(x2d, oa, zr, gm, gt, wa, wr, wo)


def _ffn_body(x_ref, nw_ref, sh_ref, sc_ref, gt_ref, wu_ref, cw_ref, cb_ref, wd_ref, nf_ref, p1_ref, p2_ref,
              y_ref, a_ref, carry, *, blocks_per_seq, seq_rows):
    i = pl.program_id(0)
    x = x_ref[...]
    tm = x.shape[0]
    ms = jnp.mean(x * x, axis=-1, keepdims=True)
    h = (x * lax.rsqrt(ms + EPS)) * nw_ref[...]
    h = (h * (1.0 + sc_ref[...]) + sh_ref[...]).astype(BF)
    a = _nn(h, wu_ref[:, 0:D_FF])
    b = _nn(h, wu_ref[:, D_FF:2 * D_FF])
    a_ref[...] = a
    rid = lax.broadcasted_iota(jnp.int32, (tm, D_FF), 0)
    if seq_rows is None:
        first = (i % blocks_per_seq) == 0
        prev = jnp.where(first, p1_ref[...], carry[...])
        carry[...] = a[tm - 8:tm, :]
        am1 = jnp.where(rid == 0, prev[7:8, :], pltpu.roll(a, 1, 0))
        am2 = jnp.where(rid == 0, prev[6:7, :], jnp.where(rid == 1, prev[7:8, :], pltpu.roll(a, 2, 0)))
    else:
        s = rid % seq_rows
        am1 = jnp.where(s == 0, p1_ref[...], pltpu.roll(a, 1, 0))
        am2 = jnp.where(s <= 1, p2_ref[...], pltpu.roll(a, 2, 0))
    u = cb_ref[...] + am2 * cw_ref[0:1, :] + am1 * cw_ref[1:2, :] + a * cw_ref[2:3, :]
    ff = _nn((_gelu(u) * b).astype(BF), wd_ref[...])
    x2 = x + gt_ref[...] * ff
    ms2 = jnp.mean(x2 * x2, axis=-1, keepdims=True)
    y_ref[...] = (x2 * lax.rsqrt(ms2 + EPS)) * nf_ref[...]


def _ffn(x2d, nw, sh, sc, gt, wu, cw, cb, wd, nf, p1, p2, tm, blocks_per_seq, per_row_mods, seq_rows):
    rows = x2d.shape[0]
    if per_row_mods:
        mod_spec = pl.BlockSpec((tm, D_MODEL), lambda i: (i, 0))
    else:
        mod_spec = pl.BlockSpec((None, 1, D_MODEL), lambda i: (i // blocks_per_seq, 0, 0))
    row = lambda w: pl.BlockSpec((tm, w), lambda i: (i, 0))
    full = lambda a, b: pl.BlockSpec((a, b), lambda i: (0, 0), pipeline_mode=pl.Buffered(1))
    vec = lambda w: pl.BlockSpec((1, w), lambda i: (0, 0))
    pspec = pl.BlockSpec(p1.shape, lambda i: (0, 0))
    return pl.pallas_call(
        functools.partial(_ffn_body, blocks_per_seq=blocks_per_seq, seq_rows=seq_rows),
        grid=(rows // tm,),
        in_specs=[row(D_MODEL), vec(D_MODEL), mod_spec, mod_spec, mod_spec,
                  full(D_MODEL, 2 * D_FF), pl.BlockSpec((CONV_W, D_FF), lambda i: (0, 0)), vec(D_FF),
                  full(D_FF, D_MODEL), vec(D_MODEL), pspec, pspec],
        out_specs=[row(D_MODEL), row(D_FF)],
        out_shape=[jax.ShapeDtypeStruct((rows, D_MODEL), F32), jax.ShapeDtypeStruct((rows, D_FF), F32)],
        scratch_shapes=[pltpu.VMEM((8, D_FF), F32)],
        compiler_params=_cparams(("arbitrary",)),
        name="ffn",
    )(x2d, nw, sh, sc, gt, wu, cw, cb, wd, nf, p1, p2)


def _prompt_path(x_prompt, mods_p, W):
    B, T, _ = x_prompt.shape
    rows = B * T
    x2d = x_prompt.reshape(rows, D_MODEL)
    pos = np.arange(T)
    tm = 256
    sh1, sc1, gt1, sh2, sc2, gt2 = [mods_p[:, None, j * D_MODEL:(j + 1) * D_MODEL] for j in range(6)]
    (q, qr, kvc, kvs, kvw, kvb, ret3, gr, gm, ga) = _inproj(
        x2d, W["norm1"], sh1, sc1, W["w_in"], _rope_tab(pos, DH ** -0.5), _rope_tab(pos, 1.0), _ret_tab(pos),
        tm, T // tm, False)
    nc = T // L_CMP
    comp = _compress(kvc.reshape(B * nc, L_CMP * 256), W["phi_pos"], W["phi_w1"], W["phi_w2"])
    comp = comp.reshape(B, nc // 2, 2, 2, G_A, DH).transpose(3, 0, 4, 2, 1, 5)
    comp = comp.reshape(2, B, G_A, nc, DH)
    comp2 = jnp.concatenate([comp, comp], axis=-1).astype(BF)
    oa = _nsa_prompt(q, qr, kvb, comp2[0], comp2[1], ga, B, T)
    C = 256 if T % 256 == 0 else T
    zr, ret_new = _retention(ret3, gr, jnp.zeros((B, H_R, DK_R, DV_R), F32), W["gnw"], B, T, C, C)
    x1 = _merge(x2d, oa, zr, gm, gt1, W["w_up_a"], W["w_up_r"], W["w_out"], tm, T // tm, False)
    zeros8 = jnp.zeros((8, D_FF), F32)
    y, a_up = _ffn(x1, W["norm2"], sh2, sc2, gt2, W["w_ffn_up"], W["conv_w"], W["conv_b"], W["w_ffn_down"],
                   W["normf"], zeros8, zeros8, tm, T // tm, False, None)
    shp = (1, B, T, 2, G_A, DH)
    wsz = min(WINDOW, T)
    outs = dict(
        y=y.reshape(B, T, D_MODEL),
        cmp=kvc.reshape(shp), slc=kvs.reshape(shp),
        win=kvw.reshape(B, T, 2, G_A, DH)[None, :, T - wsz:],
        ret=ret_new[None],
        conv=a_up.reshape(B, T, D_FF)[None, :, T - (CONV_W - 1):],
    )
    return outs


def _cmp_sample_body(pt_ref, cache_ref, q_ref, pos_ref, w1_ref, w2_ref, ocmp_ref, topi_ref, xbuf, sem, kcs,
                     *, ppg, page_rows, n_pick, n_q):
    b, hf = pl.program_id(0), pl.program_id(1)
    nh = pl.num_programs(1)
    t = b * nh + hf
    n_steps = pl.num_programs(0) * nh
    slot = t % 2

    def copies(bb, hh, sl, p):
        page = pt_ref[bb, hh * ppg + p]
        return [pltpu.make_async_copy(cache_ref.at[page, :, kv, g, :],
                                      xbuf.at[sl, kv, g, pl.ds(p * page_rows, page_rows), :], sem.at[sl])
                for kv in range(2) for g in range(G_A)]

    def issue(bb, hh, sl):
        def f(p, c):
            for cp in copies(bb, hh, sl, p):
                cp.start()
            return c
        lax.fori_loop(0, ppg, f, 0)

    @pl.when(t == 0)
    def _():
        issue(b, hf, slot)

    @pl.when(t + 1 < n_steps)
    def _():
        issue((t + 1) // nh, (t + 1) % nh, 1 - slot)

    def wait(p, c):
        for cp in copies(b, hf, slot, p):
            cp.wait()
        return c
    lax.fori_loop(0, ppg, wait, 0)

    nblk = ppg * page_rows // L_CMP
    hb = nblk // 2
    for kv in range(2):
        for g in range(G_A):
            acc = jnp.zeros((nblk, DH), F32)
            for l in range(L_CMP):
                xe = xbuf[slot, kv, g, pl.ds(l, hb, stride=2 * L_CMP), :]
                xo = xbuf[slot, kv, g, pl.ds(l + L_CMP, hb, stride=2 * L_CMP), :]
                xl = jnp.concatenate([xe, xo], axis=0) + pos_ref[kv, l]
                acc = acc + _nn(xl.astype(BF), w1_ref[kv, l])
            kcs[kv, g, pl.ds(pl.multiple_of(hf * nblk, nblk), nblk), :] = _nn(_gelu(acc).astype(BF), w2_ref[kv])

    @pl.when(hf == nh - 1)
    def _():
        nhs = kcs.shape[2] // nblk
        for g in range(G_A):
            kc = kcs[0, g].astype(BF)
            vc = kcs[1, g].astype(BF)
            st = _nt(kc, q_ref[g])
            p = jnp.exp(st - jnp.max(st, axis=0, keepdims=True))
            p = p / jnp.sum(p, axis=0, keepdims=True)
            ocmp_ref[g] = _tn(p.astype(BF), vc)
            pair = jnp.concatenate([p[h * nblk:h * nblk + hb] + p[h * nblk + hb:(h + 1) * nblk]
                                    for h in range(nhs)], axis=0)
            imp = pair
            for r in range(1, HPG):
                imp = imp + pltpu.roll(pair, LANE - r * n_q, 1)
            nsel = pair.shape[0]
            blk = lax.broadcasted_iota(jnp.int32, (nsel, LANE), 0)
            score = imp + jnp.where(blk == 0, FORCE_BONUS, 0.0)
            topi_ref[g] = jnp.zeros((8, LANE), jnp.int32)
            for k in range(n_pick):
                m = jnp.max(score, axis=0, keepdims=True)
                idx = jnp.min(jnp.where(score == m, blk, nsel), axis=0, keepdims=True)
                topi_ref[g, k:k + 1, :] = idx
                score = jnp.where(blk == idx, -jnp.inf, score)


def _cmp_sample(page_table, cache, q_cmp, pos, w1, w2, n_q):
    DB, n_pages = page_table.shape
    page_rows = cache.shape[1]
    nh = 2
    ppg = n_pages // nh
    ncb = n_pages * page_rows // L_CMP
    kern = functools.partial(_cmp_sample_body, ppg=ppg, page_rows=page_rows, n_pick=N_SEL - 1, n_q=n_q)
    return pl.pallas_call(
        kern,
        grid_spec=pltpu.PrefetchScalarGridSpec(
            num_scalar_prefetch=1, grid=(DB, nh),
            in_specs=[pl.BlockSpec(memory_space=pl.ANY),
                      pl.BlockSpec((None, G_A, LANE, DH), lambda b, h, pt: (b, 0, 0, 0)),
                      pl.BlockSpec((2, L_CMP, 1, DH), lambda b, h, pt: (0, 0, 0, 0)),
                      pl.BlockSpec((2, L_CMP, DH, DH), lambda b, h, pt: (0, 0, 0, 0)),
                      pl.BlockSpec((2, DH, DH), lambda b, h, pt: (0, 0, 0))],
            out_specs=[pl.BlockSpec((None, G_A, LANE, DH), lambda b, h, pt: (b, 0, 0, 0)),
                       pl.BlockSpec((None, G_A, 8, LANE), lambda b, h, pt: (b, 0, 0, 0))],
            scratch_shapes=[pltpu.VMEM((2, 2, G_A, ppg * page_rows, DH), F32),
                            pltpu.SemaphoreType.DMA((2,)),
                            pltpu.VMEM((2, G_A, ncb, DH), F32)]),
        out_shape=[jax.ShapeDtypeStruct((DB, G_A, LANE, DH), F32),
                   jax.ShapeDtypeStruct((DB, G_A, 8, LANE), jnp.int32)],
        compiler_params=_cparams(("arbitrary", "arbitrary")),
        name="cmp_sample",
    )(page_table, cache, q_cmp, pos, w1, w2)


def _slc_win_sample_body(pt_ref, ti_ref, cslc_ref, swin_ref, q_ref, ns_ref, nw_ref, oslc_ref, owin_ref,
                         sbuf, wbuf, sem, *, n_q, n_pick, blocks_per_page):
    b = pl.program_id(0)
    nb = pl.num_programs(0)
    slot = b % 2
    wb = wbuf.shape[3]

    def copies(bb, sl):
        cps = []
        for g in range(G_A):
            for s in range(n_q):
                for k in range(n_pick):
                    blk = ti_ref[bb, (g * n_q + s) * n_pick + k]
                    page = pt_ref[bb, blk // blocks_per_page]
                    off = (blk % blocks_per_page) * L_SEL
                    for kv in range(2):
                        cps.append(pltpu.make_async_copy(
                            cslc_ref.at[page, pl.ds(off, L_SEL), kv, g, :],
                            sbuf.at[sl, g, kv, pl.ds((s * n_pick + k) * L_SEL, L_SEL), :], sem.at[sl]))
            for kv in range(2):
                cps.append(pltpu.make_async_copy(swin_ref.at[bb, :, kv, g, :], wbuf.at[sl, g, kv], sem.at[sl]))
        return cps

    @pl.when(b == 0)
    def _():
        for cp in copies(b, slot):
            cp.start()

    @pl.when(b + 1 < nb)
    def _():
        for cp in copies(b + 1, 1 - slot):
            cp.start()

    for cp in copies(b, slot):
        cp.wait()

    nq_rows = q_ref.shape[1]
    per_q = n_pick * L_SEL
    nk = n_q * per_q
    row_s = lax.broadcasted_iota(jnp.int32, (nq_rows, nk), 0) % n_q
    own = row_s == (lax.broadcasted_iota(jnp.int32, (nq_rows, nk), 1) // per_q)
    nnew = ns_ref.shape[2]
    new_ok = lax.broadcasted_iota(jnp.int32, (nq_rows, nnew), 1) <= (lax.broadcasted_iota(jnp.int32, (nq_rows, nnew), 0) % n_q)
    dwin = wb + (lax.broadcasted_iota(jnp.int32, (nq_rows, wb), 0) % n_q) - lax.broadcasted_iota(jnp.int32, (nq_rows, wb), 1)
    win_ok = (dwin >= 0) & (dwin < WINDOW)

    def attend(q, k_old, v_old, ok_old, k_new, v_new):
        s_o = jnp.where(ok_old, _nt(q, k_old), NEG)
        s_n = jnp.where(new_ok, _nt(q, k_new), NEG)
        m = jnp.maximum(jnp.max(s_o, axis=1, keepdims=True), jnp.max(s_n, axis=1, keepdims=True))
        p_o = jnp.exp(s_o - m)
        p_n = jnp.exp(s_n - m)
        den = jnp.sum(p_o, axis=1, keepdims=True) + jnp.sum(p_n, axis=1, keepdims=True)
        return (_nn(p_o.astype(BF), v_old) + _nn(p_n.astype(BF), v_new)) / den

    for g in range(G_A):
        q = q_ref[g]
        oslc_ref[g] = attend(q, sbuf[slot, g, 0].astype(BF), sbuf[slot, g, 1].astype(BF), own,
                             ns_ref[g, 0].astype(BF), ns_ref[g, 1].astype(BF))
        owin_ref[g] = attend(q, wbuf[slot, g, 0].astype(BF), wbuf[slot, g, 1].astype(BF), win_ok,
                             nw_ref[g, 0].astype(BF), nw_ref[g, 1].astype(BF))


def _slc_win_sample(page_table, topi, cache_slc, state_win, q_rot, new_slc, new_win, n_q):
    DB = page_table.shape[0]
    n_pick = N_SEL - 1
    wb = state_win.shape[1]
    nq_rows = q_rot.shape[2]
    kern = functools.partial(_slc_win_sample_body, n_q=n_q, n_pick=n_pick,
                             blocks_per_page=cache_slc.shape[1] // L_SEL)
    bspec = lambda shp: pl.BlockSpec((None,) + shp, lambda b, pt, ti: (b,) + (0,) * len(shp))
    return pl.pallas_call(
        kern,
        grid_spec=pltpu.PrefetchScalarGridSpec(
            num_scalar_prefetch=2, grid=(DB,),
            in_specs=[pl.BlockSpec(memory_space=pl.ANY), pl.BlockSpec(memory_space=pl.ANY),
                      bspec((G_A, nq_rows, DH)), bspec((G_A, 2, 8, DH)), bspec((G_A, 2, 8, DH))],
            out_specs=[bspec((G_A, nq_rows, DH)), bspec((G_A, nq_rows, DH))],
            scratch_shapes=[pltpu.VMEM((2, G_A, 2, n_q * n_pick * L_SEL, DH), F32),
                            pltpu.VMEM((2, G_A, 2, wb, DH), F32),
                            pltpu.SemaphoreType.DMA((2,))]),
        out_shape=[jax.ShapeDtypeStruct((DB, G_A, nq_rows, DH), F32)] * 2,
        compiler_params=_cparams(("arbitrary",)),
        name="slc_win_sample",
    )(page_table, topi, cache_slc, state_win, q_rot, new_slc, new_win)


_BPP = 4
_QC = 32


def _pack_phi_paged(phi_pos_k, phi_k1, phi_k2, phi_pos_v, phi_v1, phi_v2):
    eye = jnp.eye(_BPP, dtype=F32)
    w1, w2, pos = [], [], []
    for p_, a, b_ in ((phi_pos_k, phi_k1, phi_k2), (phi_pos_v, phi_v1, phi_v2)):
        wl = a.reshape(L_CMP, DH, DH).transpose(1, 0, 2)
        big = eye[None, :, None, :, None] * wl[:, None, :, None, :]
        w1.append(big.reshape(DH, _BPP * L_CMP, _BPP * DH))
        w2.append((eye[:, None, :, None] * b_[None, :, None, :]).reshape(_BPP * DH, _BPP * DH))
        pos.append(jnp.tile(p_.T, (1, _BPP)))
    return jnp.stack(pos), jnp.stack(w1).astype(BF), jnp.stack(w2).astype(BF)


def _cmp_paged_body(pt_ref, cache_ref, q_ref, pos_ref, w1_ref, w2_ref, ocmp_ref, topi_ref, xbuf, sem,
                    *, n_pages, n_pick, n_q):
    b = pl.program_id(0)
    nb = pl.num_programs(0)
    slot = b % 2
    SL = 2 * G_A * DH

    def copy(bb, sl, p):
        return pltpu.make_async_copy(cache_ref.at[pt_ref[bb, p]], xbuf.at[sl, pl.ds(p * SL, SL), :], sem.at[sl])

    def issue(bb, sl):
        def f(p, c):
            copy(bb, sl, p).start()
            return c
        lax.fori_loop(0, n_pages, f, 0)

    @pl.when(b == 0)
    def _():
        issue(b, slot)

    @pl.when(b + 1 < nb)
    def _():
        issue(b + 1, 1 - slot)

    def wait(p, c):
        copy(b, slot, p).wait()
        return c
    lax.fori_loop(0, n_pages, wait, 0)

    comp = []
    for kv in range(2):
        acc = jnp.zeros((G_A * n_pages, _BPP * DH), F32)
        for d in range(DH):
            x = jnp.concatenate(
                [xbuf[slot, pl.ds((kv * G_A + g) * DH + d, n_pages, stride=SL), :] for g in range(G_A)], axis=0)
            x = x + pos_ref[kv, d:d + 1, :]
            acc = acc + _nn(x.astype(BF), w1_ref[kv, d])
        comp.append(_nn(_gelu(acc).astype(BF), w2_ref[kv]))

    def lane_groups(x, op):
        r = x
        for j in range(1, _BPP):
            r = op(r, pltpu.roll(x, j * _QC, 1))
        return r

    for g in range(G_A):
        kc = comp[0][g * n_pages:(g + 1) * n_pages].astype(BF)
        vc = comp[1][g * n_pages:(g + 1) * n_pages].astype(BF)
        st = _nn(kc, q_ref[g])
        m = lane_groups(jnp.max(st, axis=0, keepdims=True), jnp.maximum)
        p = jnp.exp(st - m)
        p = p / lane_groups(jnp.sum(p, axis=0, keepdims=True), jnp.add)
        r_full = _tn(p.astype(BF), vc)
        o = r_full[0:_QC, 0:DH]
        for j in range(1, _BPP):
            o = o + r_full[j * _QC:(j + 1) * _QC, j * DH:(j + 1) * DH]
        ocmp_ref[g] = o
        pair = p + pltpu.roll(p, LANE - _QC, 1)
        imp = pair
        for r in range(1, HPG):
            imp = imp + pltpu.roll(pair, LANE - r * n_q, 1)
        sc = jnp.concatenate([imp, pltpu.roll(imp, 2 * _QC, 1)], axis=0)
        row = lax.broadcasted_iota(jnp.int32, (2 * n_pages, LANE), 0)
        blk = jnp.where(row < n_pages, 2 * row, 2 * (row - n_pages) + 1)
        score = sc + jnp.where(blk == 0, FORCE_BONUS, 0.0)
        topi_ref[g] = jnp.zeros((8, LANE), jnp.int32)
        for k in range(n_pick):
            mx = jnp.max(score, axis=0, keepdims=True)
            idx = jnp.min(jnp.where(score == mx, blk, 2 * n_pages), axis=0, keepdims=True)
            topi_ref[g, k:k + 1, :] = idx
            score = jnp.where(blk == idx, -jnp.inf, score)


def _cmp_paged(page_table, cache_t, q_bd, pos, w1, w2, n_q):
    DB, n_pages = page_table.shape
    SL = cache_t.shape[1]
    page_rows = cache_t.shape[2]
    kern = functools.partial(_cmp_paged_body, n_pages=n_pages, n_pick=N_SEL - 1, n_q=n_q)
    one = dict(pipeline_mode=pl.Buffered(1))
    return pl.pallas_call(
        kern,
        grid_spec=pltpu.PrefetchScalarGridSpec(
            num_scalar_prefetch=1, grid=(DB,),
            in_specs=[pl.BlockSpec(memory_space=pl.ANY),
                      pl.BlockSpec((None, G_A, _BPP * DH, LANE), lambda b, pt: (b, 0, 0, 0)),
                      pl.BlockSpec((2, DH, page_rows), lambda b, pt: (0, 0, 0), **one),
                      pl.BlockSpec((2, DH, page_rows, _BPP * DH), lambda b, pt: (0, 0, 0, 0), **one),
                      pl.BlockSpec((2, _BPP * DH, _BPP * DH), lambda b, pt: (0, 0, 0), **one)],
            out_specs=[pl.BlockSpec((None, G_A, _QC, DH), lambda b, pt: (b, 0, 0, 0)),
                       pl.BlockSpec((None, G_A, 8, LANE), lambda b, pt: (b, 0, 0, 0))],
            scratch_shapes=[pltpu.VMEM((2, n_pages * SL, page_rows), F32),
                            pltpu.SemaphoreType.DMA((2,))]),
        out_shape=[jax.ShapeDtypeStruct((DB, G_A, _QC, DH), F32),
                   jax.ShapeDtypeStruct((DB, G_A, 8, LANE), jnp.int32)],
        compiler_params=_cparams(("arbitrary",)),
        name="cmp_paged",
    )(page_table, cache_t, q_bd, pos, w1, w2)


def _slc_win_paged_body(pt_ref, ti_ref, cslc_ref, win_ref, q_ref, tiv_ref, ns_ref, nw_ref, ex_ref,
                        oslc_ref, owin_ref, kbuf, sem, *, n_q, n_pick, page_rows):
    b = pl.program_id(0)
    nb = pl.num_programs(0)
    slot = b % 2
    n_slab = n_q * n_pick
    bpp = page_rows // L_SEL
    wb = win_ref.shape[1]

    def copies(bb, sl):
        cps = []
        for g in range(G_A):
            for j in range(n_slab):
                page = pt_ref[bb, ti_ref[bb, g * n_slab + j] // bpp]
                for kv in range(2):
                    cps.append(pltpu.make_async_copy(
                        cslc_ref.at[page, pl.ds((kv * G_A + g) * DH, DH), :],
                        kbuf.at[sl, g, kv, :, pl.ds(j * page_rows, page_rows)], sem.at[sl]))
        return cps

    @pl.when(b == 0)
    def _():
        for cp in copies(b, slot):
            cp.start()

    @pl.when(b + 1 < nb)
    def _():
        for cp in copies(b + 1, 1 - slot):
            cp.start()

    for cp in copies(b, slot):
        cp.wait()

    nq_rows = q_ref.shape[1]
    nk = n_slab * page_rows
    row_q = lax.broadcasted_iota(jnp.int32, (nq_rows, nk), 0) % n_q
    col = lax.broadcasted_iota(jnp.int32, (nq_rows, nk), 1)
    own = row_q == col // (n_pick * page_rows)
    half = ((col % page_rows) // L_SEL).astype(F32)
    nnew = ns_ref.shape[2]
    new_ok = (lax.broadcasted_iota(jnp.int32, (nq_rows, nnew), 1)
              <= lax.broadcasted_iota(jnp.int32, (nq_rows, nnew), 0) % n_q)
    dwin = (wb + lax.broadcasted_iota(jnp.int32, (nq_rows, wb), 0) % n_q
            - lax.broadcasted_iota(jnp.int32, (nq_rows, wb), 1))
    win_ok = (dwin >= 0) & (dwin < WINDOW)

    def attend(q, kt_old, vt_old, ok_old, k_new, v_new):
        s_o = jnp.where(ok_old, _nn(q, kt_old), NEG)
        s_n = jnp.where(new_ok, _nt(q, k_new), NEG)
        m = jnp.maximum(jnp.max(s_o, axis=1, keepdims=True), jnp.max(s_n, axis=1, keepdims=True))
        p_o = jnp.exp(s_o - m)
        p_n = jnp.exp(s_n - m)
        den = jnp.sum(p_o, axis=1, keepdims=True) + jnp.sum(p_n, axis=1, keepdims=True)
        return (_nt(p_o.astype(BF), vt_old) + _nn(p_n.astype(BF), v_new)) / den

    for g in range(G_A):
        q = q_ref[g]
        par = (tiv_ref[g] % bpp).astype(F32).astype(BF)
        want = _nn(par, ex_ref[...])[0:1, :]
        ok = own & (half == want)
        oslc_ref[g] = attend(q, kbuf[slot, g, 0].astype(BF), kbuf[slot, g, 1].astype(BF), ok,
                             ns_ref[g, 0].astype(BF), ns_ref[g, 1].astype(BF))
        kw = win_ref[pl.ds((0 * G_A + g) * DH, DH), :].astype(BF)
        vw = win_ref[pl.ds((1 * G_A + g) * DH, DH), :].astype(BF)
        owin_ref[g] = attend(q, kw, vw, win_ok, nw_ref[g, 0].astype(BF), nw_ref[g, 1].astype(BF))


def _slc_win_paged(page_table, topi_flat, topi_vec, cache_t, win_t, q_rot, new_slc, new_win, n_q):
    DB = page_table.shape[0]
    n_pick = N_SEL - 1
    page_rows = cache_t.shape[2]
    wb = win_t.shape[2]
    nq_rows = q_rot.shape[2]
    n_slab = n_q * n_pick
    ex = (np.arange(n_slab * page_rows)[None, :] // page_rows) == np.arange(LANE)[:, None]
    kern = functools.partial(_slc_win_paged_body, n_q=n_q, n_pick=n_pick, page_rows=page_rows)
    bspec = lambda shp: pl.BlockSpec((None,) + shp, lambda b, pt, ti: (b,) + (0,) * len(shp))
    return pl.pallas_call(
        kern,
        grid_spec=pltpu.PrefetchScalarGridSpec(
            num_scalar_prefetch=2, grid=(DB,),
            in_specs=[pl.BlockSpec(memory_space=pl.ANY), bspec((2 * G_A * DH, wb)),
                      bspec((G_A, nq_rows, DH)), bspec((G_A, 16, LANE)),
                      bspec((G_A, 2, 16, DH)), bspec((G_A, 2, 16, DH)),
                      pl.BlockSpec((LANE, n_slab * page_rows), lambda b, pt, ti: (0, 0))],
            out_specs=[bspec((G_A, nq_rows, DH)), bspec((G_A, nq_rows, DH))],
            scratch_shapes=[pltpu.VMEM((2, G_A, 2, DH, n_slab * page_rows), F32),
                            pltpu.SemaphoreType.DMA((2,))]),
        out_shape=[jax.ShapeDtypeStruct((DB, G_A, nq_rows, DH), F32)] * 2,
        compiler_params=_cparams(("arbitrary",)),
        name="slc_win_paged",
    )(page_table, topi_flat, cache_t, win_t, q_rot, topi_vec, new_slc, new_win, jnp.asarray(ex, BF))


def _gate_sample_body(oc_ref, os_ref, ow_ref, ga_ref, o_ref):
    rows = o_ref.shape[0]
    lo64 = lax.broadcasted_iota(jnp.int32, (rows, LANE), 1) < DH
    for c4 in range(H_A // 2):
        g, c = c4 // 2, c4 % 2
        sl = slice(c4 * LANE, (c4 + 1) * LANE)
        acc = jnp.zeros((rows, LANE), F32)
        for br, ref in enumerate((oc_ref, os_ref, ow_ref)):
            col = g * LANE + br * HPG + 2 * c
            gate = jnp.where(lo64, ga_ref[:, col:col + 1], ga_ref[:, col + 1:col + 2])
            acc = acc + gate * ref[:, sl]
        o_ref[:, sl] = acc.astype(BF)


def _gate_sample(oc, osl, ow, ga):
    rows = oc.shape[0]
    full = lambda w: pl.BlockSpec((rows, w), lambda i: (0, 0))
    return pl.pallas_call(
        _gate_sample_body, grid=(1,),
        in_specs=[full(512), full(512), full(512), full(256)],
        out_specs=full(512),
        out_shape=jax.ShapeDtypeStruct((rows, 512), BF),
        compiler_params=_cparams(("arbitrary",)),
        name="gate_sample",
    )(oc, osl, ow, ga)


def _sample_path(x_sample, mods_s, W, cache_cmp, cache_slc, state_win, state_ret, state_conv, page_table):
    DB, S, _ = x_sample.shape
    rows = DB * S
    page_rows = cache_cmp.shape[1]
    P = page_table.shape[1] * page_rows
    wb = state_win.shape[1]
    assert P % L_SEL == 0 and S < L_CMP and S <= 8 and wb == WINDOW and page_rows % L_SEL == 0
    assert P // L_SEL >= N_SEL and CONV_W == 3 and S >= CONV_W - 1
    pos = P + np.arange(S)
    pos_rows = np.tile(pos, DB)
    x2d = x_sample.reshape(rows, D_MODEL)
    modr = jnp.repeat(mods_s, S, axis=0)
    sh1, sc1, gt1, sh2, sc2, gt2 = [modr[:, j * D_MODEL:(j + 1) * D_MODEL] for j in range(6)]
    (q, qr, kvc, kvs, kvw, _, ret3, gr, gm, ga) = _inproj(
        x2d, W["norm1"], sh1, sc1, W["w_in"], _rope_tab(pos_rows, DH ** -0.5), _rope_tab(pos_rows, 1.0),
        _ret_tab(pos_rows), rows, 1, True)

    def to_heads(t):
        return t.reshape(DB, S, G_A, HPG, DH).transpose(0, 2, 3, 1, 4).reshape(DB, G_A, HPG * S, DH)

    def from_heads(t):
        return t.reshape(DB, G_A, HPG, S, DH).transpose(0, 3, 1, 2, 4).reshape(rows, H_A * DH)

    assert page_rows == _BPP * L_CMP and HPG * S <= _QC and S * (N_SEL - 1) <= LANE
    slab = lambda t: t.transpose(0, 2, 3, 4, 1).reshape(t.shape[0], 2 * G_A * DH, t.shape[1])
    qt = jnp.pad(to_heads(q).transpose(0, 1, 3, 2), ((0, 0), (0, 0), (0, 0), (0, _QC - HPG * S)))
    eye = jnp.eye(_BPP, dtype=qt.dtype)
    q_bd = (eye[None, None, :, None, :, None] * qt[:, :, None, :, None, :]).reshape(DB, G_A, _BPP * DH, _BPP * _QC)
    o_cmp, topi = _cmp_paged(page_table, slab(cache_cmp), q_bd, W["phi_posT"], W["phi_w1p"], W["phi_w2p"], S)
    n_pick = N_SEL - 1
    topi = topi[:, :, :n_pick, :S].transpose(0, 1, 3, 2).reshape(DB, G_A, S * n_pick)
    topi_vec = jnp.broadcast_to(jnp.pad(topi, ((0, 0), (0, 0), (0, LANE - S * n_pick)))[:, :, None, :],
                                (DB, G_A, 16, LANE))

    def new_rows(t):
        t = t.reshape(DB, S, 2, G_A, DH).transpose(0, 3, 2, 1, 4)
        return jnp.pad(t, ((0, 0), (0, 0), (0, 0), (0, 16 - S), (0, 0)))

    o_slc, o_win = _slc_win_paged(page_table, topi.reshape(DB, G_A * S * n_pick), topi_vec, slab(cache_slc),
                                  slab(state_win), to_heads(qr), new_rows(kvs), new_rows(kvw), S)
    oa = _gate_sample(from_heads(o_cmp[:, :, :HPG * S]), from_heads(o_slc), from_heads(o_win), ga)

    RP = 16
    padr = lambda t: jnp.pad(t.reshape(t.shape[:-2] + (DB, S, 512)),
                             ((0, 0),) * (t.ndim - 1) + ((0, RP - S), (0, 0))).reshape(t.shape[:-2] + (DB * RP, 512))
    zr, ret_new = _retention(padr(ret3), padr(gr), state_ret, W["gnw"], DB, RP, RP, S)
    zr = zr.reshape(DB, RP, 512)[:, :S].reshape(rows, 512)
    x1 = _merge(x2d, oa, zr, gm, gt1, W["w_up_a"], W["w_up_r"], W["w_out"], rows, 1, True)
    zs = jnp.zeros((DB, S, D_FF), F32)
    p1 = zs.at[:, 0].set(state_conv[:, 1]).reshape(rows, D_FF)
    p2 = zs.at[:, 0].set(state_conv[:, 0]).at[:, 1].set(state_conv[:, 1]).reshape(rows, D_FF)
    y, a_up = _ffn(x1, W["norm2"], sh2, sc2, gt2, W["w_ffn_up"], W["conv_w"], W["conv_b"], W["w_ffn_down"],
                   W["normf"], p1, p2, rows, 1, True, S)
    shp = (1, DB, S, 2, G_A, DH)
    return dict(
        y=y.reshape(DB, S, D_MODEL),
        cmp=kvc.reshape(shp), slc=kvs.reshape(shp),
        win=jnp.concatenate([state_win[:, S:], kvw.reshape(DB, S, 2, G_A, DH)], axis=1)[None],
        ret=ret_new[None],
        conv=a_up.reshape(DB, S, D_FF)[None, :, S - (CONV_W - 1):],
    )


def kernel(x_prompt, x_sample, cache_cmp_kv, cache_slc_kv, state_win_kv, state_ret, state_conv, page_table,
           c_prompt, c_sample, norm1_w, norm2_w, w_ada, b_ada, w_in, phi_pos_k, phi_k1, phi_k2, phi_pos_v,
           phi_v1, phi_v2, w_up_a, ret_gn_w, w_up_r, w_out, w_ffn_up, ffn_conv_w, ffn_conv_b, w_ffn_down, normf_w):
    B = x_prompt.shape[0]
    l = 0
    pos, w1, w2 = _pack_phi(phi_pos_k[l], phi_k1[l], phi_k2[l], phi_pos_v[l], phi_v1[l], phi_v2[l])
    W = dict(
        norm1=norm1_w[l].reshape(1, D_MODEL), norm2=norm2_w[l].reshape(1, D_MODEL), normf=normf_w.reshape(1, D_MODEL),
        w_in=_pack_w_in(w_in[l]), phi_pos=pos, phi_w1=w1, phi_w2=w2,
        w_up_a=w_up_a[l].astype(BF), w_up_r=w_up_r[l].astype(BF), w_out=w_out[l].astype(BF),
        gnw=ret_gn_w[l].reshape(1, H_R * DV_R),
        w_ffn_up=w_ffn_up[l].astype(BF), conv_w=ffn_conv_w[l], conv_b=ffn_conv_b[l].reshape(1, D_FF),
        w_ffn_down=w_ffn_down[l].astype(BF),
    )
    W["phi_posT"], W["phi_w1p"], W["phi_w2p"] = _pack_phi_paged(
        phi_pos_k[l], phi_k1[l], phi_k2[l], phi_pos_v[l], phi_v1[l], phi_v2[l])
    mods = _mods(jnp.concatenate([c_prompt, c_sample], axis=0), w_ada[l], b_ada[l])
    p = _prompt_path(x_prompt, mods[:B], W)
    s = _sample_path(x_sample, mods[B:], W, cache_cmp_kv[l], cache_slc_kv[l], state_win_kv[l], state_ret[l],
                     state_conv[l], page_table)
    return (p["y"], s["y"], p["cmp"], s["cmp"], p["slc"], s["slc"], p["win"], s["win"],
            p["ret"], s["ret"], p["conv"], s["conv"])
```

```python
import functools

import numpy as np
import jax
import jax.numpy as jnp
from jax import lax
from jax.experimental import pallas as pl
from jax.experimental.pallas import tpu as pltpu

BF = jnp.bfloat16
F32 = jnp.float32

D_MODEL = 1024
H_A, G_A, DH = 8, 2, 64
HPG = H_A // G_A
ROPE_DIM = DH // 4
ROPE_THETA = 500000.0
L_CMP, L_SEL, N_SEL = 32, 64, 8
WINDOW = 512
Q_BLOCK = 128
FORCE_BONUS = 1e4
H_R, DK_R, DV_R = 4, 128, 128
RET_THETA = 10000.0
D_FF = 2816
CONV_W = 3
EPS = 1e-6
NEG = -1e30
Q_SCALE = DH ** -0.5 * 1.4426950408889634
LANE = 128
VMEM_LIMIT = 56 * 1024 * 1024


def _cparams(sem):
    return pltpu.CompilerParams(dimension_semantics=sem, vmem_limit_bytes=VMEM_LIMIT)


def _sigmoid(x):
    return 1.0 / (1.0 + jnp.exp(-x))


def _gelu(x):
    return 0.5 * x * (1.0 + jnp.tanh(0.7978845608028654 * (x + 0.044715 * (x * x * x))))


def _nt(a, b):
    return lax.dot_general(a, b, (((1,), (1,)), ((), ())), preferred_element_type=F32)


def _tn(a, b):
    return lax.dot_general(a, b, (((0,), (0,)), ((), ())), preferred_element_type=F32)


def _nn(a, b):
    return jnp.dot(a, b, preferred_element_type=F32)


def _rope_tab(pos, scale):
    half = ROPE_DIM // 2
    inv = ROPE_THETA ** (-np.arange(half, dtype=np.float64) * (2.0 / ROPE_DIM))
    ang = pos.astype(np.float64)[:, None] * inv
    cos, sin = np.cos(ang), np.sin(ang)
    n = pos.shape[0]
    c = np.ones((n, DH)); s_lo = np.zeros((n, DH)); s_hi = np.zeros((n, DH))
    c[:, :half] = cos; c[:, half:ROPE_DIM] = cos
    s_lo[:, half:ROPE_DIM] = sin
    s_hi[:, :half] = -sin
    tab = np.concatenate([np.tile(t, (1, 2)) for t in (c, s_lo, s_hi)], axis=1) * scale
    return jnp.asarray(tab, F32)


def _ret_tab(pos):
    half = DK_R // 2
    inv = RET_THETA ** (-np.arange(half, dtype=np.float64) * (2.0 / DK_R))
    ang = pos.astype(np.float64)[:, None] * inv
    cos, sin = np.cos(ang), np.sin(ang)
    c = np.concatenate([cos, cos], axis=1)
    s = np.concatenate([-sin, sin], axis=1)
    ks = DK_R ** -0.5
    return jnp.asarray(np.concatenate([c, s, c * ks, s * ks], axis=1), F32)


def _ret_decay(C, c_true):
    h = np.arange(H_R, dtype=np.float64)
    log_g = np.log1p(-np.exp2(-5.0 - h))
    i = np.arange(C, dtype=np.float64)
    diff = i[:, None] - i[None, :]
    dm = np.where(diff >= 0, np.exp(log_g[:, None, None] * np.maximum(diff, 0.0)), 0.0)
    dq = np.exp(log_g[:, None] * (i + 1.0))[:, :, None] * np.ones((1, 1, LANE))
    wk = np.exp(log_g[:, None] * (c_true - 1.0 - i))[:, :, None] * np.ones((1, 1, LANE))
    wk = np.where(i[None, :, None] < c_true, wk, 0.0)
    gc = np.exp(log_g * c_true)[:, None, None] * np.ones((1, 8, LANE))
    return (jnp.asarray(dm, F32), jnp.asarray(dq, F32), jnp.asarray(wk, F32), jnp.asarray(gc, F32))


def _mods_body(c_ref, w_ref, b_ref, o_ref):
    c = c_ref[...]
    s = c * _sigmoid(c)
    o_ref[...] = _nn(s.astype(BF), w_ref[...].astype(BF)) + b_ref[...]


def _mods(c_all, w_ada, b_ada):
    n = c_all.shape[0]
    nout = w_ada.shape[1]
    tn = 1024
    return pl.pallas_call(
        _mods_body,
        grid=(nout // tn,),
        in_specs=[pl.BlockSpec((n, D_MODEL), lambda j: (0, 0)),
                  pl.BlockSpec((D_MODEL, tn), lambda j: (0, j)),
                  pl.BlockSpec((1, tn), lambda j: (0, j))],
        out_specs=pl.BlockSpec((n, tn), lambda j: (0, j)),
        out_shape=jax.ShapeDtypeStruct((n, nout), F32),
        compiler_params=_cparams(("arbitrary",)),
        name="mods",
    )(c_all, w_ada, b_ada.reshape(1, nout))


_C_Q, _C_KV, _C_QR, _C_KR, _C_VR, _C_GR, _C_GM, _C_GA, _C_END = (
    0, 512, 1280, 1792, 2304, 2816, 3328, 5376, 5632)


def _pack_w_in(w_in):
    o = np.cumsum((0, 512, 768, 24, 512, 512, 512, 512, 2048))
    q, kv, ga, qr, kr, vr, gr, gm = [w_in[:, o[i]:o[i + 1]] for i in range(8)]
    ga = ga.reshape(D_MODEL, 3, G_A, HPG).transpose(0, 2, 1, 3).reshape(D_MODEL, G_A, 3 * HPG)
    ga = jnp.pad(ga, ((0, 0), (0, 0), (0, LANE - 3 * HPG))).reshape(D_MODEL, G_A * LANE)
    return jnp.concatenate([q, kv, qr, kr, vr, gr, gm, ga], axis=1).astype(BF)


def _inproj_body(x_ref, nw_ref, sh_ref, sc_ref, w_ref, rq_ref, rk_ref, rr_ref,
                 q_ref, qr_ref, kvc_ref, kvs_ref, kvw_ref, kvb_ref, ret_ref, gr_ref, gm_ref, ga_ref, *, kv_t):
    def put_kv(out_ref, k, v):
        if kv_t:
            out_ref[0:LANE, :] = k.T
            out_ref[LANE:2 * LANE, :] = v.T
        else:
            out_ref[:, 0:LANE] = k
            out_ref[:, LANE:2 * LANE] = v

    x = x_ref[...]
    tm = x.shape[0]
    ms = jnp.mean(x * x, axis=-1, keepdims=True)
    h = (x * lax.rsqrt(ms + EPS)) * nw_ref[...]
    h = h * (1.0 + sc_ref[...]) + sh_ref[...]
    hb = h.astype(BF)
    lo64 = lax.broadcasted_iota(jnp.int32, (tm, LANE), 1) < DH

    def mm(lo, hi):
        return _nn(hb, w_ref[:, lo:hi])

    def rope(xc, tab_ref):
        return (xc * tab_ref[:, 0:LANE] + pltpu.roll(xc, 8, 1) * tab_ref[:, LANE:2 * LANE]
                + pltpu.roll(xc, LANE - 8, 1) * tab_ref[:, 2 * LANE:3 * LANE])

    qa = mm(_C_Q, _C_KV)
    q_ref[...] = (qa * Q_SCALE).astype(BF)
    for c in range(4):
        qr_ref[:, c * LANE:(c + 1) * LANE] = rope(qa[:, c * LANE:(c + 1) * LANE], rq_ref).astype(BF)

    kv = mm(_C_KV, _C_QR)
    put_kv(kvc_ref, kv[:, 0:LANE], kv[:, LANE:2 * LANE])
    for kind, out_ref in ((0, kvs_ref), (1, kvw_ref)):
        base = 256 + kind * 256
        k = rope(kv[:, base:base + LANE], rk_ref)
        v = kv[:, base + LANE:base + 2 * LANE]
        put_kv(out_ref, k, v)
        kr_ = pltpu.roll(k, DH, 1)
        vr_ = pltpu.roll(v, DH, 1)
        kvb_ref[kind, 0, 0] = jnp.where(lo64, k, vr_).astype(BF)
        kvb_ref[kind, 0, 1] = jnp.where(lo64, v, kr_).astype(BF)
        kvb_ref[kind, 1, 0] = jnp.where(lo64, kr_, v).astype(BF)
        kvb_ref[kind, 1, 1] = jnp.where(lo64, vr_, k).astype(BF)

    qr = mm(_C_QR, _C_KR)
    kr = mm(_C_KR, _C_VR)
    for hh in range(H_R):
        sl = slice(hh * LANE, (hh + 1) * LANE)
        xq = qr[:, sl]
        ret_ref[0, :, sl] = (xq * rr_ref[:, 0:LANE] + pltpu.roll(xq, DK_R // 2, 1) * rr_ref[:, LANE:2 * LANE]).astype(BF)
        xk = kr[:, sl]
        ret_ref[1, :, sl] = (xk * rr_ref[:, 2 * LANE:3 * LANE]
                             + pltpu.roll(xk, DK_R // 2, 1) * rr_ref[:, 3 * LANE:4 * LANE]).astype(BF)
    ret_ref[2] = mm(_C_VR, _C_GR).astype(BF)
    g = mm(_C_GR, _C_GM)
    gr_ref[...] = (g * _sigmoid(g)).astype(BF)
    gm_ref[...] = _sigmoid(mm(_C_GM, _C_GA)).astype(BF)
    ga_ref[...] = _sigmoid(mm(_C_GA, _C_END))


def _inproj(x2d, nw, sh, sc, w_pack, rq, rk, rr, tm, tab_blocks, per_row_mods, kv_t):
    rows = x2d.shape[0]
    nb = rows // tm
    if kv_t:
        kv_shape = jax.ShapeDtypeStruct((nb // tab_blocks, 256, tab_blocks * tm), F32)
        kv_spec = pl.BlockSpec((None, 256, tm), lambda i: (i // tab_blocks, 0, i % tab_blocks))
    else:
        kv_shape = jax.ShapeDtypeStruct((rows, 256), F32)
        kv_spec = pl.BlockSpec((tm, 256), lambda i: (i, 0))
    if per_row_mods:
        mod_spec = pl.BlockSpec((tm, D_MODEL), lambda i: (i, 0))
    else:
        mod_spec = pl.BlockSpec((None, 1, D_MODEL), lambda i: (i // tab_blocks, 0, 0))
    tab = lambda w: pl.BlockSpec((tm, w), lambda i: (i % tab_blocks, 0))
    row = lambda w: pl.BlockSpec((tm, w), lambda i: (i, 0))
    out_shapes = [
        jax.ShapeDtypeStruct((rows, 512), BF),
        jax.ShapeDtypeStruct((rows, 512), BF),
        kv_shape,
        kv_shape,
        kv_shape,
        jax.ShapeDtypeStruct((2, G_A, 2, rows, LANE), BF),
        jax.ShapeDtypeStruct((3, rows, 512), BF),
        jax.ShapeDtypeStruct((rows, 512), BF),
        jax.ShapeDtypeStruct((rows, 2048), BF),
        jax.ShapeDtypeStruct((rows, 256), F32),
    ]
    out_specs = [row(512), row(512), kv_spec, kv_spec, kv_spec,
                 pl.BlockSpec((2, G_A, 2, tm, LANE), lambda i: (0, 0, 0, i, 0)),
                 pl.BlockSpec((3, tm, 512), lambda i: (0, i, 0)),
                 row(512), row(2048), row(256)]
    return pl.pallas_call(
        functools.partial(_inproj_body, kv_t=kv_t),
        grid=(nb,),
        in_specs=[row(D_MODEL),
                  pl.BlockSpec((1, D_MODEL), lambda i: (0, 0)),
                  mod_spec, mod_spec,
                  pl.BlockSpec((D_MODEL, _C_END), lambda i: (0, 0), pipeline_mode=pl.Buffered(1)),
                  tab(384), tab(384), tab(512)],
        out_specs=out_specs,
        out_shape=out_shapes,
        compiler_params=_cparams(("arbitrary",)),
        name="inproj",
    )(x2d, nw, sh, sc, w_pack, rq, rk, rr)


_BPP = 4
_PAGE = _BPP * L_CMP


def _pack_phi_paged(phi_pos_k, phi_k1, phi_k2, phi_pos_v, phi_v1, phi_v2):
    def blockdiag(w):
        z = jnp.zeros_like(w)
        return jnp.concatenate(
            [jnp.concatenate([w if j == i else z for j in range(_BPP)], axis=-1) for i in range(_BPP)], axis=-2)
    w1, w2, pos = [], [], []
    for p_, a, b_ in ((phi_pos_k, phi_k1, phi_k2), (phi_pos_v, phi_v1, phi_v2)):
        w1.append(blockdiag(a.astype(BF).reshape(L_CMP, DH, DH).transpose(1, 0, 2)))
        w2.append(blockdiag(b_.astype(BF)))
        pos.append(jnp.tile(p_.T, (1, _BPP)))
    return jnp.stack(pos), jnp.stack(w1), jnp.stack(w2)


def _compress_slab_rows(xbuf, base, n_rows, stride, pos_ref, w1_ref, w2_ref):
    out = []
    for kv in range(2):
        acc = jnp.zeros((G_A * n_rows, _BPP * DH), F32)
        for d in range(DH):
            x = jnp.concatenate(
                [xbuf[base(pl.ds((kv * G_A + g) * DH + d, n_rows, stride=stride))] for g in range(G_A)], axis=0)
            acc = acc + _nn((x + pos_ref[kv, d:d + 1, :]).astype(BF), w1_ref[kv, d])
        out.append(_nn(_gelu(acc).astype(BF), w2_ref[kv]))
    return out


def _compress_t_body(src_ref, pos_ref, w1_ref, w2_ref, o_ref, xbuf, sem, *, nseq, n_pages):
    SL = 2 * G_A * DH

    def copy(t):
        b, p = t // n_pages, t % n_pages
        return pltpu.make_async_copy(src_ref.at[b, :, pl.ds(pl.multiple_of(p * _PAGE, _PAGE), _PAGE)],
                                     xbuf.at[pl.ds(pl.multiple_of(t * SL, SL), SL), :], sem)

    def start(t, c):
        copy(t).start()
        return c

    def wait(t, c):
        copy(t).wait()
        return c
    lax.fori_loop(0, nseq * n_pages, start, 0)
    lax.fori_loop(0, nseq * n_pages, wait, 0)
    kc, vc = _compress_slab_rows(xbuf, lambda rows: (rows, slice(None)), nseq * n_pages, SL, pos_ref, w1_ref, w2_ref)
    o_ref[0] = kc
    o_ref[1] = vc


def _compress_t(kv_t, pos, w1, w2):
    nseq, SL, T = kv_t.shape
    n_pages = T // _PAGE
    one = lambda a: pl.BlockSpec(a.shape, lambda i: (0,) * a.ndim, pipeline_mode=pl.Buffered(1))
    n_out = G_A * nseq * n_pages
    return pl.pallas_call(
        functools.partial(_compress_t_body, nseq=nseq, n_pages=n_pages),
        grid=(1,),
        in_specs=[pl.BlockSpec(memory_space=pl.ANY), one(pos), one(w1), one(w2)],
        out_specs=pl.BlockSpec((2, n_out, _BPP * DH), lambda i: (0, 0, 0)),
        out_shape=jax.ShapeDtypeStruct((2, n_out, _BPP * DH), F32),
        scratch_shapes=[pltpu.VMEM((nseq * n_pages * SL, _PAGE), F32), pltpu.SemaphoreType.DMA(())],
        compiler_params=_cparams(("arbitrary",)),
        name="compress_t",
    )(kv_t, pos, w1, w2)


_KT = 512


def _nsa_prompt_body(q_ref, qr_ref, kvb_ref, kc_ref, vc_ref, ga_ref, e_ref, o_ref, os_scr, *, nc, ns):
    i = pl.program_id(1)
    QB = Q_BLOCK
    lane = lax.broadcasted_iota(jnp.int32, (QB, LANE), 1)
    lo64 = lane < DH
    zero_b = jnp.zeros((QB, LANE), BF)
    groups = range(G_A)

    def split_heads(ref, g):
        ev, od = [], []
        for c in range(2):
            xc = ref[:, (2 * g + c) * LANE:(2 * g + c + 1) * LANE]
            ev.append(jnp.where(lo64, xc, zero_b))
            od.append(jnp.where(lo64, zero_b, xc))
        return jnp.concatenate(ev, axis=0), jnp.concatenate(od, axis=0)

    tq_l = i * QB + lax.broadcasted_iota(jnp.int32, (nc, QB), 1)
    r_c = lax.broadcasted_iota(jnp.int32, (nc, QB), 0)
    half = nc // 2
    blk_c = jnp.where(r_c < half, 2 * r_c, 2 * (r_c - half) + 1)
    cmask = (blk_c * L_CMP + (L_CMP - 1)) <= tq_l
    cmask_f = cmask.astype(F32)
    tq_s = i * QB + lax.broadcasted_iota(jnp.int32, (ns, QB), 1)
    blk_s = lax.broadcasted_iota(jnp.int32, (ns, QB), 0)
    valid = (blk_s * L_SEL) <= tq_s
    forced = (blk_s == 0) | (blk_s == tq_s // L_SEL)
    n_top = min(N_SEL, ns)

    def select(imp):
        score = jnp.where(valid, imp + jnp.where(forced, FORCE_BONUS, 0.0), NEG)
        rank = jnp.zeros((ns, QB), F32)
        for b2 in range(ns):
            row = score[b2:b2 + 1, :]
            rank = rank + jnp.where(blk_s > b2, jnp.where(row >= score, 1.0, 0.0), jnp.where(row > score, 1.0, 0.0))
        return jnp.where((rank < n_top) & (score > 0.5 * NEG), 1.0, 0.0)

    sel_b, o_cmp = [], []
    for g in groups:
        q_e, q_o = split_heads(q_ref, g)
        kc = kc_ref[g]
        vc = vc_ref[g]
        imp = jnp.zeros((ns, QB), F32)
        oc = {}
        for stack, qs in ((0, q_e), (1, q_o)):
            st = _nt(kc, qs)
            for c in range(2):
                s = jnp.where(cmask, st[:, c * QB:(c + 1) * QB], NEG)
                p = jnp.exp2(s - jnp.max(s, axis=0, keepdims=True)) * cmask_f
                p = p / jnp.maximum(jnp.sum(p, axis=0, keepdims=True), 1e-30)
                imp = imp + p[0:half] + p[half:nc]
                oc[(c, stack)] = _tn(p.astype(BF), vc)
        o_cmp.append(oc)
        sel_b.append(select(imp).astype(BF))

    qr = [split_heads(qr_ref, g) for g in groups]
    n_tiles = (i * QB + QB + _KT - 1) // _KT

    def masked_attend(qs, kmat, vmat, bias2):
        s = _nt(qs, kmat) + bias2
        p = jnp.exp2(s - jnp.max(s, axis=1, keepdims=True))
        return _nn(p.astype(BF), vmat) / jnp.sum(p, axis=1, keepdims=True)

    def slc_variant(nk):
        tq_r = i * QB + lax.broadcasted_iota(jnp.int32, (QB, nk), 0)
        kcol = lax.broadcasted_iota(jnp.int32, (QB, nk), 1)
        causal = kcol <= tq_r
        for g in groups:
            picked = _tn(sel_b[g], e_ref[:, 0:nk])
            bias = jnp.where((picked > 0.5) & causal, 0.0, NEG)
            bias2 = jnp.concatenate([bias, bias], axis=0)
            kv = kvb_ref[0, g, 0, 0:nk, :]
            vk = kvb_ref[0, g, 1, 0:nk, :]
            os_scr[g, 0] = masked_attend(qr[g][0], kv, vk, bias2)
            os_scr[g, 1] = masked_attend(qr[g][1], vk, kv, bias2)

    for k in range(1, kvb_ref.shape[3] // _KT + 1):
        pl.when(n_tiles == k)(functools.partial(slc_variant, k * _KT))

    WK = WINDOW + QB
    start = pl.multiple_of(jnp.maximum(i * QB - WINDOW, 0), QB)
    diff = (i * QB + lax.broadcasted_iota(jnp.int32, (QB, WK), 0)) - (start + lax.broadcasted_iota(jnp.int32, (QB, WK), 1))
    wb = jnp.where((diff >= 0) & (diff < WINDOW), 0.0, NEG)
    wb2 = jnp.concatenate([wb, wb], axis=0)

    for g in groups:
        wkv = kvb_ref[1, g, 0, pl.ds(start, WK), :]
        wvk = kvb_ref[1, g, 1, pl.ds(start, WK), :]
        ow_e = masked_attend(qr[g][0], wkv, wvk, wb2)
        ow_o = masked_attend(qr[g][1], wvk, wkv, wb2)
        os_e = os_scr[g, 0]
        os_o = os_scr[g, 1]
        ga = ga_ref[:, g * LANE:(g + 1) * LANE]
        for c in range(2):
            rows = slice(c * QB, (c + 1) * QB)
            branches = (
                jnp.where(lo64, o_cmp[g][(c, 0)], o_cmp[g][(c, 1)]),
                jnp.where(lo64, os_e[rows], os_o[rows]),
                jnp.where(lo64, ow_e[rows], ow_o[rows]),
            )
            acc = jnp.zeros((QB, LANE), F32)
            for br in range(3):
                col = br * HPG + 2 * c
                gate = jnp.where(lo64, ga[:, col:col + 1], ga[:, col + 1:col + 2])
                acc = acc + gate * branches[br]
            o_ref[:, (2 * g + c) * LANE:(2 * g + c + 1) * LANE] = acc.astype(BF)


def _sel_expand(ns, nkeys):
    e = (np.arange(nkeys)[None, :] // L_SEL) == np.arange(ns)[:, None]
    return jnp.asarray(e, BF)


def _nsa_prompt(q, qr, kvb, kc2, vc2, ga, B, T):
    nqb = T // Q_BLOCK
    nc, ns = T // L_CMP, T // L_SEL
    assert T >= WINDOW + Q_BLOCK and T % _KT == 0
    qspec = pl.BlockSpec((Q_BLOCK, 512), lambda b, i: (b * nqb + i, 0))
    cspec = pl.BlockSpec((None, G_A, nc, LANE), lambda b, i: (b, 0, 0, 0))
    return pl.pallas_call(
        functools.partial(_nsa_prompt_body, nc=nc, ns=ns),
        grid=(B, nqb),
        in_specs=[qspec, qspec,
                  pl.BlockSpec((2, G_A, 2, T, LANE), lambda b, i: (0, 0, 0, b, 0)),
                  cspec, cspec,
                  pl.BlockSpec((Q_BLOCK, G_A * LANE), lambda b, i: (b * nqb + i, 0)),
                  pl.BlockSpec((ns, T), lambda b, i: (0, 0))],
        out_specs=qspec,
        out_shape=jax.ShapeDtypeStruct((B * T, 512), BF),
        scratch_shapes=[pltpu.VMEM((G_A, 2, 2 * Q_BLOCK, LANE), F32)],
        compiler_params=_cparams(("arbitrary", "arbitrary")),
        name="nsa_prompt",
    )(q, qr, kvb, kc2, vc2, ga, _sel_expand(ns, T))


def _ret_body(qkv_ref, gr_ref, s0_ref, dm_ref, dq_ref, wk_ref, gc_ref, gnw_ref, z_ref, sout_ref, s_scr, *, C, sb):
    c = pl.program_id(1)

    @pl.when(c == 0)
    def _():
        s_scr[...] = s0_ref[...]

    for j in range(sb):
        rows = slice(j * C, (j + 1) * C)
        for h in range(H_R):
            sl = slice(h * LANE, (h + 1) * LANE)
            q = qkv_ref[0, rows, sl]
            k = qkv_ref[1, rows, sl]
            v = qkv_ref[2, rows, sl]
            s_old = s_scr[j, h]
            inner = _nt(q, k) * dm_ref[h]
            o = _nn(inner.astype(BF), v) + _nn(q, s_old.astype(BF)) * dq_ref[h]
            kw = (k.astype(F32) * wk_ref[h]).astype(BF)
            s_new = gc_ref[h, 0:1, :] * s_old + _tn(kw, v)
            s_scr[j, h] = s_new
            sout_ref[j, h] = s_new
            mu = jnp.mean(o, axis=-1, keepdims=True)
            d = o - mu
            var = jnp.mean(d * d, axis=-1, keepdims=True)
            on = d * lax.rsqrt(var + EPS) * gnw_ref[:, sl]
            z_ref[rows, sl] = (gr_ref[rows, sl].astype(F32) * on).astype(BF)


def _retention(ret3, gr, s0, gnw, nseq, rows_per_seq, C, c_true, sb):
    nC = rows_per_seq // C
    assert sb == 1 or nC == 1
    rows = nseq * rows_per_seq
    dm, dq, wk, gc = _ret_decay(C, c_true)
    full = lambda a: pl.BlockSpec(a.shape, lambda b, c: (0,) * a.ndim)
    return pl.pallas_call(
        functools.partial(_ret_body, C=C, sb=sb),
        grid=(nseq // sb, nC),
        in_specs=[pl.BlockSpec((3, sb * C, H_R * LANE), lambda b, c: (0, b * nC + c, 0)),
                  pl.BlockSpec((sb * C, H_R * LANE), lambda b, c: (b * nC + c, 0)),
                  pl.BlockSpec((sb, H_R, DK_R, DV_R), lambda b, c: (b, 0, 0, 0)),
                  full(dm), full(dq), full(wk), full(gc), full(gnw)],
        out_specs=[pl.BlockSpec((sb * C, H_R * LANE), lambda b, c: (b * nC + c, 0)),
                   pl.BlockSpec((sb, H_R, DK_R, DV_R), lambda b, c: (b, 0, 0, 0))],
        out_shape=[jax.ShapeDtypeStruct((rows, 512), BF),
                   jax.ShapeDtypeStruct((nseq, H_R, DK_R, DV_R), F32)],
        scratch_shapes=[pltpu.VMEM((sb, H_R, DK_R, DV_R), F32)],
        compiler_params=_cparams(("arbitrary", "arbitrary")),
        name="retention",
    )(ret3, gr, s0, dm, dq, wk, gc, gnw)


def _merge_body(x_ref, oa_ref, zr_ref, gm_ref, gt_ref, wa_ref, wr_ref, wo_ref, o_ref):
    ya = _nn(oa_ref[...], wa_ref[...])
    yr = _nn(zr_ref[...], wr_ref[...])
    gm = gm_ref[...].astype(F32)
    merged = gm[:, 0:D_MODEL] * ya + gm[:, D_MODEL:2 * D_MODEL] * yr
    mix = _nn(merged.astype(BF), wo_ref[...])
    o_ref[...] = x_ref[...] + gt_ref[...] * mix


def _merge(x2d, oa, zr, gm, gt, wa, wr, wo, tm, blocks_per_seq, per_row_mods):
    rows = x2d.shape[0]
    if per_row_mods:
        mod_spec = pl.BlockSpec((tm, D_MODEL), lambda i: (i, 0))
    else:
        mod_spec = pl.BlockSpec((None, 1, D_MODEL), lambda i: (i // blocks_per_seq, 0, 0))
    row = lambda w: pl.BlockSpec((tm, w), lambda i: (i, 0))
    full = lambda a, b: pl.BlockSpec((a, b), lambda i: (0, 0))
    return pl.pallas_call(
        _merge_body,
        grid=(rows // tm,),
        in_specs=[row(D_MODEL), row(512), row(512), row(2048), mod_spec,
                  full(512, D_MODEL), full(512, D_MODEL), full(D_MODEL, D_MODEL)],
        out_specs=row(D_MODEL),
        out_shape=jax.ShapeDtypeStruct((rows, D_MODEL), F32),
        compiler_params=_cparams(("arbitrary",)),
        name="merge",
    )(x2d, oa, zr, gm, gt, wa, wr, wo)


def _ffn_body(x_ref, nw_ref, sh_ref, sc_ref, gt_ref, wu_ref, cw_ref, cb_ref, wd_ref, nf_ref, p1_ref, p2_ref,
              y_ref, a_ref, carry, *, blocks_per_seq, seq_rows):
    i = pl.program_id(0)
    x = x_ref[...]
    tm = x.shape[0]
    ms = jnp.mean(x * x, axis=-1, keepdims=True)
    h = (x * lax.rsqrt(ms + EPS)) * nw_ref[...]
    h = (h * (1.0 + sc_ref[...]) + sh_ref[...]).astype(BF)
    a = _nn(h, wu_ref[:, 0:D_FF])
    b = _nn(h, wu_ref[:, D_FF:2 * D_FF])
    a_ref[...] = a[tm - a_ref.shape[0]:tm, :]
    rid = lax.broadcasted_iota(jnp.int32, (tm, D_FF), 0)
    if seq_rows is None:
        first = (i % blocks_per_seq) == 0
        prev = jnp.where(first, p1_ref[...], carry[...])
        carry[...] = a[tm - 8:tm, :]
        am1 = jnp.where(rid == 0, prev[7:8, :], pltpu.roll(a, 1, 0))
        am2 = jnp.where(rid == 0, prev[6:7, :], jnp.where(rid == 1, prev[7:8, :], pltpu.roll(a, 2, 0)))
    else:
        s = rid % seq_rows
        am1 = jnp.where(s == 0, p1_ref[...], pltpu.roll(a, 1, 0))
        am2 = jnp.where(s <= 1, p2_ref[...], pltpu.roll(a, 2, 0))
    u = cb_ref[...] + am2 * cw_ref[0:1, :] + am1 * cw_ref[1:2, :] + a * cw_ref[2:3, :]
    ff = _nn((_gelu(u) * b).astype(BF), wd_ref[...])
    x2 = x + gt_ref[...] * ff
    ms2 = jnp.mean(x2 * x2, axis=-1, keepdims=True)
    y_ref[...] = (x2 * lax.rsqrt(ms2 + EPS)) * nf_ref[...]


def _ffn(x2d, nw, sh, sc, gt, wu, cw, cb, wd, nf, p1, p2, tm, blocks_per_seq, per_row_mods, seq_rows):
    rows = x2d.shape[0]
    a_rows = tm if seq_rows is not None else 8
    if per_row_mods:
        mod_spec = pl.BlockSpec((tm, D_MODEL), lambda i: (i, 0))
    else:
        mod_spec = pl.BlockSpec((None, 1, D_MODEL), lambda i: (i // blocks_per_seq, 0, 0))
    row = lambda w: pl.BlockSpec((tm, w), lambda i: (i, 0))
    full = lambda a, b: pl.BlockSpec((a, b), lambda i: (0, 0), pipeline_mode=pl.Buffered(1))
    vec = lambda w: pl.BlockSpec((1, w), lambda i: (0, 0))
    pspec = pl.BlockSpec(p1.shape, lambda i: (0, 0))
    return pl.pallas_call(
        functools.partial(_ffn_body, blocks_per_seq=blocks_per_seq, seq_rows=seq_rows),
        grid=(rows // tm,),
        in_specs=[row(D_MODEL), vec(D_MODEL), mod_spec, mod_spec, mod_spec,
                  full(D_MODEL, 2 * D_FF), pl.BlockSpec((CONV_W, D_FF), lambda i: (0, 0)), vec(D_FF),
                  full(D_FF, D_MODEL), vec(D_MODEL), pspec, pspec],
        out_specs=[row(D_MODEL), pl.BlockSpec((a_rows, D_FF), lambda i: (i, 0))],
        out_shape=[jax.ShapeDtypeStruct((rows, D_MODEL), F32),
                   jax.ShapeDtypeStruct((rows // tm * a_rows, D_FF), F32)],
        scratch_shapes=[pltpu.VMEM((8, D_FF), F32)],
        compiler_params=_cparams(("arbitrary",)),
        name="ffn",
    )(x2d, nw, sh, sc, gt, wu, cw, cb, wd, nf, p1, p2)


def _prompt_path(x_prompt, mods_p, W):
    B, T, _ = x_prompt.shape
    rows = B * T
    x2d = x_prompt.reshape(rows, D_MODEL)
    pos = np.arange(T)
    tm = 256
    sh1, sc1, gt1, sh2, sc2, gt2 = [mods_p[:, None, j * D_MODEL:(j + 1) * D_MODEL] for j in range(6)]
    (q, qr, kvc, kvs, kvw, kvb, ret3, gr, gm, ga) = _inproj(
        x2d, W["norm1"], sh1, sc1, W["w_in"], _rope_tab(pos, Q_SCALE), _rope_tab(pos, 1.0), _ret_tab(pos),
        tm, T // tm, False, True)
    nc, n_pages = T // L_CMP, T // _PAGE
    comp = _compress_t(kvc, W["phi_posT"], W["phi_w1p"], W["phi_w2p"])
    comp = comp.reshape(2, G_A, B, n_pages, 2, 2, DH).transpose(0, 2, 1, 5, 3, 4, 6).reshape(2, B, G_A, nc, DH)
    comp2 = jnp.concatenate([comp, comp], axis=-1).astype(BF)
    oa = _nsa_prompt(q, qr, kvb, comp2[0], comp2[1], ga, B, T)
    C = 256 if T % 256 == 0 else T
    zr, ret_new = _retention(ret3, gr, jnp.zeros((B, H_R, DK_R, DV_R), F32), W["gnw"], B, T, C, C, 1)
    x1 = _merge(x2d, oa, zr, gm, gt1, W["w_up_a"], W["w_up_r"], W["w_out"], tm, T // tm, False)
    zeros8 = jnp.zeros((8, D_FF), F32)
    y, a_up = _ffn(x1, W["norm2"], sh2, sc2, gt2, W["w_ffn_up"], W["conv_w"], W["conv_b"], W["w_ffn_down"],
                   W["normf"], zeros8, zeros8, tm, T // tm, False, None)
    wsz = min(WINDOW, T)
    rows_major = lambda t: t.reshape(B, 2, G_A, DH, t.shape[-1]).transpose(0, 4, 1, 2, 3)[None]
    outs = dict(
        y=y.reshape(B, T, D_MODEL),
        cmp=rows_major(kvc), slc=rows_major(kvs), win=rows_major(kvw[:, :, T - wsz:]),
        ret=ret_new[None],
        conv=a_up.reshape(B, T // tm, 8, D_FF)[None, :, T // tm - 1, 8 - (CONV_W - 1):],
    )
    return outs


_QC = 32


def _cmp_paged_body(pt_ref, cache_ref, q_ref, pos_ref, w1_ref, w2_ref, ocmp_ref, topi_ref, xbuf, sem,
                    *, n_pages, n_pick, n_q):
    b = pl.program_id(0)
    nb = pl.num_programs(0)
    slot = b % 2
    SL = 2 * G_A * DH

    def copy(bb, sl, p):
        return pltpu.make_async_copy(cache_ref.at[pt_ref[bb, p]], xbuf.at[sl, pl.ds(p * SL, SL), :], sem.at[sl])

    def issue(bb, sl):
        def f(p, c):
            copy(bb, sl, p).start()
            return c
        lax.fori_loop(0, n_pages, f, 0)

    @pl.when(b == 0)
    def _():
        issue(b, slot)

    @pl.when(b + 1 < nb)
    def _():
        issue(b + 1, 1 - slot)

    def wait(p, c):
        copy(b, slot, p).wait()
        return c
    lax.fori_loop(0, n_pages, wait, 0)

    comp = _compress_slab_rows(xbuf, lambda rows: (slot, rows, slice(None)), n_pages, SL, pos_ref, w1_ref, w2_ref)

    def lane_groups(x, op):
        r = x
        for j in range(1, _BPP):
            r = op(r, pltpu.roll(x, j * _QC, 1))
        return r

    for g in range(G_A):
        kc = comp[0][g * n_pages:(g + 1) * n_pages].astype(BF)
        vc = comp[1][g * n_pages:(g + 1) * n_pages].astype(BF)
        st = _nn(kc, q_ref[g])
        m = lane_groups(jnp.max(st, axis=0, keepdims=True), jnp.maximum)
        p = jnp.exp2(st - m)
        p = p / lane_groups(jnp.sum(p, axis=0, keepdims=True), jnp.add)
        r_full = _tn(p.astype(BF), vc)
        o = r_full[0:_QC, 0:DH]
        for j in range(1, _BPP):
            o = o + r_full[j * _QC:(j + 1) * _QC, j * DH:(j + 1) * DH]
        ocmp_ref[g] = o
        pair = p + pltpu.roll(p, LANE - _QC, 1)
        imp = pair
        for r in range(1, HPG):
            imp = imp + pltpu.roll(pair, LANE - r * n_q, 1)
        sc = jnp.concatenate([imp, pltpu.roll(imp, 2 * _QC, 1)], axis=0)
        row = lax.broadcasted_iota(jnp.int32, (2 * n_pages, LANE), 0)
        blk = jnp.where(row < n_pages, 2 * row, 2 * (row - n_pages) + 1)
        score = sc + jnp.where(blk == 0, FORCE_BONUS, 0.0)
        topi_ref[g] = jnp.zeros((8, LANE), jnp.int32)
        for k in range(n_pick):
            mx = jnp.max(score, axis=0, keepdims=True)
            idx = jnp.min(jnp.where(score == mx, blk, 2 * n_pages), axis=0, keepdims=True)
            topi_ref[g, k:k + 1, :] = idx
            score = jnp.where(blk == idx, -jnp.inf, score)


def _cmp_paged(page_table, cache_t, q_bd, pos, w1, w2, n_q):
    DB, n_pages = page_table.shape
    SL = cache_t.shape[1]
    page_rows = cache_t.shape[2]
    kern = functools.partial(_cmp_paged_body, n_pages=n_pages, n_pick=N_SEL - 1, n_q=n_q)
    one = dict(pipeline_mode=pl.Buffered(1))
    return pl.pallas_call(
        kern,
        grid_spec=pltpu.PrefetchScalarGridSpec(
            num_scalar_prefetch=1, grid=(DB,),
            in_specs=[pl.BlockSpec(memory_space=pl.ANY),
                      pl.BlockSpec((None, G_A, _BPP * DH, LANE), lambda b, pt: (b, 0, 0, 0)),
                      pl.BlockSpec((2, DH, page_rows), lambda b, pt: (0, 0, 0), **one),
                      pl.BlockSpec((2, DH, page_rows, _BPP * DH), lambda b, pt: (0, 0, 0, 0), **one),
                      pl.BlockSpec((2, _BPP * DH, _BPP * DH), lambda b, pt: (0, 0, 0), **one)],
            out_specs=[pl.BlockSpec((None, G_A, _QC, DH), lambda b, pt: (b, 0, 0, 0)),
                       pl.BlockSpec((None, G_A, 8, LANE), lambda b, pt: (b, 0, 0, 0))],
            scratch_shapes=[pltpu.VMEM((2, n_pages * SL, page_rows), F32),
                            pltpu.SemaphoreType.DMA((2,))]),
        out_shape=[jax.ShapeDtypeStruct((DB, G_A, _QC, DH), F32),
                   jax.ShapeDtypeStruct((DB, G_A, 8, LANE), jnp.int32)],
        compiler_params=_cparams(("arbitrary",)),
        name="cmp_paged",
    )(page_table, cache_t, q_bd, pos, w1, w2)


def _slc_win_paged_body(pt_ref, ti_ref, cslc_ref, win_ref, q_ref, tiv_ref, ns_ref, nw_ref, ex_ref,
                        oslc_ref, owin_ref, kbuf, sem, *, n_q, n_pick, page_rows):
    b = pl.program_id(0)
    nb = pl.num_programs(0)
    slot = b % 2
    n_slab = n_q * n_pick
    bpp = page_rows // L_SEL
    wb = win_ref.shape[1]

    def copies(bb, sl):
        cps = []
        for g in range(G_A):
            for j in range(n_slab):
                page = pt_ref[bb, ti_ref[bb, g * n_slab + j] // bpp]
                for kv in range(2):
                    cps.append(pltpu.make_async_copy(
                        cslc_ref.at[page, pl.ds((kv * G_A + g) * DH, DH), :],
                        kbuf.at[sl, g, kv, :, pl.ds(j * page_rows, page_rows)], sem.at[sl]))
        return cps

    @pl.when(b == 0)
    def _():
        for cp in copies(b, slot):
            cp.start()

    @pl.when(b + 1 < nb)
    def _():
        for cp in copies(b + 1, 1 - slot):
            cp.start()

    for cp in copies(b, slot):
        cp.wait()

    nq_rows = q_ref.shape[1]
    nk = n_slab * page_rows
    row_q = lax.broadcasted_iota(jnp.int32, (nq_rows, nk), 0) % n_q
    col = lax.broadcasted_iota(jnp.int32, (nq_rows, nk), 1)
    own = row_q == col // (n_pick * page_rows)
    half = ((col % page_rows) // L_SEL).astype(F32)
    nnew = ns_ref.shape[2]
    new_ok = (lax.broadcasted_iota(jnp.int32, (nq_rows, nnew), 1)
              <= lax.broadcasted_iota(jnp.int32, (nq_rows, nnew), 0) % n_q)
    dwin = (wb + lax.broadcasted_iota(jnp.int32, (nq_rows, wb), 0) % n_q
            - lax.broadcasted_iota(jnp.int32, (nq_rows, wb), 1))
    win_ok = (dwin >= 0) & (dwin < WINDOW)

    def attend(q, kt_old, vt_old, ok_old, k_new, v_new):
        s_o = jnp.where(ok_old, _nn(q, kt_old), NEG)
        s_n = jnp.where(new_ok, _nt(q, k_new), NEG)
        m = jnp.maximum(jnp.max(s_o, axis=1, keepdims=True), jnp.max(s_n, axis=1, keepdims=True))
        p_o = jnp.exp2(s_o - m)
        p_n = jnp.exp2(s_n - m)
        den = jnp.sum(p_o, axis=1, keepdims=True) + jnp.sum(p_n, axis=1, keepdims=True)
        return (_nt(p_o.astype(BF), vt_old) + _nn(p_n.astype(BF), v_new)) / den

    for g in range(G_A):
        q = q_ref[g]
        par = (tiv_ref[g] % bpp).astype(F32).astype(BF)
        want = _nn(par, ex_ref[...])[0:1, :]
        ok = own & (half == want)
        oslc_ref[g] = attend(q, kbuf[slot, g, 0].astype(BF), kbuf[slot, g, 1].astype(BF), ok,
                             ns_ref[g, 0].astype(BF), ns_ref[g, 1].astype(BF))
        kw = win_ref[pl.ds((0 * G_A + g) * DH, DH), :].astype(BF)
        vw = win_ref[pl.ds((1 * G_A + g) * DH, DH), :].astype(BF)
        owin_ref[g] = attend(q, kw, vw, win_ok, nw_ref[g, 0].astype(BF), nw_ref[g, 1].astype(BF))


def _slc_win_paged(page_table, topi_flat, topi_vec, cache_t, win_t, q_rot, new_slc, new_win, n_q):
    DB = page_table.shape[0]
    n_pick = N_SEL - 1
    page_rows = cache_t.shape[2]
    wb = win_t.shape[2]
    nq_rows = q_rot.shape[2]
    n_slab = n_q * n_pick
    ex = (np.arange(n_slab * page_rows)[None, :] // page_rows) == np.arange(LANE)[:, None]
    kern = functools.partial(_slc_win_paged_body, n_q=n_q, n_pick=n_pick, page_rows=page_rows)
    bspec = lambda shp: pl.BlockSpec((None,) + shp, lambda b, pt, ti: (b,) + (0,) * len(shp))
    return pl.pallas_call(
        kern,
        grid_spec=pltpu.PrefetchScalarGridSpec(
            num_scalar_prefetch=2, grid=(DB,),
            in_specs=[pl.BlockSpec(memory_space=pl.ANY), bspec((2 * G_A * DH, wb)),
                      bspec((G_A, nq_rows, DH)), bspec((G_A, 16, LANE)),
                      bspec((G_A, 2, 16, DH)), bspec((G_A, 2, 16, DH)),
                      pl.BlockSpec((LANE, n_slab * page_rows), lambda b, pt, ti: (0, 0))],
            out_specs=[bspec((G_A, nq_rows, DH)), bspec((G_A, nq_rows, DH))],
            scratch_shapes=[pltpu.VMEM((2, G_A, 2, DH, n_slab * page_rows), F32),
                            pltpu.SemaphoreType.DMA((2,))]),
        out_shape=[jax.ShapeDtypeStruct((DB, G_A, nq_rows, DH), F32)] * 2,
        compiler_params=_cparams(("arbitrary",)),
        name="slc_win_paged",
    )(page_table, topi_flat, cache_t, win_t, q_rot, topi_vec, new_slc, new_win, jnp.asarray(ex, BF))


def _gate_sample_body(oc_ref, os_ref, ow_ref, ga_ref, o_ref):
    rows = o_ref.shape[0]
    lo64 = lax.broadcasted_iota(jnp.int32, (rows, LANE), 1) < DH
    for c4 in range(H_A // 2):
        g, c = c4 // 2, c4 % 2
        sl = slice(c4 * LANE, (c4 + 1) * LANE)
        acc = jnp.zeros((rows, LANE), F32)
        for br, ref in enumerate((oc_ref, os_ref, ow_ref)):
            col = g * LANE + br * HPG + 2 * c
            gate = jnp.where(lo64, ga_ref[:, col:col + 1], ga_ref[:, col + 1:col + 2])
            acc = acc + gate * ref[:, sl]
        o_ref[:, sl] = acc.astype(BF)


def _gate_sample(oc, osl, ow, ga):
    rows = oc.shape[0]
    full = lambda w: pl.BlockSpec((rows, w), lambda i: (0, 0))
    return pl.pallas_call(
        _gate_sample_body, grid=(1,),
        in_specs=[full(512), full(512), full(512), full(256)],
        out_specs=full(512),
        out_shape=jax.ShapeDtypeStruct((rows, 512), BF),
        compiler_params=_cparams(("arbitrary",)),
        name="gate_sample",
    )(oc, osl, ow, ga)


def _sample_path(x_sample, mods_s, W, cache_cmp, cache_slc, state_win, state_ret, state_conv, page_table):
    DB, S, _ = x_sample.shape
    rows = DB * S
    page_rows = cache_cmp.shape[1]
    P = page_table.shape[1] * page_rows
    wb = state_win.shape[1]
    assert P % L_SEL == 0 and S < L_CMP and S <= 8 and wb == WINDOW and page_rows % L_SEL == 0
    assert P // L_SEL >= N_SEL and CONV_W == 3 and S >= CONV_W - 1
    pos = P + np.arange(S)
    pos_rows = np.tile(pos, DB)
    x2d = x_sample.reshape(rows, D_MODEL)
    modr = jnp.repeat(mods_s, S, axis=0)
    sh1, sc1, gt1, sh2, sc2, gt2 = [modr[:, j * D_MODEL:(j + 1) * D_MODEL] for j in range(6)]
    (q, qr, kvc, kvs, kvw, _, ret3, gr, gm, ga) = _inproj(
        x2d, W["norm1"], sh1, sc1, W["w_in"], _rope_tab(pos_rows, Q_SCALE), _rope_tab(pos_rows, 1.0),
        _ret_tab(pos_rows), rows, 1, True, False)

    def to_heads(t):
        return t.reshape(DB, S, G_A, HPG, DH).transpose(0, 2, 3, 1, 4).reshape(DB, G_A, HPG * S, DH)

    def from_heads(t):
        return t.reshape(DB, G_A, HPG, S, DH).transpose(0, 3, 1, 2, 4).reshape(rows, H_A * DH)

    assert page_rows == _BPP * L_CMP and HPG * S <= _QC and S * (N_SEL - 1) <= LANE
    slab = lambda t: t.transpose(0, 2, 3, 4, 1).reshape(t.shape[0], 2 * G_A * DH, t.shape[1])
    qt = jnp.pad(to_heads(q).transpose(0, 1, 3, 2), ((0, 0), (0, 0), (0, 0), (0, _QC - HPG * S)))
    eye = jnp.eye(_BPP, dtype=qt.dtype)
    q_bd = (eye[None, None, :, None, :, None] * qt[:, :, None, :, None, :]).reshape(DB, G_A, _BPP * DH, _BPP * _QC)
    o_cmp, topi = _cmp_paged(page_table, slab(cache_cmp), q_bd, W["phi_posT"], W["phi_w1p"], W["phi_w2p"], S)
    n_pick = N_SEL - 1
    topi = topi[:, :, :n_pick, :S].transpose(0, 1, 3, 2).reshape(DB, G_A, S * n_pick)
    topi_vec = jnp.broadcast_to(jnp.pad(topi, ((0, 0), (0, 0), (0, LANE - S * n_pick)))[:, :, None, :],
                                (DB, G_A, 16, LANE))

    def new_rows(t):
        t = t.reshape(DB, S, 2, G_A, DH).transpose(0, 3, 2, 1, 4)
        return jnp.pad(t, ((0, 0), (0, 0), (0, 0), (0, 16 - S), (0, 0)))

    o_slc, o_win = _slc_win_paged(page_table, topi.reshape(DB, G_A * S * n_pick), topi_vec, slab(cache_slc),
                                  slab(state_win), to_heads(qr), new_rows(kvs), new_rows(kvw), S)
    oa = _gate_sample(from_heads(o_cmp[:, :, :HPG * S]), from_heads(o_slc), from_heads(o_win), ga)

    RP = 16
    padr = lambda t: jnp.pad(t.reshape(t.shape[:-2] + (DB, S, 512)),
                             ((0, 0),) * (t.ndim - 1) + ((0, RP - S), (0, 0))).reshape(t.shape[:-2] + (DB * RP, 512))
    zr, ret_new = _retention(padr(ret3), padr(gr), state_ret, W["gnw"], DB, RP, RP, S, 8 if DB % 8 == 0 else 1)
    zr = zr.reshape(DB, RP, 512)[:, :S].reshape(rows, 512)
    x1 = _merge(x2d, oa, zr, gm, gt1, W["w_up_a"], W["w_up_r"], W["w_out"], rows, 1, True)
    zs = jnp.zeros((DB, S, D_FF), F32)
    p1 = zs.at[:, 0].set(state_conv[:, 1]).reshape(rows, D_FF)
    p2 = zs.at[:, 0].set(state_conv[:, 0]).at[:, 1].set(state_conv[:, 1]).reshape(rows, D_FF)
    y, a_up = _ffn(x1, W["norm2"], sh2, sc2, gt2, W["w_ffn_up"], W["conv_w"], W["conv_b"], W["w_ffn_down"],
                   W["normf"], p1, p2, rows, 1, True, S)
    shp = (1, DB, S, 2, G_A, DH)
    return dict(
        y=y.reshape(DB, S, D_MODEL),
        cmp=kvc.reshape(shp), slc=kvs.reshape(shp),
        win=jnp.concatenate([state_win[:, S:], kvw.reshape(DB, S, 2, G_A, DH)], axis=1)[None],
        ret=ret_new[None],
        conv=a_up.reshape(DB, S, D_FF)[None, :, S - (CONV_W - 1):],
    )


def kernel(x_prompt, x_sample, cache_cmp_kv, cache_slc_kv, state_win_kv, state_ret, state_conv, page_table,
           c_prompt, c_sample, norm1_w, norm2_w, w_ada, b_ada, w_in, phi_pos_k, phi_k1, phi_k2, phi_pos_v,
           phi_v1, phi_v2, w_up_a, ret_gn_w, w_up_r, w_out, w_ffn_up, ffn_conv_w, ffn_conv_b, w_ffn_down, normf_w):
    B = x_prompt.shape[0]
    l = 0
    W = dict(
        norm1=norm1_w[l].reshape(1, D_MODEL), norm2=norm2_w[l].reshape(1, D_MODEL), normf=normf_w.reshape(1, D_MODEL),
        w_in=_pack_w_in(w_in[l]),
        w_up_a=w_up_a[l].astype(BF), w_up_r=w_up_r[l].astype(BF), w_out=w_out[l].astype(BF),
        gnw=ret_gn_w[l].reshape(1, H_R * DV_R),
        w_ffn_up=w_ffn_up[l].astype(BF), conv_w=ffn_conv_w[l], conv_b=ffn_conv_b[l].reshape(1, D_FF),
        w_ffn_down=w_ffn_down[l].astype(BF),
    )
    W["phi_posT"], W["phi_w1p"], W["phi_w2p"] = _pack_phi_paged(
        phi_pos_k[l], phi_k1[l], phi_k2[l], phi_pos_v[l], phi_v1[l], phi_v2[l])
    mods = _mods(jnp.concatenate([c_prompt, c_sample], axis=0), w_ada[l], b_ada[l])
    p = _prompt_path(x_prompt, mods[:B], W)
    s = _sample_path(x_sample, mods[B:], W, cache_cmp_kv[l], cache_slc_kv[l], state_win_kv[l], state_ret[l],
                     state_conv[l], page_table)
    return (p["y"], s["y"], p["cmp"], s["cmp"], p["slc"], s["slc"], p["win"], s["win"],
            p["ret"], s["ret"], p["conv"], s["conv"])
```

```python
import functools

import numpy as np
import jax
import jax.numpy as jnp
from jax import lax
from jax.experimental import pallas as pl
from jax.experimental.pallas import tpu as pltpu

BF = jnp.bfloat16
F32 = jnp.float32

D_MODEL = 1024
H_A, G_A, DH = 8, 2, 64
HPG = H_A // G_A
ROPE_DIM = DH // 4
ROPE_THETA = 500000.0
L_CMP, L_SEL, N_SEL = 32, 64, 8
WINDOW = 512
Q_BLOCK = 128
FORCE_BONUS = 1e4
H_R, DK_R, DV_R = 4, 128, 128
RET_THETA = 10000.0
D_FF = 2816
CONV_W = 3
EPS = 1e-6
NEG = -1e30
Q_SCALE = DH ** -0.5 * 1.4426950408889634
LANE = 128
VMEM_LIMIT = 56 * 1024 * 1024


def _cparams(sem):
    return pltpu.CompilerParams(dimension_semantics=sem, vmem_limit_bytes=VMEM_LIMIT)


def _sigmoid(x):
    return 1.0 / (1.0 + jnp.exp(-x))


def _gelu(x):
    return 0.5 * x * (1.0 + jnp.tanh(0.7978845608028654 * (x + 0.044715 * (x * x * x))))


def _nt(a, b):
    return lax.dot_general(a, b, (((1,), (1,)), ((), ())), preferred_element_type=F32)


def _tn(a, b):
    return lax.dot_general(a, b, (((0,), (0,)), ((), ())), preferred_element_type=F32)


def _nn(a, b):
    return jnp.dot(a, b, preferred_element_type=F32)


def _rope_tab(pos, scale):
    half = ROPE_DIM // 2
    inv = ROPE_THETA ** (-np.arange(half, dtype=np.float64) * (2.0 / ROPE_DIM))
    ang = pos.astype(np.float64)[:, None] * inv
    cos, sin = np.cos(ang), np.sin(ang)
    n = pos.shape[0]
    c = np.ones((n, DH)); s_lo = np.zeros((n, DH)); s_hi = np.zeros((n, DH))
    c[:, :half] = cos; c[:, half:ROPE_DIM] = cos
    s_lo[:, half:ROPE_DIM] = sin
    s_hi[:, :half] = -sin
    tab = np.concatenate([np.tile(t, (1, 2)) for t in (c, s_lo, s_hi)], axis=1) * scale
    return jnp.asarray(tab, F32)


def _ret_tab(pos):
    half = DK_R // 2
    inv = RET_THETA ** (-np.arange(half, dtype=np.float64) * (2.0 / DK_R))
    ang = pos.astype(np.float64)[:, None] * inv
    cos, sin = np.cos(ang), np.sin(ang)
    c = np.concatenate([cos, cos], axis=1)
    s = np.concatenate([-sin, sin], axis=1)
    ks = DK_R ** -0.5
    return jnp.asarray(np.concatenate([c, s, c * ks, s * ks], axis=1), F32)


def _ret_decay(C, c_true):
    h = np.arange(H_R, dtype=np.float64)
    log_g = np.log1p(-np.exp2(-5.0 - h))
    i = np.arange(C, dtype=np.float64)
    diff = i[:, None] - i[None, :]
    dm = np.where(diff >= 0, np.exp(log_g[:, None, None] * np.maximum(diff, 0.0)), 0.0)
    dq = np.exp(log_g[:, None] * (i + 1.0))[:, :, None] * np.ones((1, 1, LANE))
    wk = np.exp(log_g[:, None] * (c_true - 1.0 - i))[:, :, None] * np.ones((1, 1, LANE))
    wk = np.where(i[None, :, None] < c_true, wk, 0.0)
    gc = np.exp(log_g * c_true)[:, None, None] * np.ones((1, 8, LANE))
    return (jnp.asarray(dm, F32), jnp.asarray(dq, F32), jnp.asarray(wk, F32), jnp.asarray(gc, F32))


def _mods_body(c_ref, w_ref, b_ref, o_ref):
    c = c_ref[...]
    s = c * _sigmoid(c)
    o_ref[...] = _nn(s.astype(BF), w_ref[...].astype(BF)) + b_ref[...]


def _mods(c_all, w_ada, b_ada):
    n = c_all.shape[0]
    nout = w_ada.shape[1]
    tn = 1024
    return pl.pallas_call(
        _mods_body,
        grid=(nout // tn,),
        in_specs=[pl.BlockSpec((n, D_MODEL), lambda j: (0, 0)),
                  pl.BlockSpec((D_MODEL, tn), lambda j: (0, j)),
                  pl.BlockSpec((1, tn), lambda j: (0, j))],
        out_specs=pl.BlockSpec((n, tn), lambda j: (0, j)),
        out_shape=jax.ShapeDtypeStruct((n, nout), F32),
        compiler_params=_cparams(("arbitrary",)),
        name="mods",
    )(c_all, w_ada, b_ada.reshape(1, nout))


_C_Q, _C_KV, _C_QR, _C_KR, _C_VR, _C_GR, _C_GM, _C_GA, _C_END = (
    0, 512, 1280, 1792, 2304, 2816, 3328, 5376, 5632)


def _pack_w_in(w_in):
    o = np.cumsum((0, 512, 768, 24, 512, 512, 512, 512, 2048))
    q, kv, ga, qr, kr, vr, gr, gm = [w_in[:, o[i]:o[i + 1]] for i in range(8)]
    ga = ga.reshape(D_MODEL, 3, G_A, HPG).transpose(0, 2, 1, 3).reshape(D_MODEL, G_A, 3 * HPG)
    ga = jnp.pad(ga, ((0, 0), (0, 0), (0, LANE - 3 * HPG))).reshape(D_MODEL, G_A * LANE)
    return jnp.concatenate([q, kv, qr, kr, vr, gr, gm, ga], axis=1).astype(BF)


def _inproj_body(x_ref, nw_ref, sh_ref, sc_ref, w_ref, rq_ref, rk_ref, rr_ref,
                 q_ref, qr_ref, kvc_ref, kvs_ref, kvw_ref, kvb_ref, ret_ref, gr_ref, gm_ref, ga_ref, *, kv_t):
    def put_kv(out_ref, k, v):
        if kv_t:
            out_ref[0:LANE, :] = k.T
            out_ref[LANE:2 * LANE, :] = v.T
        else:
            out_ref[:, 0:LANE] = k
            out_ref[:, LANE:2 * LANE] = v

    x = x_ref[...]
    tm = x.shape[0]
    ms = jnp.mean(x * x, axis=-1, keepdims=True)
    h = (x * lax.rsqrt(ms + EPS)) * nw_ref[...]
    h = h * (1.0 + sc_ref[...]) + sh_ref[...]
    hb = h.astype(BF)
    lo64 = lax.broadcasted_iota(jnp.int32, (tm, LANE), 1) < DH

    def mm(lo, hi):
        return _nn(hb, w_ref[:, lo:hi])

    def rope(xc, tab_ref):
        return (xc * tab_ref[:, 0:LANE] + pltpu.roll(xc, 8, 1) * tab_ref[:, LANE:2 * LANE]
                + pltpu.roll(xc, LANE - 8, 1) * tab_ref[:, 2 * LANE:3 * LANE])

    qa = mm(_C_Q, _C_KV)
    q_ref[...] = (qa * Q_SCALE).astype(BF)
    for c in range(4):
        qr_ref[:, c * LANE:(c + 1) * LANE] = rope(qa[:, c * LANE:(c + 1) * LANE], rq_ref).astype(BF)

    kv = mm(_C_KV, _C_QR)
    put_kv(kvc_ref, kv[:, 0:LANE], kv[:, LANE:2 * LANE])
    for kind, out_ref in ((0, kvs_ref), (1, kvw_ref)):
        base = 256 + kind * 256
        k = rope(kv[:, base:base + LANE], rk_ref)
        v = kv[:, base + LANE:base + 2 * LANE]
        put_kv(out_ref, k, v)
        kr_ = pltpu.roll(k, DH, 1)
        vr_ = pltpu.roll(v, DH, 1)
        kvb_ref[kind, 0, 0] = jnp.where(lo64, k, vr_).astype(BF)
        kvb_ref[kind, 0, 1] = jnp.where(lo64, v, kr_).astype(BF)
        kvb_ref[kind, 1, 0] = jnp.where(lo64, kr_, v).astype(BF)
        kvb_ref[kind, 1, 1] = jnp.where(lo64, vr_, k).astype(BF)

    qr = mm(_C_QR, _C_KR)
    kr = mm(_C_KR, _C_VR)
    for hh in range(H_R):
        sl = slice(hh * LANE, (hh + 1) * LANE)
        xq = qr[:, sl]
        ret_ref[0, :, sl] = (xq * rr_ref[:, 0:LANE] + pltpu.roll(xq, DK_R // 2, 1) * rr_ref[:, LANE:2 * LANE]).astype(BF)
        xk = kr[:, sl]
        ret_ref[1, :, sl] = (xk * rr_ref[:, 2 * LANE:3 * LANE]
                             + pltpu.roll(xk, DK_R // 2, 1) * rr_ref[:, 3 * LANE:4 * LANE]).astype(BF)
    ret_ref[2] = mm(_C_VR, _C_GR).astype(BF)
    g = mm(_C_GR, _C_GM)
    gr_ref[...] = (g * _sigmoid(g)).astype(BF)
    gm_ref[...] = _sigmoid(mm(_C_GM, _C_GA)).astype(BF)
    ga_ref[...] = _sigmoid(mm(_C_GA, _C_END))


def _inproj(x2d, nw, sh, sc, w_pack, rq, rk, rr, tm, tab_blocks, per_row_mods, kv_t):
    rows = x2d.shape[0]
    nb = rows // tm
    if kv_t:
        kv_shape = jax.ShapeDtypeStruct((nb // tab_blocks, 256, tab_blocks * tm), F32)
        kv_spec = pl.BlockSpec((None, 256, tm), lambda i: (i // tab_blocks, 0, i % tab_blocks))
    else:
        kv_shape = jax.ShapeDtypeStruct((rows, 256), F32)
        kv_spec = pl.BlockSpec((tm, 256), lambda i: (i, 0))
    if per_row_mods:
        mod_spec = pl.BlockSpec((tm, D_MODEL), lambda i: (i, 0))
    else:
        mod_spec = pl.BlockSpec((None, 1, D_MODEL), lambda i: (i // tab_blocks, 0, 0))
    tab = lambda w: pl.BlockSpec((tm, w), lambda i: (i % tab_blocks, 0))
    row = lambda w: pl.BlockSpec((tm, w), lambda i: (i, 0))
    out_shapes = [
        jax.ShapeDtypeStruct((rows, 512), BF),
        jax.ShapeDtypeStruct((rows, 512), BF),
        kv_shape,
        kv_shape,
        kv_shape,
        jax.ShapeDtypeStruct((2, G_A, 2, rows, LANE), BF),
        jax.ShapeDtypeStruct((3, rows, 512), BF),
        jax.ShapeDtypeStruct((rows, 512), BF),
        jax.ShapeDtypeStruct((rows, 2048), BF),
        jax.ShapeDtypeStruct((rows, 256), F32),
    ]
    out_specs = [row(512), row(512), kv_spec, kv_spec, kv_spec,
                 pl.BlockSpec((2, G_A, 2, tm, LANE), lambda i: (0, 0, 0, i, 0)),
                 pl.BlockSpec((3, tm, 512), lambda i: (0, i, 0)),
                 row(512), row(2048), row(256)]
    return pl.pallas_call(
        functools.partial(_inproj_body, kv_t=kv_t),
        grid=(nb,),
        in_specs=[row(D_MODEL),
                  pl.BlockSpec((1, D_MODEL), lambda i: (0, 0)),
                  mod_spec, mod_spec,
                  pl.BlockSpec((D_MODEL, _C_END), lambda i: (0, 0), pipeline_mode=pl.Buffered(1)),
                  tab(384), tab(384), tab(512)],
        out_specs=out_specs,
        out_shape=out_shapes,
        compiler_params=_cparams(("arbitrary",)),
        name="inproj",
    )(x2d, nw, sh, sc, w_pack, rq, rk, rr)


_BPP = 4
_PAGE = _BPP * L_CMP


def _pack_phi_paged(phi_pos_k, phi_k1, phi_k2, phi_pos_v, phi_v1, phi_v2):
    def blockdiag(w):
        z = jnp.zeros_like(w)
        return jnp.concatenate(
            [jnp.concatenate([w if j == i else z for j in range(_BPP)], axis=-1) for i in range(_BPP)], axis=-2)
    w1, w2, pos = [], [], []
    for p_, a, b_ in ((phi_pos_k, phi_k1, phi_k2), (phi_pos_v, phi_v1, phi_v2)):
        w1.append(blockdiag(a.astype(BF).reshape(L_CMP, DH, DH).transpose(1, 0, 2)))
        w2.append(blockdiag(b_.astype(BF)))
        pos.append(jnp.tile(p_.T, (1, _BPP)))
    pos = jnp.stack(pos).reshape(2, DH // 2, 2 * _PAGE)
    w1 = jnp.stack(w1).reshape(2, DH // 2, 2 * _PAGE, _BPP * DH)
    return pos, w1, jnp.stack(w2)


def _compress_slab_rows(load, n_rows, pos_ref, w1_ref, w2_ref):
    out = []
    for kv in range(2):
        acc = jnp.zeros((G_A * n_rows, _BPP * DH), F32)
        for dp in range(DH // 2):
            x = jnp.concatenate(
                [jnp.concatenate([load((kv * G_A + g) * DH + 2 * dp + j) for j in range(2)], axis=1)
                 for g in range(G_A)], axis=0)
            acc = acc + _nn((x + pos_ref[kv, dp:dp + 1, :]).astype(BF), w1_ref[kv, dp])
        out.append(_nn(_gelu(acc).astype(BF), w2_ref[kv]))
    return out


def _compress_t_body(src_ref, pos_ref, w1_ref, w2_ref, o_ref, xbuf, sem, *, nseq, n_pages):
    def copy(t):
        b, p = t // n_pages, t % n_pages
        return pltpu.make_async_copy(src_ref.at[b, :, pl.ds(pl.multiple_of(p * _PAGE, _PAGE), _PAGE)],
                                     xbuf.at[:, t, :], sem)

    def start(t, c):
        copy(t).start()
        return c

    def wait(t, c):
        copy(t).wait()
        return c
    lax.fori_loop(0, nseq * n_pages, start, 0)
    lax.fori_loop(0, nseq * n_pages, wait, 0)
    kc, vc = _compress_slab_rows(lambda r: xbuf[r], nseq * n_pages, pos_ref, w1_ref, w2_ref)
    o_ref[0] = kc
    o_ref[1] = vc


def _compress_t(kv_t, pos, w1, w2):
    nseq, SL, T = kv_t.shape
    n_pages = T // _PAGE
    one = lambda a: pl.BlockSpec(a.shape, lambda i: (0,) * a.ndim, pipeline_mode=pl.Buffered(1))
    n_out = G_A * nseq * n_pages
    return pl.pallas_call(
        functools.partial(_compress_t_body, nseq=nseq, n_pages=n_pages),
        grid=(1,),
        in_specs=[pl.BlockSpec(memory_space=pl.ANY), one(pos), one(w1), one(w2)],
        out_specs=pl.BlockSpec((2, n_out, _BPP * DH), lambda i: (0, 0, 0)),
        out_shape=jax.ShapeDtypeStruct((2, n_out, _BPP * DH), F32),
        scratch_shapes=[pltpu.VMEM((SL, nseq * n_pages, _PAGE), F32), pltpu.SemaphoreType.DMA(())],
        compiler_params=_cparams(("arbitrary",)),
        name="compress_t",
    )(kv_t, pos, w1, w2)


_KT = 512


def _nsa_prompt_body(q_ref, qr_ref, kvb_ref, kc_ref, vc_ref, ga_ref, e_ref, o_ref, os_scr, *, nc, ns):
    i = pl.program_id(1)
    QB = Q_BLOCK
    lane = lax.broadcasted_iota(jnp.int32, (QB, LANE), 1)
    lo64 = lane < DH
    zero_b = jnp.zeros((QB, LANE), BF)
    groups = range(G_A)

    def split_heads(ref, g):
        ev, od = [], []
        for c in range(2):
            xc = ref[:, (2 * g + c) * LANE:(2 * g + c + 1) * LANE]
            ev.append(jnp.where(lo64, xc, zero_b))
            od.append(jnp.where(lo64, zero_b, xc))
        return jnp.concatenate(ev, axis=0), jnp.concatenate(od, axis=0)

    tq_l = i * QB + lax.broadcasted_iota(jnp.int32, (nc, QB), 1)
    r_c = lax.broadcasted_iota(jnp.int32, (nc, QB), 0)
    half = nc // 2
    blk_c = jnp.where(r_c < half, 2 * r_c, 2 * (r_c - half) + 1)
    cmask = (blk_c * L_CMP + (L_CMP - 1)) <= tq_l
    cmask_f = cmask.astype(F32)
    tq_s = i * QB + lax.broadcasted_iota(jnp.int32, (ns, QB), 1)
    blk_s = lax.broadcasted_iota(jnp.int32, (ns, QB), 0)
    valid = (blk_s * L_SEL) <= tq_s
    forced = (blk_s == 0) | (blk_s == tq_s // L_SEL)
    n_top = min(N_SEL, ns)

    def select(imp):
        score = jnp.where(valid, imp + jnp.where(forced, FORCE_BONUS, 0.0), NEG)
        rank = jnp.zeros((ns, QB), F32)
        for b2 in range(ns):
            row = score[b2:b2 + 1, :]
            rank = rank + jnp.where(blk_s > b2, jnp.where(row >= score, 1.0, 0.0), jnp.where(row > score, 1.0, 0.0))
        return jnp.where((rank < n_top) & (score > 0.5 * NEG), 1.0, 0.0)

    sel_b, o_cmp = [], []
    for g in groups:
        q_e, q_o = split_heads(q_ref, g)
        kc = kc_ref[g]
        vc = vc_ref[g]
        imp = jnp.zeros((ns, QB), F32)
        oc = {}
        for stack, qs in ((0, q_e), (1, q_o)):
            st = _nt(kc, qs)
            for c in range(2):
                s = jnp.where(cmask, st[:, c * QB:(c + 1) * QB], NEG)
                p = jnp.exp2(s - jnp.max(s, axis=0, keepdims=True)) * cmask_f
                p = p / jnp.maximum(jnp.sum(p, axis=0, keepdims=True), 1e-30)
                imp = imp + p[0:half] + p[half:nc]
                oc[(c, stack)] = _tn(p.astype(BF), vc)
        o_cmp.append(oc)
        sel_b.append(select(imp).astype(BF))

    qr = [split_heads(qr_ref, g) for g in groups]
    n_tiles = (i * QB + QB + _KT - 1) // _KT

    def masked_attend(qs, kmat, vmat, bias2):
        s = _nt(qs, kmat) + bias2
        p = jnp.exp2(s - jnp.max(s, axis=1, keepdims=True))
        return _nn(p.astype(BF), vmat) / jnp.sum(p, axis=1, keepdims=True)

    def slc_variant(nk):
        tq_r = i * QB + lax.broadcasted_iota(jnp.int32, (QB, nk), 0)
        kcol = lax.broadcasted_iota(jnp.int32, (QB, nk), 1)
        causal = kcol <= tq_r
        for g in groups:
            picked = _tn(sel_b[g], e_ref[:, 0:nk])
            bias = jnp.where((picked > 0.5) & causal, 0.0, NEG)
            bias2 = jnp.concatenate([bias, bias], axis=0)
            kv = kvb_ref[0, g, 0, 0:nk, :]
            vk = kvb_ref[0, g, 1, 0:nk, :]
            os_scr[g, 0] = masked_attend(qr[g][0], kv, vk, bias2)
            os_scr[g, 1] = masked_attend(qr[g][1], vk, kv, bias2)

    for k in range(1, kvb_ref.shape[3] // _KT + 1):
        pl.when(n_tiles == k)(functools.partial(slc_variant, k * _KT))

    WK = WINDOW + QB
    start = pl.multiple_of(jnp.maximum(i * QB - WINDOW, 0), QB)
    diff = (i * QB + lax.broadcasted_iota(jnp.int32, (QB, WK), 0)) - (start + lax.broadcasted_iota(jnp.int32, (QB, WK), 1))
    wb = jnp.where((diff >= 0) & (diff < WINDOW), 0.0, NEG)
    wb2 = jnp.concatenate([wb, wb], axis=0)

    for g in groups:
        wkv = kvb_ref[1, g, 0, pl.ds(start, WK), :]
        wvk = kvb_ref[1, g, 1, pl.ds(start, WK), :]
        ow_e = masked_attend(qr[g][0], wkv, wvk, wb2)
        ow_o = masked_attend(qr[g][1], wvk, wkv, wb2)
        os_e = os_scr[g, 0]
        os_o = os_scr[g, 1]
        ga = ga_ref[:, g * LANE:(g + 1) * LANE]
        for c in range(2):
            rows = slice(c * QB, (c + 1) * QB)
            branches = (
                jnp.where(lo64, o_cmp[g][(c, 0)], o_cmp[g][(c, 1)]),
                jnp.where(lo64, os_e[rows], os_o[rows]),
                jnp.where(lo64, ow_e[rows], ow_o[rows]),
            )
            acc = jnp.zeros((QB, LANE), F32)
            for br in range(3):
                col = br * HPG + 2 * c
                gate = jnp.where(lo64, ga[:, col:col + 1], ga[:, col + 1:col + 2])
                acc = acc + gate * branches[br]
            o_ref[:, (2 * g + c) * LANE:(2 * g + c + 1) * LANE] = acc.astype(BF)


def _sel_expand(ns, nkeys):
    e = (np.arange(nkeys)[None, :] // L_SEL) == np.arange(ns)[:, None]
    return jnp.asarray(e, BF)


def _nsa_prompt(q, qr, kvb, kc2, vc2, ga, B, T):
    nqb = T // Q_BLOCK
    nc, ns = T // L_CMP, T // L_SEL
    assert T >= WINDOW + Q_BLOCK and T % _KT == 0
    qspec = pl.BlockSpec((Q_BLOCK, 512), lambda b, i: (b * nqb + i, 0))
    cspec = pl.BlockSpec((None, G_A, nc, LANE), lambda b, i: (b, 0, 0, 0))
    return pl.pallas_call(
        functools.partial(_nsa_prompt_body, nc=nc, ns=ns),
        grid=(B, nqb),
        in_specs=[qspec, qspec,
                  pl.BlockSpec((2, G_A, 2, T, LANE), lambda b, i: (0, 0, 0, b, 0)),
                  cspec, cspec,
                  pl.BlockSpec((Q_BLOCK, G_A * LANE), lambda b, i: (b * nqb + i, 0)),
                  pl.BlockSpec((ns, T), lambda b, i: (0, 0))],
        out_specs=qspec,
        out_shape=jax.ShapeDtypeStruct((B * T, 512), BF),
        scratch_shapes=[pltpu.VMEM((G_A, 2, 2 * Q_BLOCK, LANE), F32)],
        compiler_params=_cparams(("arbitrary", "arbitrary")),
        name="nsa_prompt",
    )(q, qr, kvb, kc2, vc2, ga, _sel_expand(ns, T))


def _ret_body(qkv_ref, gr_ref, s0_ref, dm_ref, dq_ref, wk_ref, gc_ref, gnw_ref, z_ref, sout_ref, s_scr, *, C, sb):
    c = pl.program_id(1)

    @pl.when(c == 0)
    def _():
        s_scr[...] = s0_ref[...]

    for j in range(sb):
        rows = slice(j * C, (j + 1) * C)
        for h in range(H_R):
            sl = slice(h * LANE, (h + 1) * LANE)
            q = qkv_ref[0, rows, sl]
            k = qkv_ref[1, rows, sl]
            v = qkv_ref[2, rows, sl]
            s_old = s_scr[j, h]
            inner = _nt(q, k) * dm_ref[h]
            o = _nn(inner.astype(BF), v) + _nn(q, s_old.astype(BF)) * dq_ref[h]
            kw = (k.astype(F32) * wk_ref[h]).astype(BF)
            s_new = gc_ref[h, 0:1, :] * s_old + _tn(kw, v)
            s_scr[j, h] = s_new
            sout_ref[j, h] = s_new
            mu = jnp.mean(o, axis=-1, keepdims=True)
            d = o - mu
            var = jnp.mean(d * d, axis=-1, keepdims=True)
            on = d * lax.rsqrt(var + EPS) * gnw_ref[:, sl]
            z_ref[rows, sl] = (gr_ref[rows, sl].astype(F32) * on).astype(BF)


def _retention(ret3, gr, s0, gnw, nseq, rows_per_seq, C, c_true, sb):
    nC = rows_per_seq // C
    assert sb == 1 or nC == 1
    rows = nseq * rows_per_seq
    dm, dq, wk, gc = _ret_decay(C, c_true)
    full = lambda a: pl.BlockSpec(a.shape, lambda b, c: (0,) * a.ndim)
    return pl.pallas_call(
        functools.partial(_ret_body, C=C, sb=sb),
        grid=(nseq // sb, nC),
        in_specs=[pl.BlockSpec((3, sb * C, H_R * LANE), lambda b, c: (0, b * nC + c, 0)),
                  pl.BlockSpec((sb * C, H_R * LANE), lambda b, c: (b * nC + c, 0)),
                  pl.BlockSpec((sb, H_R, DK_R, DV_R), lambda b, c: (b, 0, 0, 0)),
                  full(dm), full(dq), full(wk), full(gc), full(gnw)],
        out_specs=[pl.BlockSpec((sb * C, H_R * LANE), lambda b, c: (b * nC + c, 0)),
                   pl.BlockSpec((sb, H_R, DK_R, DV_R), lambda b, c: (b, 0, 0, 0))],
        out_shape=[jax.ShapeDtypeStruct((rows, 512), BF),
                   jax.ShapeDtypeStruct((nseq, H_R, DK_R, DV_R), F32)],
        scratch_shapes=[pltpu.VMEM((sb, H_R, DK_R, DV_R), F32)],
        compiler_params=_cparams(("arbitrary", "arbitrary")),
        name="retention",
    )(ret3, gr, s0, dm, dq, wk, gc, gnw)


def _mix_ffn_body(x_ref, oa_ref, zr_ref, gm_ref, gt1_ref, wa_ref, wr_ref, wo_ref,
                  nw_ref, sh_ref, sc_ref, gt_ref, wu_ref, cw_ref, cb_ref, wd_ref, nf_ref, p1_ref, p2_ref,
                  y_ref, a_ref, carry, *, blocks_per_seq, seq_rows):
    i = pl.program_id(0)
    ya = _nn(oa_ref[...], wa_ref[...])
    yr = _nn(zr_ref[...], wr_ref[...])
    gm = gm_ref[...].astype(F32)
    merged = gm[:, 0:D_MODEL] * ya + gm[:, D_MODEL:2 * D_MODEL] * yr
    x = x_ref[...] + gt1_ref[...] * _nn(merged.astype(BF), wo_ref[...])
    tm = x.shape[0]
    ms = jnp.mean(x * x, axis=-1, keepdims=True)
    h = (x * lax.rsqrt(ms + EPS)) * nw_ref[...]
    h = (h * (1.0 + sc_ref[...]) + sh_ref[...]).astype(BF)
    a = _nn(h, wu_ref[:, 0:D_FF])
    b = _nn(h, wu_ref[:, D_FF:2 * D_FF])
    a_ref[...] = a[tm - a_ref.shape[0]:tm, :]
    rid = lax.broadcasted_iota(jnp.int32, (tm, D_FF), 0)
    if seq_rows is None:
        first = (i % blocks_per_seq) == 0
        prev = jnp.where(first, p1_ref[...], carry[...])
        carry[...] = a[tm - 8:tm, :]
        am1 = jnp.where(rid == 0, prev[7:8, :], pltpu.roll(a, 1, 0))
        am2 = jnp.where(rid == 0, prev[6:7, :], jnp.where(rid == 1, prev[7:8, :], pltpu.roll(a, 2, 0)))
    else:
        s = rid % seq_rows
        am1 = jnp.where(s == 0, p1_ref[...], pltpu.roll(a, 1, 0))
        am2 = jnp.where(s <= 1, p2_ref[...], pltpu.roll(a, 2, 0))
    u = cb_ref[...] + am2 * cw_ref[0:1, :] + am1 * cw_ref[1:2, :] + a * cw_ref[2:3, :]
    ff = _nn((_gelu(u) * b).astype(BF), wd_ref[...])
    x2 = x + gt_ref[...] * ff
    ms2 = jnp.mean(x2 * x2, axis=-1, keepdims=True)
    y_ref[...] = (x2 * lax.rsqrt(ms2 + EPS)) * nf_ref[...]


def _mix_ffn(x2d, oa, zr, gm, gt1, wa, wr, wo, nw, sh, sc, gt, wu, cw, cb, wd, nf, p1, p2,
             tm, blocks_per_seq, per_row_mods, seq_rows):
    rows = x2d.shape[0]
    a_rows = tm if seq_rows is not None else 8
    if per_row_mods:
        mod_spec = pl.BlockSpec((tm, D_MODEL), lambda i: (i, 0))
    else:
        mod_spec = pl.BlockSpec((None, 1, D_MODEL), lambda i: (i // blocks_per_seq, 0, 0))
    row = lambda w: pl.BlockSpec((tm, w), lambda i: (i, 0))
    full = lambda a, b: pl.BlockSpec((a, b), lambda i: (0, 0), pipeline_mode=pl.Buffered(1))
    vec = lambda w: pl.BlockSpec((1, w), lambda i: (0, 0))
    pspec = pl.BlockSpec(p1.shape, lambda i: (0, 0))
    return pl.pallas_call(
        functools.partial(_mix_ffn_body, blocks_per_seq=blocks_per_seq, seq_rows=seq_rows),
        grid=(rows // tm,),
        in_specs=[row(D_MODEL), row(512), row(512), row(2048), mod_spec,
                  full(512, D_MODEL), full(512, D_MODEL), full(D_MODEL, D_MODEL),
                  vec(D_MODEL), mod_spec, mod_spec, mod_spec,
                  full(D_MODEL, 2 * D_FF), pl.BlockSpec((CONV_W, D_FF), lambda i: (0, 0)), vec(D_FF),
                  full(D_FF, D_MODEL), vec(D_MODEL), pspec, pspec],
        out_specs=[row(D_MODEL), pl.BlockSpec((a_rows, D_FF), lambda i: (i, 0))],
        out_shape=[jax.ShapeDtypeStruct((rows, D_MODEL), F32),
                   jax.ShapeDtypeStruct((rows // tm * a_rows, D_FF), F32)],
        scratch_shapes=[pltpu.VMEM((8, D_FF), F32)],
        compiler_params=_cparams(("arbitrary",)),
        name="mix_ffn",
    )(x2d, oa, zr, gm, gt1, wa, wr, wo, nw, sh, sc, gt, wu, cw, cb, wd, nf, p1, p2)


def _prompt_path(x_prompt, mods_p, W):
    B, T, _ = x_prompt.shape
    rows = B * T
    x2d = x_prompt.reshape(rows, D_MODEL)
    pos = np.arange(T)
    tm = 256
    sh1, sc1, gt1, sh2, sc2, gt2 = [mods_p[:, None, j * D_MODEL:(j + 1) * D_MODEL] for j in range(6)]
    (q, qr, kvc, kvs, kvw, kvb, ret3, gr, gm, ga) = _inproj(
        x2d, W["norm1"], sh1, sc1, W["w_in"], _rope_tab(pos, Q_SCALE), _rope_tab(pos, 1.0), _ret_tab(pos),
        tm, T // tm, False, True)
    nc, n_pages = T // L_CMP, T // _PAGE
    comp = _compress_t(kvc, W["phi_posT"], W["phi_w1p"], W["phi_w2p"])
    comp = comp.reshape(2, G_A, B, n_pages, 2, 2, DH).transpose(0, 2, 1, 5, 3, 4, 6).reshape(2, B, G_A, nc, DH)
    comp2 = jnp.concatenate([comp, comp], axis=-1).astype(BF)
    oa = _nsa_prompt(q, qr, kvb, comp2[0], comp2[1], ga, B, T)
    C = 256 if T % 256 == 0 else T
    zr, ret_new = _retention(ret3, gr, jnp.zeros((B, H_R, DK_R, DV_R), F32), W["gnw"], B, T, C, C, 1)
    zeros8 = jnp.zeros((8, D_FF), F32)
    y, a_up = _mix_ffn(x2d, oa, zr, gm, gt1, W["w_up_a"], W["w_up_r"], W["w_out"],
                       W["norm2"], sh2, sc2, gt2, W["w_ffn_up"], W["conv_w"], W["conv_b"], W["w_ffn_down"],
                       W["normf"], zeros8, zeros8, tm, T // tm, False, None)
    wsz = min(WINDOW, T)
    rows_major = lambda t: t.reshape(B, 2, G_A, DH, t.shape[-1]).transpose(0, 4, 1, 2, 3)[None]
    outs = dict(
        y=y.reshape(B, T, D_MODEL),
        cmp=rows_major(kvc), slc=rows_major(kvs), win=rows_major(kvw[:, :, T - wsz:]),
        ret=ret_new[None],
        conv=a_up.reshape(B, T // tm, 8, D_FF)[None, :, T // tm - 1, 8 - (CONV_W - 1):],
    )
    return outs


_QC = 32


def _cmp_paged_body(pt_ref, cache_ref, q_ref, pos_ref, w1_ref, w2_ref, ocmp_ref, topi_ref, xbuf, sem,
                    *, n_pages, n_pick, n_q):
    b = pl.program_id(0)
    nb = pl.num_programs(0)
    slot = b % 2

    def copy(bb, sl, p):
        return pltpu.make_async_copy(cache_ref.at[pt_ref[bb, p]], xbuf.at[sl, :, p, :], sem.at[sl])

    def issue(bb, sl):
        def f(p, c):
            copy(bb, sl, p).start()
            return c
        lax.fori_loop(0, n_pages, f, 0)

    @pl.when(b == 0)
    def _():
        issue(b, slot)

    @pl.when(b + 1 < nb)
    def _():
        issue(b + 1, 1 - slot)

    def wait(p, c):
        copy(b, slot, p).wait()
        return c
    lax.fori_loop(0, n_pages, wait, 0)

    comp = _compress_slab_rows(lambda r: xbuf[slot, r], n_pages, pos_ref, w1_ref, w2_ref)

    def lane_groups(x, op):
        r = x
        for j in range(1, _BPP):
            r = op(r, pltpu.roll(x, j * _QC, 1))
        return r

    for g in range(G_A):
        kc = comp[0][g * n_pages:(g + 1) * n_pages].astype(BF)
        vc = comp[1][g * n_pages:(g + 1) * n_pages].astype(BF)
        st = _nn(kc, q_ref[g])
        m = lane_groups(jnp.max(st, axis=0, keepdims=True), jnp.maximum)
        p = jnp.exp2(st - m)
        p = p / lane_groups(jnp.sum(p, axis=0, keepdims=True), jnp.add)
        r_full = _tn(p.astype(BF), vc)
        o = r_full[0:_QC, 0:DH]
        for j in range(1, _BPP):
            o = o + r_full[j * _QC:(j + 1) * _QC, j * DH:(j + 1) * DH]
        ocmp_ref[g] = o
        pair = p + pltpu.roll(p, LANE - _QC, 1)
        imp = pair
        for r in range(1, HPG):
            imp = imp + pltpu.roll(pair, LANE - r * n_q, 1)
        sc = jnp.concatenate([imp, pltpu.roll(imp, 2 * _QC, 1)], axis=0)
        row = lax.broadcasted_iota(jnp.int32, (2 * n_pages, LANE), 0)
        blk = jnp.where(row < n_pages, 2 * row, 2 * (row - n_pages) + 1)
        score = sc + jnp.where(blk == 0, FORCE_BONUS, 0.0)
        topi_ref[g] = jnp.zeros((8, LANE), jnp.int32)
        for k in range(n_pick):
            mx = jnp.max(score, axis=0, keepdims=True)
            idx = jnp.min(jnp.where(score == mx, blk, 2 * n_pages), axis=0, keepdims=True)
            topi_ref[g, k:k + 1, :] = idx
            score = jnp.where(blk == idx, -jnp.inf, score)


def _cmp_paged(page_table, cache_t, q_bd, pos, w1, w2, n_q):
    DB, n_pages = page_table.shape
    SL = cache_t.shape[1]
    page_rows = cache_t.shape[2]
    kern = functools.partial(_cmp_paged_body, n_pages=n_pages, n_pick=N_SEL - 1, n_q=n_q)
    one = dict(pipeline_mode=pl.Buffered(1))
    return pl.pallas_call(
        kern,
        grid_spec=pltpu.PrefetchScalarGridSpec(
            num_scalar_prefetch=1, grid=(DB,),
            in_specs=[pl.BlockSpec(memory_space=pl.ANY),
                      pl.BlockSpec((None, G_A, _BPP * DH, LANE), lambda b, pt: (b, 0, 0, 0)),
                      pl.BlockSpec(pos.shape, lambda b, pt: (0, 0, 0), **one),
                      pl.BlockSpec(w1.shape, lambda b, pt: (0, 0, 0, 0), **one),
                      pl.BlockSpec((2, _BPP * DH, _BPP * DH), lambda b, pt: (0, 0, 0), **one)],
            out_specs=[pl.BlockSpec((None, G_A, _QC, DH), lambda b, pt: (b, 0, 0, 0)),
                       pl.BlockSpec((None, G_A, 8, LANE), lambda b, pt: (b, 0, 0, 0))],
            scratch_shapes=[pltpu.VMEM((2, SL, n_pages, page_rows), F32),
                            pltpu.SemaphoreType.DMA((2,))]),
        out_shape=[jax.ShapeDtypeStruct((DB, G_A, _QC, DH), F32),
                   jax.ShapeDtypeStruct((DB, G_A, 8, LANE), jnp.int32)],
        compiler_params=_cparams(("arbitrary",)),
        name="cmp_paged",
    )(page_table, cache_t, q_bd, pos, w1, w2)


def _slc_win_paged_body(pt_ref, ti_ref, cslc_ref, win_ref, q_ref, tiv_ref, ns_ref, nw_ref, ex_ref,
                        oslc_ref, owin_ref, kbuf, sem, *, n_q, n_pick, page_rows):
    b = pl.program_id(0)
    nb = pl.num_programs(0)
    slot = b % 2
    n_slab = n_q * n_pick
    bpp = page_rows // L_SEL
    wb = win_ref.shape[1]

    def copies(bb, sl):
        cps = []
        for g in range(G_A):
            for j in range(n_slab):
                page = pt_ref[bb, ti_ref[bb, g * n_slab + j] // bpp]
                for kv in range(2):
                    cps.append(pltpu.make_async_copy(
                        cslc_ref.at[page, pl.ds((kv * G_A + g) * DH, DH), :],
                        kbuf.at[sl, g, kv, :, pl.ds(j * page_rows, page_rows)], sem.at[sl]))
        return cps

    @pl.when(b == 0)
    def _():
        for cp in copies(b, slot):
            cp.start()

    @pl.when(b + 1 < nb)
    def _():
        for cp in copies(b + 1, 1 - slot):
            cp.start()

    for cp in copies(b, slot):
        cp.wait()

    nq_rows = q_ref.shape[1]
    nk = n_slab * page_rows
    row_q = lax.broadcasted_iota(jnp.int32, (nq_rows, nk), 0) % n_q
    col = lax.broadcasted_iota(jnp.int32, (nq_rows, nk), 1)
    own = row_q == col // (n_pick * page_rows)
    half = ((col % page_rows) // L_SEL).astype(F32)
    nnew = ns_ref.shape[2]
    new_ok = (lax.broadcasted_iota(jnp.int32, (nq_rows, nnew), 1)
              <= lax.broadcasted_iota(jnp.int32, (nq_rows, nnew), 0) % n_q)
    dwin = (wb + lax.broadcasted_iota(jnp.int32, (nq_rows, wb), 0) % n_q
            - lax.broadcasted_iota(jnp.int32, (nq_rows, wb), 1))
    win_ok = (dwin >= 0) & (dwin < WINDOW)

    def attend(q, kt_old, vt_old, ok_old, k_new, v_new):
        s_o = jnp.where(ok_old, _nn(q, kt_old), NEG)
        s_n = jnp.where(new_ok, _nt(q, k_new), NEG)
        m = jnp.maximum(jnp.max(s_o, axis=1, keepdims=True), jnp.max(s_n, axis=1, keepdims=True))
        p_o = jnp.exp2(s_o - m)
        p_n = jnp.exp2(s_n - m)
        den = jnp.sum(p_o, axis=1, keepdims=True) + jnp.sum(p_n, axis=1, keepdims=True)
        return (_nt(p_o.astype(BF), vt_old) + _nn(p_n.astype(BF), v_new)) / den

    for g in range(G_A):
        q = q_ref[g]
        par = (tiv_ref[g] % bpp).astype(F32).astype(BF)
        want = _nn(par, ex_ref[...])[0:1, :]
        ok = own & (half == want)
        oslc_ref[g] = attend(q, kbuf[slot, g, 0].astype(BF), kbuf[slot, g, 1].astype(BF), ok,
                             ns_ref[g, 0].astype(BF), ns_ref[g, 1].astype(BF))
        kw = win_ref[pl.ds((0 * G_A + g) * DH, DH), :].astype(BF)
        vw = win_ref[pl.ds((1 * G_A + g) * DH, DH), :].astype(BF)
        owin_ref[g] = attend(q, kw, vw, win_ok, nw_ref[g, 0].astype(BF), nw_ref[g, 1].astype(BF))


def _slc_win_paged(page_table, topi_flat, topi_vec, cache_t, win_t, q_rot, new_slc, new_win, n_q):
    DB = page_table.shape[0]
    n_pick = N_SEL - 1
    page_rows = cache_t.shape[2]
    wb = win_t.shape[2]
    nq_rows = q_rot.shape[2]
    n_slab = n_q * n_pick
    ex = (np.arange(n_slab * page_rows)[None, :] // page_rows) == np.arange(LANE)[:, None]
    kern = functools.partial(_slc_win_paged_body, n_q=n_q, n_pick=n_pick, page_rows=page_rows)
    bspec = lambda shp: pl.BlockSpec((None,) + shp, lambda b, pt, ti: (b,) + (0,) * len(shp))
    return pl.pallas_call(
        kern,
        grid_spec=pltpu.PrefetchScalarGridSpec(
            num_scalar_prefetch=2, grid=(DB,),
            in_specs=[pl.BlockSpec(memory_space=pl.ANY), bspec((2 * G_A * DH, wb)),
                      bspec((G_A, nq_rows, DH)), bspec((G_A, 16, LANE)),
                      bspec((G_A, 2, 16, DH)), bspec((G_A, 2, 16, DH)),
                      pl.BlockSpec((LANE, n_slab * page_rows), lambda b, pt, ti: (0, 0))],
            out_specs=[bspec((G_A, nq_rows, DH)), bspec((G_A, nq_rows, DH))],
            scratch_shapes=[pltpu.VMEM((2, G_A, 2, DH, n_slab * page_rows), F32),
                            pltpu.SemaphoreType.DMA((2,))]),
        out_shape=[jax.ShapeDtypeStruct((DB, G_A, nq_rows, DH), F32)] * 2,
        compiler_params=_cparams(("arbitrary",)),
        name="slc_win_paged",
    )(page_table, topi_flat, cache_t, win_t, q_rot, topi_vec, new_slc, new_win, jnp.asarray(ex, BF))


def _gate_sample_body(oc_ref, os_ref, ow_ref, ga_ref, o_ref):
    rows = o_ref.shape[0]
    lo64 = lax.broadcasted_iota(jnp.int32, (rows, LANE), 1) < DH
    for c4 in range(H_A // 2):
        g, c = c4 // 2, c4 % 2
        sl = slice(c4 * LANE, (c4 + 1) * LANE)
        acc = jnp.zeros((rows, LANE), F32)
        for br, ref in enumerate((oc_ref, os_ref, ow_ref)):
            col = g * LANE + br * HPG + 2 * c
            gate = jnp.where(lo64, ga_ref[:, col:col + 1], ga_ref[:, col + 1:col + 2])
            acc = acc + gate * ref[:, sl]
        o_ref[:, sl] = acc.astype(BF)


def _gate_sample(oc, osl, ow, ga):
    rows = oc.shape[0]
    full = lambda w: pl.BlockSpec((rows, w), lambda i: (0, 0))
    return pl.pallas_call(
        _gate_sample_body, grid=(1,),
        in_specs=[full(512), full(512), full(512), full(256)],
        out_specs=full(512),
        out_shape=jax.ShapeDtypeStruct((rows, 512), BF),
        compiler_params=_cparams(("arbitrary",)),
        name="gate_sample",
    )(oc, osl, ow, ga)


def _sample_path(x_sample, mods_s, W, cache_cmp, cache_slc, state_win, state_ret, state_conv, page_table):
    DB, S, _ = x_sample.shape
    rows = DB * S
    page_rows = cache_cmp.shape[1]
    P = page_table.shape[1] * page_rows
    wb = state_win.shape[1]
    assert P % L_SEL == 0 and S < L_CMP and S <= 8 and wb == WINDOW and page_rows % L_SEL == 0
    assert P // L_SEL >= N_SEL and CONV_W == 3 and S >= CONV_W - 1
    pos = P + np.arange(S)
    pos_rows = np.tile(pos, DB)
    x2d = x_sample.reshape(rows, D_MODEL)
    modr = jnp.repeat(mods_s, S, axis=0)
    sh1, sc1, gt1, sh2, sc2, gt2 = [modr[:, j * D_MODEL:(j + 1) * D_MODEL] for j in range(6)]
    (q, qr, kvc, kvs, kvw, _, ret3, gr, gm, ga) = _inproj(
        x2d, W["norm1"], sh1, sc1, W["w_in"], _rope_tab(pos_rows, Q_SCALE), _rope_tab(pos_rows, 1.0),
        _ret_tab(pos_rows), rows, 1, True, False)

    def to_heads(t):
        return t.reshape(DB, S, G_A, HPG, DH).transpose(0, 2, 3, 1, 4).reshape(DB, G_A, HPG * S, DH)

    def from_heads(t):
        return t.reshape(DB, G_A, HPG, S, DH).transpose(0, 3, 1, 2, 4).reshape(rows, H_A * DH)

    assert page_rows == _BPP * L_CMP and HPG * S <= _QC and S * (N_SEL - 1) <= LANE
    slab = lambda t: t.transpose(0, 2, 3, 4, 1).reshape(t.shape[0], 2 * G_A * DH, t.shape[1])
    qt = jnp.pad(to_heads(q).transpose(0, 1, 3, 2), ((0, 0), (0, 0), (0, 0), (0, _QC - HPG * S)))
    eye = jnp.eye(_BPP, dtype=qt.dtype)
    q_bd = (eye[None, None, :, None, :, None] * qt[:, :, None, :, None, :]).reshape(DB, G_A, _BPP * DH, _BPP * _QC)
    o_cmp, topi = _cmp_paged(page_table, slab(cache_cmp), q_bd, W["phi_posT"], W["phi_w1p"], W["phi_w2p"], S)
    n_pick = N_SEL - 1
    topi = topi[:, :, :n_pick, :S].transpose(0, 1, 3, 2).reshape(DB, G_A, S * n_pick)
    topi_vec = jnp.broadcast_to(jnp.pad(topi, ((0, 0), (0, 0), (0, LANE - S * n_pick)))[:, :, None, :],
                                (DB, G_A, 16, LANE))

    def new_rows(t):
        t = t.reshape(DB, S, 2, G_A, DH).transpose(0, 3, 2, 1, 4)
        return jnp.pad(t, ((0, 0), (0, 0), (0, 0), (0, 16 - S), (0, 0)))

    o_slc, o_win = _slc_win_paged(page_table, topi.reshape(DB, G_A * S * n_pick), topi_vec, slab(cache_slc),
                                  slab(state_win), to_heads(qr), new_rows(kvs), new_rows(kvw), S)
    oa = _gate_sample(from_heads(o_cmp[:, :, :HPG * S]), from_heads(o_slc), from_heads(o_win), ga)

    RP = 16
    padr = lambda t: jnp.pad(t.reshape(t.shape[:-2] + (DB, S, 512)),
                             ((0, 0),) * (t.ndim - 1) + ((0, RP - S), (0, 0))).reshape(t.shape[:-2] + (DB * RP, 512))
    zr, ret_new = _retention(padr(ret3), padr(gr), state_ret, W["gnw"], DB, RP, RP, S, 8 if DB % 8 == 0 else 1)
    zr = zr.reshape(DB, RP, 512)[:, :S].reshape(rows, 512)
    zs = jnp.zeros((DB, S, D_FF), F32)
    p1 = zs.at[:, 0].set(state_conv[:, 1]).reshape(rows, D_FF)
    p2 = zs.at[:, 0].set(state_conv[:, 0]).at[:, 1].set(state_conv[:, 1]).reshape(rows, D_FF)
    y, a_up = _mix_ffn(x2d, oa, zr, gm, gt1, W["w_up_a"], W["w_up_r"], W["w_out"],
                       W["norm2"], sh2, sc2, gt2, W["w_ffn_up"], W["conv_w"], W["conv_b"], W["w_ffn_down"],
                       W["normf"], p1, p2, rows, 1, True, S)
    shp = (1, DB, S, 2, G_A, DH)
    return dict(
        y=y.reshape(DB, S, D_MODEL),
        cmp=kvc.reshape(shp), slc=kvs.reshape(shp),
        win=jnp.concatenate([state_win[:, S:], kvw.reshape(DB, S, 2, G_A, DH)], axis=1)[None],
        ret=ret_new[None],
        conv=a_up.reshape(DB, S, D_FF)[None, :, S - (CONV_W - 1):],
    )


def kernel(x_prompt, x_sample, cache_cmp_kv, cache_slc_kv, state_win_kv, state_ret, state_conv, page_table,
           c_prompt, c_sample, norm1_w, norm2_w, w_ada, b_ada, w_in, phi_pos_k, phi_k1, phi_k2, phi_pos_v,
           phi_v1, phi_v2, w_up_a, ret_gn_w, w_up_r, w_out, w_ffn_up, ffn_conv_w, ffn_conv_b, w_ffn_down, normf_w):
    B = x_prompt.shape[0]
    l = 0
    W = dict(
        norm1=norm1_w[l].reshape(1, D_MODEL), norm2=norm2_w[l].reshape(1, D_MODEL), normf=normf_w.reshape(1, D_MODEL),
        w_in=_pack_w_in(w_in[l]),
        w_up_a=w_up_a[l].astype(BF), w_up_r=w_up_r[l].astype(BF), w_out=w_out[l].astype(BF),
        gnw=ret_gn_w[l].reshape(1, H_R * DV_R),
        w_ffn_up=w_ffn_up[l].astype(BF), conv_w=ffn_conv_w[l], conv_b=ffn_conv_b[l].reshape(1, D_FF),
        w_ffn_down=w_ffn_down[l].astype(BF),
    )
    W["phi_posT"], W["phi_w1p"], W["phi_w2p"] = _pack_phi_paged(
        phi_pos_k[l], phi_k1[l], phi_k2[l], phi_pos_v[l], phi_v1[l], phi_v2[l])
    mods = _mods(jnp.concatenate([c_prompt, c_sample], axis=0), w_ada[l], b_ada[l])
    p = _prompt_path(x_prompt, mods[:B], W)
    s = _sample_path(x_sample, mods[B:], W, cache_cmp_kv[l], cache_slc_kv[l], state_win_kv[l], state_ret[l],
                     state_conv[l], page_table)
    return (p["y"], s["y"], p["cmp"], s["cmp"], p["slc"], s["slc"], p["win"], s["win"],
            p["ret"], s["ret"], p["conv"], s["conv"])
```

```python
import functools

import numpy as np
import jax
import jax.numpy as jnp
from jax import lax
from jax.experimental import pallas as pl
from jax.experimental.pallas import tpu as pltpu

BF = jnp.bfloat16
F32 = jnp.float32

D_MODEL = 1024
H_A, G_A, DH = 8, 2, 64
HPG = H_A // G_A
ROPE_DIM = DH // 4
ROPE_THETA = 500000.0
L_CMP, L_SEL, N_SEL = 32, 64, 8
WINDOW = 512
Q_BLOCK = 128
FORCE_BONUS = 1e4
H_R, DK_R, DV_R = 4, 128, 128
RET_THETA = 10000.0
D_FF = 2816
CONV_W = 3
EPS = 1e-6
NEG = -1e30
MASK_BIG = 2.0 ** 100
Q_SCALE = DH ** -0.5 * 1.4426950408889634
LANE = 128
VMEM_LIMIT = 56 * 1024 * 1024


def _cparams(sem):
    return pltpu.CompilerParams(dimension_semantics=sem, vmem_limit_bytes=VMEM_LIMIT)


def _sigmoid(x):
    return 1.0 / (1.0 + jnp.exp(-x))


def _gelu(x):
    return 0.5 * x * (1.0 + jnp.tanh(0.7978845608028654 * (x + 0.044715 * (x * x * x))))


def _nt(a, b):
    return lax.dot_general(a, b, (((1,), (1,)), ((), ())), preferred_element_type=F32)


def _tn(a, b):
    return lax.dot_general(a, b, (((0,), (0,)), ((), ())), preferred_element_type=F32)


def _nn(a, b):
    return jnp.dot(a, b, preferred_element_type=F32)


def _rope_tab(pos, scale):
    half = ROPE_DIM // 2
    inv = ROPE_THETA ** (-np.arange(half, dtype=np.float64) * (2.0 / ROPE_DIM))
    ang = pos.astype(np.float64)[:, None] * inv
    cos, sin = np.cos(ang), np.sin(ang)
    n = pos.shape[0]
    c = np.ones((n, DH)); s_lo = np.zeros((n, DH)); s_hi = np.zeros((n, DH))
    c[:, :half] = cos; c[:, half:ROPE_DIM] = cos
    s_lo[:, half:ROPE_DIM] = sin
    s_hi[:, :half] = -sin
    tab = np.concatenate([np.tile(t, (1, 2)) for t in (c, s_lo, s_hi)], axis=1) * scale
    return jnp.asarray(tab, F32)


def _ret_tab(pos):
    half = DK_R // 2
    inv = RET_THETA ** (-np.arange(half, dtype=np.float64) * (2.0 / DK_R))
    ang = pos.astype(np.float64)[:, None] * inv
    cos, sin = np.cos(ang), np.sin(ang)
    c = np.concatenate([cos, cos], axis=1)
    s = np.concatenate([-sin, sin], axis=1)
    ks = DK_R ** -0.5
    return jnp.asarray(np.concatenate([c, s, c * ks, s * ks], axis=1), F32)


def _ret_decay(C, c_true):
    h = np.arange(H_R, dtype=np.float64)
    log_g = np.log1p(-np.exp2(-5.0 - h))
    i = np.arange(C, dtype=np.float64)
    diff = i[:, None] - i[None, :]
    dm = np.where(diff >= 0, np.exp(log_g[:, None, None] * np.maximum(diff, 0.0)), 0.0)
    dq = np.exp(log_g[:, None] * (i + 1.0))[:, :, None] * np.ones((1, 1, LANE))
    wk = np.exp(log_g[:, None] * (c_true - 1.0 - i))[:, :, None] * np.ones((1, 1, LANE))
    wk = np.where(i[None, :, None] < c_true, wk, 0.0)
    gc = np.exp(log_g * c_true)[:, None, None] * np.ones((1, 8, LANE))
    return (jnp.asarray(dm, F32), jnp.asarray(dq, F32), jnp.asarray(wk, F32), jnp.asarray(gc, F32))


def _mods_body(c_ref, w_ref, b_ref, o_ref):
    c = c_ref[...]
    s = c * _sigmoid(c)
    o_ref[...] = _nn(s.astype(BF), w_ref[...].astype(BF)) + b_ref[...]


def _mods(c_all, w_ada, b_ada):
    n = c_all.shape[0]
    nout = w_ada.shape[1]
    tn = 1024
    return pl.pallas_call(
        _mods_body,
        grid=(nout // tn,),
        in_specs=[pl.BlockSpec((n, D_MODEL), lambda j: (0, 0)),
                  pl.BlockSpec((D_MODEL, tn), lambda j: (0, j)),
                  pl.BlockSpec((1, tn), lambda j: (0, j))],
        out_specs=pl.BlockSpec((n, tn), lambda j: (0, j)),
        out_shape=jax.ShapeDtypeStruct((n, nout), F32),
        compiler_params=_cparams(("arbitrary",)),
        name="mods",
    )(c_all, w_ada, b_ada.reshape(1, nout))


_C_Q, _C_KV, _C_QR, _C_KR, _C_VR, _C_GR, _C_GM, _C_GA, _C_END = (
    0, 512, 1280, 1792, 2304, 2816, 3328, 5376, 5632)


def _pack_w_in(w_in):
    o = np.cumsum((0, 512, 768, 24, 512, 512, 512, 512, 2048))
    q, kv, ga, qr, kr, vr, gr, gm = [w_in[:, o[i]:o[i + 1]] for i in range(8)]
    ga = ga.reshape(D_MODEL, 3, G_A, HPG).transpose(0, 2, 1, 3).reshape(D_MODEL, G_A, 3 * HPG)
    ga = jnp.pad(ga, ((0, 0), (0, 0), (0, LANE - 3 * HPG))).reshape(D_MODEL, G_A * LANE)
    return jnp.concatenate([q, kv, qr, kr, vr, gr, gm, ga], axis=1).astype(BF)


def _inproj_body(x_ref, nw_ref, sh_ref, sc_ref, w_ref, rq_ref, rk_ref, rr_ref,
                 q_ref, qr_ref, kvc_ref, kvs_ref, kvw_ref, kvb_ref, ret_ref, gr_ref, gm_ref, ga_ref, *, kv_t):
    def put_kv(out_ref, k, v):
        if kv_t:
            out_ref[0:LANE, :] = k.T
            out_ref[LANE:2 * LANE, :] = v.T
        else:
            out_ref[:, 0:LANE] = k
            out_ref[:, LANE:2 * LANE] = v

    x = x_ref[...]
    tm = x.shape[0]
    ms = jnp.mean(x * x, axis=-1, keepdims=True)
    h = (x * lax.rsqrt(ms + EPS)) * nw_ref[...]
    h = h * (1.0 + sc_ref[...]) + sh_ref[...]
    hb = h.astype(BF)
    lo64 = lax.broadcasted_iota(jnp.int32, (tm, LANE), 1) < DH

    def mm(lo, hi):
        return _nn(hb, w_ref[:, lo:hi])

    def rope(xc, tab_ref):
        return (xc * tab_ref[:, 0:LANE] + pltpu.roll(xc, 8, 1) * tab_ref[:, LANE:2 * LANE]
                + pltpu.roll(xc, LANE - 8, 1) * tab_ref[:, 2 * LANE:3 * LANE])

    qa = mm(_C_Q, _C_KV)
    q_ref[...] = (qa * Q_SCALE).astype(BF)
    for c in range(4):
        qr_ref[:, c * LANE:(c + 1) * LANE] = rope(qa[:, c * LANE:(c + 1) * LANE], rq_ref).astype(BF)

    kv = mm(_C_KV, _C_QR)
    put_kv(kvc_ref, kv[:, 0:LANE], kv[:, LANE:2 * LANE])
    for kind, out_ref in ((0, kvs_ref), (1, kvw_ref)):
        base = 256 + kind * 256
        k = rope(kv[:, base:base + LANE], rk_ref)
        v = kv[:, base + LANE:base + 2 * LANE]
        put_kv(out_ref, k, v)
        kr_ = pltpu.roll(k, DH, 1)
        vr_ = pltpu.roll(v, DH, 1)
        kvb_ref[kind, 0, 0] = jnp.where(lo64, k, vr_).astype(BF)
        kvb_ref[kind, 0, 1] = jnp.where(lo64, v, kr_).astype(BF)
        kvb_ref[kind, 1, 0] = jnp.where(lo64, kr_, v).astype(BF)
        kvb_ref[kind, 1, 1] = jnp.where(lo64, vr_, k).astype(BF)

    qr = mm(_C_QR, _C_KR)
    kr = mm(_C_KR, _C_VR)
    for hh in range(H_R):
        sl = slice(hh * LANE, (hh + 1) * LANE)
        xq = qr[:, sl]
        ret_ref[0, :, sl] = (xq * rr_ref[:, 0:LANE] + pltpu.roll(xq, DK_R // 2, 1) * rr_ref[:, LANE:2 * LANE]).astype(BF)
        xk = kr[:, sl]
        ret_ref[1, :, sl] = (xk * rr_ref[:, 2 * LANE:3 * LANE]
                             + pltpu.roll(xk, DK_R // 2, 1) * rr_ref[:, 3 * LANE:4 * LANE]).astype(BF)
    ret_ref[2] = mm(_C_VR, _C_GR).astype(BF)
    g = mm(_C_GR, _C_GM)
    gr_ref[...] = (g * _sigmoid(g)).astype(BF)
    gm_ref[...] = _sigmoid(mm(_C_GM, _C_GA)).astype(BF)
    ga_ref[...] = _sigmoid(mm(_C_GA, _C_END))


def _inproj(x2d, nw, sh, sc, w_pack, rq, rk, rr, tm, tab_blocks, per_row_mods, kv_t):
    rows = x2d.shape[0]
    nb = rows // tm
    if kv_t:
        kv_shape = jax.ShapeDtypeStruct((nb // tab_blocks, 256, tab_blocks * tm), F32)
        kv_spec = pl.BlockSpec((None, 256, tm), lambda i: (i // tab_blocks, 0, i % tab_blocks))
    else:
        kv_shape = jax.ShapeDtypeStruct((rows, 256), F32)
        kv_spec = pl.BlockSpec((tm, 256), lambda i: (i, 0))
    if per_row_mods:
        mod_spec = pl.BlockSpec((tm, D_MODEL), lambda i: (i, 0))
    else:
        mod_spec = pl.BlockSpec((None, 1, D_MODEL), lambda i: (i // tab_blocks, 0, 0))
    tab = lambda w: pl.BlockSpec((tm, w), lambda i: (i % tab_blocks, 0))
    row = lambda w: pl.BlockSpec((tm, w), lambda i: (i, 0))
    out_shapes = [
        jax.ShapeDtypeStruct((rows, 512), BF),
        jax.ShapeDtypeStruct((rows, 512), BF),
        kv_shape,
        kv_shape,
        kv_shape,
        jax.ShapeDtypeStruct((2, G_A, 2, rows, LANE), BF),
        jax.ShapeDtypeStruct((3, rows, 512), BF),
        jax.ShapeDtypeStruct((rows, 512), BF),
        jax.ShapeDtypeStruct((rows, 2048), BF),
        jax.ShapeDtypeStruct((rows, 256), F32),
    ]
    out_specs = [row(512), row(512), kv_spec, kv_spec, kv_spec,
                 pl.BlockSpec((2, G_A, 2, tm, LANE), lambda i: (0, 0, 0, i, 0)),
                 pl.BlockSpec((3, tm, 512), lambda i: (0, i, 0)),
                 row(512), row(2048), row(256)]
    return pl.pallas_call(
        functools.partial(_inproj_body, kv_t=kv_t),
        grid=(nb,),
        in_specs=[row(D_MODEL),
                  pl.BlockSpec((1, D_MODEL), lambda i: (0, 0)),
                  mod_spec, mod_spec,
                  pl.BlockSpec((D_MODEL, _C_END), lambda i: (0, 0), pipeline_mode=pl.Buffered(1)),
                  tab(384), tab(384), tab(512)],
        out_specs=out_specs,
        out_shape=out_shapes,
        compiler_params=_cparams(("arbitrary",)),
        name="inproj",
    )(x2d, nw, sh, sc, w_pack, rq, rk, rr)


_BPP = 4
_PAGE = _BPP * L_CMP


def _pack_phi_paged(phi_pos_k, phi_k1, phi_k2, phi_pos_v, phi_v1, phi_v2):
    def blockdiag(w):
        z = jnp.zeros_like(w)
        return jnp.concatenate(
            [jnp.concatenate([w if j == i else z for j in range(_BPP)], axis=-1) for i in range(_BPP)], axis=-2)
    w1, w2, pos = [], [], []
    for p_, a, b_ in ((phi_pos_k, phi_k1, phi_k2), (phi_pos_v, phi_v1, phi_v2)):
        w1.append(blockdiag(a.astype(BF).reshape(L_CMP, DH, DH).transpose(1, 0, 2)))
        w2.append(blockdiag(b_.astype(BF)))
        pos.append(jnp.tile(p_.T, (1, _BPP)))
    pos = jnp.stack(pos).reshape(2, DH // 2, 2 * _PAGE)
    w1 = jnp.stack(w1).reshape(2, DH // 2, 2 * _PAGE, _BPP * DH)
    return pos, w1, jnp.stack(w2)


def _compress_slab_rows(load, n_rows, pos_ref, w1_ref, w2_ref):
    out = []
    for kv in range(2):
        acc = jnp.zeros((G_A * n_rows, _BPP * DH), F32)
        for dp in range(DH // 2):
            x = jnp.concatenate(
                [jnp.concatenate([load((kv * G_A + g) * DH + 2 * dp + j) for j in range(2)], axis=1)
                 for g in range(G_A)], axis=0)
            acc = acc + _nn((x + pos_ref[kv, dp:dp + 1, :]).astype(BF), w1_ref[kv, dp])
        out.append(_nn(_gelu(acc).astype(BF), w2_ref[kv]))
    return out


def _compress_t_body(src_ref, pos_ref, w1_ref, w2_ref, o_ref, xbuf, sem, *, nseq, n_pages):
    def copy(t):
        b, p = t // n_pages, t % n_pages
        return pltpu.make_async_copy(src_ref.at[b, :, pl.ds(pl.multiple_of(p * _PAGE, _PAGE), _PAGE)],
                                     xbuf.at[:, t, :], sem)

    def start(t, c):
        copy(t).start()
        return c

    def wait(t, c):
        copy(t).wait()
        return c
    lax.fori_loop(0, nseq * n_pages, start, 0)
    lax.fori_loop(0, nseq * n_pages, wait, 0)
    kc, vc = _compress_slab_rows(lambda r: xbuf[r], nseq * n_pages, pos_ref, w1_ref, w2_ref)
    o_ref[0] = kc
    o_ref[1] = vc


def _compress_t(kv_t, pos, w1, w2):
    nseq, SL, T = kv_t.shape
    n_pages = T // _PAGE
    one = lambda a: pl.BlockSpec(a.shape, lambda i: (0,) * a.ndim, pipeline_mode=pl.Buffered(1))
    n_out = G_A * nseq * n_pages
    return pl.pallas_call(
        functools.partial(_compress_t_body, nseq=nseq, n_pages=n_pages),
        grid=(1,),
        in_specs=[pl.BlockSpec(memory_space=pl.ANY), one(pos), one(w1), one(w2)],
        out_specs=pl.BlockSpec((2, n_out, _BPP * DH), lambda i: (0, 0, 0)),
        out_shape=jax.ShapeDtypeStruct((2, n_out, _BPP * DH), F32),
        scratch_shapes=[pltpu.VMEM((SL, nseq * n_pages, _PAGE), F32), pltpu.SemaphoreType.DMA(())],
        compiler_params=_cparams(("arbitrary",)),
        name="compress_t",
    )(kv_t, pos, w1, w2)


_KT = 512


def _nsa_prompt_body(q_ref, qr_ref, kvb_ref, kc_ref, vc_ref, ga_ref, e_ref, o_ref, os_scr, *, nc, ns):
    i = pl.program_id(1)
    QB = Q_BLOCK
    lane = lax.broadcasted_iota(jnp.int32, (QB, LANE), 1)
    lo64 = lane < DH
    zero_b = jnp.zeros((QB, LANE), BF)
    groups = range(G_A)

    def split_heads(ref, g):
        ev, od = [], []
        for c in range(2):
            xc = ref[:, (2 * g + c) * LANE:(2 * g + c + 1) * LANE]
            ev.append(jnp.where(lo64, xc, zero_b))
            od.append(jnp.where(lo64, zero_b, xc))
        return jnp.concatenate(ev, axis=0), jnp.concatenate(od, axis=0)

    tq_l = i * QB + lax.broadcasted_iota(jnp.int32, (nc, QB), 1)
    r_c = lax.broadcasted_iota(jnp.int32, (nc, QB), 0)
    half = nc // 2
    blk_c = jnp.where(r_c < half, 2 * r_c, 2 * (r_c - half) + 1)
    cmask = (blk_c * L_CMP + (L_CMP - 1)) <= tq_l
    cmask_f = cmask.astype(F32)
    tq_s = i * QB + lax.broadcasted_iota(jnp.int32, (ns, QB), 1)
    blk_s = lax.broadcasted_iota(jnp.int32, (ns, QB), 0)
    valid = (blk_s * L_SEL) <= tq_s
    forced = (blk_s == 0) | (blk_s == tq_s // L_SEL)
    n_top = min(N_SEL, ns)

    def select(imp):
        score = jnp.where(valid, imp + jnp.where(forced, FORCE_BONUS, 0.0), NEG)
        rank = jnp.zeros((ns, QB), F32)
        for b2 in range(ns):
            row = score[b2:b2 + 1, :]
            rank = rank + jnp.where(blk_s > b2, jnp.where(row >= score, 1.0, 0.0), jnp.where(row > score, 1.0, 0.0))
        return jnp.where((rank < n_top) & (score > 0.5 * NEG), 1.0, 0.0)

    sel_b, o_cmp = [], []
    for g in groups:
        q_e, q_o = split_heads(q_ref, g)
        kc = kc_ref[g]
        vc = vc_ref[g]
        imp = jnp.zeros((ns, QB), F32)
        oc = {}
        for stack, qs in ((0, q_e), (1, q_o)):
            st = _nt(kc, qs)
            for c in range(2):
                s = jnp.where(cmask, st[:, c * QB:(c + 1) * QB], NEG)
                p = jnp.exp2(s - jnp.max(s, axis=0, keepdims=True)) * cmask_f
                p = p / jnp.maximum(jnp.sum(p, axis=0, keepdims=True), 1e-30)
                imp = imp + p[0:half] + p[half:nc]
                oc[(c, stack)] = _tn(p.astype(BF), vc)
        o_cmp.append(oc)
        sel_b.append(select(imp).astype(BF))

    qr = [split_heads(qr_ref, g) for g in groups]
    n_tiles = (i * QB + QB + _KT - 1) // _KT

    def masked_attend(chains):
        scored = []
        for qs, kmat, _, bias2 in chains:
            s = _nt(qs, kmat) + bias2
            scored.append((s, jnp.max(s, axis=1, keepdims=True)))
        outs = []
        for (s, m), (_, _, vmat, _) in zip(scored, chains):
            p = jnp.exp2(s - m)
            outs.append(_nn(p.astype(BF), vmat) / jnp.sum(p, axis=1, keepdims=True))
        return outs

    def slc_variant(nk):
        tq_r = i * QB + lax.broadcasted_iota(jnp.int32, (QB, _KT), 0)
        causal = (nk - _KT + lax.broadcasted_iota(jnp.int32, (QB, _KT), 1)) <= tq_r
        chains = []
        for g in groups:
            bias = _tn(sel_b[g], e_ref[:, 0:nk]) - MASK_BIG
            tail = jnp.where(causal, bias[:, nk - _KT:], -MASK_BIG)
            bias = tail if nk == _KT else jnp.concatenate([bias[:, :nk - _KT], tail], axis=1)
            bias2 = jnp.concatenate([bias, bias], axis=0)
            kv = kvb_ref[0, g, 0, 0:nk, :]
            vk = kvb_ref[0, g, 1, 0:nk, :]
            chains.append((qr[g][0], kv, vk, bias2))
            chains.append((qr[g][1], vk, kv, bias2))
        for c, o in enumerate(masked_attend(chains)):
            os_scr[c // 2, c % 2] = o

    for k in range(1, kvb_ref.shape[3] // _KT + 1):
        pl.when(n_tiles == k)(functools.partial(slc_variant, k * _KT))

    WK = WINDOW + QB
    start = pl.multiple_of(jnp.maximum(i * QB - WINDOW, 0), QB)
    diff = (i * QB + lax.broadcasted_iota(jnp.int32, (QB, WK), 0)) - (start + lax.broadcasted_iota(jnp.int32, (QB, WK), 1))
    wb = jnp.where((diff >= 0) & (diff < WINDOW), 0.0, NEG)
    wb2 = jnp.concatenate([wb, wb], axis=0)

    chains = []
    for g in groups:
        wkv = kvb_ref[1, g, 0, pl.ds(start, WK), :]
        wvk = kvb_ref[1, g, 1, pl.ds(start, WK), :]
        chains += [(qr[g][0], wkv, wvk, wb2), (qr[g][1], wvk, wkv, wb2)]
    ow = masked_attend(chains)

    for g in groups:
        ow_e, ow_o = ow[2 * g], ow[2 * g + 1]
        os_e = os_scr[g, 0]
        os_o = os_scr[g, 1]
        ga = ga_ref[:, g * LANE:(g + 1) * LANE]
        for c in range(2):
            rows = slice(c * QB, (c + 1) * QB)
            branches = (
                jnp.where(lo64, o_cmp[g][(c, 0)], o_cmp[g][(c, 1)]),
                jnp.where(lo64, os_e[rows], os_o[rows]),
                jnp.where(lo64, ow_e[rows], ow_o[rows]),
            )
            acc = jnp.zeros((QB, LANE), F32)
            for br in range(3):
                col = br * HPG + 2 * c
                gate = jnp.where(lo64, ga[:, col:col + 1], ga[:, col + 1:col + 2])
                acc = acc + gate * branches[br]
            o_ref[:, (2 * g + c) * LANE:(2 * g + c + 1) * LANE] = acc.astype(BF)


def _sel_expand(ns, nkeys):
    e = (np.arange(nkeys)[None, :] // L_SEL) == np.arange(ns)[:, None]
    return jnp.asarray(e * MASK_BIG, BF)


def _nsa_prompt(q, qr, kvb, kc2, vc2, ga, B, T):
    nqb = T // Q_BLOCK
    nc, ns = T // L_CMP, T // L_SEL
    assert T >= WINDOW + Q_BLOCK and T % _KT == 0
    qspec = pl.BlockSpec((Q_BLOCK, 512), lambda b, i: (b * nqb + i, 0))
    cspec = pl.BlockSpec((None, G_A, nc, LANE), lambda b, i: (b, 0, 0, 0))
    return pl.pallas_call(
        functools.partial(_nsa_prompt_body, nc=nc, ns=ns),
        grid=(B, nqb),
        in_specs=[qspec, qspec,
                  pl.BlockSpec((2, G_A, 2, T, LANE), lambda b, i: (0, 0, 0, b, 0)),
                  cspec, cspec,
                  pl.BlockSpec((Q_BLOCK, G_A * LANE), lambda b, i: (b * nqb + i, 0)),
                  pl.BlockSpec((ns, T), lambda b, i: (0, 0))],
        out_specs=qspec,
        out_shape=jax.ShapeDtypeStruct((B * T, 512), BF),
        scratch_shapes=[pltpu.VMEM((G_A, 2, 2 * Q_BLOCK, LANE), F32)],
        compiler_params=_cparams(("arbitrary", "arbitrary")),
        name="nsa_prompt",
    )(q, qr, kvb, kc2, vc2, ga, _sel_expand(ns, T))


def _ret_body(qkv_ref, gr_ref, s0_ref, dm_ref, dq_ref, wk_ref, gc_ref, gnw_ref, z_ref, sout_ref, s_scr, *, C, sb):
    c = pl.program_id(1)

    @pl.when(c == 0)
    def _():
        s_scr[...] = s0_ref[...]

    for j in range(sb):
        rows = slice(j * C, (j + 1) * C)
        for h in range(H_R):
            sl = slice(h * LANE, (h + 1) * LANE)
            q = qkv_ref[0, rows, sl]
            k = qkv_ref[1, rows, sl]
            v = qkv_ref[2, rows, sl]
            s_old = s_scr[j, h]
            inner = _nt(q, k) * dm_ref[h]
            o = _nn(inner.astype(BF), v) + _nn(q, s_old.astype(BF)) * dq_ref[h]
            kw = (k.astype(F32) * wk_ref[h]).astype(BF)
            s_new = gc_ref[h, 0:1, :] * s_old + _tn(kw, v)
            s_scr[j, h] = s_new
            sout_ref[j, h] = s_new
            mu = jnp.mean(o, axis=-1, keepdims=True)
            d = o - mu
            var = jnp.mean(d * d, axis=-1, keepdims=True)
            on = d * lax.rsqrt(var + EPS) * gnw_ref[:, sl]
            z_ref[rows, sl] = (gr_ref[rows, sl].astype(F32) * on).astype(BF)


def _retention(ret3, gr, s0, gnw, nseq, rows_per_seq, C, c_true, sb):
    nC = rows_per_seq // C
    assert sb == 1 or nC == 1
    rows = nseq * rows_per_seq
    dm, dq, wk, gc = _ret_decay(C, c_true)
    full = lambda a: pl.BlockSpec(a.shape, lambda b, c: (0,) * a.ndim)
    return pl.pallas_call(
        functools.partial(_ret_body, C=C, sb=sb),
        grid=(nseq // sb, nC),
        in_specs=[pl.BlockSpec((3, sb * C, H_R * LANE), lambda b, c: (0, b * nC + c, 0)),
                  pl.BlockSpec((sb * C, H_R * LANE), lambda b, c: (b * nC + c, 0)),
                  pl.BlockSpec((sb, H_R, DK_R, DV_R), lambda b, c: (b, 0, 0, 0)),
                  full(dm), full(dq), full(wk), full(gc), full(gnw)],
        out_specs=[pl.BlockSpec((sb * C, H_R * LANE), lambda b, c: (b * nC + c, 0)),
                   pl.BlockSpec((sb, H_R, DK_R, DV_R), lambda b, c: (b, 0, 0, 0))],
        out_shape=[jax.ShapeDtypeStruct((rows, 512), BF),
                   jax.ShapeDtypeStruct((nseq, H_R, DK_R, DV_R), F32)],
        scratch_shapes=[pltpu.VMEM((sb, H_R, DK_R, DV_R), F32)],
        compiler_params=_cparams(("arbitrary", "arbitrary")),
        name="retention",
    )(ret3, gr, s0, dm, dq, wk, gc, gnw)


def _mix_ffn_body(x_ref, oa_ref, zr_ref, gm_ref, gt1_ref, wa_ref, wr_ref, wo_ref,
                  nw_ref, sh_ref, sc_ref, gt_ref, wu_ref, cw_ref, cb_ref, wd_ref, nf_ref, p1_ref, p2_ref,
                  y_ref, a_ref, carry, *, blocks_per_seq, seq_rows):
    i = pl.program_id(0)
    ya = _nn(oa_ref[...], wa_ref[...])
    yr = _nn(zr_ref[...], wr_ref[...])
    gm = gm_ref[...].astype(F32)
    merged = gm[:, 0:D_MODEL] * ya + gm[:, D_MODEL:2 * D_MODEL] * yr
    x = x_ref[...] + gt1_ref[...] * _nn(merged.astype(BF), wo_ref[...])
    tm = x.shape[0]
    ms = jnp.mean(x * x, axis=-1, keepdims=True)
    h = (x * lax.rsqrt(ms + EPS)) * nw_ref[...]
    h = (h * (1.0 + sc_ref[...]) + sh_ref[...]).astype(BF)
    a = _nn(h, wu_ref[:, 0:D_FF])
    b = _nn(h, wu_ref[:, D_FF:2 * D_FF])
    a_ref[...] = a[tm - a_ref.shape[0]:tm, :]
    rid = lax.broadcasted_iota(jnp.int32, (tm, D_FF), 0)
    if seq_rows is None:
        first = (i % blocks_per_seq) == 0
        prev = jnp.where(first, p1_ref[...], carry[...])
        carry[...] = a[tm - 8:tm, :]
        am1 = jnp.where(rid == 0, prev[7:8, :], pltpu.roll(a, 1, 0))
        am2 = jnp.where(rid == 0, prev[6:7, :], jnp.where(rid == 1, prev[7:8, :], pltpu.roll(a, 2, 0)))
    else:
        s = rid % seq_rows
        am1 = jnp.where(s == 0, p1_ref[...], pltpu.roll(a, 1, 0))
        am2 = jnp.where(s <= 1, p2_ref[...], pltpu.roll(a, 2, 0))
    u = cb_ref[...] + am2 * cw_ref[0:1, :] + am1 * cw_ref[1:2, :] + a * cw_ref[2:3, :]
    ff = _nn((_gelu(u) * b).astype(BF), wd_ref[...])
    x2 = x + gt_ref[...] * ff
    ms2 = jnp.mean(x2 * x2, axis=-1, keepdims=True)
    y_ref[...] = (x2 * lax.rsqrt(ms2 + EPS)) * nf_ref[...]


def _mix_ffn(x2d, oa, zr, gm, gt1, wa, wr, wo, nw, sh, sc, gt, wu, cw, cb, wd, nf, p1, p2,
             tm, blocks_per_seq, per_row_mods, seq_rows):
    rows = x2d.shape[0]
    a_rows = tm if seq_rows is not None else 8
    if per_row_mods:
        mod_spec = pl.BlockSpec((tm, D_MODEL), lambda i: (i, 0))
    else:
        mod_spec = pl.BlockSpec((None, 1, D_MODEL), lambda i: (i // blocks_per_seq, 0, 0))
    row = lambda w: pl.BlockSpec((tm, w), lambda i: (i, 0))
    full = lambda a, b: pl.BlockSpec((a, b), lambda i: (0, 0), pipeline_mode=pl.Buffered(1))
    vec = lambda w: pl.BlockSpec((1, w), lambda i: (0, 0))
    pspec = pl.BlockSpec(p1.shape, lambda i: (0, 0))
    return pl.pallas_call(
        functools.partial(_mix_ffn_body, blocks_per_seq=blocks_per_seq, seq_rows=seq_rows),
        grid=(rows // tm,),
        in_specs=[row(D_MODEL), row(512), row(512), row(2048), mod_spec,
                  full(512, D_MODEL), full(512, D_MODEL), full(D_MODEL, D_MODEL),
                  vec(D_MODEL), mod_spec, mod_spec, mod_spec,
                  full(D_MODEL, 2 * D_FF), pl.BlockSpec((CONV_W, D_FF), lambda i: (0, 0)), vec(D_FF),
                  full(D_FF, D_MODEL), vec(D_MODEL), pspec, pspec],
        out_specs=[row(D_MODEL), pl.BlockSpec((a_rows, D_FF), lambda i: (i, 0))],
        out_shape=[jax.ShapeDtypeStruct((rows, D_MODEL), F32),
                   jax.ShapeDtypeStruct((rows // tm * a_rows, D_FF), F32)],
        scratch_shapes=[pltpu.VMEM((8, D_FF), F32)],
        compiler_params=_cparams(("arbitrary",)),
        name="mix_ffn",
    )(x2d, oa, zr, gm, gt1, wa, wr, wo, nw, sh, sc, gt, wu, cw, cb, wd, nf, p1, p2)


def _prompt_path(x_prompt, mods_p, W):
    B, T, _ = x_prompt.shape
    rows = B * T
    x2d = x_prompt.reshape(rows, D_MODEL)
    pos = np.arange(T)
    tm = 256
    sh1, sc1, gt1, sh2, sc2, gt2 = [mods_p[:, None, j * D_MODEL:(j + 1) * D_MODEL] for j in range(6)]
    (q, qr, kvc, kvs, kvw, kvb, ret3, gr, gm, ga) = _inproj(
        x2d, W["norm1"], sh1, sc1, W["w_in"], _rope_tab(pos, Q_SCALE), _rope_tab(pos, 1.0), _ret_tab(pos),
        tm, T // tm, False, True)
    nc, n_pages = T // L_CMP, T // _PAGE
    comp = _compress_t(kvc, W["phi_posT"], W["phi_w1p"], W["phi_w2p"])
    comp = comp.reshape(2, G_A, B, n_pages, 2, 2, DH).transpose(0, 2, 1, 5, 3, 4, 6).reshape(2, B, G_A, nc, DH)
    comp2 = jnp.concatenate([comp, comp], axis=-1).astype(BF)
    oa = _nsa_prompt(q, qr, kvb, comp2[0], comp2[1], ga, B, T)
    C = 256 if T % 256 == 0 else T
    zr, ret_new = _retention(ret3, gr, jnp.zeros((B, H_R, DK_R, DV_R), F32), W["gnw"], B, T, C, C, 1)
    zeros8 = jnp.zeros((8, D_FF), F32)
    y, a_up = _mix_ffn(x2d, oa, zr, gm, gt1, W["w_up_a"], W["w_up_r"], W["w_out"],
                       W["norm2"], sh2, sc2, gt2, W["w_ffn_up"], W["conv_w"], W["conv_b"], W["w_ffn_down"],
                       W["normf"], zeros8, zeros8, tm, T // tm, False, None)
    wsz = min(WINDOW, T)
    rows_major = lambda t: t.reshape(B, 2, G_A, DH, t.shape[-1]).transpose(0, 4, 1, 2, 3)[None]
    outs = dict(
        y=y.reshape(B, T, D_MODEL),
        cmp=rows_major(kvc), slc=rows_major(kvs), win=rows_major(kvw[:, :, T - wsz:]),
        ret=ret_new[None],
        conv=a_up.reshape(B, T // tm, 8, D_FF)[None, :, T // tm - 1, 8 - (CONV_W - 1):],
    )
    return outs


_QC = 32


def _cmp_paged_body(pt_ref, cache_ref, q_ref, pos_ref, w1_ref, w2_ref, ocmp_ref, topi_ref, xbuf, sem,
                    *, n_pages, n_pick, n_q):
    b = pl.program_id(0)
    nb = pl.num_programs(0)
    slot = b % 2

    def copy(bb, sl, p):
        return pltpu.make_async_copy(cache_ref.at[pt_ref[bb, p]], xbuf.at[sl, :, p, :], sem.at[sl])

    def issue(bb, sl):
        def f(p, c):
            copy(bb, sl, p).start()
            return c
        lax.fori_loop(0, n_pages, f, 0)

    @pl.when(b == 0)
    def _():
        issue(b, slot)

    @pl.when(b + 1 < nb)
    def _():
        issue(b + 1, 1 - slot)

    def wait(p, c):
        copy(b, slot, p).wait()
        return c
    lax.fori_loop(0, n_pages, wait, 0)

    comp = _compress_slab_rows(lambda r: xbuf[slot, r], n_pages, pos_ref, w1_ref, w2_ref)

    def lane_groups(x, op):
        r = x
        for j in range(1, _BPP):
            r = op(r, pltpu.roll(x, j * _QC, 1))
        return r

    for g in range(G_A):
        kc = comp[0][g * n_pages:(g + 1) * n_pages].astype(BF)
        vc = comp[1][g * n_pages:(g + 1) * n_pages].astype(BF)
        st = _nn(kc, q_ref[g])
        m = lane_groups(jnp.max(st, axis=0, keepdims=True), jnp.maximum)
        p = jnp.exp2(st - m)
        p = p / lane_groups(jnp.sum(p, axis=0, keepdims=True), jnp.add)
        r_full = _tn(p.astype(BF), vc)
        o = r_full[0:_QC, 0:DH]
        for j in range(1, _BPP):
            o = o + r_full[j * _QC:(j + 1) * _QC, j * DH:(j + 1) * DH]
        ocmp_ref[g] = o
        pair = p + pltpu.roll(p, LANE - _QC, 1)
        imp = pair
        for r in range(1, HPG):
            imp = imp + pltpu.roll(pair, LANE - r * n_q, 1)
        sc = jnp.concatenate([imp, pltpu.roll(imp, 2 * _QC, 1)], axis=0)
        row = lax.broadcasted_iota(jnp.int32, (2 * n_pages, LANE), 0)
        blk = jnp.where(row < n_pages, 2 * row, 2 * (row - n_pages) + 1)
        score = sc + jnp.where(blk == 0, FORCE_BONUS, 0.0)
        topi_ref[g] = jnp.zeros((8, LANE), jnp.int32)
        for k in range(n_pick):
            mx = jnp.max(score, axis=0, keepdims=True)
            idx = jnp.min(jnp.where(score == mx, blk, 2 * n_pages), axis=0, keepdims=True)
            topi_ref[g, k:k + 1, :] = idx
            score = jnp.where(blk == idx, -jnp.inf, score)


def _cmp_paged(page_table, cache_t, q_bd, pos, w1, w2, n_q):
    DB, n_pages = page_table.shape
    SL = cache_t.shape[1]
    page_rows = cache_t.shape[2]
    kern = functools.partial(_cmp_paged_body, n_pages=n_pages, n_pick=N_SEL - 1, n_q=n_q)
    one = dict(pipeline_mode=pl.Buffered(1))
    return pl.pallas_call(
        kern,
        grid_spec=pltpu.PrefetchScalarGridSpec(
            num_scalar_prefetch=1, grid=(DB,),
            in_specs=[pl.BlockSpec(memory_space=pl.ANY),
                      pl.BlockSpec((None, G_A, _BPP * DH, LANE), lambda b, pt: (b, 0, 0, 0)),
                      pl.BlockSpec(pos.shape, lambda b, pt: (0, 0, 0), **one),
                      pl.BlockSpec(w1.shape, lambda b, pt: (0, 0, 0, 0), **one),
                      pl.BlockSpec((2, _BPP * DH, _BPP * DH), lambda b, pt: (0, 0, 0), **one)],
            out_specs=[pl.BlockSpec((None, G_A, _QC, DH), lambda b, pt: (b, 0, 0, 0)),
                       pl.BlockSpec((None, G_A, 8, LANE), lambda b, pt: (b, 0, 0, 0))],
            scratch_shapes=[pltpu.VMEM((2, SL, n_pages, page_rows), F32),
                            pltpu.SemaphoreType.DMA((2,))]),
        out_shape=[jax.ShapeDtypeStruct((DB, G_A, _QC, DH), F32),
                   jax.ShapeDtypeStruct((DB, G_A, 8, LANE), jnp.int32)],
        compiler_params=_cparams(("arbitrary",)),
        name="cmp_paged",
    )(page_table, cache_t, q_bd, pos, w1, w2)


def _slc_win_paged_body(pt_ref, ti_ref, cslc_ref, win_ref, q_ref, tiv_ref, ns_ref, nw_ref, ex_ref,
                        oslc_ref, owin_ref, kbuf, sem, *, n_q, n_pick, page_rows):
    b = pl.program_id(0)
    nb = pl.num_programs(0)
    slot = b % 2
    n_slab = n_q * n_pick
    bpp = page_rows // L_SEL
    wb = win_ref.shape[1]

    def copies(bb, sl):
        cps = []
        for g in range(G_A):
            for j in range(n_slab):
                page = pt_ref[bb, ti_ref[bb, g * n_slab + j] // bpp]
                for kv in range(2):
                    cps.append(pltpu.make_async_copy(
                        cslc_ref.at[page, pl.ds((kv * G_A + g) * DH, DH), :],
                        kbuf.at[sl, g, kv, :, pl.ds(j * page_rows, page_rows)], sem.at[sl]))
        return cps

    @pl.when(b == 0)
    def _():
        for cp in copies(b, slot):
            cp.start()

    @pl.when(b + 1 < nb)
    def _():
        for cp in copies(b + 1, 1 - slot):
            cp.start()

    for cp in copies(b, slot):
        cp.wait()

    nq_rows = q_ref.shape[1]
    nk = n_slab * page_rows
    row_q = lax.broadcasted_iota(jnp.int32, (nq_rows, nk), 0) % n_q
    col = lax.broadcasted_iota(jnp.int32, (nq_rows, nk), 1)
    own = row_q == col // (n_pick * page_rows)
    half = ((col % page_rows) // L_SEL).astype(F32)
    nnew = ns_ref.shape[2]
    new_ok = (lax.broadcasted_iota(jnp.int32, (nq_rows, nnew), 1)
              <= lax.broadcasted_iota(jnp.int32, (nq_rows, nnew), 0) % n_q)
    dwin = (wb + lax.broadcasted_iota(jnp.int32, (nq_rows, wb), 0) % n_q
            - lax.broadcasted_iota(jnp.int32, (nq_rows, wb), 1))
    win_ok = (dwin >= 0) & (dwin < WINDOW)

    def attend(q, kt_old, vt_old, ok_old, k_new, v_new):
        s_o = jnp.where(ok_old, _nn(q, kt_old), NEG)
        s_n = jnp.where(new_ok, _nt(q, k_new), NEG)
        m = jnp.maximum(jnp.max(s_o, axis=1, keepdims=True), jnp.max(s_n, axis=1, keepdims=True))
        p_o = jnp.exp2(s_o - m)
        p_n = jnp.exp2(s_n - m)
        den = jnp.sum(p_o, axis=1, keepdims=True) + jnp.sum(p_n, axis=1, keepdims=True)
        return (_nt(p_o.astype(BF), vt_old) + _nn(p_n.astype(BF), v_new)) / den

    for g in range(G_A):
        q = q_ref[g]
        par = (tiv_ref[g] % bpp).astype(F32).astype(BF)
        want = _nn(par, ex_ref[...])[0:1, :]
        ok = own & (half == want)
        oslc_ref[g] = attend(q, kbuf[slot, g, 0].astype(BF), kbuf[slot, g, 1].astype(BF), ok,
                             ns_ref[g, 0].astype(BF), ns_ref[g, 1].astype(BF))
        kw = win_ref[pl.ds((0 * G_A + g) * DH, DH), :].astype(BF)
        vw = win_ref[pl.ds((1 * G_A + g) * DH, DH), :].astype(BF)
        owin_ref[g] = attend(q, kw, vw, win_ok, nw_ref[g, 0].astype(BF), nw_ref[g, 1].astype(BF))


def _slc_win_paged(page_table, topi_flat, topi_vec, cache_t, win_t, q_rot, new_slc, new_win, n_q):
    DB = page_table.shape[0]
    n_pick = N_SEL - 1
    page_rows = cache_t.shape[2]
    wb = win_t.shape[2]
    nq_rows = q_rot.shape[2]
    n_slab = n_q * n_pick
    ex = (np.arange(n_slab * page_rows)[None, :] // page_rows) == np.arange(LANE)[:, None]
    kern = functools.partial(_slc_win_paged_body, n_q=n_q, n_pick=n_pick, page_rows=page_rows)
    bspec = lambda shp: pl.BlockSpec((None,) + shp, lambda b, pt, ti: (b,) + (0,) * len(shp))
    return pl.pallas_call(
        kern,
        grid_spec=pltpu.PrefetchScalarGridSpec(
            num_scalar_prefetch=2, grid=(DB,),
            in_specs=[pl.BlockSpec(memory_space=pl.ANY), bspec((2 * G_A * DH, wb)),
                      bspec((G_A, nq_rows, DH)), bspec((G_A, 16, LANE)),
                      bspec((G_A, 2, 16, DH)), bspec((G_A, 2, 16, DH)),
                      pl.BlockSpec((LANE, n_slab * page_rows), lambda b, pt, ti: (0, 0))],
            out_specs=[bspec((G_A, nq_rows, DH)), bspec((G_A, nq_rows, DH))],
            scratch_shapes=[pltpu.VMEM((2, G_A, 2, DH, n_slab * page_rows), F32),
                            pltpu.SemaphoreType.DMA((2,))]),
        out_shape=[jax.ShapeDtypeStruct((DB, G_A, nq_rows, DH), F32)] * 2,
        compiler_params=_cparams(("arbitrary",)),
        name="slc_win_paged",
    )(page_table, topi_flat, cache_t, win_t, q_rot, topi_vec, new_slc, new_win, jnp.asarray(ex, BF))


def _gate_sample_body(oc_ref, os_ref, ow_ref, ga_ref, o_ref):
    rows = o_ref.shape[0]
    lo64 = lax.broadcasted_iota(jnp.int32, (rows, LANE), 1) < DH
    for c4 in range(H_A // 2):
        g, c = c4 // 2, c4 % 2
        sl = slice(c4 * LANE, (c4 + 1) * LANE)
        acc = jnp.zeros((rows, LANE), F32)
        for br, ref in enumerate((oc_ref, os_ref, ow_ref)):
            col = g * LANE + br * HPG + 2 * c
            gate = jnp.where(lo64, ga_ref[:, col:col + 1], ga_ref[:, col + 1:col + 2])
            acc = acc + gate * ref[:, sl]
        o_ref[:, sl] = acc.astype(BF)


def _gate_sample(oc, osl, ow, ga):
    rows = oc.shape[0]
    full = lambda w: pl.BlockSpec((rows, w), lambda i: (0, 0))
    return pl.pallas_call(
        _gate_sample_body, grid=(1,),
        in_specs=[full(512), full(512), full(512), full(256)],
        out_specs=full(512),
        out_shape=jax.ShapeDtypeStruct((rows, 512), BF),
        compiler_params=_cparams(("arbitrary",)),
        name="gate_sample",
    )(oc, osl, ow, ga)


def _sample_path(x_sample, mods_s, W, cache_cmp, cache_slc, state_win, state_ret, state_conv, page_table):
    DB, S, _ = x_sample.shape
    rows = DB * S
    page_rows = cache_cmp.shape[1]
    P = page_table.shape[1] * page_rows
    wb = state_win.shape[1]
    assert P % L_SEL == 0 and S < L_CMP and S <= 8 and wb == WINDOW and page_rows % L_SEL == 0
    assert P // L_SEL >= N_SEL and CONV_W == 3 and S >= CONV_W - 1
    pos = P + np.arange(S)
    pos_rows = np.tile(pos, DB)
    x2d = x_sample.reshape(rows, D_MODEL)
    modr = jnp.repeat(mods_s, S, axis=0)
    sh1, sc1, gt1, sh2, sc2, gt2 = [modr[:, j * D_MODEL:(j + 1) * D_MODEL] for j in range(6)]
    (q, qr, kvc, kvs, kvw, _, ret3, gr, gm, ga) = _inproj(
        x2d, W["norm1"], sh1, sc1, W["w_in"], _rope_tab(pos_rows, Q_SCALE), _rope_tab(pos_rows, 1.0),
        _ret_tab(pos_rows), rows, 1, True, False)

    def to_heads(t):
        return t.reshape(DB, S, G_A, HPG, DH).transpose(0, 2, 3, 1, 4).reshape(DB, G_A, HPG * S, DH)

    def from_heads(t):
        return t.reshape(DB, G_A, HPG, S, DH).transpose(0, 3, 1, 2, 4).reshape(rows, H_A * DH)

    assert page_rows == _BPP * L_CMP and HPG * S <= _QC and S * (N_SEL - 1) <= LANE
    slab = lambda t: t.transpose(0, 2, 3, 4, 1).reshape(t.shape[0], 2 * G_A * DH, t.shape[1])
    qt = jnp.pad(to_heads(q).transpose(0, 1, 3, 2), ((0, 0), (0, 0), (0, 0), (0, _QC - HPG * S)))
    eye = jnp.eye(_BPP, dtype=qt.dtype)
    q_bd = (eye[None, None, :, None, :, None] * qt[:, :, None, :, None, :]).reshape(DB, G_A, _BPP * DH, _BPP * _QC)
    o_cmp, topi = _cmp_paged(page_table, slab(cache_cmp), q_bd, W["phi_posT"], W["phi_w1p"], W["phi_w2p"], S)
    n_pick = N_SEL - 1
    topi = topi[:, :, :n_pick, :S].transpose(0, 1, 3, 2).reshape(DB, G_A, S * n_pick)
    topi_vec = jnp.broadcast_to(jnp.pad(topi, ((0, 0), (0, 0), (0, LANE - S * n_pick)))[:, :, None, :],
                                (DB, G_A, 16, LANE))

    def new_rows(t):
        t = t.reshape(DB, S, 2, G_A, DH).transpose(0, 3, 2, 1, 4)
        return jnp.pad(t, ((0, 0), (0, 0), (0, 0), (0, 16 - S), (0, 0)))

    o_slc, o_win = _slc_win_paged(page_table, topi.reshape(DB, G_A * S * n_pick), topi_vec, slab(cache_slc),
                                  slab(state_win), to_heads(qr), new_rows(kvs), new_rows(kvw), S)
    oa = _gate_sample(from_heads(o_cmp[:, :, :HPG * S]), from_heads(o_slc), from_heads(o_win), ga)

    RP = 16
    padr = lambda t: jnp.pad(t.reshape(t.shape[:-2] + (DB, S, 512)),
                             ((0, 0),) * (t.ndim - 1) + ((0, RP - S), (0, 0))).reshape(t.shape[:-2] + (DB * RP, 512))
    zr, ret_new = _retention(padr(ret3), padr(gr), state_ret, W["gnw"], DB, RP, RP, S, 8 if DB % 8 == 0 else 1)
    zr = zr.reshape(DB, RP, 512)[:, :S].reshape(rows, 512)
    zs = jnp.zeros((DB, S, D_FF), F32)
    p1 = zs.at[:, 0].set(state_conv[:, 1]).reshape(rows, D_FF)
    p2 = zs.at[:, 0].set(state_conv[:, 0]).at[:, 1].set(state_conv[:, 1]).reshape(rows, D_FF)
    y, a_up = _mix_ffn(x2d, oa, zr, gm, gt1, W["w_up_a"], W["w_up_r"], W["w_out"],
                       W["norm2"], sh2, sc2, gt2, W["w_ffn_up"], W["conv_w"], W["conv_b"], W["w_ffn_down"],
                       W["normf"], p1, p2, rows, 1, True, S)
    shp = (1, DB, S, 2, G_A, DH)
    return dict(
        y=y.reshape(DB, S, D_MODEL),
        cmp=kvc.reshape(shp), slc=kvs.reshape(shp),
        win=jnp.concatenate([state_win[:, S:], kvw.reshape(DB, S, 2, G_A, DH)], axis=1)[None],
        ret=ret_new[None],
        conv=a_up.reshape(DB, S, D_FF)[None, :, S - (CONV_W - 1):],
    )


def kernel(x_prompt, x_sample, cache_cmp_kv, cache_slc_kv, state_win_kv, state_ret, state_conv, page_table,
           c_prompt, c_sample, norm1_w, norm2_w, w_ada, b_ada, w_in, phi_pos_k, phi_k1, phi_k2, phi_pos_v,
           phi_v1, phi_v2, w_up_a, ret_gn_w, w_up_r, w_out, w_ffn_up, ffn_conv_w, ffn_conv_b, w_ffn_down, normf_w):
    B = x_prompt.shape[0]
    l = 0
    W = dict(
        norm1=norm1_w[l].reshape(1, D_MODEL), norm2=norm2_w[l].reshape(1, D_MODEL), normf=normf_w.reshape(1, D_MODEL),
        w_in=_pack_w_in(w_in[l]),
        w_up_a=w_up_a[l].astype(BF), w_up_r=w_up_r[l].astype(BF), w_out=w_out[l].astype(BF),
        gnw=ret_gn_w[l].reshape(1, H_R * DV_R),
        w_ffn_up=w_ffn_up[l].astype(BF), conv_w=ffn_conv_w[l], conv_b=ffn_conv_b[l].reshape(1, D_FF),
        w_ffn_down=w_ffn_down[l].astype(BF),
    )
    W["phi_posT"], W["phi_w1p"], W["phi_w2p"] = _pack_phi_paged(
        phi_pos_k[l], phi_k1[l], phi_k2[l], phi_pos_v[l], phi_v1[l], phi_v2[l])
    mods = _mods(jnp.concatenate([c_prompt, c_sample], axis=0), w_ada[l], b_ada[l])
    p = _prompt_path(x_prompt, mods[:B], W)
    s = _sample_path(x_sample, mods[B:], W, cache_cmp_kv[l], cache_slc_kv[l], state_win_kv[l], state_ret[l],
                     state_conv[l], page_table)
    return (p["y"], s["y"], p["cmp"], s["cmp"], p["slc"], s["slc"], p["win"], s["win"],
            p["ret"], s["ret"], p["conv"], s["conv"])
```

```python
import functools

import numpy as np
import jax
import jax.numpy as jnp
from jax import lax
from jax.experimental import pallas as pl
from jax.experimental.pallas import tpu as pltpu

BF = jnp.bfloat16
F32 = jnp.float32

D_MODEL = 1024
H_A, G_A, DH = 8, 2, 64
HPG = H_A // G_A
ROPE_DIM = DH // 4
ROPE_THETA = 500000.0
L_CMP, L_SEL, N_SEL = 32, 64, 8
WINDOW = 512
Q_BLOCK = 128
FORCE_BONUS = 1e4
H_R, DK_R, DV_R = 4, 128, 128
RET_THETA = 10000.0
D_FF = 2816
CONV_W = 3
EPS = 1e-6
NEG = -1e30
MASK_BIG = 2.0 ** 100
Q_SCALE = DH ** -0.5 * 1.4426950408889634
LANE = 128
VMEM_LIMIT = 56 * 1024 * 1024


def _cparams(sem):
    return pltpu.CompilerParams(dimension_semantics=sem, vmem_limit_bytes=VMEM_LIMIT)


def _sigmoid(x):
    return 1.0 / (1.0 + jnp.exp(-x))


def _gelu(x):
    return 0.5 * x * (1.0 + jnp.tanh(0.7978845608028654 * (x + 0.044715 * (x * x * x))))


def _nt(a, b):
    return lax.dot_general(a, b, (((1,), (1,)), ((), ())), preferred_element_type=F32)


def _tn(a, b):
    return lax.dot_general(a, b, (((0,), (0,)), ((), ())), preferred_element_type=F32)


def _nn(a, b):
    return jnp.dot(a, b, preferred_element_type=F32)


def _rope_tab(pos, scale):
    half = ROPE_DIM // 2
    inv = ROPE_THETA ** (-np.arange(half, dtype=np.float64) * (2.0 / ROPE_DIM))
    ang = pos.astype(np.float64)[:, None] * inv
    cos, sin = np.cos(ang), np.sin(ang)
    n = pos.shape[0]
    c = np.ones((n, DH)); s_lo = np.zeros((n, DH)); s_hi = np.zeros((n, DH))
    c[:, :half] = cos; c[:, half:ROPE_DIM] = cos
    s_lo[:, half:ROPE_DIM] = sin
    s_hi[:, :half] = -sin
    tab = np.concatenate([np.tile(t, (1, 2)) for t in (c, s_lo, s_hi)], axis=1) * scale
    return jnp.asarray(tab, F32)


def _ret_tab(pos):
    half = DK_R // 2
    inv = RET_THETA ** (-np.arange(half, dtype=np.float64) * (2.0 / DK_R))
    ang = pos.astype(np.float64)[:, None] * inv
    cos, sin = np.cos(ang), np.sin(ang)
    c = np.concatenate([cos, cos], axis=1)
    s = np.concatenate([-sin, sin], axis=1)
    ks = DK_R ** -0.5
    return jnp.asarray(np.concatenate([c, s, c * ks, s * ks], axis=1), F32)


def _ret_decay(C, c_true):
    h = np.arange(H_R, dtype=np.float64)
    log_g = np.log1p(-np.exp2(-5.0 - h))
    i = np.arange(C, dtype=np.float64)
    diff = i[:, None] - i[None, :]
    dm = np.where(diff >= 0, np.exp(log_g[:, None, None] * np.maximum(diff, 0.0)), 0.0)
    dq = np.exp(log_g[:, None] * (i + 1.0))[:, :, None] * np.ones((1, 1, LANE))
    wk = np.exp(log_g[:, None] * (c_true - 1.0 - i))[:, :, None] * np.ones((1, 1, LANE))
    wk = np.where(i[None, :, None] < c_true, wk, 0.0)
    gc = np.exp(log_g * c_true)[:, None, None] * np.ones((1, 8, LANE))
    return (jnp.asarray(dm, F32), jnp.asarray(dq, F32), jnp.asarray(wk, F32), jnp.asarray(gc, F32))


def _mods_body(c_ref, w_ref, b_ref, o_ref):
    c = c_ref[...]
    s = c * _sigmoid(c)
    o_ref[...] = _nn(s.astype(BF), w_ref[...].astype(BF)) + b_ref[...]


def _mods(c_all, w_ada, b_ada):
    n = c_all.shape[0]
    nout = w_ada.shape[1]
    tn = 1024
    return pl.pallas_call(
        _mods_body,
        grid=(nout // tn,),
        in_specs=[pl.BlockSpec((n, D_MODEL), lambda j: (0, 0)),
                  pl.BlockSpec((D_MODEL, tn), lambda j: (0, j)),
                  pl.BlockSpec((1, tn), lambda j: (0, j))],
        out_specs=pl.BlockSpec((n, tn), lambda j: (0, j)),
        out_shape=jax.ShapeDtypeStruct((n, nout), F32),
        compiler_params=_cparams(("arbitrary",)),
        name="mods",
    )(c_all, w_ada, b_ada.reshape(1, nout))


_C_Q, _C_KV, _C_QR, _C_KR, _C_VR, _C_GR, _C_GM, _C_GA, _C_END = (
    0, 512, 1280, 1792, 2304, 2816, 3328, 5376, 5632)


def _pack_w_in(w_in):
    o = np.cumsum((0, 512, 768, 24, 512, 512, 512, 512, 2048))
    q, kv, ga, qr, kr, vr, gr, gm = [w_in[:, o[i]:o[i + 1]] for i in range(8)]
    ga = ga.reshape(D_MODEL, 3, G_A, HPG).transpose(0, 2, 1, 3).reshape(D_MODEL, G_A, 3 * HPG)
    ga = jnp.pad(ga, ((0, 0), (0, 0), (0, LANE - 3 * HPG))).reshape(D_MODEL, G_A * LANE)
    return jnp.concatenate([q, kv, qr, kr, vr, gr, gm, ga], axis=1).astype(BF)


def _inproj_body(x_ref, nw_ref, sh_ref, sc_ref, w_ref, rq_ref, rk_ref, rr_ref,
                 q_ref, qr_ref, kvc_ref, kvs_ref, kvw_ref, kvb_ref, ret_ref, gr_ref, gm_ref, ga_ref, *, kv_t):
    def put_kv(out_ref, k, v):
        if kv_t:
            out_ref[0:LANE, :] = k.T
            out_ref[LANE:2 * LANE, :] = v.T
        else:
            out_ref[:, 0:LANE] = k
            out_ref[:, LANE:2 * LANE] = v

    x = x_ref[...]
    tm = x.shape[0]
    ms = jnp.mean(x * x, axis=-1, keepdims=True)
    h = (x * lax.rsqrt(ms + EPS)) * nw_ref[...]
    h = h * (1.0 + sc_ref[...]) + sh_ref[...]
    hb = h.astype(BF)
    lo64 = lax.broadcasted_iota(jnp.int32, (tm, LANE), 1) < DH

    def mm(lo, hi):
        return _nn(hb, w_ref[:, lo:hi])

    def rope(xc, tab_ref):
        return (xc * tab_ref[:, 0:LANE] + pltpu.roll(xc, 8, 1) * tab_ref[:, LANE:2 * LANE]
                + pltpu.roll(xc, LANE - 8, 1) * tab_ref[:, 2 * LANE:3 * LANE])

    qa = mm(_C_Q, _C_KV)
    q_ref[...] = (qa * Q_SCALE).astype(BF)
    for c in range(4):
        qr_ref[:, c * LANE:(c + 1) * LANE] = rope(qa[:, c * LANE:(c + 1) * LANE], rq_ref).astype(BF)

    kv = mm(_C_KV, _C_QR)
    put_kv(kvc_ref, kv[:, 0:LANE], kv[:, LANE:2 * LANE])
    for kind, out_ref in ((0, kvs_ref), (1, kvw_ref)):
        base = 256 + kind * 256
        k = rope(kv[:, base:base + LANE], rk_ref)
        v = kv[:, base + LANE:base + 2 * LANE]
        put_kv(out_ref, k, v)
        kr_ = pltpu.roll(k, DH, 1)
        vr_ = pltpu.roll(v, DH, 1)
        kvb_ref[kind, 0, 0] = jnp.where(lo64, k, vr_).astype(BF)
        kvb_ref[kind, 0, 1] = jnp.where(lo64, v, kr_).astype(BF)
        kvb_ref[kind, 1, 0] = jnp.where(lo64, kr_, v).astype(BF)
        kvb_ref[kind, 1, 1] = jnp.where(lo64, vr_, k).astype(BF)

    qr = mm(_C_QR, _C_KR)
    kr = mm(_C_KR, _C_VR)
    for hh in range(H_R):
        sl = slice(hh * LANE, (hh + 1) * LANE)
        xq = qr[:, sl]
        ret_ref[0, :, sl] = (xq * rr_ref[:, 0:LANE] + pltpu.roll(xq, DK_R // 2, 1) * rr_ref[:, LANE:2 * LANE]).astype(BF)
        xk = kr[:, sl]
        ret_ref[1, :, sl] = (xk * rr_ref[:, 2 * LANE:3 * LANE]
                             + pltpu.roll(xk, DK_R // 2, 1) * rr_ref[:, 3 * LANE:4 * LANE]).astype(BF)
    ret_ref[2] = mm(_C_VR, _C_GR).astype(BF)
    g = mm(_C_GR, _C_GM)
    gr_ref[...] = (g * _sigmoid(g)).astype(BF)
    gm_ref[...] = _sigmoid(mm(_C_GM, _C_GA)).astype(BF)
    ga_ref[...] = _sigmoid(mm(_C_GA, _C_END))


def _inproj(x2d, nw, sh, sc, w_pack, rq, rk, rr, tm, tab_blocks, per_row_mods, kv_t):
    rows = x2d.shape[0]
    nb = rows // tm
    if kv_t:
        kv_shape = jax.ShapeDtypeStruct((nb // tab_blocks, 256, tab_blocks * tm), F32)
        kv_spec = pl.BlockSpec((None, 256, tm), lambda i: (i // tab_blocks, 0, i % tab_blocks))
    else:
        kv_shape = jax.ShapeDtypeStruct((rows, 256), F32)
        kv_spec = pl.BlockSpec((tm, 256), lambda i: (i, 0))
    if per_row_mods:
        mod_spec = pl.BlockSpec((tm, D_MODEL), lambda i: (i, 0))
    else:
        mod_spec = pl.BlockSpec((None, 1, D_MODEL), lambda i: (i // tab_blocks, 0, 0))
    tab = lambda w: pl.BlockSpec((tm, w), lambda i: (i % tab_blocks, 0))
    row = lambda w: pl.BlockSpec((tm, w), lambda i: (i, 0))
    out_shapes = [
        jax.ShapeDtypeStruct((rows, 512), BF),
        jax.ShapeDtypeStruct((rows, 512), BF),
        kv_shape,
        kv_shape,
        kv_shape,
        jax.ShapeDtypeStruct((2, G_A, 2, rows, LANE), BF),
        jax.ShapeDtypeStruct((3, rows, 512), BF),
        jax.ShapeDtypeStruct((rows, 512), BF),
        jax.ShapeDtypeStruct((rows, 2048), BF),
        jax.ShapeDtypeStruct((rows, 256), F32),
    ]
    out_specs = [row(512), row(512), kv_spec, kv_spec, kv_spec,
                 pl.BlockSpec((2, G_A, 2, tm, LANE), lambda i: (0, 0, 0, i, 0)),
                 pl.BlockSpec((3, tm, 512), lambda i: (0, i, 0)),
                 row(512), row(2048), row(256)]
    return pl.pallas_call(
        functools.partial(_inproj_body, kv_t=kv_t),
        grid=(nb,),
        in_specs=[row(D_MODEL),
                  pl.BlockSpec((1, D_MODEL), lambda i: (0, 0)),
                  mod_spec, mod_spec,
                  pl.BlockSpec((D_MODEL, _C_END), lambda i: (0, 0), pipeline_mode=pl.Buffered(1)),
                  tab(384), tab(384), tab(512)],
        out_specs=out_specs,
        out_shape=out_shapes,
        compiler_params=_cparams(("arbitrary",)),
        name="inproj",
    )(x2d, nw, sh, sc, w_pack, rq, rk, rr)


_BPP = 4
_PAGE = _BPP * L_CMP


def _blockdiag(w):
    z = jnp.zeros_like(w)
    return jnp.concatenate(
        [jnp.concatenate([w if j == i else z for j in range(_BPP)], axis=-1) for i in range(_BPP)], axis=-2)


def _pack_phi_paged(phi_pos_k, phi_k1, phi_k2, phi_pos_v, phi_v1, phi_v2):
    w1, w2, pos = [], [], []
    for p_, a, b_ in ((phi_pos_k, phi_k1, phi_k2), (phi_pos_v, phi_v1, phi_v2)):
        w1.append(_blockdiag(a.astype(BF).reshape(L_CMP, DH, DH).transpose(1, 0, 2)))
        w2.append(_blockdiag(b_.astype(BF)))
        pos.append(jnp.tile(p_.T, (1, _BPP)))
    pos = jnp.stack(pos).reshape(2, DH // 2, 2 * _PAGE)
    w1 = jnp.stack(w1).reshape(2, DH // 2, 2 * _PAGE, _BPP * DH)
    return pos, w1, jnp.stack(w2)


def _compress_slab_rows(load, n_rows, pos_ref, w1_ref, w2_ref):
    out = []
    for kv in range(2):
        acc = jnp.zeros((G_A * n_rows, _BPP * DH), F32)
        for dp in range(DH // 2):
            x = jnp.concatenate(
                [jnp.concatenate([load((kv * G_A + g) * DH + 2 * dp + j) for j in range(2)], axis=1)
                 for g in range(G_A)], axis=0)
            acc = acc + _nn((x + pos_ref[kv, dp:dp + 1, :]).astype(BF), w1_ref[kv, dp])
        out.append(_nn(_gelu(acc).astype(BF), w2_ref[kv]))
    return out


def _compress_t_body(src_ref, pos_ref, w1_ref, w2_ref, o_ref, xbuf, sem, *, nseq, n_pages):
    def copy(t):
        b, p = t // n_pages, t % n_pages
        return pltpu.make_async_copy(src_ref.at[b, :, pl.ds(pl.multiple_of(p * _PAGE, _PAGE), _PAGE)],
                                     xbuf.at[:, t, :], sem)

    def start(t, c):
        copy(t).start()
        return c

    def wait(t, c):
        copy(t).wait()
        return c
    lax.fori_loop(0, nseq * n_pages, start, 0)
    lax.fori_loop(0, nseq * n_pages, wait, 0)
    kc, vc = _compress_slab_rows(lambda r: xbuf[r], nseq * n_pages, pos_ref, w1_ref, w2_ref)
    o_ref[0] = kc
    o_ref[1] = vc


def _compress_t(kv_t, pos, w1, w2):
    nseq, SL, T = kv_t.shape
    n_pages = T // _PAGE
    one = lambda a: pl.BlockSpec(a.shape, lambda i: (0,) * a.ndim, pipeline_mode=pl.Buffered(1))
    n_out = G_A * nseq * n_pages
    return pl.pallas_call(
        functools.partial(_compress_t_body, nseq=nseq, n_pages=n_pages),
        grid=(1,),
        in_specs=[pl.BlockSpec(memory_space=pl.ANY), one(pos), one(w1), one(w2)],
        out_specs=pl.BlockSpec((2, n_out, _BPP * DH), lambda i: (0, 0, 0)),
        out_shape=jax.ShapeDtypeStruct((2, n_out, _BPP * DH), F32),
        scratch_shapes=[pltpu.VMEM((SL, nseq * n_pages, _PAGE), F32), pltpu.SemaphoreType.DMA(())],
        compiler_params=_cparams(("arbitrary",)),
        name="compress_t",
    )(kv_t, pos, w1, w2)


_KT = 512


def _nsa_prompt_body(q_ref, qr_ref, kvb_ref, kc_ref, vc_ref, ga_ref, e_ref, gx_ref, o_ref, os_scr, *, nc, ns):
    i = pl.program_id(1)
    QB = Q_BLOCK
    lane = lax.broadcasted_iota(jnp.int32, (QB, LANE), 1)
    lo64 = lane < DH
    zero_b = jnp.zeros((QB, LANE), BF)
    groups = range(G_A)

    def split_heads(ref, g):
        ev, od = [], []
        for c in range(2):
            xc = ref[:, (2 * g + c) * LANE:(2 * g + c + 1) * LANE]
            ev.append(jnp.where(lo64, xc, zero_b))
            od.append(jnp.where(lo64, zero_b, xc))
        return jnp.concatenate(ev, axis=0), jnp.concatenate(od, axis=0)

    tq_l = i * QB + lax.broadcasted_iota(jnp.int32, (nc, QB), 1)
    r_c = lax.broadcasted_iota(jnp.int32, (nc, QB), 0)
    half = nc // 2
    blk_c = jnp.where(r_c < half, 2 * r_c, 2 * (r_c - half) + 1)
    cmask = (blk_c * L_CMP + (L_CMP - 1)) <= tq_l
    cmask_f = cmask.astype(F32)
    tq_s = i * QB + lax.broadcasted_iota(jnp.int32, (ns, QB), 1)
    blk_s = lax.broadcasted_iota(jnp.int32, (ns, QB), 0)
    valid = (blk_s * L_SEL) <= tq_s
    forced = (blk_s == 0) | (blk_s == tq_s // L_SEL)
    n_top = min(N_SEL, ns)

    def select(imp):
        score = jnp.where(valid, imp + jnp.where(forced, FORCE_BONUS, 0.0), NEG)
        rank = jnp.zeros((ns, QB), F32)
        for b2 in range(ns):
            row = score[b2:b2 + 1, :]
            rank = rank + jnp.where(blk_s > b2, jnp.where(row >= score, 1.0, 0.0), jnp.where(row > score, 1.0, 0.0))
        return jnp.where((rank < n_top) & (score > 0.5 * NEG), 1.0, 0.0)

    def score_stage(chains):
        scored = []
        for qs, kmat, _, bias2 in chains:
            s = _nt(qs, kmat) + bias2
            scored.append((s, jnp.max(s, axis=1, keepdims=True)))
        return scored

    def value_stage(scored, chains):
        outs = []
        for (s, m), (_, _, vmat, _) in zip(scored, chains):
            p = jnp.exp2(s - m)
            outs.append(_nn(p.astype(BF), vmat) / jnp.sum(p, axis=1, keepdims=True))
        return outs

    def masked_attend(chains):
        return value_stage(score_stage(chains), chains)

    qr = [split_heads(qr_ref, g) for g in groups]
    n_tiles = (i * QB + QB + _KT - 1) // _KT

    cmp_scores = [[_nt(kc_ref[g], qs) for qs in split_heads(q_ref, g)] for g in groups]

    WK = WINDOW + QB
    start = pl.multiple_of(jnp.maximum(i * QB - WINDOW, 0), QB)
    diff = (i * QB + lax.broadcasted_iota(jnp.int32, (QB, WK), 0)) - (start + lax.broadcasted_iota(jnp.int32, (QB, WK), 1))
    wb = jnp.where((diff >= 0) & (diff < WINDOW), 0.0, NEG)
    wb2 = jnp.concatenate([wb, wb], axis=0)
    win_chains = []
    for g in groups:
        wkv = kvb_ref[1, g, 0, pl.ds(start, WK), :]
        wvk = kvb_ref[1, g, 1, pl.ds(start, WK), :]
        win_chains += [(qr[g][0], wkv, wvk, wb2), (qr[g][1], wvk, wkv, wb2)]
    win_scored = score_stage(win_chains)

    imps, o_cmp = [], []
    for g in groups:
        vc = vc_ref[g]
        imp = jnp.zeros((ns, QB), F32)
        oc = {}
        for stack in range(2):
            st = cmp_scores[g][stack]
            for c in range(2):
                s = jnp.where(cmask, st[:, c * QB:(c + 1) * QB], NEG)
                p = jnp.exp2(s - jnp.max(s, axis=0, keepdims=True)) * cmask_f
                p = p / jnp.maximum(jnp.sum(p, axis=0, keepdims=True), 1e-30)
                imp = imp + p[0:half] + p[half:nc]
                oc[(c, stack)] = _tn(p.astype(BF), vc)
        o_cmp.append(oc)
        imps.append(imp)

    ow = value_stage(win_scored, win_chains)
    sel_b = [select(imp).astype(BF) for imp in imps]

    gates = []
    for g in groups:
        ga = ga_ref[:, g * LANE:(g + 1) * LANE]
        hi = ga.astype(BF)
        lo = (ga - hi.astype(F32)).astype(BF)
        gates.append(_nn(jnp.concatenate([hi, lo], axis=1), gx_ref[...]))

    def gate_of(g, br, c):
        j = br * 2 + c
        return gates[g][:, j * LANE:(j + 1) * LANE]

    part = [[gate_of(g, 0, c) * jnp.where(lo64, o_cmp[g][(c, 0)], o_cmp[g][(c, 1)])
             + gate_of(g, 2, c) * jnp.where(lo64, ow[2 * g][c * QB:(c + 1) * QB], ow[2 * g + 1][c * QB:(c + 1) * QB])
             for c in range(2)] for g in groups]


    def slc_variant(nk):
        tq_r = i * QB + lax.broadcasted_iota(jnp.int32, (QB, _KT), 0)
        causal = (nk - _KT + lax.broadcasted_iota(jnp.int32, (QB, _KT), 1)) <= tq_r
        chains = []
        for g in groups:
            bias = _tn(sel_b[g], e_ref[:, 0:nk]) - MASK_BIG
            tail = jnp.where(causal, bias[:, nk - _KT:], -MASK_BIG)
            bias = tail if nk == _KT else jnp.concatenate([bias[:, :nk - _KT], tail], axis=1)
            bias2 = jnp.concatenate([bias, bias], axis=0)
            kv = kvb_ref[0, g, 0, 0:nk, :]
            vk = kvb_ref[0, g, 1, 0:nk, :]
            chains.append((qr[g][0], kv, vk, bias2))
            chains.append((qr[g][1], vk, kv, bias2))
        for c, o in enumerate(masked_attend(chains)):
            os_scr[c // 2, c % 2] = o

    for k in range(1, kvb_ref.shape[3] // _KT + 1):
        pl.when(n_tiles == k)(functools.partial(slc_variant, k * _KT))

    for g in groups:
        os_e = os_scr[g, 0]
        os_o = os_scr[g, 1]
        for c in range(2):
            rows = slice(c * QB, (c + 1) * QB)
            acc = part[g][c] + gate_of(g, 1, c) * jnp.where(lo64, os_e[rows], os_o[rows])
            o_ref[:, (2 * g + c) * LANE:(2 * g + c + 1) * LANE] = acc.astype(BF)


def _gate_expand():
    gx = np.zeros((LANE, 3 * 2 * LANE), np.float32)
    for br in range(3):
        for c in range(2):
            j = br * 2 + c
            gx[br * HPG + 2 * c, j * LANE:j * LANE + DH] = 1.0
            gx[br * HPG + 2 * c + 1, j * LANE + DH:(j + 1) * LANE] = 1.0
    return jnp.asarray(np.concatenate([gx, gx], axis=0), BF)


def _sel_expand(ns, nkeys):
    e = (np.arange(nkeys)[None, :] // L_SEL) == np.arange(ns)[:, None]
    return jnp.asarray(e * MASK_BIG, BF)


def _nsa_prompt(q, qr, kvb, kc2, vc2, ga, B, T):
    nqb = T // Q_BLOCK
    nc, ns = T // L_CMP, T // L_SEL
    assert T >= WINDOW + Q_BLOCK and T % _KT == 0
    qspec = pl.BlockSpec((Q_BLOCK, 512), lambda b, i: (b * nqb + i, 0))
    cspec = pl.BlockSpec((None, G_A, nc, LANE), lambda b, i: (b, 0, 0, 0))
    return pl.pallas_call(
        functools.partial(_nsa_prompt_body, nc=nc, ns=ns),
        grid=(B, nqb),
        in_specs=[qspec, qspec,
                  pl.BlockSpec((2, G_A, 2, T, LANE), lambda b, i: (0, 0, 0, b, 0)),
                  cspec, cspec,
                  pl.BlockSpec((Q_BLOCK, G_A * LANE), lambda b, i: (b * nqb + i, 0)),
                  pl.BlockSpec((ns, T), lambda b, i: (0, 0)),
                  pl.BlockSpec((2 * LANE, 6 * LANE), lambda b, i: (0, 0))],
        out_specs=qspec,
        out_shape=jax.ShapeDtypeStruct((B * T, 512), BF),
        scratch_shapes=[pltpu.VMEM((G_A, 2, 2 * Q_BLOCK, LANE), F32)],
        compiler_params=_cparams(("arbitrary", "arbitrary")),
        name="nsa_prompt",
    )(q, qr, kvb, kc2, vc2, ga, _sel_expand(ns, T), _gate_expand())


def _ret_body(qkv_ref, gr_ref, s0_ref, dm_ref, dq_ref, wk_ref, gc_ref, gnw_ref, z_ref, sout_ref, s_scr, *, C, sb):
    c = pl.program_id(1)

    @pl.when(c == 0)
    def _():
        s_scr[...] = s0_ref[...]

    for j in range(sb):
        rows = slice(j * C, (j + 1) * C)
        for h in range(H_R):
            sl = slice(h * LANE, (h + 1) * LANE)
            q = qkv_ref[0, rows, sl]
            k = qkv_ref[1, rows, sl]
            v = qkv_ref[2, rows, sl]
            s_old = s_scr[j, h]
            inner = _nt(q, k) * dm_ref[h]
            o = _nn(inner.astype(BF), v) + _nn(q, s_old.astype(BF)) * dq_ref[h]
            kw = (k.astype(F32) * wk_ref[h]).astype(BF)
            s_new = gc_ref[h, 0:1, :] * s_old + _tn(kw, v)
            s_scr[j, h] = s_new
            sout_ref[j, h] = s_new
            mu = jnp.mean(o, axis=-1, keepdims=True)
            d = o - mu
            var = jnp.mean(d * d, axis=-1, keepdims=True)
            on = d * lax.rsqrt(var + EPS) * gnw_ref[:, sl]
            z_ref[rows, sl] = (gr_ref[rows, sl].astype(F32) * on).astype(BF)


def _retention(ret3, gr, s0, gnw, nseq, rows_per_seq, C, c_true, sb):
    nC = rows_per_seq // C
    assert sb == 1 or nC == 1
    rows = nseq * rows_per_seq
    dm, dq, wk, gc = _ret_decay(C, c_true)
    full = lambda a: pl.BlockSpec(a.shape, lambda b, c: (0,) * a.ndim)
    return pl.pallas_call(
        functools.partial(_ret_body, C=C, sb=sb),
        grid=(nseq // sb, nC),
        in_specs=[pl.BlockSpec((3, sb * C, H_R * LANE), lambda b, c: (0, b * nC + c, 0)),
                  pl.BlockSpec((sb * C, H_R * LANE), lambda b, c: (b * nC + c, 0)),
                  pl.BlockSpec((sb, H_R, DK_R, DV_R), lambda b, c: (b, 0, 0, 0)),
                  full(dm), full(dq), full(wk), full(gc), full(gnw)],
        out_specs=[pl.BlockSpec((sb * C, H_R * LANE), lambda b, c: (b * nC + c, 0)),
                   pl.BlockSpec((sb, H_R, DK_R, DV_R), lambda b, c: (b, 0, 0, 0))],
        out_shape=[jax.ShapeDtypeStruct((rows, 512), BF),
                   jax.ShapeDtypeStruct((nseq, H_R, DK_R, DV_R), F32)],
        scratch_shapes=[pltpu.VMEM((sb, H_R, DK_R, DV_R), F32)],
        compiler_params=_cparams(("arbitrary", "arbitrary")),
        name="retention",
    )(ret3, gr, s0, dm, dq, wk, gc, gnw)


def _mix_ffn_body(x_ref, oa_ref, zr_ref, gm_ref, gt1_ref, wa_ref, wr_ref, wo_ref,
                  nw_ref, sh_ref, sc_ref, gt_ref, wu_ref, cw_ref, cb_ref, wd_ref, nf_ref, p1_ref, p2_ref,
                  y_ref, a_ref, carry, *, blocks_per_seq, seq_rows):
    i = pl.program_id(0)
    ya = _nn(oa_ref[...], wa_ref[...])
    yr = _nn(zr_ref[...], wr_ref[...])
    gm = gm_ref[...].astype(F32)
    merged = gm[:, 0:D_MODEL] * ya + gm[:, D_MODEL:2 * D_MODEL] * yr
    x = x_ref[...] + gt1_ref[...] * _nn(merged.astype(BF), wo_ref[...])
    tm = x.shape[0]
    ms = jnp.mean(x * x, axis=-1, keepdims=True)
    h = (x * lax.rsqrt(ms + EPS)) * nw_ref[...]
    h = (h * (1.0 + sc_ref[...]) + sh_ref[...]).astype(BF)
    a = _nn(h, wu_ref[:, 0:D_FF])
    b = _nn(h, wu_ref[:, D_FF:2 * D_FF])
    a_ref[...] = a[tm - a_ref.shape[0]:tm, :]
    rid = lax.broadcasted_iota(jnp.int32, (tm, D_FF), 0)
    if seq_rows is None:
        first = (i % blocks_per_seq) == 0
        prev = jnp.where(first, p1_ref[...], carry[...])
        carry[...] = a[tm - 8:tm, :]
        am1 = jnp.where(rid == 0, prev[7:8, :], pltpu.roll(a, 1, 0))
        am2 = jnp.where(rid == 0, prev[6:7, :], jnp.where(rid == 1, prev[7:8, :], pltpu.roll(a, 2, 0)))
    else:
        s = rid % seq_rows
        am1 = jnp.where(s == 0, p1_ref[...], pltpu.roll(a, 1, 0))
        am2 = jnp.where(s <= 1, p2_ref[...], pltpu.roll(a, 2, 0))
    u = cb_ref[...] + am2 * cw_ref[0:1, :] + am1 * cw_ref[1:2, :] + a * cw_ref[2:3, :]
    ff = _nn((_gelu(u) * b).astype(BF), wd_ref[...])
    x2 = x + gt_ref[...] * ff
    ms2 = jnp.mean(x2 * x2, axis=-1, keepdims=True)
    y_ref[...] = (x2 * lax.rsqrt(ms2 + EPS)) * nf_ref[...]


def _mix_ffn(x2d, oa, zr, gm, gt1, wa, wr, wo, nw, sh, sc, gt, wu, cw, cb, wd, nf, p1, p2,
             tm, blocks_per_seq, per_row_mods, seq_rows):
    rows = x2d.shape[0]
    a_rows = tm if seq_rows is not None else 8
    if per_row_mods:
        mod_spec = pl.BlockSpec((tm, D_MODEL), lambda i: (i, 0))
    else:
        mod_spec = pl.BlockSpec((None, 1, D_MODEL), lambda i: (i // blocks_per_seq, 0, 0))
    row = lambda w: pl.BlockSpec((tm, w), lambda i: (i, 0))
    full = lambda a, b: pl.BlockSpec((a, b), lambda i: (0, 0), pipeline_mode=pl.Buffered(1))
    vec = lambda w: pl.BlockSpec((1, w), lambda i: (0, 0))
    pspec = pl.BlockSpec(p1.shape, lambda i: (0, 0))
    return pl.pallas_call(
        functools.partial(_mix_ffn_body, blocks_per_seq=blocks_per_seq, seq_rows=seq_rows),
        grid=(rows // tm,),
        in_specs=[row(D_MODEL), row(512), row(512), row(2048), mod_spec,
                  full(512, D_MODEL), full(512, D_MODEL), full(D_MODEL, D_MODEL),
                  vec(D_MODEL), mod_spec, mod_spec, mod_spec,
                  full(D_MODEL, 2 * D_FF), pl.BlockSpec((CONV_W, D_FF), lambda i: (0, 0)), vec(D_FF),
                  full(D_FF, D_MODEL), vec(D_MODEL), pspec, pspec],
        out_specs=[row(D_MODEL), pl.BlockSpec((a_rows, D_FF), lambda i: (i, 0))],
        out_shape=[jax.ShapeDtypeStruct((rows, D_MODEL), F32),
                   jax.ShapeDtypeStruct((rows // tm * a_rows, D_FF), F32)],
        scratch_shapes=[pltpu.VMEM((8, D_FF), F32)],
        compiler_params=_cparams(("arbitrary",)),
        name="mix_ffn",
    )(x2d, oa, zr, gm, gt1, wa, wr, wo, nw, sh, sc, gt, wu, cw, cb, wd, nf, p1, p2)


def _prompt_path(x_prompt, mods_p, W):
    B, T, _ = x_prompt.shape
    rows = B * T
    x2d = x_prompt.reshape(rows, D_MODEL)
    pos = np.arange(T)
    tm = 256
    sh1, sc1, gt1, sh2, sc2, gt2 = [mods_p[:, None, j * D_MODEL:(j + 1) * D_MODEL] for j in range(6)]
    (q, qr, kvc, kvs, kvw, kvb, ret3, gr, gm, ga) = _inproj(
        x2d, W["norm1"], sh1, sc1, W["w_in"], _rope_tab(pos, Q_SCALE), _rope_tab(pos, 1.0), _ret_tab(pos),
        tm, T // tm, False, True)
    nc, n_pages = T // L_CMP, T // _PAGE
    comp = _compress_t(kvc, W["phi_posT"], W["phi_w1p"], W["phi_w2p"])
    comp = comp.reshape(2, G_A, B, n_pages, 2, 2, DH).transpose(0, 2, 1, 5, 3, 4, 6).reshape(2, B, G_A, nc, DH)
    comp2 = jnp.concatenate([comp, comp], axis=-1).astype(BF)
    oa = _nsa_prompt(q, qr, kvb, comp2[0], comp2[1], ga, B, T)
    C = 256 if T % 256 == 0 else T
    zr, ret_new = _retention(ret3, gr, jnp.zeros((B, H_R, DK_R, DV_R), F32), W["gnw"], B, T, C, C, 1)
    zeros8 = jnp.zeros((8, D_FF), F32)
    y, a_up = _mix_ffn(x2d, oa, zr, gm, gt1, W["w_up_a"], W["w_up_r"], W["w_out"],
                       W["norm2"], sh2, sc2, gt2, W["w_ffn_up"], W["conv_w"], W["conv_b"], W["w_ffn_down"],
                       W["normf"], zeros8, zeros8, tm, T // tm, False, None)
    wsz = min(WINDOW, T)
    rows_major = lambda t: t.reshape(B, 2, G_A, DH, t.shape[-1]).transpose(0, 4, 1, 2, 3)[None]
    outs = dict(
        y=y.reshape(B, T, D_MODEL),
        cmp=rows_major(kvc), slc=rows_major(kvs), win=rows_major(kvw[:, :, T - wsz:]),
        ret=ret_new[None],
        conv=a_up.reshape(B, T // tm, 8, D_FF)[None, :, T // tm - 1, 8 - (CONV_W - 1):],
    )
    return outs


_QC = 32


def _cmp_paged_body(pt_ref, cache_ref, q_ref, pos_ref, w1_ref, w2_ref, ocmp_ref, topi_ref, xbuf, sem,
                    *, n_pages, n_pick, n_q):
    b = pl.program_id(0)
    nb = pl.num_programs(0)
    slot = b % 2

    def copy(bb, sl, p):
        return pltpu.make_async_copy(cache_ref.at[pt_ref[bb, p]], xbuf.at[sl, :, p, :], sem.at[sl])

    def issue(bb, sl):
        for p in range(n_pages):
            copy(bb, sl, p).start()

    @pl.when(b == 0)
    def _():
        issue(b, slot)

    @pl.when(b + 1 < nb)
    def _():
        issue(b + 1, 1 - slot)

    for p in range(n_pages):
        copy(b, slot, p).wait()

    comp = _compress_slab_rows(lambda r: xbuf[slot, r], n_pages, pos_ref, w1_ref, w2_ref)

    def lane_groups(x, op):
        r = x
        for j in range(1, _BPP):
            r = op(r, pltpu.roll(x, j * _QC, 1))
        return r

    for g in range(G_A):
        kc = comp[0][g * n_pages:(g + 1) * n_pages].astype(BF)
        vc = comp[1][g * n_pages:(g + 1) * n_pages].astype(BF)
        st = _nn(kc, q_ref[g])
        m = lane_groups(jnp.max(st, axis=0, keepdims=True), jnp.maximum)
        p = jnp.exp2(st - m)
        p = p / lane_groups(jnp.sum(p, axis=0, keepdims=True), jnp.add)
        r_full = _tn(p.astype(BF), vc)
        o = r_full[0:_QC, 0:DH]
        for j in range(1, _BPP):
            o = o + r_full[j * _QC:(j + 1) * _QC, j * DH:(j + 1) * DH]
        ocmp_ref[g] = o
        pair = p + pltpu.roll(p, LANE - _QC, 1)
        imp = pair
        for r in range(1, HPG):
            imp = imp + pltpu.roll(pair, LANE - r * n_q, 1)
        sc = jnp.concatenate([imp, pltpu.roll(imp, 2 * _QC, 1)], axis=0)
        row = lax.broadcasted_iota(jnp.int32, (2 * n_pages, LANE), 0)
        blk = jnp.where(row < n_pages, 2 * row, 2 * (row - n_pages) + 1)
        score = sc + jnp.where(blk == 0, FORCE_BONUS, 0.0)
        topi_ref[g] = jnp.zeros((8, LANE), jnp.int32)
        for k in range(n_pick):
            mx = jnp.max(score, axis=0, keepdims=True)
            idx = jnp.min(jnp.where(score == mx, blk, 2 * n_pages), axis=0, keepdims=True)
            topi_ref[g, k:k + 1, :] = idx
            score = jnp.where(blk == idx, -jnp.inf, score)


def _cmp_paged(page_table, cache_t, q_bd, pos, w1, w2, n_q):
    DB, n_pages = page_table.shape
    SL = cache_t.shape[1]
    page_rows = cache_t.shape[2]
    kern = functools.partial(_cmp_paged_body, n_pages=n_pages, n_pick=N_SEL - 1, n_q=n_q)
    one = dict(pipeline_mode=pl.Buffered(1))
    return pl.pallas_call(
        kern,
        grid_spec=pltpu.PrefetchScalarGridSpec(
            num_scalar_prefetch=1, grid=(DB,),
            in_specs=[pl.BlockSpec(memory_space=pl.ANY),
                      pl.BlockSpec((None, G_A, _BPP * DH, LANE), lambda b, pt: (b, 0, 0, 0)),
                      pl.BlockSpec(pos.shape, lambda b, pt: (0, 0, 0), **one),
                      pl.BlockSpec(w1.shape, lambda b, pt: (0, 0, 0, 0), **one),
                      pl.BlockSpec((2, _BPP * DH, _BPP * DH), lambda b, pt: (0, 0, 0), **one)],
            out_specs=[pl.BlockSpec((None, G_A, _QC, DH), lambda b, pt: (b, 0, 0, 0)),
                       pl.BlockSpec((None, G_A, 8, LANE), lambda b, pt: (b, 0, 0, 0))],
            scratch_shapes=[pltpu.VMEM((2, SL, n_pages, page_rows), F32),
                            pltpu.SemaphoreType.DMA((2,))]),
        out_shape=[jax.ShapeDtypeStruct((DB, G_A, _QC, DH), F32),
                   jax.ShapeDtypeStruct((DB, G_A, 8, LANE), jnp.int32)],
        compiler_params=_cparams(("arbitrary",)),
        name="cmp_paged",
    )(page_table, cache_t, q_bd, pos, w1, w2)


def _slc_win_paged_body(pt_ref, ti_ref, cslc_ref, win_ref, q_ref, tiv_ref, ns_ref, nw_ref, ex_ref,
                        oslc_ref, owin_ref, kbuf, sem, *, n_q, n_pick, page_rows):
    b = pl.program_id(0)
    nb = pl.num_programs(0)
    slot = b % 2
    n_slab = n_q * n_pick
    bpp = page_rows // L_SEL
    wb = win_ref.shape[1]

    def copies(bb, sl):
        cps = []
        for g in range(G_A):
            for j in range(n_slab):
                page = pt_ref[bb, ti_ref[bb, g * n_slab + j] // bpp]
                for kv in range(2):
                    cps.append(pltpu.make_async_copy(
                        cslc_ref.at[page, pl.ds((kv * G_A + g) * DH, DH), :],
                        kbuf.at[sl, g, kv, :, pl.ds(j * page_rows, page_rows)], sem.at[sl]))
        return cps

    @pl.when(b == 0)
    def _():
        for cp in copies(b, slot):
            cp.start()

    @pl.when(b + 1 < nb)
    def _():
        for cp in copies(b + 1, 1 - slot):
            cp.start()

    for cp in copies(b, slot):
        cp.wait()

    nq_rows = q_ref.shape[1]
    nk = n_slab * page_rows
    row_q = lax.broadcasted_iota(jnp.int32, (nq_rows, nk), 0) % n_q
    col = lax.broadcasted_iota(jnp.int32, (nq_rows, nk), 1)
    own = row_q == col // (n_pick * page_rows)
    half = ((col % page_rows) // L_SEL).astype(F32)
    nnew = ns_ref.shape[2]
    new_ok = (lax.broadcasted_iota(jnp.int32, (nq_rows, nnew), 1)
              <= lax.broadcasted_iota(jnp.int32, (nq_rows, nnew), 0) % n_q)
    dwin = (wb + lax.broadcasted_iota(jnp.int32, (nq_rows, wb), 0) % n_q
            - lax.broadcasted_iota(jnp.int32, (nq_rows, wb), 1))
    win_ok = (dwin >= 0) & (dwin < WINDOW)

    def attend(q, kt_old, vt_old, ok_old, k_new, v_new):
        s_o = jnp.where(ok_old, _nn(q, kt_old), NEG)
        s_n = jnp.where(new_ok, _nt(q, k_new), NEG)
        m = jnp.maximum(jnp.max(s_o, axis=1, keepdims=True), jnp.max(s_n, axis=1, keepdims=True))
        p_o = jnp.exp2(s_o - m)
        p_n = jnp.exp2(s_n - m)
        den = jnp.sum(p_o, axis=1, keepdims=True) + jnp.sum(p_n, axis=1, keepdims=True)
        return (_nt(p_o.astype(BF), vt_old) + _nn(p_n.astype(BF), v_new)) / den

    for g in range(G_A):
        q = q_ref[g]
        par = (tiv_ref[g] % bpp).astype(F32).astype(BF)
        want = _nn(par, ex_ref[...])[0:1, :]
        ok = own & (half == want)
        oslc_ref[g] = attend(q, kbuf[slot, g, 0].astype(BF), kbuf[slot, g, 1].astype(BF), ok,
                             ns_ref[g, 0].astype(BF), ns_ref[g, 1].astype(BF))
        kw = win_ref[pl.ds((0 * G_A + g) * DH, DH), :].astype(BF)
        vw = win_ref[pl.ds((1 * G_A + g) * DH, DH), :].astype(BF)
        owin_ref[g] = attend(q, kw, vw, win_ok, nw_ref[g, 0].astype(BF), nw_ref[g, 1].astype(BF))


def _slc_win_paged(page_table, topi_flat, topi_vec, cache_t, win_t, q_rot, new_slc, new_win, n_q):
    DB = page_table.shape[0]
    n_pick = N_SEL - 1
    page_rows = cache_t.shape[2]
    wb = win_t.shape[2]
    nq_rows = q_rot.shape[2]
    n_slab = n_q * n_pick
    ex = (np.arange(n_slab * page_rows)[None, :] // page_rows) == np.arange(LANE)[:, None]
    kern = functools.partial(_slc_win_paged_body, n_q=n_q, n_pick=n_pick, page_rows=page_rows)
    bspec = lambda shp: pl.BlockSpec((None,) + shp, lambda b, pt, ti: (b,) + (0,) * len(shp))
    return pl.pallas_call(
        kern,
        grid_spec=pltpu.PrefetchScalarGridSpec(
            num_scalar_prefetch=2, grid=(DB,),
            in_specs=[pl.BlockSpec(memory_space=pl.ANY), bspec((2 * G_A * DH, wb)),
                      bspec((G_A, nq_rows, DH)), bspec((G_A, 16, LANE)),
                      bspec((G_A, 2, 16, DH)), bspec((G_A, 2, 16, DH)),
                      pl.BlockSpec((LANE, n_slab * page_rows), lambda b, pt, ti: (0, 0))],
            out_specs=[bspec((G_A, nq_rows, DH)), bspec((G_A, nq_rows, DH))],
            scratch_shapes=[pltpu.VMEM((2, G_A, 2, DH, n_slab * page_rows), F32),
                            pltpu.SemaphoreType.DMA((2,))]),
        out_shape=[jax.ShapeDtypeStruct((DB, G_A, nq_rows, DH), F32)] * 2,
        compiler_params=_cparams(("arbitrary",)),
        name="slc_win_paged",
    )(page_table, topi_flat, cache_t, win_t, q_rot, topi_vec, new_slc, new_win, jnp.asarray(ex, BF))


def _gate_sample_body(oc_ref, os_ref, ow_ref, ga_ref, o_ref):
    rows = o_ref.shape[0]
    lo64 = lax.broadcasted_iota(jnp.int32, (rows, LANE), 1) < DH
    for c4 in range(H_A // 2):
        g, c = c4 // 2, c4 % 2
        sl = slice(c4 * LANE, (c4 + 1) * LANE)
        acc = jnp.zeros((rows, LANE), F32)
        for br, ref in enumerate((oc_ref, os_ref, ow_ref)):
            col = g * LANE + br * HPG + 2 * c
            gate = jnp.where(lo64, ga_ref[:, col:col + 1], ga_ref[:, col + 1:col + 2])
            acc = acc + gate * ref[:, sl]
        o_ref[:, sl] = acc.astype(BF)


def _gate_sample(oc, osl, ow, ga):
    rows = oc.shape[0]
    full = lambda w: pl.BlockSpec((rows, w), lambda i: (0, 0))
    return pl.pallas_call(
        _gate_sample_body, grid=(1,),
        in_specs=[full(512), full(512), full(512), full(256)],
        out_specs=full(512),
        out_shape=jax.ShapeDtypeStruct((rows, 512), BF),
        compiler_params=_cparams(("arbitrary",)),
        name="gate_sample",
    )(oc, osl, ow, ga)


def _sample_path(x_sample, mods_s, W, cache_cmp, cache_slc, state_win, state_ret, state_conv, page_table):
    DB, S, _ = x_sample.shape
    rows = DB * S
    page_rows = cache_cmp.shape[1]
    P = page_table.shape[1] * page_rows
    wb = state_win.shape[1]
    assert P % L_SEL == 0 and S < L_CMP and S <= 8 and wb == WINDOW and page_rows % L_SEL == 0
    assert P // L_SEL >= N_SEL and CONV_W == 3 and S >= CONV_W - 1
    pos = P + np.arange(S)
    pos_rows = np.tile(pos, DB)
    x2d = x_sample.reshape(rows, D_MODEL)
    modr = jnp.repeat(mods_s, S, axis=0)
    sh1, sc1, gt1, sh2, sc2, gt2 = [modr[:, j * D_MODEL:(j + 1) * D_MODEL] for j in range(6)]
    (q, qr, kvc, kvs, kvw, _, ret3, gr, gm, ga) = _inproj(
        x2d, W["norm1"], sh1, sc1, W["w_in"], _rope_tab(pos_rows, Q_SCALE), _rope_tab(pos_rows, 1.0),
        _ret_tab(pos_rows), rows, 1, True, False)

    def to_heads(t):
        return t.reshape(DB, S, G_A, HPG, DH).transpose(0, 2, 3, 1, 4).reshape(DB, G_A, HPG * S, DH)

    def from_heads(t):
        return t.reshape(DB, G_A, HPG, S, DH).transpose(0, 3, 1, 2, 4).reshape(rows, H_A * DH)

    assert page_rows == _BPP * L_CMP and HPG * S <= _QC and S * (N_SEL - 1) <= LANE
    slab = lambda t: t.transpose(0, 2, 3, 4, 1).reshape(t.shape[0], 2 * G_A * DH, t.shape[1])
    qt = jnp.pad(to_heads(q).transpose(0, 1, 3, 2), ((0, 0), (0, 0), (0, 0), (0, _QC - HPG * S)))
    q_bd = _blockdiag(qt)
    o_cmp, topi = _cmp_paged(page_table, slab(cache_cmp), q_bd, W["phi_posT"], W["phi_w1p"], W["phi_w2p"], S)
    n_pick = N_SEL - 1
    topi = topi[:, :, :n_pick, :S].transpose(0, 1, 3, 2).reshape(DB, G_A, S * n_pick)
    topi_vec = jnp.broadcast_to(jnp.pad(topi, ((0, 0), (0, 0), (0, LANE - S * n_pick)))[:, :, None, :],
                                (DB, G_A, 16, LANE))

    def new_rows(t):
        t = t.reshape(DB, S, 2, G_A, DH).transpose(0, 3, 2, 1, 4)
        return jnp.pad(t, ((0, 0), (0, 0), (0, 0), (0, 16 - S), (0, 0)))

    o_slc, o_win = _slc_win_paged(page_table, topi.reshape(DB, G_A * S * n_pick), topi_vec, slab(cache_slc),
                                  slab(state_win), to_heads(qr), new_rows(kvs), new_rows(kvw), S)
    oa = _gate_sample(from_heads(o_cmp[:, :, :HPG * S]), from_heads(o_slc), from_heads(o_win), ga)

    RP = 16
    padr = lambda t: jnp.pad(t.reshape(t.shape[:-2] + (DB, S, 512)),
                             ((0, 0),) * (t.ndim - 1) + ((0, RP - S), (0, 0))).reshape(t.shape[:-2] + (DB * RP, 512))
    zr, ret_new = _retention(padr(ret3), padr(gr), state_ret, W["gnw"], DB, RP, RP, S, 8 if DB % 8 == 0 else 1)
    zr = zr.reshape(DB, RP, 512)[:, :S].reshape(rows, 512)
    zs = lambda n: jnp.zeros((DB, n, D_FF), F32)
    p1 = jnp.concatenate([state_conv[:, 1:2], zs(S - 1)], axis=1).reshape(rows, D_FF)
    p2 = jnp.concatenate([state_conv[:, 0:2], zs(S - 2)], axis=1).reshape(rows, D_FF)
    y, a_up = _mix_ffn(x2d, oa, zr, gm, gt1, W["w_up_a"], W["w_up_r"], W["w_out"],
                       W["norm2"], sh2, sc2, gt2, W["w_ffn_up"], W["conv_w"], W["conv_b"], W["w_ffn_down"],
                       W["normf"], p1, p2, rows, 1, True, S)
    shp = (1, DB, S, 2, G_A, DH)
    return dict(
        y=y.reshape(DB, S, D_MODEL),
        cmp=kvc.reshape(shp), slc=kvs.reshape(shp),
        win=jnp.concatenate([state_win[:, S:], kvw.reshape(DB, S, 2, G_A, DH)], axis=1)[None],
        ret=ret_new[None],
        conv=a_up.reshape(DB, S, D_FF)[None, :, S - (CONV_W - 1):],
    )


def kernel(x_prompt, x_sample, cache_cmp_kv, cache_slc_kv, state_win_kv, state_ret, state_conv, page_table,
           c_prompt, c_sample, norm1_w, norm2_w, w_ada, b_ada, w_in, phi_pos_k, phi_k1, phi_k2, phi_pos_v,
           phi_v1, phi_v2, w_up_a, ret_gn_w, w_up_r, w_out, w_ffn_up, ffn_conv_w, ffn_conv_b, w_ffn_down, normf_w):
    B = x_prompt.shape[0]
    l = 0
    W = dict(
        norm1=norm1_w[l].reshape(1, D_MODEL), norm2=norm2_w[l].reshape(1, D_MODEL), normf=normf_w.reshape(1, D_MODEL),
        w_in=_pack_w_in(w_in[l]),
        w_up_a=w_up_a[l].astype(BF), w_up_r=w_up_r[l].astype(BF), w_out=w_out[l].astype(BF),
        gnw=ret_gn_w[l].reshape(1, H_R * DV_R),
        w_ffn_up=w_ffn_up[l].astype(BF), conv_w=ffn_conv_w[l], conv_b=ffn_conv_b[l].reshape(1, D_FF),
        w_ffn_down=w_ffn_down[l].astype(BF),
    )
    W["phi_posT"], W["phi_w1p"], W["phi_w2p"] = _pack_phi_paged(
        phi_pos_k[l], phi_k1[l], phi_k2[l], phi_pos_v[l], phi_v1[l], phi_v2[l])
    mods = _mods(jnp.concatenate([c_prompt, c_sample], axis=0), w_ada[l], b_ada[l])
    p = _prompt_path(x_prompt, mods[:B], W)
    s = _sample_path(x_sample, mods[B:], W, cache_cmp_kv[l], cache_slc_kv[l], state_win_kv[l], state_ret[l],
                     state_conv[l], page_table)
    return (p["y"], s["y"], p["cmp"], s["cmp"], p["slc"], s["slc"], p["win"], s["win"],
            p["ret"], s["ret"], p["conv"], s["conv"])
```

```python
import functools

import numpy as np
import jax
import jax.numpy as jnp
from jax import lax
from jax.experimental import pallas as pl
from jax.experimental.pallas import tpu as pltpu

BF = jnp.bfloat16
F32 = jnp.float32

D_MODEL = 1024
H_A, G_A, DH = 8, 2, 64
HPG = H_A // G_A
ROPE_DIM = DH // 4
ROPE_THETA = 500000.0
L_CMP, L_SEL, N_SEL = 32, 64, 8
WINDOW = 512
Q_BLOCK = 128
FORCE_BONUS = 1e4
H_R, DK_R, DV_R = 4, 128, 128
RET_THETA = 10000.0
D_FF = 2816
CONV_W = 3
EPS = 1e-6
NEG = -1e30
MASK_BIG = 2.0 ** 100
Q_SCALE = DH ** -0.5 * 1.4426950408889634
LANE = 128
VMEM_LIMIT = 56 * 1024 * 1024


def _cparams(sem):
    return pltpu.CompilerParams(dimension_semantics=sem, vmem_limit_bytes=VMEM_LIMIT)


def _sigmoid(x):
    return 1.0 / (1.0 + jnp.exp(-x))


def _gelu(x):
    return 0.5 * x * (1.0 + jnp.tanh(0.7978845608028654 * (x + 0.044715 * (x * x * x))))


def _nt(a, b):
    return lax.dot_general(a, b, (((1,), (1,)), ((), ())), preferred_element_type=F32)


def _tn(a, b):
    return lax.dot_general(a, b, (((0,), (0,)), ((), ())), preferred_element_type=F32)


def _nn(a, b):
    return jnp.dot(a, b, preferred_element_type=F32)


def _rope_tab(pos, scale):
    half = ROPE_DIM // 2
    inv = ROPE_THETA ** (-np.arange(half, dtype=np.float64) * (2.0 / ROPE_DIM))
    ang = pos.astype(np.float64)[:, None] * inv
    cos, sin = np.cos(ang), np.sin(ang)
    n = pos.shape[0]
    c = np.ones((n, DH)); s_lo = np.zeros((n, DH)); s_hi = np.zeros((n, DH))
    c[:, :half] = cos; c[:, half:ROPE_DIM] = cos
    s_lo[:, half:ROPE_DIM] = sin
    s_hi[:, :half] = -sin
    tab = np.concatenate([np.tile(t, (1, 2)) for t in (c, s_lo, s_hi)], axis=1) * scale
    return jnp.asarray(tab, F32)


def _ret_tab(pos):
    half = DK_R // 2
    inv = RET_THETA ** (-np.arange(half, dtype=np.float64) * (2.0 / DK_R))
    ang = pos.astype(np.float64)[:, None] * inv
    cos, sin = np.cos(ang), np.sin(ang)
    c = np.concatenate([cos, cos], axis=1)
    s = np.concatenate([-sin, sin], axis=1)
    ks = DK_R ** -0.5
    return jnp.asarray(np.concatenate([c, s, c * ks, s * ks], axis=1), F32)


def _ret_decay(C, c_true):
    h = np.arange(H_R, dtype=np.float64)
    log_g = np.log1p(-np.exp2(-5.0 - h))
    i = np.arange(C, dtype=np.float64)
    diff = i[:, None] - i[None, :]
    dm = np.where(diff >= 0, np.exp(log_g[:, None, None] * np.maximum(diff, 0.0)), 0.0)
    dq = np.exp(log_g[:, None] * (i + 1.0))[:, :, None] * np.ones((1, 1, LANE))
    wk = np.exp(log_g[:, None] * (c_true - 1.0 - i))[:, :, None] * np.ones((1, 1, LANE))
    wk = np.where(i[None, :, None] < c_true, wk, 0.0)
    gc = np.exp(log_g * c_true)[:, None, None] * np.ones((1, 8, LANE))
    return (jnp.asarray(dm, F32), jnp.asarray(dq, F32), jnp.asarray(wk, F32), jnp.asarray(gc, F32))


def _mods_body(c_ref, w_ref, b_ref, o_ref):
    c = c_ref[...]
    s = c * _sigmoid(c)
    o_ref[...] = _nn(s.astype(BF), w_ref[...].astype(BF)) + b_ref[...]


def _mods(c_all, w_ada, b_ada):
    n = c_all.shape[0]
    nout = w_ada.shape[1]
    tn = 1024
    return pl.pallas_call(
        _mods_body,
        grid=(nout // tn,),
        in_specs=[pl.BlockSpec((n, D_MODEL), lambda j: (0, 0)),
                  pl.BlockSpec((D_MODEL, tn), lambda j: (0, j)),
                  pl.BlockSpec((1, tn), lambda j: (0, j))],
        out_specs=pl.BlockSpec((n, tn), lambda j: (0, j)),
        out_shape=jax.ShapeDtypeStruct((n, nout), F32),
        compiler_params=_cparams(("arbitrary",)),
        name="mods",
    )(c_all, w_ada, b_ada.reshape(1, nout))


_C_Q, _C_KV, _C_QR, _C_KR, _C_VR, _C_GR, _C_GM, _C_GA, _C_END = (
    0, 512, 1280, 1792, 2304, 2816, 3328, 5376, 5632)


def _pack_w_in(w_in):
    o = np.cumsum((0, 512, 768, 24, 512, 512, 512, 512, 2048))
    q, kv, ga, qr, kr, vr, gr, gm = [w_in[:, o[i]:o[i + 1]] for i in range(8)]
    ga = ga.reshape(D_MODEL, 3, G_A, HPG).transpose(0, 2, 1, 3).reshape(D_MODEL, G_A, 3 * HPG)
    ga = jnp.pad(ga, ((0, 0), (0, 0), (0, LANE - 3 * HPG))).reshape(D_MODEL, G_A * LANE)
    return jnp.concatenate([q, kv, qr, kr, vr, gr, gm, ga], axis=1).astype(BF)


def _inproj_body(x_ref, nw_ref, sh_ref, sc_ref, w_ref, rq_ref, rk_ref, rr_ref,
                 q_ref, qr_ref, kvc_ref, kvs_ref, kvw_ref, kvb_ref, ret_ref, gr_ref, gm_ref, ga_ref, *, kv_t):
    def put_kv(out_ref, k, v):
        if kv_t:
            out_ref[0:LANE, :] = k.T
            out_ref[LANE:2 * LANE, :] = v.T
        else:
            out_ref[:, 0:LANE] = k
            out_ref[:, LANE:2 * LANE] = v

    x = x_ref[...]
    tm = x.shape[0]
    ms = jnp.mean(x * x, axis=-1, keepdims=True)
    h = (x * lax.rsqrt(ms + EPS)) * nw_ref[...]
    h = h * (1.0 + sc_ref[...]) + sh_ref[...]
    hb = h.astype(BF)
    lo64 = lax.broadcasted_iota(jnp.int32, (tm, LANE), 1) < DH

    def mm(lo, hi):
        return _nn(hb, w_ref[:, lo:hi])

    def rope(xc, tab_ref):
        return (xc * tab_ref[:, 0:LANE] + pltpu.roll(xc, 8, 1) * tab_ref[:, LANE:2 * LANE]
                + pltpu.roll(xc, LANE - 8, 1) * tab_ref[:, 2 * LANE:3 * LANE])

    gm_half = (_C_GA - _C_GM) // 2
    gm_ref[:, 0:gm_half] = _sigmoid(mm(_C_GM, _C_GM + gm_half)).astype(BF)
    g = mm(_C_GR, _C_GM)
    gr_ref[...] = (g * _sigmoid(g)).astype(BF)
    gm_ref[:, gm_half:2 * gm_half] = _sigmoid(mm(_C_GM + gm_half, _C_GA)).astype(BF)

    qa = mm(_C_Q, _C_KV)
    q_ref[...] = (qa * Q_SCALE).astype(BF)
    for c in range(4):
        qr_ref[:, c * LANE:(c + 1) * LANE] = rope(qa[:, c * LANE:(c + 1) * LANE], rq_ref).astype(BF)

    kv = mm(_C_KV, _C_QR)
    put_kv(kvc_ref, kv[:, 0:LANE], kv[:, LANE:2 * LANE])
    for kind, out_ref in ((0, kvs_ref), (1, kvw_ref)):
        base = 256 + kind * 256
        k = rope(kv[:, base:base + LANE], rk_ref)
        v = kv[:, base + LANE:base + 2 * LANE]
        put_kv(out_ref, k, v)
        kr_ = pltpu.roll(k, DH, 1)
        vr_ = pltpu.roll(v, DH, 1)
        kvb_ref[kind, 0, 0] = jnp.where(lo64, k, vr_).astype(BF)
        kvb_ref[kind, 0, 1] = jnp.where(lo64, v, kr_).astype(BF)
        kvb_ref[kind, 1, 0] = jnp.where(lo64, kr_, v).astype(BF)
        kvb_ref[kind, 1, 1] = jnp.where(lo64, vr_, k).astype(BF)

    qr = mm(_C_QR, _C_KR)
    kr = mm(_C_KR, _C_VR)
    for hh in range(H_R):
        sl = slice(hh * LANE, (hh + 1) * LANE)
        xq = qr[:, sl]
        ret_ref[0, :, sl] = (xq * rr_ref[:, 0:LANE] + pltpu.roll(xq, DK_R // 2, 1) * rr_ref[:, LANE:2 * LANE]).astype(BF)
        xk = kr[:, sl]
        ret_ref[1, :, sl] = (xk * rr_ref[:, 2 * LANE:3 * LANE]
                             + pltpu.roll(xk, DK_R // 2, 1) * rr_ref[:, 3 * LANE:4 * LANE]).astype(BF)
    ga_ref[...] = _sigmoid(mm(_C_GA, _C_END))
    ret_ref[2] = mm(_C_VR, _C_GR).astype(BF)


def _inproj(x2d, nw, sh, sc, w_pack, rq, rk, rr, tm, tab_blocks, per_row_mods, kv_t):
    rows = x2d.shape[0]
    nb = rows // tm
    if kv_t:
        kv_shape = jax.ShapeDtypeStruct((nb // tab_blocks, 256, tab_blocks * tm), F32)
        kv_spec = pl.BlockSpec((None, 256, tm), lambda i: (i // tab_blocks, 0, i % tab_blocks))
    else:
        kv_shape = jax.ShapeDtypeStruct((rows, 256), F32)
        kv_spec = pl.BlockSpec((tm, 256), lambda i: (i, 0))
    if per_row_mods:
        mod_spec = pl.BlockSpec((tm, D_MODEL), lambda i: (i, 0))
    else:
        mod_spec = pl.BlockSpec((None, 1, D_MODEL), lambda i: (i // tab_blocks, 0, 0))
    tab = lambda w: pl.BlockSpec((tm, w), lambda i: (i % tab_blocks, 0))
    row = lambda w: pl.BlockSpec((tm, w), lambda i: (i, 0))
    out_shapes = [
        jax.ShapeDtypeStruct((rows, 512), BF),
        jax.ShapeDtypeStruct((rows, 512), BF),
        kv_shape,
        kv_shape,
        kv_shape,
        jax.ShapeDtypeStruct((2, G_A, 2, rows, LANE), BF),
        jax.ShapeDtypeStruct((3, rows, 512), BF),
        jax.ShapeDtypeStruct((rows, 512), BF),
        jax.ShapeDtypeStruct((rows, 2048), BF),
        jax.ShapeDtypeStruct((rows, 256), F32),
    ]
    out_specs = [row(512), row(512), kv_spec, kv_spec, kv_spec,
                 pl.BlockSpec((2, G_A, 2, tm, LANE), lambda i: (0, 0, 0, i, 0)),
                 pl.BlockSpec((3, tm, 512), lambda i: (0, i, 0)),
                 row(512), row(2048), row(256)]
    return pl.pallas_call(
        functools.partial(_inproj_body, kv_t=kv_t),
        grid=(nb,),
        in_specs=[row(D_MODEL),
                  pl.BlockSpec((1, D_MODEL), lambda i: (0, 0)),
                  mod_spec, mod_spec,
                  pl.BlockSpec((D_MODEL, _C_END), lambda i: (0, 0), pipeline_mode=pl.Buffered(1)),
                  tab(384), tab(384), tab(512)],
        out_specs=out_specs,
        out_shape=out_shapes,
        compiler_params=_cparams(("arbitrary",)),
        name="inproj",
    )(x2d, nw, sh, sc, w_pack, rq, rk, rr)


_BPP = 4
_PAGE = _BPP * L_CMP


def _blockdiag(w):
    z = jnp.zeros_like(w)
    return jnp.concatenate(
        [jnp.concatenate([w if j == i else z for j in range(_BPP)], axis=-1) for i in range(_BPP)], axis=-2)


def _pack_phi_paged(phi_pos_k, phi_k1, phi_k2, phi_pos_v, phi_v1, phi_v2):
    w1, w2, pos = [], [], []
    for p_, a, b_ in ((phi_pos_k, phi_k1, phi_k2), (phi_pos_v, phi_v1, phi_v2)):
        w1.append(_blockdiag(a.astype(BF).reshape(L_CMP, DH, DH).transpose(1, 0, 2)))
        w2.append(_blockdiag(b_.astype(BF)))
        pos.append(jnp.tile(p_.T, (1, _BPP)))
    pos = jnp.stack(pos).reshape(2, DH // 2, 2 * _PAGE)
    w1 = jnp.stack(w1).reshape(2, DH // 2, 2 * _PAGE, _BPP * DH)
    return pos, w1, jnp.stack(w2)


def _compress_slab_rows(load, n_rows, pos_ref, w1_ref, w2_ref):
    acc = [jnp.zeros((G_A * n_rows, _BPP * DH), F32) for _ in range(2)]
    for dp in range(DH // 2):
        for kv in range(2):
            x = jnp.concatenate(
                [jnp.concatenate([load((kv * G_A + g) * DH + 2 * dp + j) for j in range(2)], axis=1)
                 for g in range(G_A)], axis=0)
            acc[kv] = acc[kv] + _nn((x + pos_ref[kv, dp:dp + 1, :]).astype(BF), w1_ref[kv, dp])
    return [_nn(_gelu(acc[kv]).astype(BF), w2_ref[kv]) for kv in range(2)]


def _compress_t_body(src_ref, pos_ref, w1_ref, w2_ref, o_ref, xbuf, sem, *, nseq, n_pages):
    def copy(t):
        b, p = t // n_pages, t % n_pages
        return pltpu.make_async_copy(src_ref.at[b, :, pl.ds(pl.multiple_of(p * _PAGE, _PAGE), _PAGE)],
                                     xbuf.at[:, t, :], sem)

    def start(t, c):
        copy(t).start()
        return c

    def wait(t, c):
        copy(t).wait()
        return c
    lax.fori_loop(0, nseq * n_pages, start, 0)
    lax.fori_loop(0, nseq * n_pages, wait, 0)
    kc, vc = _compress_slab_rows(lambda r: xbuf[r], nseq * n_pages, pos_ref, w1_ref, w2_ref)
    o_ref[0] = kc
    o_ref[1] = vc


def _compress_t(kv_t, pos, w1, w2):
    nseq, SL, T = kv_t.shape
    n_pages = T // _PAGE
    one = lambda a: pl.BlockSpec(a.shape, lambda i: (0,) * a.ndim, pipeline_mode=pl.Buffered(1))
    n_out = G_A * nseq * n_pages
    return pl.pallas_call(
        functools.partial(_compress_t_body, nseq=nseq, n_pages=n_pages),
        grid=(1,),
        in_specs=[pl.BlockSpec(memory_space=pl.ANY), one(pos), one(w1), one(w2)],
        out_specs=pl.BlockSpec((2, n_out, _BPP * DH), lambda i: (0, 0, 0)),
        out_shape=jax.ShapeDtypeStruct((2, n_out, _BPP * DH), F32),
        scratch_shapes=[pltpu.VMEM((SL, nseq * n_pages, _PAGE), F32), pltpu.SemaphoreType.DMA(())],
        compiler_params=_cparams(("arbitrary",)),
        name="compress_t",
    )(kv_t, pos, w1, w2)


_KT = 512


def _nsa_prompt_body(q_ref, qr_ref, kvb_ref, kc_ref, vc_ref, ga_ref, e_ref, gx_ref, o_ref, os_scr, *, nc, ns):
    i = pl.program_id(1)
    QB = Q_BLOCK
    lane = lax.broadcasted_iota(jnp.int32, (QB, LANE), 1)
    lo64 = lane < DH
    zero_b = jnp.zeros((QB, LANE), BF)
    groups = range(G_A)

    def split_heads(ref, g):
        ev, od = [], []
        for c in range(2):
            xc = ref[:, (2 * g + c) * LANE:(2 * g + c + 1) * LANE]
            ev.append(jnp.where(lo64, xc, zero_b))
            od.append(jnp.where(lo64, zero_b, xc))
        return jnp.concatenate(ev, axis=0), jnp.concatenate(od, axis=0)

    tq_l = i * QB + lax.broadcasted_iota(jnp.int32, (nc, QB), 1)
    r_c = lax.broadcasted_iota(jnp.int32, (nc, QB), 0)
    half = nc // 2
    blk_c = jnp.where(r_c < half, 2 * r_c, 2 * (r_c - half) + 1)
    cmask = (blk_c * L_CMP + (L_CMP - 1)) <= tq_l
    cmask_f = cmask.astype(F32)
    tq_s = i * QB + lax.broadcasted_iota(jnp.int32, (ns, QB), 1)
    blk_s = lax.broadcasted_iota(jnp.int32, (ns, QB), 0)
    valid = (blk_s * L_SEL) <= tq_s
    forced = (blk_s == 0) | (blk_s == tq_s // L_SEL)
    n_top = min(N_SEL, ns)

    def select(imp):
        score = jnp.where(valid, imp + jnp.where(forced, FORCE_BONUS, 0.0), NEG)
        rank = jnp.zeros((ns, QB), F32)
        for b2 in range(ns):
            row = score[b2:b2 + 1, :]
            rank = rank + jnp.where(blk_s > b2, jnp.where(row >= score, 1.0, 0.0), jnp.where(row > score, 1.0, 0.0))
        return jnp.where((rank < n_top) & (score > 0.5 * NEG), 1.0, 0.0)

    def score_stage(chains):
        scored = []
        for qs, kmat, _, bias in chains:
            s = _nt(qs, kmat)
            s = (s.reshape(2, QB, s.shape[1]) + bias[None]).reshape(s.shape)
            scored.append((s, jnp.max(s, axis=1, keepdims=True)))
        return scored

    def value_stage(scored, chains):
        outs = []
        for (s, m), (_, _, vmat, _) in zip(scored, chains):
            p = jnp.exp2(s - m)
            outs.append(_nn(p.astype(BF), vmat) / jnp.sum(p, axis=1, keepdims=True))
        return outs

    def masked_attend(chains):
        return value_stage(score_stage(chains), chains)

    qr = [split_heads(qr_ref, g) for g in groups]
    n_tiles = (i * QB + QB + _KT - 1) // _KT

    cmp_scores = [[_nt(kc_ref[g], qs) for qs in split_heads(q_ref, g)] for g in groups]

    WK = WINDOW + QB
    start = pl.multiple_of(jnp.maximum(i * QB - WINDOW, 0), QB)
    diff = (i * QB + lax.broadcasted_iota(jnp.int32, (QB, WK), 0)) - (start + lax.broadcasted_iota(jnp.int32, (QB, WK), 1))
    wb = jnp.where((diff >= 0) & (diff < WINDOW), 0.0, NEG)
    win_chains = []
    for g in groups:
        wkv = kvb_ref[1, g, 0, pl.ds(start, WK), :]
        wvk = kvb_ref[1, g, 1, pl.ds(start, WK), :]
        win_chains += [(qr[g][0], wkv, wvk, wb), (qr[g][1], wvk, wkv, wb)]
    win_scored = score_stage(win_chains)

    imps, o_cmp = [], []
    for g in groups:
        vc = vc_ref[g]
        imp = jnp.zeros((ns, QB), F32)
        oc = {}
        for stack in range(2):
            st = cmp_scores[g][stack]
            for c in range(2):
                s = jnp.where(cmask, st[:, c * QB:(c + 1) * QB], NEG)
                p = jnp.exp2(s - jnp.max(s, axis=0, keepdims=True)) * cmask_f
                p = p / jnp.maximum(jnp.sum(p, axis=0, keepdims=True), 1e-30)
                imp = imp + p[0:half] + p[half:nc]
                oc[(c, stack)] = _tn(p.astype(BF), vc)
        o_cmp.append(oc)
        imps.append(imp)

    ow = value_stage(win_scored, win_chains)
    sel_b = [select(imp).astype(BF) for imp in imps]

    gates = []
    for g in groups:
        ga = ga_ref[:, g * LANE:(g + 1) * LANE]
        hi = ga.astype(BF)
        lo = (ga - hi.astype(F32)).astype(BF)
        gates.append(_nn(jnp.concatenate([hi, lo], axis=1), gx_ref[...]))

    def gate_of(g, br, c):
        j = br * 2 + c
        return gates[g][:, j * LANE:(j + 1) * LANE]

    part = [[gate_of(g, 0, c) * jnp.where(lo64, o_cmp[g][(c, 0)], o_cmp[g][(c, 1)])
             + gate_of(g, 2, c) * jnp.where(lo64, ow[2 * g][c * QB:(c + 1) * QB], ow[2 * g + 1][c * QB:(c + 1) * QB])
             for c in range(2)] for g in groups]


    def slc_variant(nk):
        tq_r = i * QB + lax.broadcasted_iota(jnp.int32, (QB, _KT), 0)
        causal = (nk - _KT + lax.broadcasted_iota(jnp.int32, (QB, _KT), 1)) <= tq_r
        chains = []
        for g in groups:
            bias = _tn(sel_b[g], e_ref[:, 0:nk]) - MASK_BIG
            tail = jnp.where(causal, bias[:, nk - _KT:], -MASK_BIG)
            bias = tail if nk == _KT else jnp.concatenate([bias[:, :nk - _KT], tail], axis=1)
            kv = kvb_ref[0, g, 0, 0:nk, :]
            vk = kvb_ref[0, g, 1, 0:nk, :]
            chains.append((qr[g][0], kv, vk, bias))
            chains.append((qr[g][1], vk, kv, bias))
        for c, o in enumerate(masked_attend(chains)):
            os_scr[c // 2, c % 2] = o

    for k in range(1, kvb_ref.shape[3] // _KT + 1):
        pl.when(n_tiles == k)(functools.partial(slc_variant, k * _KT))

    for g in groups:
        os_e = os_scr[g, 0]
        os_o = os_scr[g, 1]
        for c in range(2):
            rows = slice(c * QB, (c + 1) * QB)
            acc = part[g][c] + gate_of(g, 1, c) * jnp.where(lo64, os_e[rows], os_o[rows])
            o_ref[:, (2 * g + c) * LANE:(2 * g + c + 1) * LANE] = acc.astype(BF)


def _gate_expand():
    gx = np.zeros((LANE, 3 * 2 * LANE), np.float32)
    for br in range(3):
        for c in range(2):
            j = br * 2 + c
            gx[br * HPG + 2 * c, j * LANE:j * LANE + DH] = 1.0
            gx[br * HPG + 2 * c + 1, j * LANE + DH:(j + 1) * LANE] = 1.0
    return jnp.asarray(np.concatenate([gx, gx], axis=0), BF)


def _sel_expand(ns, nkeys):
    e = (np.arange(nkeys)[None, :] // L_SEL) == np.arange(ns)[:, None]
    return jnp.asarray(e * MASK_BIG, BF)


def _nsa_prompt(q, qr, kvb, kc2, vc2, ga, B, T):
    nqb = T // Q_BLOCK
    nc, ns = T // L_CMP, T // L_SEL
    assert T >= WINDOW + Q_BLOCK and T % _KT == 0
    qspec = pl.BlockSpec((Q_BLOCK, 512), lambda b, i: (b * nqb + i, 0))
    cspec = pl.BlockSpec((None, G_A, nc, LANE), lambda b, i: (b, 0, 0, 0))
    return pl.pallas_call(
        functools.partial(_nsa_prompt_body, nc=nc, ns=ns),
        grid=(B, nqb),
        in_specs=[qspec, qspec,
                  pl.BlockSpec((2, G_A, 2, T, LANE), lambda b, i: (0, 0, 0, b, 0)),
                  cspec, cspec,
                  pl.BlockSpec((Q_BLOCK, G_A * LANE), lambda b, i: (b * nqb + i, 0)),
                  pl.BlockSpec((ns, T), lambda b, i: (0, 0)),
                  pl.BlockSpec((2 * LANE, 6 * LANE), lambda b, i: (0, 0))],
        out_specs=qspec,
        out_shape=jax.ShapeDtypeStruct((B * T, 512), BF),
        scratch_shapes=[pltpu.VMEM((G_A, 2, 2 * Q_BLOCK, LANE), F32)],
        compiler_params=_cparams(("arbitrary", "arbitrary")),
        name="nsa_prompt",
    )(q, qr, kvb, kc2, vc2, ga, _sel_expand(ns, T), _gate_expand())


def _ret_body(qkv_ref, gr_ref, s0_ref, dm_ref, dq_ref, wk_ref, gc_ref, gnw_ref, z_ref, sout_ref, s_scr, *, C, sb):
    c = pl.program_id(1)

    @pl.when(c == 0)
    def _():
        s_scr[...] = s0_ref[...]

    units = [(j, h, slice(j * C, (j + 1) * C), slice(h * LANE, (h + 1) * LANE))
             for j in range(sb) for h in range(H_R)]
    qkv = [tuple(qkv_ref[t, rows, sl] for t in range(3)) for _, _, rows, sl in units]
    inner = [_nt(q, k) * dm_ref[h] for (q, k, _), (_, h, _, _) in zip(qkv, units)]
    cross = [_nn(q, s_scr[j, h].astype(BF)) * dq_ref[h] for (q, _, _), (j, h, _, _) in zip(qkv, units)]
    outs = [_nn(a.astype(BF), v) + c for a, c, (_, _, v) in zip(inner, cross, qkv)]
    for (_, k, v), (j, h, _, _) in zip(qkv, units):
        kw = (k.astype(F32) * wk_ref[h]).astype(BF)
        s_new = gc_ref[h, 0:1, :] * s_scr[j, h] + _tn(kw, v)
        s_scr[j, h] = s_new
        sout_ref[j, h] = s_new
    for o, (_, _, rows, sl) in zip(outs, units):
        mu = jnp.mean(o, axis=-1, keepdims=True)
        d = o - mu
        var = jnp.mean(d * d, axis=-1, keepdims=True)
        on = d * lax.rsqrt(var + EPS) * gnw_ref[:, sl]
        z_ref[rows, sl] = (gr_ref[rows, sl].astype(F32) * on).astype(BF)


def _retention(ret3, gr, s0, gnw, nseq, rows_per_seq, C, c_true, sb):
    nC = rows_per_seq // C
    assert sb == 1 or nC == 1
    rows = nseq * rows_per_seq
    dm, dq, wk, gc = _ret_decay(C, c_true)
    full = lambda a: pl.BlockSpec(a.shape, lambda b, c: (0,) * a.ndim)
    return pl.pallas_call(
        functools.partial(_ret_body, C=C, sb=sb),
        grid=(nseq // sb, nC),
        in_specs=[pl.BlockSpec((3, sb * C, H_R * LANE), lambda b, c: (0, b * nC + c, 0)),
                  pl.BlockSpec((sb * C, H_R * LANE), lambda b, c: (b * nC + c, 0)),
                  pl.BlockSpec((sb, H_R, DK_R, DV_R), lambda b, c: (b, 0, 0, 0)),
                  full(dm), full(dq), full(wk), full(gc), full(gnw)],
        out_specs=[pl.BlockSpec((sb * C, H_R * LANE), lambda b, c: (b * nC + c, 0)),
                   pl.BlockSpec((sb, H_R, DK_R, DV_R), lambda b, c: (b, 0, 0, 0))],
        out_shape=[jax.ShapeDtypeStruct((rows, 512), BF),
                   jax.ShapeDtypeStruct((nseq, H_R, DK_R, DV_R), F32)],
        scratch_shapes=[pltpu.VMEM((sb, H_R, DK_R, DV_R), F32)],
        compiler_params=_cparams(("arbitrary", "arbitrary")),
        name="retention",
    )(ret3, gr, s0, dm, dq, wk, gc, gnw)


def _mix_ffn_body(x_ref, oa_ref, zr_ref, gm_ref, gt1_ref, wa_ref, wr_ref, wo_ref,
                  nw_ref, sh_ref, sc_ref, gt_ref, wu_ref, cw_ref, cb_ref, wd_ref, nf_ref, p1_ref, p2_ref,
                  y_ref, a_ref, carry, *, blocks_per_seq, seq_rows):
    i = pl.program_id(0)
    ya = _nn(oa_ref[...], wa_ref[...])
    yr = _nn(zr_ref[...], wr_ref[...])
    gm = gm_ref[...].astype(F32)
    merged = gm[:, 0:D_MODEL] * ya + gm[:, D_MODEL:2 * D_MODEL] * yr
    x = x_ref[...] + gt1_ref[...] * _nn(merged.astype(BF), wo_ref[...])
    tm = x.shape[0]
    ms = jnp.mean(x * x, axis=-1, keepdims=True)
    h = (x * lax.rsqrt(ms + EPS)) * nw_ref[...]
    h = (h * (1.0 + sc_ref[...]) + sh_ref[...]).astype(BF)
    a = _nn(h, wu_ref[:, 0:D_FF])
    b = _nn(h, wu_ref[:, D_FF:2 * D_FF])
    a_ref[...] = a[tm - a_ref.shape[0]:tm, :]
    rid = lax.broadcasted_iota(jnp.int32, (tm, D_FF), 0)
    if seq_rows is None:
        first = (i % blocks_per_seq) == 0
        prev = jnp.where(first, p1_ref[...], carry[...])
        carry[...] = a[tm - 8:tm, :]
        am1 = jnp.where(rid == 0, prev[7:8, :], pltpu.roll(a, 1, 0))
        am2 = jnp.where(rid == 0, prev[6:7, :], jnp.where(rid == 1, prev[7:8, :], pltpu.roll(a, 2, 0)))
    else:
        s = rid % seq_rows
        am1 = jnp.where(s == 0, p1_ref[...], pltpu.roll(a, 1, 0))
        am2 = jnp.where(s <= 1, p2_ref[...], pltpu.roll(a, 2, 0))
    u = cb_ref[...] + am2 * cw_ref[0:1, :] + am1 * cw_ref[1:2, :] + a * cw_ref[2:3, :]
    ff = _nn((_gelu(u) * b).astype(BF), wd_ref[...])
    x2 = x + gt_ref[...] * ff
    ms2 = jnp.mean(x2 * x2, axis=-1, keepdims=True)
    y_ref[...] = (x2 * lax.rsqrt(ms2 + EPS)) * nf_ref[...]


def _mix_ffn(x2d, oa, zr, gm, gt1, wa, wr, wo, nw, sh, sc, gt, wu, cw, cb, wd, nf, p1, p2,
             tm, blocks_per_seq, per_row_mods, seq_rows):
    rows = x2d.shape[0]
    a_rows = tm if seq_rows is not None else 8
    if per_row_mods:
        mod_spec = pl.BlockSpec((tm, D_MODEL), lambda i: (i, 0))
    else:
        mod_spec = pl.BlockSpec((None, 1, D_MODEL), lambda i: (i // blocks_per_seq, 0, 0))
    row = lambda w: pl.BlockSpec((tm, w), lambda i: (i, 0))
    full = lambda a, b: pl.BlockSpec((a, b), lambda i: (0, 0), pipeline_mode=pl.Buffered(1))
    vec = lambda w: pl.BlockSpec((1, w), lambda i: (0, 0))
    pspec = pl.BlockSpec(p1.shape, lambda i: (0, 0))
    return pl.pallas_call(
        functools.partial(_mix_ffn_body, blocks_per_seq=blocks_per_seq, seq_rows=seq_rows),
        grid=(rows // tm,),
        in_specs=[row(D_MODEL), row(512), row(512), row(2048), mod_spec,
                  full(512, D_MODEL), full(512, D_MODEL), full(D_MODEL, D_MODEL),
                  vec(D_MODEL), mod_spec, mod_spec, mod_spec,
                  full(D_MODEL, 2 * D_FF), pl.BlockSpec((CONV_W, D_FF), lambda i: (0, 0)), vec(D_FF),
                  full(D_FF, D_MODEL), vec(D_MODEL), pspec, pspec],
        out_specs=[row(D_MODEL), pl.BlockSpec((a_rows, D_FF), lambda i: (i, 0))],
        out_shape=[jax.ShapeDtypeStruct((rows, D_MODEL), F32),
                   jax.ShapeDtypeStruct((rows // tm * a_rows, D_FF), F32)],
        scratch_shapes=[pltpu.VMEM((8, D_FF), F32)],
        compiler_params=_cparams(("arbitrary",)),
        name="mix_ffn",
    )(x2d, oa, zr, gm, gt1, wa, wr, wo, nw, sh, sc, gt, wu, cw, cb, wd, nf, p1, p2)


def _prompt_path(x_prompt, mods_p, W):
    B, T, _ = x_prompt.shape
    rows = B * T
    x2d = x_prompt.reshape(rows, D_MODEL)
    pos = np.arange(T)
    tm = 256
    sh1, sc1, gt1, sh2, sc2, gt2 = [mods_p[:, None, j * D_MODEL:(j + 1) * D_MODEL] for j in range(6)]
    (q, qr, kvc, kvs, kvw, kvb, ret3, gr, gm, ga) = _inproj(
        x2d, W["norm1"], sh1, sc1, W["w_in"], _rope_tab(pos, Q_SCALE), _rope_tab(pos, 1.0), _ret_tab(pos),
        tm, T // tm, False, True)
    nc, n_pages = T // L_CMP, T // _PAGE
    comp = _compress_t(kvc, W["phi_posT"], W["phi_w1p"], W["phi_w2p"])
    comp = comp.reshape(2, G_A, B, n_pages, 2, 2, DH).transpose(0, 2, 1, 5, 3, 4, 6).reshape(2, B, G_A, nc, DH)
    comp2 = jnp.concatenate([comp, comp], axis=-1).astype(BF)
    oa = _nsa_prompt(q, qr, kvb, comp2[0], comp2[1], ga, B, T)
    C = 256 if T % 256 == 0 else T
    zr, ret_new = _retention(ret3, gr, jnp.zeros((B, H_R, DK_R, DV_R), F32), W["gnw"], B, T, C, C, 1)
    zeros8 = jnp.zeros((8, D_FF), F32)
    y, a_up = _mix_ffn(x2d, oa, zr, gm, gt1, W["w_up_a"], W["w_up_r"], W["w_out"],
                       W["norm2"], sh2, sc2, gt2, W["w_ffn_up"], W["conv_w"], W["conv_b"], W["w_ffn_down"],
                       W["normf"], zeros8, zeros8, tm, T // tm, False, None)
    wsz = min(WINDOW, T)
    rows_major = lambda t: t.reshape(B, 2, G_A, DH, t.shape[-1]).transpose(0, 4, 1, 2, 3)[None]
    outs = dict(
        y=y.reshape(B, T, D_MODEL),
        cmp=rows_major(kvc), slc=rows_major(kvs), win=rows_major(kvw[:, :, T - wsz:]),
        ret=ret_new[None],
        conv=a_up.reshape(B, T // tm, 8, D_FF)[None, :, T // tm - 1, 8 - (CONV_W - 1):],
    )
    return outs


_QC = 32


def _cmp_paged_body(pt_ref, cache_ref, q_ref, pos_ref, w1_ref, w2_ref, ocmp_ref, topi_ref, xbuf, sem,
                    *, n_pages, n_pick, n_q):
    b = pl.program_id(0)
    nb = pl.num_programs(0)
    slot = b % 2

    def copy(bb, sl, p):
        return pltpu.make_async_copy(cache_ref.at[pt_ref[bb, p]], xbuf.at[sl, :, p, :], sem.at[sl])

    def issue(bb, sl):
        for p in range(n_pages):
            copy(bb, sl, p).start()

    @pl.when(b == 0)
    def _():
        issue(b, slot)

    @pl.when(b + 1 < nb)
    def _():
        issue(b + 1, 1 - slot)

    for p in range(n_pages):
        copy(b, slot, p).wait()

    comp = _compress_slab_rows(lambda r: xbuf[slot, r], n_pages, pos_ref, w1_ref, w2_ref)

    def lane_groups(x, op):
        r = x
        for j in range(1, _BPP):
            r = op(r, pltpu.roll(x, j * _QC, 1))
        return r

    grp = range(G_A)
    sts = [_nn(comp[0][g * n_pages:(g + 1) * n_pages].astype(BF), q_ref[g]) for g in grp]
    ps = []
    for g in grp:
        m = lane_groups(jnp.max(sts[g], axis=0, keepdims=True), jnp.maximum)
        p = jnp.exp2(sts[g] - m)
        ps.append(p / lane_groups(jnp.sum(p, axis=0, keepdims=True), jnp.add))
    for g in grp:
        r_full = _tn(ps[g].astype(BF), comp[1][g * n_pages:(g + 1) * n_pages].astype(BF))
        o = r_full[0:_QC, 0:DH]
        for j in range(1, _BPP):
            o = o + r_full[j * _QC:(j + 1) * _QC, j * DH:(j + 1) * DH]
        ocmp_ref[g] = o
    imp = None
    for g in grp:
        pair = ps[g] + pltpu.roll(ps[g], LANE - _QC, 1)
        t = pair
        for r in range(1, HPG):
            t = t + pltpu.roll(pair, LANE - r * n_q, 1)
        if g:
            t = pltpu.roll(t, g * n_q, 1)
        lane_g = lax.broadcasted_iota(jnp.int32, t.shape, 1) % (2 * _QC)
        t = jnp.where((lane_g >= g * n_q) & (lane_g < (g + 1) * n_q), t, 0.0)
        imp = t if imp is None else imp + t
    sc = jnp.concatenate([imp, pltpu.roll(imp, 2 * _QC, 1)], axis=0)
    row = lax.broadcasted_iota(jnp.int32, (2 * n_pages, LANE), 0)
    blk = jnp.where(row < n_pages, 2 * row, 2 * (row - n_pages) + 1)
    score = sc + jnp.where(blk == 0, FORCE_BONUS, 0.0)
    topi_ref[...] = jnp.zeros((8, LANE), jnp.int32)
    for k in range(n_pick):
        mx = jnp.max(score, axis=0, keepdims=True)
        idx = jnp.min(jnp.where(score == mx, blk, 2 * n_pages), axis=0, keepdims=True)
        topi_ref[k:k + 1, :] = idx
        score = jnp.where(blk == idx, -jnp.inf, score)


def _cmp_paged(page_table, cache_t, q_bd, pos, w1, w2, n_q):
    DB, n_pages = page_table.shape
    SL = cache_t.shape[1]
    page_rows = cache_t.shape[2]
    kern = functools.partial(_cmp_paged_body, n_pages=n_pages, n_pick=N_SEL - 1, n_q=n_q)
    one = dict(pipeline_mode=pl.Buffered(1))
    return pl.pallas_call(
        kern,
        grid_spec=pltpu.PrefetchScalarGridSpec(
            num_scalar_prefetch=1, grid=(DB,),
            in_specs=[pl.BlockSpec(memory_space=pl.ANY),
                      pl.BlockSpec((None, G_A, _BPP * DH, LANE), lambda b, pt: (b, 0, 0, 0)),
                      pl.BlockSpec(pos.shape, lambda b, pt: (0, 0, 0), **one),
                      pl.BlockSpec(w1.shape, lambda b, pt: (0, 0, 0, 0), **one),
                      pl.BlockSpec((2, _BPP * DH, _BPP * DH), lambda b, pt: (0, 0, 0), **one)],
            out_specs=[pl.BlockSpec((None, G_A, _QC, DH), lambda b, pt: (b, 0, 0, 0)),
                       pl.BlockSpec((None, 8, LANE), lambda b, pt: (b, 0, 0))],
            scratch_shapes=[pltpu.VMEM((2, SL, n_pages, page_rows), F32),
                            pltpu.SemaphoreType.DMA((2,))]),
        out_shape=[jax.ShapeDtypeStruct((DB, G_A, _QC, DH), F32),
                   jax.ShapeDtypeStruct((DB, 8, LANE), jnp.int32)],
        compiler_params=_cparams(("arbitrary",)),
        name="cmp_paged",
    )(page_table, cache_t, q_bd, pos, w1, w2)


def _slc_win_paged_body(pt_ref, ti_ref, cslc_ref, win_ref, q_ref, tiv_ref, ns_ref, nw_ref, ex_ref,
                        oslc_ref, owin_ref, kbuf, sem, *, n_q, n_pick, page_rows):
    b = pl.program_id(0)
    nb = pl.num_programs(0)
    slot = b % 2
    n_slab = n_q * n_pick
    bpp = page_rows // L_SEL
    wb = win_ref.shape[1]

    def copies(bb, sl):
        cps = []
        for g in range(G_A):
            for j in range(n_slab):
                page = pt_ref[bb, ti_ref[bb, g * n_slab + j] // bpp]
                for kv in range(2):
                    cps.append(pltpu.make_async_copy(
                        cslc_ref.at[page, pl.ds((kv * G_A + g) * DH, DH), :],
                        kbuf.at[sl, g, kv, :, pl.ds(j * page_rows, page_rows)], sem.at[sl]))
        return cps

    @pl.when(b == 0)
    def _():
        for cp in copies(b, slot):
            cp.start()

    @pl.when(b + 1 < nb)
    def _():
        for cp in copies(b + 1, 1 - slot):
            cp.start()

    for cp in copies(b, slot):
        cp.wait()

    nq_rows = q_ref.shape[1]
    nk = n_slab * page_rows
    row_q = lax.broadcasted_iota(jnp.int32, (nq_rows, nk), 0) % n_q
    col = lax.broadcasted_iota(jnp.int32, (nq_rows, nk), 1)
    own = row_q == col // (n_pick * page_rows)
    half = ((col % page_rows) // L_SEL).astype(F32)
    nnew = ns_ref.shape[2]
    new_ok = (lax.broadcasted_iota(jnp.int32, (nq_rows, nnew), 1)
              <= lax.broadcasted_iota(jnp.int32, (nq_rows, nnew), 0) % n_q)
    dwin = (wb + lax.broadcasted_iota(jnp.int32, (nq_rows, wb), 0) % n_q
            - lax.broadcasted_iota(jnp.int32, (nq_rows, wb), 1))
    win_ok = (dwin >= 0) & (dwin < WINDOW)

    chains = []
    for g in range(G_A):
        q = q_ref[g]
        par = (tiv_ref[g] % bpp).astype(F32).astype(BF)
        want = _nn(par, ex_ref[...])[0:1, :]
        chains.append((q, kbuf[slot, g, 0].astype(BF), kbuf[slot, g, 1].astype(BF), own & (half == want),
                       ns_ref[g, 0].astype(BF), ns_ref[g, 1].astype(BF), oslc_ref, g))
        kw = win_ref[pl.ds((0 * G_A + g) * DH, DH), :].astype(BF)
        vw = win_ref[pl.ds((1 * G_A + g) * DH, DH), :].astype(BF)
        chains.append((q, kw, vw, win_ok, nw_ref[g, 0].astype(BF), nw_ref[g, 1].astype(BF), owin_ref, g))
    scores = [(jnp.where(ok_old, _nn(q, kt_old), NEG), jnp.where(new_ok, _nt(q, k_new), NEG))
              for q, kt_old, _, ok_old, k_new, _, _, _ in chains]
    probs = []
    for s_o, s_n in scores:
        m = jnp.maximum(jnp.max(s_o, axis=1, keepdims=True), jnp.max(s_n, axis=1, keepdims=True))
        p_o = jnp.exp2(s_o - m)
        p_n = jnp.exp2(s_n - m)
        probs.append((p_o, p_n, jnp.sum(p_o, axis=1, keepdims=True) + jnp.sum(p_n, axis=1, keepdims=True)))
    for (p_o, p_n, den), (_, _, vt_old, _, _, v_new, out_ref, g) in zip(probs, chains):
        out_ref[g] = (_nt(p_o.astype(BF), vt_old) + _nn(p_n.astype(BF), v_new)) / den


def _slc_win_paged(page_table, topi_flat, topi_vec, cache_t, win_t, q_rot, new_slc, new_win, n_q):
    DB = page_table.shape[0]
    n_pick = N_SEL - 1
    page_rows = cache_t.shape[2]
    wb = win_t.shape[2]
    nq_rows = q_rot.shape[2]
    n_slab = n_q * n_pick
    ex = (np.arange(n_slab * page_rows)[None, :] // page_rows) == np.arange(LANE)[:, None]
    kern = functools.partial(_slc_win_paged_body, n_q=n_q, n_pick=n_pick, page_rows=page_rows)
    bspec = lambda shp: pl.BlockSpec((None,) + shp, lambda b, pt, ti: (b,) + (0,) * len(shp))
    return pl.pallas_call(
        kern,
        grid_spec=pltpu.PrefetchScalarGridSpec(
            num_scalar_prefetch=2, grid=(DB,),
            in_specs=[pl.BlockSpec(memory_space=pl.ANY), bspec((2 * G_A * DH, wb)),
                      bspec((G_A, nq_rows, DH)), bspec((G_A, 16, LANE)),
                      bspec((G_A, 2, 16, DH)), bspec((G_A, 2, 16, DH)),
                      pl.BlockSpec((LANE, n_slab * page_rows), lambda b, pt, ti: (0, 0))],
            out_specs=[bspec((G_A, nq_rows, DH)), bspec((G_A, nq_rows, DH))],
            scratch_shapes=[pltpu.VMEM((2, G_A, 2, DH, n_slab * page_rows), F32),
                            pltpu.SemaphoreType.DMA((2,))]),
        out_shape=[jax.ShapeDtypeStruct((DB, G_A, nq_rows, DH), F32)] * 2,
        compiler_params=_cparams(("arbitrary",)),
        name="slc_win_paged",
    )(page_table, topi_flat, cache_t, win_t, q_rot, topi_vec, new_slc, new_win, jnp.asarray(ex, BF))


def _gate_sample_body(oc_ref, os_ref, ow_ref, ga_ref, o_ref):
    rows = o_ref.shape[0]
    lo64 = lax.broadcasted_iota(jnp.int32, (rows, LANE), 1) < DH
    for c4 in range(H_A // 2):
        g, c = c4 // 2, c4 % 2
        sl = slice(c4 * LANE, (c4 + 1) * LANE)
        acc = jnp.zeros((rows, LANE), F32)
        for br, ref in enumerate((oc_ref, os_ref, ow_ref)):
            col = g * LANE + br * HPG + 2 * c
            gate = jnp.where(lo64, ga_ref[:, col:col + 1], ga_ref[:, col + 1:col + 2])
            acc = acc + gate * ref[:, sl]
        o_ref[:, sl] = acc.astype(BF)


def _gate_sample(oc, osl, ow, ga):
    rows = oc.shape[0]
    full = lambda w: pl.BlockSpec((rows, w), lambda i: (0, 0))
    return pl.pallas_call(
        _gate_sample_body, grid=(1,),
        in_specs=[full(512), full(512), full(512), full(256)],
        out_specs=full(512),
        out_shape=jax.ShapeDtypeStruct((rows, 512), BF),
        compiler_params=_cparams(("arbitrary",)),
        name="gate_sample",
    )(oc, osl, ow, ga)


def _sample_path(x_sample, mods_s, W, cache_cmp, cache_slc, state_win, state_ret, state_conv, page_table):
    DB, S, _ = x_sample.shape
    rows = DB * S
    page_rows = cache_cmp.shape[1]
    P = page_table.shape[1] * page_rows
    wb = state_win.shape[1]
    assert P % L_SEL == 0 and S < L_CMP and S <= 8 and wb == WINDOW and page_rows % L_SEL == 0
    assert P // L_SEL >= N_SEL and CONV_W == 3 and S >= CONV_W - 1
    pos = P + np.arange(S)
    pos_rows = np.tile(pos, DB)
    x2d = x_sample.reshape(rows, D_MODEL)
    modr = jnp.repeat(mods_s, S, axis=0)
    sh1, sc1, gt1, sh2, sc2, gt2 = [modr[:, j * D_MODEL:(j + 1) * D_MODEL] for j in range(6)]
    (q, qr, kvc, kvs, kvw, _, ret3, gr, gm, ga) = _inproj(
        x2d, W["norm1"], sh1, sc1, W["w_in"], _rope_tab(pos_rows, Q_SCALE), _rope_tab(pos_rows, 1.0),
        _ret_tab(pos_rows), rows, 1, True, False)

    def to_heads(t):
        return t.reshape(DB, S, G_A, HPG, DH).transpose(0, 2, 3, 1, 4).reshape(DB, G_A, HPG * S, DH)

    def from_heads(t):
        return t.reshape(DB, G_A, HPG, S, DH).transpose(0, 3, 1, 2, 4).reshape(rows, H_A * DH)

    assert page_rows == _BPP * L_CMP and HPG * S <= _QC and S * (N_SEL - 1) <= LANE
    slab = lambda t: t.transpose(0, 2, 3, 4, 1).reshape(t.shape[0], 2 * G_A * DH, t.shape[1])
    qt = jnp.pad(to_heads(q).transpose(0, 1, 3, 2), ((0, 0), (0, 0), (0, 0), (0, _QC - HPG * S)))
    q_bd = _blockdiag(qt)
    o_cmp, topi = _cmp_paged(page_table, slab(cache_cmp), q_bd, W["phi_posT"], W["phi_w1p"], W["phi_w2p"], S)
    n_pick = N_SEL - 1
    topi = topi[:, :n_pick, :G_A * S].transpose(0, 2, 1).reshape(DB, G_A, S * n_pick)
    topi_vec = jnp.broadcast_to(jnp.pad(topi, ((0, 0), (0, 0), (0, LANE - S * n_pick)))[:, :, None, :],
                                (DB, G_A, 16, LANE))

    def new_rows(t):
        t = t.reshape(DB, S, 2, G_A, DH).transpose(0, 3, 2, 1, 4)
        return jnp.pad(t, ((0, 0), (0, 0), (0, 0), (0, 16 - S), (0, 0)))

    o_slc, o_win = _slc_win_paged(page_table, topi.reshape(DB, G_A * S * n_pick), topi_vec, slab(cache_slc),
                                  slab(state_win), to_heads(qr), new_rows(kvs), new_rows(kvw), S)
    oa = _gate_sample(from_heads(o_cmp[:, :, :HPG * S]), from_heads(o_slc), from_heads(o_win), ga)

    RP = 16
    padr = lambda t: jnp.pad(t.reshape(t.shape[:-2] + (DB, S, 512)),
                             ((0, 0),) * (t.ndim - 1) + ((0, RP - S), (0, 0))).reshape(t.shape[:-2] + (DB * RP, 512))
    zr, ret_new = _retention(padr(ret3), padr(gr), state_ret, W["gnw"], DB, RP, RP, S, 8 if DB % 8 == 0 else 1)
    zr = zr.reshape(DB, RP, 512)[:, :S].reshape(rows, 512)
    zs = lambda n: jnp.zeros((DB, n, D_FF), F32)
    p1 = jnp.concatenate([state_conv[:, 1:2], zs(S - 1)], axis=1).reshape(rows, D_FF)
    p2 = jnp.concatenate([state_conv[:, 0:2], zs(S - 2)], axis=1).reshape(rows, D_FF)
    y, a_up = _mix_ffn(x2d, oa, zr, gm, gt1, W["w_up_a"], W["w_up_r"], W["w_out"],
                       W["norm2"], sh2, sc2, gt2, W["w_ffn_up"], W["conv_w"], W["conv_b"], W["w_ffn_down"],
                       W["normf"], p1, p2, rows, 1, True, S)
    shp = (1, DB, S, 2, G_A, DH)
    return dict(
        y=y.reshape(DB, S, D_MODEL),
        cmp=kvc.reshape(shp), slc=kvs.reshape(shp),
        win=jnp.concatenate([state_win[:, S:], kvw.reshape(DB, S, 2, G_A, DH)], axis=1)[None],
        ret=ret_new[None],
        conv=a_up.reshape(DB, S, D_FF)[None, :, S - (CONV_W - 1):],
    )


def kernel(x_prompt, x_sample, cache_cmp_kv, cache_slc_kv, state_win_kv, state_ret, state_conv, page_table,
           c_prompt, c_sample, norm1_w, norm2_w, w_ada, b_ada, w_in, phi_pos_k, phi_k1, phi_k2, phi_pos_v,
           phi_v1, phi_v2, w_up_a, ret_gn_w, w_up_r, w_out, w_ffn_up, ffn_conv_w, ffn_conv_b, w_ffn_down, normf_w):
    B = x_prompt.shape[0]
    l = 0
    W = dict(
        norm1=norm1_w[l].reshape(1, D_MODEL), norm2=norm2_w[l].reshape(1, D_MODEL), normf=normf_w.reshape(1, D_MODEL),
        w_in=_pack_w_in(w_in[l]),
        w_up_a=w_up_a[l].astype(BF), w_up_r=w_up_r[l].astype(BF), w_out=w_out[l].astype(BF),
        gnw=ret_gn_w[l].reshape(1, H_R * DV_R),
        w_ffn_up=w_ffn_up[l].astype(BF), conv_w=ffn_conv_w[l], conv_b=ffn_conv_b[l].reshape(1, D_FF),
        w_ffn_down=w_ffn_down[l].astype(BF),
    )
    W["phi_posT"], W["phi_w1p"], W["phi_w2p"] = _pack_phi_paged(
        phi_pos_k[l], phi_k1[l], phi_k2[l], phi_pos_v[l], phi_v1[l], phi_v2[l])
    mods = _mods(jnp.concatenate([c_prompt, c_sample], axis=0), w_ada[l], b_ada[l])
    p = _prompt_path(x_prompt, mods[:B], W)
    s = _sample_path(x_sample, mods[B:], W, cache_cmp_kv[l], cache_slc_kv[l], state_win_kv[l], state_ret[l],
                     state_conv[l], page_table)
    return (p["y"], s["y"], p["cmp"], s["cmp"], p["slc"], s["slc"], p["win"], s["win"],
            p["ret"], s["ret"], p["conv"], s["conv"])
```

```python
import functools

import numpy as np
import jax
import jax.numpy as jnp
from jax import lax
from jax.experimental import pallas as pl
from jax.experimental.pallas import tpu as pltpu

BF = jnp.bfloat16
F32 = jnp.float32

D_MODEL = 1024
H_A, G_A, DH = 8, 2, 64
HPG = H_A // G_A
ROPE_DIM = DH // 4
ROPE_THETA = 500000.0
L_CMP, L_SEL, N_SEL = 32, 64, 8
WINDOW = 512
Q_BLOCK = 128
FORCE_BONUS = 1e4
H_R, DK_R, DV_R = 4, 128, 128
RET_THETA = 10000.0
D_FF = 2816
CONV_W = 3
EPS = 1e-6
NEG = -1e30
MASK_BIG = 2.0 ** 100
Q_SCALE = DH ** -0.5 * 1.4426950408889634
LANE = 128
VMEM_LIMIT = 56 * 1024 * 1024


def _cparams(sem):
    return pltpu.CompilerParams(dimension_semantics=sem, vmem_limit_bytes=VMEM_LIMIT)


def _sigmoid(x):
    return 1.0 / (1.0 + jnp.exp(-x))


def _gelu(x):
    return 0.5 * x * (1.0 + jnp.tanh(0.7978845608028654 * (x + 0.044715 * (x * x * x))))


def _nt(a, b):
    return lax.dot_general(a, b, (((1,), (1,)), ((), ())), preferred_element_type=F32)


def _tn(a, b):
    return lax.dot_general(a, b, (((0,), (0,)), ((), ())), preferred_element_type=F32)


def _nn(a, b):
    return jnp.dot(a, b, preferred_element_type=F32)


def _rope_tab(pos, scale):
    half = ROPE_DIM // 2
    inv = ROPE_THETA ** (-np.arange(half, dtype=np.float64) * (2.0 / ROPE_DIM))
    ang = pos.astype(np.float64)[:, None] * inv
    cos, sin = np.cos(ang), np.sin(ang)
    n = pos.shape[0]
    c = np.ones((n, DH)); s_lo = np.zeros((n, DH)); s_hi = np.zeros((n, DH))
    c[:, :half] = cos; c[:, half:ROPE_DIM] = cos
    s_lo[:, half:ROPE_DIM] = sin
    s_hi[:, :half] = -sin
    tab = np.concatenate([np.tile(t, (1, 2)) for t in (c, s_lo, s_hi)], axis=1) * scale
    return jnp.asarray(tab, F32)


def _ret_tab(pos):
    half = DK_R // 2
    inv = RET_THETA ** (-np.arange(half, dtype=np.float64) * (2.0 / DK_R))
    ang = pos.astype(np.float64)[:, None] * inv
    cos, sin = np.cos(ang), np.sin(ang)
    c = np.concatenate([cos, cos], axis=1)
    s = np.concatenate([-sin, sin], axis=1)
    ks = DK_R ** -0.5
    return jnp.asarray(np.concatenate([c, s, c * ks, s * ks], axis=1), F32)


def _ret_decay(C, c_true):
    h = np.arange(H_R, dtype=np.float64)
    log_g = np.log1p(-np.exp2(-5.0 - h))
    i = np.arange(C, dtype=np.float64)
    diff = i[:, None] - i[None, :]
    dm = np.where(diff >= 0, np.exp(log_g[:, None, None] * np.maximum(diff, 0.0)), 0.0)
    dq = np.exp(log_g[:, None] * (i + 1.0))[:, :, None] * np.ones((1, 1, LANE))
    wk = np.exp(log_g[:, None] * (c_true - 1.0 - i))[:, :, None] * np.ones((1, 1, LANE))
    wk = np.where(i[None, :, None] < c_true, wk, 0.0)
    gc = np.exp(log_g * c_true)[:, None, None] * np.ones((1, 8, LANE))
    return (jnp.asarray(dm, F32), jnp.asarray(dq, F32), jnp.asarray(wk, F32), jnp.asarray(gc, F32))


def _mods_body(c_ref, w_ref, b_ref, o_ref):
    c = c_ref[...]
    s = c * _sigmoid(c)
    o_ref[...] = _nn(s.astype(BF), w_ref[...].astype(BF)) + b_ref[...]


def _mods(c_all, w_ada, b_ada):
    n = c_all.shape[0]
    nout = w_ada.shape[1]
    tn = 1024
    return pl.pallas_call(
        _mods_body,
        grid=(nout // tn,),
        in_specs=[pl.BlockSpec((n, D_MODEL), lambda j: (0, 0)),
                  pl.BlockSpec((D_MODEL, tn), lambda j: (0, j)),
                  pl.BlockSpec((1, tn), lambda j: (0, j))],
        out_specs=pl.BlockSpec((n, tn), lambda j: (0, j)),
        out_shape=jax.ShapeDtypeStruct((n, nout), F32),
        compiler_params=_cparams(("arbitrary",)),
        name="mods",
    )(c_all, w_ada, b_ada.reshape(1, nout))


_C_Q, _C_KV, _C_QR, _C_KR, _C_VR, _C_GR, _C_GM, _C_GA, _C_END = (
    0, 512, 1280, 1792, 2304, 2816, 3328, 5376, 5632)


def _pack_w_in(w_in):
    wt = w_in.T
    o = np.cumsum((0, 512, 768, 24, 512, 512, 512, 512, 2048))
    q, kv, ga, qr, kr, vr, gr, gm = [wt[o[i]:o[i + 1]] for i in range(8)]
    ga = ga.reshape(3, G_A, HPG, D_MODEL).transpose(1, 0, 2, 3).reshape(G_A, 3 * HPG, D_MODEL)
    ga = jnp.pad(ga, ((0, 0), (0, LANE - 3 * HPG), (0, 0))).reshape(G_A * LANE, D_MODEL)
    return jnp.concatenate([q, kv, qr, kr, vr, gr, gm, ga], axis=0).astype(BF)


def _inproj_body(x_ref, nw_ref, sh_ref, sc_ref, w_ref, rq_ref, rk_ref, rr_ref,
                 q_ref, qr_ref, kvc_ref, kvs_ref, kvw_ref, kvb_ref, ret_ref, gr_ref, gm_ref, ga_ref, *, kv_t):
    def put_kv(out_ref, k, v):
        if kv_t:
            out_ref[0:LANE, :] = k.T
            out_ref[LANE:2 * LANE, :] = v.T
        else:
            out_ref[:, 0:LANE] = k
            out_ref[:, LANE:2 * LANE] = v

    x = x_ref[...]
    tm = x.shape[0]
    ms = jnp.mean(x * x, axis=-1, keepdims=True)
    h = (x * lax.rsqrt(ms + EPS)) * nw_ref[...]
    h = h * (1.0 + sc_ref[...]) + sh_ref[...]
    hb = h.astype(BF)
    lo64 = lax.broadcasted_iota(jnp.int32, (tm, LANE), 1) < DH

    def mm(lo, hi):
        return _nt(hb, w_ref[lo:hi, :])

    def rope(xc, tab_ref):
        return (xc * tab_ref[:, 0:LANE] + pltpu.roll(xc, 8, 1) * tab_ref[:, LANE:2 * LANE]
                + pltpu.roll(xc, LANE - 8, 1) * tab_ref[:, 2 * LANE:3 * LANE])

    gm_half = (_C_GA - _C_GM) // 2
    gm_ref[:, 0:gm_half] = _sigmoid(mm(_C_GM, _C_GM + gm_half)).astype(BF)
    g = mm(_C_GR, _C_GM)
    gr_ref[...] = (g * _sigmoid(g)).astype(BF)
    gm_ref[:, gm_half:2 * gm_half] = _sigmoid(mm(_C_GM + gm_half, _C_GA)).astype(BF)

    qa = mm(_C_Q, _C_KV)
    q_ref[...] = (qa * Q_SCALE).astype(BF)
    for c in range(4):
        qr_ref[:, c * LANE:(c + 1) * LANE] = rope(qa[:, c * LANE:(c + 1) * LANE], rq_ref).astype(BF)

    kv = mm(_C_KV, _C_QR)
    put_kv(kvc_ref, kv[:, 0:LANE], kv[:, LANE:2 * LANE])
    for kind, out_ref in ((0, kvs_ref), (1, kvw_ref)):
        base = 256 + kind * 256
        k = rope(kv[:, base:base + LANE], rk_ref)
        v = kv[:, base + LANE:base + 2 * LANE]
        put_kv(out_ref, k, v)
        kr_ = pltpu.roll(k, DH, 1)
        vr_ = pltpu.roll(v, DH, 1)
        kvb_ref[kind, 0, 0] = jnp.where(lo64, k, vr_).astype(BF)
        kvb_ref[kind, 0, 1] = jnp.where(lo64, v, kr_).astype(BF)
        kvb_ref[kind, 1, 0] = jnp.where(lo64, kr_, v).astype(BF)
        kvb_ref[kind, 1, 1] = jnp.where(lo64, vr_, k).astype(BF)

    qr = mm(_C_QR, _C_KR)
    kr = mm(_C_KR, _C_VR)
    for hh in range(H_R):
        sl = slice(hh * LANE, (hh + 1) * LANE)
        xq = qr[:, sl]
        ret_ref[0, :, sl] = (xq * rr_ref[:, 0:LANE] + pltpu.roll(xq, DK_R // 2, 1) * rr_ref[:, LANE:2 * LANE]).astype(BF)
        xk = kr[:, sl]
        ret_ref[1, :, sl] = (xk * rr_ref[:, 2 * LANE:3 * LANE]
                             + pltpu.roll(xk, DK_R // 2, 1) * rr_ref[:, 3 * LANE:4 * LANE]).astype(BF)
    ga_ref[...] = _sigmoid(mm(_C_GA, _C_END))
    ret_ref[2] = mm(_C_VR, _C_GR).astype(BF)


def _inproj(x2d, nw, sh, sc, w_pack, rq, rk, rr, tm, tab_blocks, per_row_mods, kv_t):
    rows = x2d.shape[0]
    nb = rows // tm
    if kv_t:
        kv_shape = jax.ShapeDtypeStruct((nb // tab_blocks, 256, tab_blocks * tm), F32)
        kv_spec = pl.BlockSpec((None, 256, tm), lambda i: (i // tab_blocks, 0, i % tab_blocks))
    else:
        kv_shape = jax.ShapeDtypeStruct((rows, 256), F32)
        kv_spec = pl.BlockSpec((tm, 256), lambda i: (i, 0))
    if per_row_mods:
        mod_spec = pl.BlockSpec((tm, D_MODEL), lambda i: (i, 0))
    else:
        mod_spec = pl.BlockSpec((None, 1, D_MODEL), lambda i: (i // tab_blocks, 0, 0))
    tab = lambda w: pl.BlockSpec((tm, w), lambda i: (i % tab_blocks, 0))
    row = lambda w: pl.BlockSpec((tm, w), lambda i: (i, 0))
    out_shapes = [
        jax.ShapeDtypeStruct((rows, 512), BF),
        jax.ShapeDtypeStruct((rows, 512), BF),
        kv_shape,
        kv_shape,
        kv_shape,
        jax.ShapeDtypeStruct((2, G_A, 2, rows, LANE), BF),
        jax.ShapeDtypeStruct((3, rows, 512), BF),
        jax.ShapeDtypeStruct((rows, 512), BF),
        jax.ShapeDtypeStruct((rows, 2048), BF),
        jax.ShapeDtypeStruct((rows, 256), F32),
    ]
    out_specs = [row(512), row(512), kv_spec, kv_spec, kv_spec,
                 pl.BlockSpec((2, G_A, 2, tm, LANE), lambda i: (0, 0, 0, i, 0)),
                 pl.BlockSpec((3, tm, 512), lambda i: (0, i, 0)),
                 row(512), row(2048), row(256)]
    return pl.pallas_call(
        functools.partial(_inproj_body, kv_t=kv_t),
        grid=(nb,),
        in_specs=[row(D_MODEL),
                  pl.BlockSpec((1, D_MODEL), lambda i: (0, 0)),
                  mod_spec, mod_spec,
                  pl.BlockSpec((_C_END, D_MODEL), lambda i: (0, 0), pipeline_mode=pl.Buffered(1)),
                  tab(384), tab(384), tab(512)],
        out_specs=out_specs,
        out_shape=out_shapes,
        compiler_params=_cparams(("arbitrary",)),
        name="inproj",
    )(x2d, nw, sh, sc, w_pack, rq, rk, rr)


_BPP = 4
_PAGE = _BPP * L_CMP


def _blockdiag(w):
    z = jnp.zeros_like(w)
    return jnp.concatenate(
        [jnp.concatenate([w if j == i else z for j in range(_BPP)], axis=-1) for i in range(_BPP)], axis=-2)


def _pack_phi_paged(phi_pos_k, phi_k1, phi_k2, phi_pos_v, phi_v1, phi_v2):
    w1, w2, pos = [], [], []
    for p_, a, b_ in ((phi_pos_k, phi_k1, phi_k2), (phi_pos_v, phi_v1, phi_v2)):
        w1.append(_blockdiag(a.astype(BF).reshape(L_CMP, DH, DH).transpose(1, 0, 2)))
        w2.append(_blockdiag(b_.astype(BF)))
        pos.append(jnp.tile(p_.T, (1, _BPP)))
    pos = jnp.stack(pos).reshape(2, DH // 2, 2 * _PAGE)
    w1 = jnp.stack(w1).reshape(2, DH // 2, 2 * _PAGE, _BPP * DH)
    return pos, w1, jnp.stack(w2)


def _compress_slab_rows(load, n_rows, pos_ref, w1_ref, w2_ref):
    acc = [jnp.zeros((G_A * n_rows, _BPP * DH), F32) for _ in range(2)]
    for dp in range(DH // 2):
        for kv in range(2):
            x = jnp.concatenate(
                [jnp.concatenate([load((kv * G_A + g) * DH + 2 * dp + j) for j in range(2)], axis=1)
                 for g in range(G_A)], axis=0)
            acc[kv] = acc[kv] + _nn((x + pos_ref[kv, dp:dp + 1, :]).astype(BF), w1_ref[kv, dp])
    return [_nn(_gelu(acc[kv]).astype(BF), w2_ref[kv]) for kv in range(2)]


def _compress_t_body(src_ref, pos_ref, w1_ref, w2_ref, o_ref, xbuf, sem, *, nseq, n_pages):
    def copy(t):
        b, p = t // n_pages, t % n_pages
        return pltpu.make_async_copy(src_ref.at[b, :, pl.ds(pl.multiple_of(p * _PAGE, _PAGE), _PAGE)],
                                     xbuf.at[:, t, :], sem)

    def start(t, c):
        copy(t).start()
        return c

    def wait(t, c):
        copy(t).wait()
        return c
    lax.fori_loop(0, nseq * n_pages, start, 0)
    lax.fori_loop(0, nseq * n_pages, wait, 0)
    kc, vc = _compress_slab_rows(lambda r: xbuf[r], nseq * n_pages, pos_ref, w1_ref, w2_ref)
    o_ref[0] = kc
    o_ref[1] = vc


def _compress_t(kv_t, pos, w1, w2):
    nseq, SL, T = kv_t.shape
    n_pages = T // _PAGE
    one = lambda a: pl.BlockSpec(a.shape, lambda i: (0,) * a.ndim, pipeline_mode=pl.Buffered(1))
    n_out = G_A * nseq * n_pages
    return pl.pallas_call(
        functools.partial(_compress_t_body, nseq=nseq, n_pages=n_pages),
        grid=(1,),
        in_specs=[pl.BlockSpec(memory_space=pl.ANY), one(pos), one(w1), one(w2)],
        out_specs=pl.BlockSpec((2, n_out, _BPP * DH), lambda i: (0, 0, 0)),
        out_shape=jax.ShapeDtypeStruct((2, n_out, _BPP * DH), F32),
        scratch_shapes=[pltpu.VMEM((SL, nseq * n_pages, _PAGE), F32), pltpu.SemaphoreType.DMA(())],
        compiler_params=_cparams(("arbitrary",)),
        name="compress_t",
    )(kv_t, pos, w1, w2)


_KT = 512


def _nsa_prompt_body(q_ref, qr_ref, kvb_ref, kc_ref, vc_ref, ga_ref, e_ref, gx_ref, o_ref, os_scr, *, nc, ns):
    i = pl.program_id(1)
    QB = Q_BLOCK
    lane = lax.broadcasted_iota(jnp.int32, (QB, LANE), 1)
    lo64 = lane < DH
    zero_b = jnp.zeros((QB, LANE), BF)
    groups = range(G_A)

    def split_heads(ref, g):
        ev, od = [], []
        for c in range(2):
            xc = ref[:, (2 * g + c) * LANE:(2 * g + c + 1) * LANE]
            ev.append(jnp.where(lo64, xc, zero_b))
            od.append(jnp.where(lo64, zero_b, xc))
        return jnp.concatenate(ev, axis=0), jnp.concatenate(od, axis=0)

    tq_l = i * QB + lax.broadcasted_iota(jnp.int32, (nc, QB), 1)
    r_c = lax.broadcasted_iota(jnp.int32, (nc, QB), 0)
    half = nc // 2
    blk_c = jnp.where(r_c < half, 2 * r_c, 2 * (r_c - half) + 1)
    cmask = (blk_c * L_CMP + (L_CMP - 1)) <= tq_l
    cmask_f = cmask.astype(F32)
    tq_s = i * QB + lax.broadcasted_iota(jnp.int32, (ns, QB), 1)
    blk_s = lax.broadcasted_iota(jnp.int32, (ns, QB), 0)
    valid = (blk_s * L_SEL) <= tq_s
    forced = (blk_s == 0) | (blk_s == tq_s // L_SEL)
    n_top = min(N_SEL, ns)

    def select(imp):
        score = jnp.where(valid, imp + jnp.where(forced, FORCE_BONUS, 0.0), NEG)
        rank = jnp.zeros((ns, QB), F32)
        for b2 in range(ns):
            row = score[b2:b2 + 1, :]
            rank = rank + jnp.where(blk_s > b2, jnp.where(row >= score, 1.0, 0.0), jnp.where(row > score, 1.0, 0.0))
        return jnp.where((rank < n_top) & (score > 0.5 * NEG), 1.0, 0.0)

    def score_stage(chains):
        scored = []
        for qs, kmat, _, bias in chains:
            s = _nt(qs, kmat)
            s = (s.reshape(2, QB, s.shape[1]) + bias[None]).reshape(s.shape)
            scored.append((s, jnp.max(s, axis=1, keepdims=True)))
        return scored

    def value_stage(scored, chains):
        outs = []
        for (s, m), (_, _, vmat, _) in zip(scored, chains):
            p = jnp.exp2(s - m)
            outs.append(_nn(p.astype(BF), vmat) / jnp.sum(p, axis=1, keepdims=True))
        return outs

    def masked_attend(chains):
        return value_stage(score_stage(chains), chains)


    qr = [split_heads(qr_ref, g) for g in groups]
    n_tiles = (i * QB + QB + _KT - 1) // _KT

    cmp_scores = [[_nt(kc_ref[g], qs) for qs in split_heads(q_ref, g)] for g in groups]

    WK = WINDOW + QB
    start = pl.multiple_of(jnp.maximum(i * QB - WINDOW, 0), QB)
    diff = (i * QB + lax.broadcasted_iota(jnp.int32, (QB, WK), 0)) - (start + lax.broadcasted_iota(jnp.int32, (QB, WK), 1))
    wb = jnp.where((diff >= 0) & (diff < WINDOW), 0.0, NEG)
    win_chains = []
    for g in groups:
        wkv = kvb_ref[1, g, 0, pl.ds(start, WK), :]
        wvk = kvb_ref[1, g, 1, pl.ds(start, WK), :]
        win_chains += [(qr[g][0], wkv, wvk, wb), (qr[g][1], wvk, wkv, wb)]
    win_scored = score_stage(win_chains)

    imps, o_cmp = [], []
    for g in groups:
        vc = vc_ref[g]
        imp = jnp.zeros((ns, QB), F32)
        oc = {}
        for stack in range(2):
            st = cmp_scores[g][stack]
            for c in range(2):
                s = jnp.where(cmask, st[:, c * QB:(c + 1) * QB], NEG)
                p = jnp.exp2(s - jnp.max(s, axis=0, keepdims=True)) * cmask_f
                p = p / jnp.maximum(jnp.sum(p, axis=0, keepdims=True), 1e-30)
                imp = imp + p[0:half] + p[half:nc]
                oc[(c, stack)] = _tn(p.astype(BF), vc)
        o_cmp.append(oc)
        imps.append(imp)

    ow = value_stage(win_scored, win_chains)
    sel_b = [select(imp).astype(BF) for imp in imps]

    gates = []
    for g in groups:
        ga = ga_ref[:, g * LANE:(g + 1) * LANE]
        hi = ga.astype(BF)
        lo = (ga - hi.astype(F32)).astype(BF)
        gates.append(_nn(jnp.concatenate([hi, lo], axis=1), gx_ref[...]))

    def gate_of(g, br, c):
        j = br * 2 + c
        return gates[g][:, j * LANE:(j + 1) * LANE]

    part = [[gate_of(g, 0, c) * jnp.where(lo64, o_cmp[g][(c, 0)], o_cmp[g][(c, 1)])
             + gate_of(g, 2, c) * jnp.where(lo64, ow[2 * g][c * QB:(c + 1) * QB], ow[2 * g + 1][c * QB:(c + 1) * QB])
             for c in range(2)] for g in groups]


    def slc_variant(nk):
        tq_r = i * QB + lax.broadcasted_iota(jnp.int32, (QB, _KT), 0)
        causal = (nk - _KT + lax.broadcasted_iota(jnp.int32, (QB, _KT), 1)) <= tq_r
        chains = []
        for g in groups:
            bias = _tn(sel_b[g], e_ref[:, 0:nk]) - MASK_BIG
            tail = jnp.where(causal, bias[:, nk - _KT:], -MASK_BIG)
            bias = tail if nk == _KT else jnp.concatenate([bias[:, :nk - _KT], tail], axis=1)
            kv = kvb_ref[0, g, 0, 0:nk, :]
            vk = kvb_ref[0, g, 1, 0:nk, :]
            chains.append((qr[g][0], kv, vk, bias))
            chains.append((qr[g][1], vk, kv, bias))
        for c, o in enumerate(masked_attend(chains)):
            os_scr[c // 2, c % 2] = o

    for k in range(1, kvb_ref.shape[3] // _KT + 1):
        pl.when(n_tiles == k)(functools.partial(slc_variant, k * _KT))

    for g in groups:
        os_e = os_scr[g, 0]
        os_o = os_scr[g, 1]
        for c in range(2):
            rows = slice(c * QB, (c + 1) * QB)
            acc = part[g][c] + gate_of(g, 1, c) * jnp.where(lo64, os_e[rows], os_o[rows])
            o_ref[:, (2 * g + c) * LANE:(2 * g + c + 1) * LANE] = acc.astype(BF)


def _gate_expand():
    gx = np.zeros((LANE, 3 * 2 * LANE), np.float32)
    for br in range(3):
        for c in range(2):
            j = br * 2 + c
            gx[br * HPG + 2 * c, j * LANE:j * LANE + DH] = 1.0
            gx[br * HPG + 2 * c + 1, j * LANE + DH:(j + 1) * LANE] = 1.0
    return jnp.asarray(np.concatenate([gx, gx], axis=0), BF)


def _sel_expand(ns, nkeys):
    e = (np.arange(nkeys)[None, :] // L_SEL) == np.arange(ns)[:, None]
    return jnp.asarray(e * MASK_BIG, BF)


def _nsa_prompt(q, qr, kvb, kc2, vc2, ga, B, T):
    nqb = T // Q_BLOCK
    nc, ns = T // L_CMP, T // L_SEL
    assert T >= WINDOW + Q_BLOCK and T % _KT == 0
    qspec = pl.BlockSpec((Q_BLOCK, 512), lambda b, i: (b * nqb + i, 0))
    cspec = pl.BlockSpec((None, G_A, nc, LANE), lambda b, i: (b, 0, 0, 0))
    return pl.pallas_call(
        functools.partial(_nsa_prompt_body, nc=nc, ns=ns),
        grid=(B, nqb),
        in_specs=[qspec, qspec,
                  pl.BlockSpec((2, G_A, 2, T, LANE), lambda b, i: (0, 0, 0, b, 0)),
                  cspec, cspec,
                  pl.BlockSpec((Q_BLOCK, G_A * LANE), lambda b, i: (b * nqb + i, 0)),
                  pl.BlockSpec((ns, T), lambda b, i: (0, 0)),
                  pl.BlockSpec((2 * LANE, 6 * LANE), lambda b, i: (0, 0))],
        out_specs=qspec,
        out_shape=jax.ShapeDtypeStruct((B * T, 512), BF),
        scratch_shapes=[pltpu.VMEM((G_A, 2, 2 * Q_BLOCK, LANE), F32)],
        compiler_params=_cparams(("arbitrary", "arbitrary")),
        name="nsa_prompt",
    )(q, qr, kvb, kc2, vc2, ga, _sel_expand(ns, T), _gate_expand())


def _ret_body(qkv_ref, gr_ref, s0_ref, dm_ref, dq_ref, wk_ref, gc_ref, gnw_ref, z_ref, sout_ref, s_scr, *, C, sb):
    c = pl.program_id(1)

    @pl.when(c == 0)
    def _():
        s_scr[...] = s0_ref[...]

    units = [(j, h, slice(j * C, (j + 1) * C), slice(h * LANE, (h + 1) * LANE))
             for j in range(sb) for h in range(H_R)]
    qkv = [tuple(qkv_ref[t, rows, sl] for t in range(3)) for _, _, rows, sl in units]
    inner = [_nt(q, k) * dm_ref[h] for (q, k, _), (_, h, _, _) in zip(qkv, units)]
    cross = [_nn(q, s_scr[j, h].astype(BF)) * dq_ref[h] for (q, _, _), (j, h, _, _) in zip(qkv, units)]
    outs = [_nn(a.astype(BF), v) + c for a, c, (_, _, v) in zip(inner, cross, qkv)]
    for (_, k, v), (j, h, _, _) in zip(qkv, units):
        kw = (k.astype(F32) * wk_ref[h]).astype(BF)
        s_new = gc_ref[h, 0:1, :] * s_scr[j, h] + _tn(kw, v)
        s_scr[j, h] = s_new
        sout_ref[j, h] = s_new
    for o, (_, _, rows, sl) in zip(outs, units):
        mu = jnp.mean(o, axis=-1, keepdims=True)
        d = o - mu
        var = jnp.mean(d * d, axis=-1, keepdims=True)
        on = d * lax.rsqrt(var + EPS) * gnw_ref[:, sl]
        z_ref[rows, sl] = (gr_ref[rows, sl].astype(F32) * on).astype(BF)


def _retention(ret3, gr, s0, gnw, nseq, rows_per_seq, C, c_true, sb):
    nC = rows_per_seq // C
    assert sb == 1 or nC == 1
    rows = nseq * rows_per_seq
    dm, dq, wk, gc = _ret_decay(C, c_true)
    full = lambda a: pl.BlockSpec(a.shape, lambda b, c: (0,) * a.ndim)
    return pl.pallas_call(
        functools.partial(_ret_body, C=C, sb=sb),
        grid=(nseq // sb, nC),
        in_specs=[pl.BlockSpec((3, sb * C, H_R * LANE), lambda b, c: (0, b * nC + c, 0)),
                  pl.BlockSpec((sb * C, H_R * LANE), lambda b, c: (b * nC + c, 0)),
                  pl.BlockSpec((sb, H_R, DK_R, DV_R), lambda b, c: (b, 0, 0, 0)),
                  full(dm), full(dq), full(wk), full(gc), full(gnw)],
        out_specs=[pl.BlockSpec((sb * C, H_R * LANE), lambda b, c: (b * nC + c, 0)),
                   pl.BlockSpec((sb, H_R, DK_R, DV_R), lambda b, c: (b, 0, 0, 0))],
        out_shape=[jax.ShapeDtypeStruct((rows, 512), BF),
                   jax.ShapeDtypeStruct((nseq, H_R, DK_R, DV_R), F32)],
        scratch_shapes=[pltpu.VMEM((sb, H_R, DK_R, DV_R), F32)],
        compiler_params=_cparams(("arbitrary", "arbitrary")),
        name="retention",
    )(ret3, gr, s0, dm, dq, wk, gc, gnw)


_FFN_PART = 256


def _mix_ffn_body(x_ref, oa_ref, zr_ref, gm_ref, gt1_ref, wa_ref, wr_ref, wo_ref,
                  nw_ref, sh_ref, sc_ref, gt_ref, wu_ref, cw_ref, cb_ref, wd_ref, nf_ref, p1_ref, p2_ref,
                  y_ref, a_ref, carry, *, blocks_per_seq, seq_rows):
    i = pl.program_id(0)
    tm = x_ref.shape[0]
    parts = [slice(k * _FFN_PART, (k + 1) * _FFN_PART) for k in range(tm // _FFN_PART)] if tm > _FFN_PART else [slice(0, tm)]

    def mod(ref, r):
        return ref[...] if ref.shape[0] == 1 else ref[r, :]

    ya = [_nn(oa_ref[r, :], wa_ref[...]) for r in parts]
    yr = [_nn(zr_ref[r, :], wr_ref[...]) for r in parts]
    xs = []
    for r, ya_k, yr_k in zip(parts, ya, yr):
        gm = gm_ref[r, :].astype(F32)
        merged = gm[:, 0:D_MODEL] * ya_k + gm[:, D_MODEL:2 * D_MODEL] * yr_k
        xs.append(x_ref[r, :] + mod(gt1_ref, r) * _nn(merged.astype(BF), wo_ref[...]))
    hs = []
    for r, x in zip(parts, xs):
        ms = jnp.mean(x * x, axis=-1, keepdims=True)
        h = (x * lax.rsqrt(ms + EPS)) * nw_ref[...]
        hs.append((h * (1.0 + mod(sc_ref, r)) + mod(sh_ref, r)).astype(BF))
    a_parts = [_nn(h, wu_ref[:, 0:D_FF]) for h in hs]
    b_parts = [_nn(h, wu_ref[:, D_FF:2 * D_FF]) for h in hs]
    a = a_parts[0] if len(parts) == 1 else jnp.concatenate(a_parts, axis=0)
    a_ref[...] = a[tm - a_ref.shape[0]:tm, :]
    rid = lax.broadcasted_iota(jnp.int32, (tm, D_FF), 0)
    if seq_rows is None:
        first = (i % blocks_per_seq) == 0
        prev = jnp.where(first, p1_ref[...], carry[...])
        carry[...] = a[tm - 8:tm, :]
        am1 = jnp.where(rid == 0, prev[7:8, :], pltpu.roll(a, 1, 0))
        am2 = jnp.where(rid == 0, prev[6:7, :], jnp.where(rid == 1, prev[7:8, :], pltpu.roll(a, 2, 0)))
    else:
        s = rid % seq_rows
        am1 = jnp.where(s == 0, p1_ref[...], pltpu.roll(a, 1, 0))
        am2 = jnp.where(s <= 1, p2_ref[...], pltpu.roll(a, 2, 0))
    u = cb_ref[...] + am2 * cw_ref[0:1, :] + am1 * cw_ref[1:2, :] + a * cw_ref[2:3, :]
    ffs = [_nn((_gelu(u[r]) * b_k).astype(BF), wd_ref[...]) for r, b_k in zip(parts, b_parts)]
    for r, x, ff in zip(parts, xs, ffs):
        x2 = x + mod(gt_ref, r) * ff
        ms2 = jnp.mean(x2 * x2, axis=-1, keepdims=True)
        y_ref[r, :] = (x2 * lax.rsqrt(ms2 + EPS)) * nf_ref[...]


def _mix_ffn(x2d, oa, zr, gm, gt1, wa, wr, wo, nw, sh, sc, gt, wu, cw, cb, wd, nf, p1, p2,
             tm, blocks_per_seq, per_row_mods, seq_rows):
    rows = x2d.shape[0]
    a_rows = tm if seq_rows is not None else 8
    if per_row_mods:
        mod_spec = pl.BlockSpec((tm, D_MODEL), lambda i: (i, 0))
    else:
        mod_spec = pl.BlockSpec((None, 1, D_MODEL), lambda i: (i // blocks_per_seq, 0, 0))
    row = lambda w: pl.BlockSpec((tm, w), lambda i: (i, 0))
    full = lambda a, b: pl.BlockSpec((a, b), lambda i: (0, 0), pipeline_mode=pl.Buffered(1))
    vec = lambda w: pl.BlockSpec((1, w), lambda i: (0, 0))
    pspec = pl.BlockSpec(p1.shape, lambda i: (0, 0))
    return pl.pallas_call(
        functools.partial(_mix_ffn_body, blocks_per_seq=blocks_per_seq, seq_rows=seq_rows),
        grid=(rows // tm,),
        in_specs=[row(D_MODEL), row(512), row(512), row(2048), mod_spec,
                  full(512, D_MODEL), full(512, D_MODEL), full(D_MODEL, D_MODEL),
                  vec(D_MODEL), mod_spec, mod_spec, mod_spec,
                  full(D_MODEL, 2 * D_FF), pl.BlockSpec((CONV_W, D_FF), lambda i: (0, 0)), vec(D_FF),
                  full(D_FF, D_MODEL), vec(D_MODEL), pspec, pspec],
        out_specs=[row(D_MODEL), pl.BlockSpec((a_rows, D_FF), lambda i: (i, 0))],
        out_shape=[jax.ShapeDtypeStruct((rows, D_MODEL), F32),
                   jax.ShapeDtypeStruct((rows // tm * a_rows, D_FF), F32)],
        scratch_shapes=[pltpu.VMEM((8, D_FF), F32)],
        compiler_params=_cparams(("arbitrary",)),
        name="mix_ffn",
    )(x2d, oa, zr, gm, gt1, wa, wr, wo, nw, sh, sc, gt, wu, cw, cb, wd, nf, p1, p2)


def _prompt_path(x_prompt, mods_p, W):
    B, T, _ = x_prompt.shape
    rows = B * T
    x2d = x_prompt.reshape(rows, D_MODEL)
    pos = np.arange(T)
    tm = 256
    sh1, sc1, gt1, sh2, sc2, gt2 = [mods_p[:, None, j * D_MODEL:(j + 1) * D_MODEL] for j in range(6)]
    (q, qr, kvc, kvs, kvw, kvb, ret3, gr, gm, ga) = _inproj(
        x2d, W["norm1"], sh1, sc1, W["w_in"], _rope_tab(pos, Q_SCALE), _rope_tab(pos, 1.0), _ret_tab(pos),
        tm, T // tm, False, True)
    nc, n_pages = T // L_CMP, T // _PAGE
    comp = _compress_t(kvc, W["phi_posT"], W["phi_w1p"], W["phi_w2p"])
    comp = comp.reshape(2, G_A, B, n_pages, 2, 2, DH).transpose(0, 2, 1, 5, 3, 4, 6).reshape(2, B, G_A, nc, DH)
    comp2 = jnp.concatenate([comp, comp], axis=-1).astype(BF)
    oa = _nsa_prompt(q, qr, kvb, comp2[0], comp2[1], ga, B, T)
    C = 256 if T % 256 == 0 else T
    zr, ret_new = _retention(ret3, gr, jnp.zeros((B, H_R, DK_R, DV_R), F32), W["gnw"], B, T, C, C, 1)
    zeros8 = jnp.zeros((8, D_FF), F32)
    tf = 2 * _FFN_PART if T % (2 * _FFN_PART) == 0 else tm
    y, a_up = _mix_ffn(x2d, oa, zr, gm, gt1, W["w_up_a"], W["w_up_r"], W["w_out"],
                       W["norm2"], sh2, sc2, gt2, W["w_ffn_up"], W["conv_w"], W["conv_b"], W["w_ffn_down"],
                       W["normf"], zeros8, zeros8, tf, T // tf, False, None)
    wsz = min(WINDOW, T)
    rows_major = lambda t: t.reshape(B, 2, G_A, DH, t.shape[-1]).transpose(0, 4, 1, 2, 3)[None]
    outs = dict(
        y=y.reshape(B, T, D_MODEL),
        cmp=rows_major(kvc), slc=rows_major(kvs), win=rows_major(kvw[:, :, T - wsz:]),
        ret=ret_new[None],
        conv=a_up.reshape(B, T // tf, 8, D_FF)[None, :, T // tf - 1, 8 - (CONV_W - 1):],
    )
    return outs


_QC = 32


def _cmp_paged_body(pt_ref, cache_ref, q_ref, pos_ref, w1_ref, w2_ref, ocmp_ref, topi_ref, xbuf, sem,
                    *, n_pages, n_pick, n_q):
    b = pl.program_id(0)
    nb = pl.num_programs(0)
    slot = b % 2

    def copy(bb, sl, p):
        return pltpu.make_async_copy(cache_ref.at[pt_ref[bb, p]], xbuf.at[sl, :, p, :], sem.at[sl])

    def issue(bb, sl):
        for p in range(n_pages):
            copy(bb, sl, p).start()

    @pl.when(b == 0)
    def _():
        issue(b, slot)

    @pl.when(b + 1 < nb)
    def _():
        issue(b + 1, 1 - slot)

    for p in range(n_pages):
        copy(b, slot, p).wait()

    comp = _compress_slab_rows(lambda r: xbuf[slot, r], n_pages, pos_ref, w1_ref, w2_ref)

    def lane_groups(x, op):
        r = x
        for j in range(1, _BPP):
            r = op(r, pltpu.roll(x, j * _QC, 1))
        return r

    grp = range(G_A)
    sts = [_nn(comp[0][g * n_pages:(g + 1) * n_pages].astype(BF), q_ref[g]) for g in grp]
    ps = []
    for g in grp:
        m = lane_groups(jnp.max(sts[g], axis=0, keepdims=True), jnp.maximum)
        p = jnp.exp2(sts[g] - m)
        ps.append(p / lane_groups(jnp.sum(p, axis=0, keepdims=True), jnp.add))
    for g in grp:
        r_full = _tn(ps[g].astype(BF), comp[1][g * n_pages:(g + 1) * n_pages].astype(BF))
        o = r_full[0:_QC, 0:DH]
        for j in range(1, _BPP):
            o = o + r_full[j * _QC:(j + 1) * _QC, j * DH:(j + 1) * DH]
        ocmp_ref[g] = o
    imp = None
    for g in grp:
        pair = ps[g] + pltpu.roll(ps[g], LANE - _QC, 1)
        t = pair
        for r in range(1, HPG):
            t = t + pltpu.roll(pair, LANE - r * n_q, 1)
        if g:
            t = pltpu.roll(t, g * n_q, 1)
        lane_g = lax.broadcasted_iota(jnp.int32, t.shape, 1) % (2 * _QC)
        t = jnp.where((lane_g >= g * n_q) & (lane_g < (g + 1) * n_q), t, 0.0)
        imp = t if imp is None else imp + t
    sc = jnp.concatenate([imp, pltpu.roll(imp, 2 * _QC, 1)], axis=0)
    row = lax.broadcasted_iota(jnp.int32, (2 * n_pages, LANE), 0)
    blk = jnp.where(row < n_pages, 2 * row, 2 * (row - n_pages) + 1)
    score = sc + jnp.where(blk == 0, FORCE_BONUS, 0.0)
    topi_ref[...] = jnp.zeros((8, LANE), jnp.int32)
    for k in range(n_pick):
        mx = jnp.max(score, axis=0, keepdims=True)
        idx = jnp.min(jnp.where(score == mx, blk, 2 * n_pages), axis=0, keepdims=True)
        topi_ref[k:k + 1, :] = idx
        score = jnp.where(blk == idx, -jnp.inf, score)


def _cmp_paged(page_table, cache_t, q_bd, pos, w1, w2, n_q):
    DB, n_pages = page_table.shape
    SL = cache_t.shape[1]
    page_rows = cache_t.shape[2]
    kern = functools.partial(_cmp_paged_body, n_pages=n_pages, n_pick=N_SEL - 1, n_q=n_q)
    one = dict(pipeline_mode=pl.Buffered(1))
    return pl.pallas_call(
        kern,
        grid_spec=pltpu.PrefetchScalarGridSpec(
            num_scalar_prefetch=1, grid=(DB,),
            in_specs=[pl.BlockSpec(memory_space=pl.ANY),
                      pl.BlockSpec((None, G_A, _BPP * DH, LANE), lambda b, pt: (b, 0, 0, 0)),
                      pl.BlockSpec(pos.shape, lambda b, pt: (0, 0, 0), **one),
                      pl.BlockSpec(w1.shape, lambda b, pt: (0, 0, 0, 0), **one),
                      pl.BlockSpec((2, _BPP * DH, _BPP * DH), lambda b, pt: (0, 0, 0), **one)],
            out_specs=[pl.BlockSpec((None, G_A, _QC, DH), lambda b, pt: (b, 0, 0, 0)),
                       pl.BlockSpec((None, 8, LANE), lambda b, pt: (b, 0, 0))],
            scratch_shapes=[pltpu.VMEM((2, SL, n_pages, page_rows), F32),
                            pltpu.SemaphoreType.DMA((2,))]),
        out_shape=[jax.ShapeDtypeStruct((DB, G_A, _QC, DH), F32),
                   jax.ShapeDtypeStruct((DB, 8, LANE), jnp.int32)],
        compiler_params=_cparams(("arbitrary",)),
        name="cmp_paged",
    )(page_table, cache_t, q_bd, pos, w1, w2)


def _slc_win_paged_body(pt_ref, ti_ref, cslc_ref, win_ref, q_ref, tiv_ref, ns_ref, nw_ref, ex_ref,
                        oslc_ref, owin_ref, kbuf, sem, *, n_q, n_pick, page_rows):
    b = pl.program_id(0)
    nb = pl.num_programs(0)
    slot = b % 2
    n_slab = n_q * n_pick
    bpp = page_rows // L_SEL
    wb = win_ref.shape[1]

    def copies(bb, sl):
        cps = []
        for g in range(G_A):
            for j in range(n_slab):
                page = pt_ref[bb, ti_ref[bb, g * n_slab + j] // bpp]
                for kv in range(2):
                    cps.append(pltpu.make_async_copy(
                        cslc_ref.at[page, pl.ds((kv * G_A + g) * DH, DH), :],
                        kbuf.at[sl, g, kv, :, pl.ds(j * page_rows, page_rows)], sem.at[sl]))
        return cps

    @pl.when(b == 0)
    def _():
        for cp in copies(b, slot):
            cp.start()

    @pl.when(b + 1 < nb)
    def _():
        for cp in copies(b + 1, 1 - slot):
            cp.start()

    for cp in copies(b, slot):
        cp.wait()

    nq_rows = q_ref.shape[1]
    nk = n_slab * page_rows
    row_q = lax.broadcasted_iota(jnp.int32, (nq_rows, nk), 0) % n_q
    col = lax.broadcasted_iota(jnp.int32, (nq_rows, nk), 1)
    own = row_q == col // (n_pick * page_rows)
    half = ((col % page_rows) // L_SEL).astype(F32)
    nnew = ns_ref.shape[2]
    new_ok = (lax.broadcasted_iota(jnp.int32, (nq_rows, nnew), 1)
              <= lax.broadcasted_iota(jnp.int32, (nq_rows, nnew), 0) % n_q)
    dwin = (wb + lax.broadcasted_iota(jnp.int32, (nq_rows, wb), 0) % n_q
            - lax.broadcasted_iota(jnp.int32, (nq_rows, wb), 1))
    win_ok = (dwin >= 0) & (dwin < WINDOW)

    chains = []
    for g in range(G_A):
        q = q_ref[g]
        par = (tiv_ref[g] % bpp).astype(F32).astype(BF)
        want = _nn(par, ex_ref[...])[0:1, :]
        chains.append((q, kbuf[slot, g, 0].astype(BF), kbuf[slot, g, 1].astype(BF), own & (half == want),
                       ns_ref[g, 0].astype(BF), ns_ref[g, 1].astype(BF), oslc_ref, g))
        kw = win_ref[pl.ds((0 * G_A + g) * DH, DH), :].astype(BF)
        vw = win_ref[pl.ds((1 * G_A + g) * DH, DH), :].astype(BF)
        chains.append((q, kw, vw, win_ok, nw_ref[g, 0].astype(BF), nw_ref[g, 1].astype(BF), owin_ref, g))
    scores = [(jnp.where(ok_old, _nn(q, kt_old), NEG), jnp.where(new_ok, _nt(q, k_new), NEG))
              for q, kt_old, _, ok_old, k_new, _, _, _ in chains]
    probs = []
    for s_o, s_n in scores:
        m = jnp.maximum(jnp.max(s_o, axis=1, keepdims=True), jnp.max(s_n, axis=1, keepdims=True))
        p_o = jnp.exp2(s_o - m)
        p_n = jnp.exp2(s_n - m)
        probs.append((p_o, p_n, jnp.sum(p_o, axis=1, keepdims=True) + jnp.sum(p_n, axis=1, keepdims=True)))
    for (p_o, p_n, den), (_, _, vt_old, _, _, v_new, out_ref, g) in zip(probs, chains):
        out_ref[g] = (_nt(p_o.astype(BF), vt_old) + _nn(p_n.astype(BF), v_new)) / den


def _slc_win_paged(page_table, topi_flat, topi_vec, cache_t, win_t, q_rot, new_slc, new_win, n_q):
    DB = page_table.shape[0]
    n_pick = N_SEL - 1
    page_rows = cache_t.shape[2]
    wb = win_t.shape[2]
    nq_rows = q_rot.shape[2]
    n_slab = n_q * n_pick
    ex = (np.arange(n_slab * page_rows)[None, :] // page_rows) == np.arange(LANE)[:, None]
    kern = functools.partial(_slc_win_paged_body, n_q=n_q, n_pick=n_pick, page_rows=page_rows)
    bspec = lambda shp: pl.BlockSpec((None,) + shp, lambda b, pt, ti: (b,) + (0,) * len(shp))
    return pl.pallas_call(
        kern,
        grid_spec=pltpu.PrefetchScalarGridSpec(
            num_scalar_prefetch=2, grid=(DB,),
            in_specs=[pl.BlockSpec(memory_space=pl.ANY), bspec((2 * G_A * DH, wb)),
                      bspec((G_A, nq_rows, DH)), bspec((G_A, 16, LANE)),
                      bspec((G_A, 2, 16, DH)), bspec((G_A, 2, 16, DH)),
                      pl.BlockSpec((LANE, n_slab * page_rows), lambda b, pt, ti: (0, 0))],
            out_specs=[bspec((G_A, nq_rows, DH)), bspec((G_A, nq_rows, DH))],
            scratch_shapes=[pltpu.VMEM((2, G_A, 2, DH, n_slab * page_rows), F32),
                            pltpu.SemaphoreType.DMA((2,))]),
        out_shape=[jax.ShapeDtypeStruct((DB, G_A, nq_rows, DH), F32)] * 2,
        compiler_params=_cparams(("arbitrary",)),
        name="slc_win_paged",
    )(page_table, topi_flat, cache_t, win_t, q_rot, topi_vec, new_slc, new_win, jnp.asarray(ex, BF))


def _gate_sample_body(oc_ref, os_ref, ow_ref, ga_ref, o_ref):
    rows = o_ref.shape[0]
    lo64 = lax.broadcasted_iota(jnp.int32, (rows, LANE), 1) < DH
    for c4 in range(H_A // 2):
        g, c = c4 // 2, c4 % 2
        sl = slice(c4 * LANE, (c4 + 1) * LANE)
        acc = jnp.zeros((rows, LANE), F32)
        for br, ref in enumerate((oc_ref, os_ref, ow_ref)):
            col = g * LANE + br * HPG + 2 * c
            gate = jnp.where(lo64, ga_ref[:, col:col + 1], ga_ref[:, col + 1:col + 2])
            acc = acc + gate * ref[:, sl]
        o_ref[:, sl] = acc.astype(BF)


def _gate_sample(oc, osl, ow, ga):
    rows = oc.shape[0]
    full = lambda w: pl.BlockSpec((rows, w), lambda i: (0, 0))
    return pl.pallas_call(
        _gate_sample_body, grid=(1,),
        in_specs=[full(512), full(512), full(512), full(256)],
        out_specs=full(512),
        out_shape=jax.ShapeDtypeStruct((rows, 512), BF),
        compiler_params=_cparams(("arbitrary",)),
        name="gate_sample",
    )(oc, osl, ow, ga)


def _sample_path(x_sample, mods_s, W, cache_cmp, cache_slc, state_win, state_ret, state_conv, page_table):
    DB, S, _ = x_sample.shape
    rows = DB * S
    page_rows = cache_cmp.shape[1]
    P = page_table.shape[1] * page_rows
    wb = state_win.shape[1]
    assert P % L_SEL == 0 and S < L_CMP and S <= 8 and wb == WINDOW and page_rows % L_SEL == 0
    assert P // L_SEL >= N_SEL and CONV_W == 3 and S >= CONV_W - 1
    pos = P + np.arange(S)
    pos_rows = np.tile(pos, DB)
    x2d = x_sample.reshape(rows, D_MODEL)
    modr = jnp.repeat(mods_s, S, axis=0)
    sh1, sc1, gt1, sh2, sc2, gt2 = [modr[:, j * D_MODEL:(j + 1) * D_MODEL] for j in range(6)]
    (q, qr, kvc, kvs, kvw, _, ret3, gr, gm, ga) = _inproj(
        x2d, W["norm1"], sh1, sc1, W["w_in"], _rope_tab(pos_rows, Q_SCALE), _rope_tab(pos_rows, 1.0),
        _ret_tab(pos_rows), rows, 1, True, False)

    def to_heads(t):
        return t.reshape(DB, S, G_A, HPG, DH).transpose(0, 2, 3, 1, 4).reshape(DB, G_A, HPG * S, DH)

    def from_heads(t):
        return t.reshape(DB, G_A, HPG, S, DH).transpose(0, 3, 1, 2, 4).reshape(rows, H_A * DH)

    assert page_rows == _BPP * L_CMP and HPG * S <= _QC and S * (N_SEL - 1) <= LANE
    slab = lambda t: t.transpose(0, 2, 3, 4, 1).reshape(t.shape[0], 2 * G_A * DH, t.shape[1])
    qt = jnp.pad(to_heads(q).transpose(0, 1, 3, 2), ((0, 0), (0, 0), (0, 0), (0, _QC - HPG * S)))
    q_bd = _blockdiag(qt)
    o_cmp, topi = _cmp_paged(page_table, slab(cache_cmp), q_bd, W["phi_posT"], W["phi_w1p"], W["phi_w2p"], S)
    n_pick = N_SEL - 1
    topi = topi[:, :n_pick, :G_A * S].transpose(0, 2, 1).reshape(DB, G_A, S * n_pick)
    topi_vec = jnp.broadcast_to(jnp.pad(topi, ((0, 0), (0, 0), (0, LANE - S * n_pick)))[:, :, None, :],
                                (DB, G_A, 16, LANE))

    def new_rows(t):
        t = t.reshape(DB, S, 2, G_A, DH).transpose(0, 3, 2, 1, 4)
        return jnp.pad(t, ((0, 0), (0, 0), (0, 0), (0, 16 - S), (0, 0)))

    o_slc, o_win = _slc_win_paged(page_table, topi.reshape(DB, G_A * S * n_pick), topi_vec, slab(cache_slc),
                                  slab(state_win), to_heads(qr), new_rows(kvs), new_rows(kvw), S)
    oa = _gate_sample(from_heads(o_cmp[:, :, :HPG * S]), from_heads(o_slc), from_heads(o_win), ga)

    RP = 16
    padr = lambda t: jnp.pad(t.reshape(t.shape[:-2] + (DB, S, 512)),
                             ((0, 0),) * (t.ndim - 1) + ((0, RP - S), (0, 0))).reshape(t.shape[:-2] + (DB * RP, 512))
    zr, ret_new = _retention(padr(ret3), padr(gr), state_ret, W["gnw"], DB, RP, RP, S, 8 if DB % 8 == 0 else 1)
    zr = zr.reshape(DB, RP, 512)[:, :S].reshape(rows, 512)
    zs = lambda n: jnp.zeros((DB, n, D_FF), F32)
    p1 = jnp.concatenate([state_conv[:, 1:2], zs(S - 1)], axis=1).reshape(rows, D_FF)
    p2 = jnp.concatenate([state_conv[:, 0:2], zs(S - 2)], axis=1).reshape(rows, D_FF)
    y, a_up = _mix_ffn(x2d, oa, zr, gm, gt1, W["w_up_a"], W["w_up_r"], W["w_out"],
                       W["norm2"], sh2, sc2, gt2, W["w_ffn_up"], W["conv_w"], W["conv_b"], W["w_ffn_down"],
                       W["normf"], p1, p2, rows, 1, True, S)
    shp = (1, DB, S, 2, G_A, DH)
    return dict(
        y=y.reshape(DB, S, D_MODEL),
        cmp=kvc.reshape(shp), slc=kvs.reshape(shp),
        win=jnp.concatenate([state_win[:, S:], kvw.reshape(DB, S, 2, G_A, DH)], axis=1)[None],
        ret=ret_new[None],
        conv=a_up.reshape(DB, S, D_FF)[None, :, S - (CONV_W - 1):],
    )


def kernel(x_prompt, x_sample, cache_cmp_kv, cache_slc_kv, state_win_kv, state_ret, state_conv, page_table,
           c_prompt, c_sample, norm1_w, norm2_w, w_ada, b_ada, w_in, phi_pos_k, phi_k1, phi_k2, phi_pos_v,
           phi_v1, phi_v2, w_up_a, ret_gn_w, w_up_r, w_out, w_ffn_up, ffn_conv_w, ffn_conv_b, w_ffn_down, normf_w):
    B = x_prompt.shape[0]
    l = 0
    W = dict(
        norm1=norm1_w[l].reshape(1, D_MODEL), norm2=norm2_w[l].reshape(1, D_MODEL), normf=normf_w.reshape(1, D_MODEL),
        w_in=_pack_w_in(w_in[l]),
        w_up_a=w_up_a[l].astype(BF), w_up_r=w_up_r[l].astype(BF), w_out=w_out[l].astype(BF),
        gnw=ret_gn_w[l].reshape(1, H_R * DV_R),
        w_ffn_up=w_ffn_up[l].astype(BF), conv_w=ffn_conv_w[l], conv_b=ffn_conv_b[l].reshape(1, D_FF),
        w_ffn_down=w_ffn_down[l].astype(BF),
    )
    W["phi_posT"], W["phi_w1p"], W["phi_w2p"] = _pack_phi_paged(
        phi_pos_k[l], phi_k1[l], phi_k2[l], phi_pos_v[l], phi_v1[l], phi_v2[l])
    mods = _mods(jnp.concatenate([c_prompt, c_sample], axis=0), w_ada[l], b_ada[l])
    p = _prompt_path(x_prompt, mods[:B], W)
    s = _sample_path(x_sample, mods[B:], W, cache_cmp_kv[l], cache_slc_kv[l], state_win_kv[l], state_ret[l],
                     state_conv[l], page_table)
    return (p["y"], s["y"], p["cmp"], s["cmp"], p["slc"], s["slc"], p["win"], s["win"],
            p["ret"], s["ret"], p["conv"], s["conv"])
```

```python
import functools

import numpy as np
import jax
import jax.numpy as jnp
from jax import lax
from jax.experimental import pallas as pl
from jax.experimental.pallas import tpu as pltpu

BF = jnp.bfloat16
F32 = jnp.float32

D_MODEL = 1024
H_A, G_A, DH = 8, 2, 64
HPG = H_A // G_A
ROPE_DIM = DH // 4
ROPE_THETA = 500000.0
L_CMP, L_SEL, N_SEL = 32, 64, 8
WINDOW = 512
Q_BLOCK = 128
FORCE_BONUS = 1e4
H_R, DK_R, DV_R = 4, 128, 128
RET_THETA = 10000.0
D_FF = 2816
CONV_W = 3
EPS = 1e-6
NEG = -1e30
MASK_BIG = 2.0 ** 100
Q_SCALE = DH ** -0.5 * 1.4426950408889634
LANE = 128
_ROW_PART = 256
VMEM_LIMIT = 56 * 1024 * 1024


def _cparams(sem):
    return pltpu.CompilerParams(dimension_semantics=sem, vmem_limit_bytes=VMEM_LIMIT)


def _sigmoid(x):
    return 1.0 / (1.0 + jnp.exp(-x))


def _gelu(x):
    return 0.5 * x * (1.0 + jnp.tanh(0.7978845608028654 * (x + 0.044715 * (x * x * x))))


def _nt(a, b):
    return lax.dot_general(a, b, (((1,), (1,)), ((), ())), preferred_element_type=F32)


def _tn(a, b):
    return lax.dot_general(a, b, (((0,), (0,)), ((), ())), preferred_element_type=F32)


def _nn(a, b):
    return jnp.dot(a, b, preferred_element_type=F32)


def _rope_tab(pos, scale):
    half = ROPE_DIM // 2
    inv = ROPE_THETA ** (-np.arange(half, dtype=np.float64) * (2.0 / ROPE_DIM))
    ang = pos.astype(np.float64)[:, None] * inv
    cos, sin = np.cos(ang), np.sin(ang)
    n = pos.shape[0]
    c = np.ones((n, DH)); s_lo = np.zeros((n, DH)); s_hi = np.zeros((n, DH))
    c[:, :half] = cos; c[:, half:ROPE_DIM] = cos
    s_lo[:, half:ROPE_DIM] = sin
    s_hi[:, :half] = -sin
    tab = np.concatenate([np.tile(t, (1, 2)) for t in (c, s_lo, s_hi)], axis=1) * scale
    return jnp.asarray(tab, F32)


def _ret_tab(pos):
    half = DK_R // 2
    inv = RET_THETA ** (-np.arange(half, dtype=np.float64) * (2.0 / DK_R))
    ang = pos.astype(np.float64)[:, None] * inv
    cos, sin = np.cos(ang), np.sin(ang)
    c = np.concatenate([cos, cos], axis=1)
    s = np.concatenate([-sin, sin], axis=1)
    ks = DK_R ** -0.5
    return jnp.asarray(np.concatenate([c, s, c * ks, s * ks], axis=1), F32)


def _ret_decay(C, c_true):
    h = np.arange(H_R, dtype=np.float64)
    log_g = np.log1p(-np.exp2(-5.0 - h))
    i = np.arange(C, dtype=np.float64)
    diff = i[:, None] - i[None, :]
    dm = np.where(diff >= 0, np.exp(log_g[:, None, None] * np.maximum(diff, 0.0)), 0.0)
    dq = np.exp(log_g[:, None] * (i + 1.0))[:, :, None] * np.ones((1, 1, LANE))
    wk = np.exp(log_g[:, None] * (c_true - 1.0 - i))[:, :, None] * np.ones((1, 1, LANE))
    wk = np.where(i[None, :, None] < c_true, wk, 0.0)
    gc = np.exp(log_g * c_true)[:, None, None] * np.ones((1, 8, LANE))
    return (jnp.asarray(dm, F32), jnp.asarray(dq, F32), jnp.asarray(wk, F32), jnp.asarray(gc, F32))


def _mods_body(c_ref, w_ref, b_ref, o_ref):
    c = c_ref[...]
    s = c * _sigmoid(c)
    o_ref[...] = _nn(s.astype(BF), w_ref[...].astype(BF)) + b_ref[...]


def _mods(c_all, w_ada, b_ada):
    n = c_all.shape[0]
    nout = w_ada.shape[1]
    tn = 1024
    return pl.pallas_call(
        _mods_body,
        grid=(nout // tn,),
        in_specs=[pl.BlockSpec((n, D_MODEL), lambda j: (0, 0)),
                  pl.BlockSpec((D_MODEL, tn), lambda j: (0, j)),
                  pl.BlockSpec((1, tn), lambda j: (0, j))],
        out_specs=pl.BlockSpec((n, tn), lambda j: (0, j)),
        out_shape=jax.ShapeDtypeStruct((n, nout), F32),
        compiler_params=_cparams(("arbitrary",)),
        name="mods",
    )(c_all, w_ada, b_ada.reshape(1, nout))


_C_Q, _C_KV, _C_QR, _C_KR, _C_VR, _C_GR, _C_GM, _C_GA, _C_END = (
    0, 512, 1280, 1792, 2304, 2816, 3328, 5376, 5632)


def _pack_w_in(w_in):
    wt = w_in.T
    o = np.cumsum((0, 512, 768, 24, 512, 512, 512, 512, 2048))
    q, kv, ga, qr, kr, vr, gr, gm = [wt[o[i]:o[i + 1]] for i in range(8)]
    ga = ga.reshape(3, G_A, HPG, D_MODEL).transpose(1, 0, 2, 3).reshape(G_A, 3 * HPG, D_MODEL)
    ga = jnp.pad(ga, ((0, 0), (0, LANE - 3 * HPG), (0, 0))).reshape(G_A * LANE, D_MODEL)
    return jnp.concatenate([q, kv, qr, kr, vr, gr, gm, ga], axis=0).astype(BF)


def _inproj_body(x_ref, nw_ref, sh_ref, sc_ref, w_ref, rq_ref, rk_ref, rr_ref,
                 q_ref, qr_ref, kvc_ref, kvs_ref, kvw_ref, kvb_ref, ret_ref, gr_ref, gm_ref, ga_ref, *, kv_t):
    tm = x_ref.shape[0]
    np_ = tm // _ROW_PART if tm > _ROW_PART else 1
    pr = tm // np_
    parts = [slice(k * pr, (k + 1) * pr) for k in range(np_)]
    lo64 = lax.broadcasted_iota(jnp.int32, (pr, LANE), 1) < DH

    def put_kv(out_ref, r, k, v):
        if kv_t:
            out_ref[0:LANE, r] = k.T
            out_ref[LANE:2 * LANE, r] = v.T
        else:
            out_ref[r, 0:LANE] = k
            out_ref[r, LANE:2 * LANE] = v

    def mod(ref, r):
        return ref[...] if ref.shape[0] == 1 else ref[r, :]

    hbs = []
    for r in parts:
        x = x_ref[r, :]
        ms = jnp.mean(x * x, axis=-1, keepdims=True)
        h = (x * lax.rsqrt(ms + EPS)) * nw_ref[...]
        hbs.append((h * (1.0 + mod(sc_ref, r)) + mod(sh_ref, r)).astype(BF))

    def mm(k, lo, hi):
        return _nt(hbs[k], w_ref[lo:hi, :])

    def rope(xc, tab_ref, r):
        return (xc * tab_ref[r, 0:LANE] + pltpu.roll(xc, 8, 1) * tab_ref[r, LANE:2 * LANE]
                + pltpu.roll(xc, LANE - 8, 1) * tab_ref[r, 2 * LANE:3 * LANE])

    gm_half = (_C_GA - _C_GM) // 2
    for k, r in enumerate(parts):
        gm_ref[r, 0:gm_half] = _sigmoid(mm(k, _C_GM, _C_GM + gm_half)).astype(BF)
    for k, r in enumerate(parts):
        g = mm(k, _C_GR, _C_GM)
        gr_ref[r, :] = (g * _sigmoid(g)).astype(BF)
    for k, r in enumerate(parts):
        gm_ref[r, gm_half:2 * gm_half] = _sigmoid(mm(k, _C_GM + gm_half, _C_GA)).astype(BF)

    for k, r in enumerate(parts):
        qa = mm(k, _C_Q, _C_KV)
        q_ref[r, :] = (qa * Q_SCALE).astype(BF)
        for c in range(4):
            qr_ref[r, c * LANE:(c + 1) * LANE] = rope(qa[:, c * LANE:(c + 1) * LANE], rq_ref, r).astype(BF)

    for k, r in enumerate(parts):
        kv = mm(k, _C_KV, _C_QR)
        put_kv(kvc_ref, r, kv[:, 0:LANE], kv[:, LANE:2 * LANE])
        for kind, out_ref in ((0, kvs_ref), (1, kvw_ref)):
            base = 256 + kind * 256
            kk = rope(kv[:, base:base + LANE], rk_ref, r)
            v = kv[:, base + LANE:base + 2 * LANE]
            put_kv(out_ref, r, kk, v)
            kr_ = pltpu.roll(kk, DH, 1)
            vr_ = pltpu.roll(v, DH, 1)
            kvb_ref[kind, 0, 0, r, :] = jnp.where(lo64, kk, vr_).astype(BF)
            kvb_ref[kind, 0, 1, r, :] = jnp.where(lo64, v, kr_).astype(BF)
            kvb_ref[kind, 1, 0, r, :] = jnp.where(lo64, kr_, v).astype(BF)
            kvb_ref[kind, 1, 1, r, :] = jnp.where(lo64, vr_, kk).astype(BF)

    for k, r in enumerate(parts):
        qr = mm(k, _C_QR, _C_KR)
        kr = mm(k, _C_KR, _C_VR)
        for hh in range(H_R):
            sl = slice(hh * LANE, (hh + 1) * LANE)
            xq = qr[:, sl]
            ret_ref[0, r, sl] = (xq * rr_ref[r, 0:LANE]
                                 + pltpu.roll(xq, DK_R // 2, 1) * rr_ref[r, LANE:2 * LANE]).astype(BF)
            xk = kr[:, sl]
            ret_ref[1, r, sl] = (xk * rr_ref[r, 2 * LANE:3 * LANE]
                                 + pltpu.roll(xk, DK_R // 2, 1) * rr_ref[r, 3 * LANE:4 * LANE]).astype(BF)
    for k, r in enumerate(parts):
        ga_ref[r, :] = _sigmoid(mm(k, _C_GA, _C_END))
    for k, r in enumerate(parts):
        ret_ref[2, r, :] = mm(k, _C_VR, _C_GR).astype(BF)


def _inproj(x2d, nw, sh, sc, w_pack, rq, rk, rr, tm, tab_blocks, per_row_mods, kv_t):
    rows = x2d.shape[0]
    nb = rows // tm
    if kv_t:
        kv_shape = jax.ShapeDtypeStruct((nb // tab_blocks, 256, tab_blocks * tm), F32)
        kv_spec = pl.BlockSpec((None, 256, tm), lambda i: (i // tab_blocks, 0, i % tab_blocks))
    else:
        kv_shape = jax.ShapeDtypeStruct((rows, 256), F32)
        kv_spec = pl.BlockSpec((tm, 256), lambda i: (i, 0))
    if per_row_mods:
        mod_spec = pl.BlockSpec((tm, D_MODEL), lambda i: (i, 0))
    else:
        mod_spec = pl.BlockSpec((None, 1, D_MODEL), lambda i: (i // tab_blocks, 0, 0))
    tab = lambda w: pl.BlockSpec((tm, w), lambda i: (i % tab_blocks, 0))
    row = lambda w: pl.BlockSpec((tm, w), lambda i: (i, 0))
    out_shapes = [
        jax.ShapeDtypeStruct((rows, 512), BF),
        jax.ShapeDtypeStruct((rows, 512), BF),
        kv_shape,
        kv_shape,
        kv_shape,
        jax.ShapeDtypeStruct((2, G_A, 2, rows, LANE), BF),
        jax.ShapeDtypeStruct((3, rows, 512), BF),
        jax.ShapeDtypeStruct((rows, 512), BF),
        jax.ShapeDtypeStruct((rows, 2048), BF),
        jax.ShapeDtypeStruct((rows, 256), F32),
    ]
    out_specs = [row(512), row(512), kv_spec, kv_spec, kv_spec,
                 pl.BlockSpec((2, G_A, 2, tm, LANE), lambda i: (0, 0, 0, i, 0)),
                 pl.BlockSpec((3, tm, 512), lambda i: (0, i, 0)),
                 row(512), row(2048), row(256)]
    return pl.pallas_call(
        functools.partial(_inproj_body, kv_t=kv_t),
        grid=(nb,),
        in_specs=[row(D_MODEL),
                  pl.BlockSpec((1, D_MODEL), lambda i: (0, 0)),
                  mod_spec, mod_spec,
                  pl.BlockSpec((_C_END, D_MODEL), lambda i: (0, 0), pipeline_mode=pl.Buffered(1)),
                  tab(384), tab(384), tab(512)],
        out_specs=out_specs,
        out_shape=out_shapes,
        compiler_params=_cparams(("arbitrary",)),
        name="inproj",
    )(x2d, nw, sh, sc, w_pack, rq, rk, rr)


_BPP = 4
_PAGE = _BPP * L_CMP


def _blockdiag(w):
    z = jnp.zeros_like(w)
    return jnp.concatenate(
        [jnp.concatenate([w if j == i else z for j in range(_BPP)], axis=-1) for i in range(_BPP)], axis=-2)


def _pack_phi_paged(phi_pos_k, phi_k1, phi_k2, phi_pos_v, phi_v1, phi_v2):
    w1, w2, pos = [], [], []
    for p_, a, b_ in ((phi_pos_k, phi_k1, phi_k2), (phi_pos_v, phi_v1, phi_v2)):
        w1.append(_blockdiag(a.astype(BF).reshape(L_CMP, DH, DH).transpose(1, 0, 2)))
        w2.append(_blockdiag(b_.astype(BF)))
        pos.append(jnp.tile(p_.T, (1, _BPP)))
    pos = jnp.stack(pos).reshape(2, DH // 2, 2 * _PAGE)
    w1 = jnp.stack(w1).reshape(2, DH // 2, 2 * _PAGE, _BPP * DH)
    return pos, w1, jnp.stack(w2)


def _compress_slab_rows(load, n_rows, pos_ref, w1_ref, w2_ref):
    acc = [jnp.zeros((G_A * n_rows, _BPP * DH), F32) for _ in range(2)]
    for dp in range(DH // 2):
        for kv in range(2):
            x = jnp.concatenate(
                [jnp.concatenate([load((kv * G_A + g) * DH + 2 * dp + j) for j in range(2)], axis=1)
                 for g in range(G_A)], axis=0)
            acc[kv] = acc[kv] + _nn((x + pos_ref[kv, dp:dp + 1, :]).astype(BF), w1_ref[kv, dp])
    return [_nn(_gelu(acc[kv]).astype(BF), w2_ref[kv]) for kv in range(2)]


def _compress_t_body(src_ref, pos_ref, w1_ref, w2_ref, o_ref, xbuf, sem, *, nseq, n_pages):
    def copy(t):
        b, p = t // n_pages, t % n_pages
        return pltpu.make_async_copy(src_ref.at[b, :, pl.ds(pl.multiple_of(p * _PAGE, _PAGE), _PAGE)],
                                     xbuf.at[:, t, :], sem)

    def start(t, c):
        copy(t).start()
        return c

    def wait(t, c):
        copy(t).wait()
        return c
    lax.fori_loop(0, nseq * n_pages, start, 0)
    lax.fori_loop(0, nseq * n_pages, wait, 0)
    kc, vc = _compress_slab_rows(lambda r: xbuf[r], nseq * n_pages, pos_ref, w1_ref, w2_ref)
    o_ref[0] = kc
    o_ref[1] = vc


def _compress_t(kv_t, pos, w1, w2):
    nseq, SL, T = kv_t.shape
    n_pages = T // _PAGE
    one = lambda a: pl.BlockSpec(a.shape, lambda i: (0,) * a.ndim, pipeline_mode=pl.Buffered(1))
    n_out = G_A * nseq * n_pages
    return pl.pallas_call(
        functools.partial(_compress_t_body, nseq=nseq, n_pages=n_pages),
        grid=(1,),
        in_specs=[pl.BlockSpec(memory_space=pl.ANY), one(pos), one(w1), one(w2)],
        out_specs=pl.BlockSpec((2, n_out, _BPP * DH), lambda i: (0, 0, 0)),
        out_shape=jax.ShapeDtypeStruct((2, n_out, _BPP * DH), F32),
        scratch_shapes=[pltpu.VMEM((SL, nseq * n_pages, _PAGE), F32), pltpu.SemaphoreType.DMA(())],
        compiler_params=_cparams(("arbitrary",)),
        name="compress_t",
    )(kv_t, pos, w1, w2)


_KT = 512


def _nsa_prompt_body(q_ref, qr_ref, kvb_ref, kc_ref, vc_ref, ga_ref, e_ref, gx_ref, o_ref, os_scr, *, nc, ns):
    i = pl.program_id(1)
    QB = Q_BLOCK
    lane = lax.broadcasted_iota(jnp.int32, (QB, LANE), 1)
    lo64 = lane < DH
    zero_b = jnp.zeros((QB, LANE), BF)
    groups = range(G_A)

    def split_heads(ref, g):
        ev, od = [], []
        for c in range(2):
            xc = ref[:, (2 * g + c) * LANE:(2 * g + c + 1) * LANE]
            ev.append(jnp.where(lo64, xc, zero_b))
            od.append(jnp.where(lo64, zero_b, xc))
        return jnp.concatenate(ev, axis=0), jnp.concatenate(od, axis=0)

    tq_l = i * QB + lax.broadcasted_iota(jnp.int32, (nc, QB), 1)
    r_c = lax.broadcasted_iota(jnp.int32, (nc, QB), 0)
    half = nc // 2
    blk_c = jnp.where(r_c < half, 2 * r_c, 2 * (r_c - half) + 1)
    cmask = (blk_c * L_CMP + (L_CMP - 1)) <= tq_l
    cmask_f = cmask.astype(F32)
    tq_s = i * QB + lax.broadcasted_iota(jnp.int32, (ns, QB), 1)
    blk_s = lax.broadcasted_iota(jnp.int32, (ns, QB), 0)
    valid = (blk_s * L_SEL) <= tq_s
    forced = (blk_s == 0) | (blk_s == tq_s // L_SEL)
    n_top = min(N_SEL, ns)

    def select(imp):
        score = jnp.where(valid, imp + jnp.where(forced, FORCE_BONUS, 0.0), NEG)
        rank = jnp.zeros((ns, QB), F32)
        for b2 in range(ns):
            row = score[b2:b2 + 1, :]
            rank = rank + jnp.where(blk_s > b2, jnp.where(row >= score, 1.0, 0.0), jnp.where(row > score, 1.0, 0.0))
        return jnp.where((rank < n_top) & (score > 0.5 * NEG), 1.0, 0.0)

    def score_stage(chains):
        scored = []
        for qs, kmat, _, bias in chains:
            s = _nt(qs, kmat)
            s = (s.reshape(2, QB, s.shape[1]) + bias[None]).reshape(s.shape)
            scored.append((s, jnp.max(s, axis=1, keepdims=True)))
        return scored

    def value_stage(scored, chains):
        outs = []
        for (s, m), (_, _, vmat, _) in zip(scored, chains):
            p = jnp.exp2(s - m)
            outs.append(_nn(p.astype(BF), vmat) / jnp.sum(p, axis=1, keepdims=True))
        return outs

    def masked_attend(chains):
        return value_stage(score_stage(chains), chains)


    qr = [split_heads(qr_ref, g) for g in groups]
    n_tiles = (i * QB + QB + _KT - 1) // _KT

    cmp_scores = [[_nt(kc_ref[g], qs) for qs in split_heads(q_ref, g)] for g in groups]

    WK = WINDOW + QB
    start = pl.multiple_of(jnp.maximum(i * QB - WINDOW, 0), QB)
    diff = (i * QB + lax.broadcasted_iota(jnp.int32, (QB, WK), 0)) - (start + lax.broadcasted_iota(jnp.int32, (QB, WK), 1))
    wb = jnp.where((diff >= 0) & (diff < WINDOW), 0.0, NEG)
    win_chains = []
    for g in groups:
        wkv = kvb_ref[1, g, 0, pl.ds(start, WK), :]
        wvk = kvb_ref[1, g, 1, pl.ds(start, WK), :]
        win_chains += [(qr[g][0], wkv, wvk, wb), (qr[g][1], wvk, wkv, wb)]
    win_scored = score_stage(win_chains)

    imps, o_cmp = [], []
    for g in groups:
        vc = vc_ref[g]
        imp = jnp.zeros((ns, QB), F32)
        oc = {}
        for stack in range(2):
            st = cmp_scores[g][stack]
            for c in range(2):
                s = jnp.where(cmask, st[:, c * QB:(c + 1) * QB], NEG)
                p = jnp.exp2(s - jnp.max(s, axis=0, keepdims=True)) * cmask_f
                p = p / jnp.maximum(jnp.sum(p, axis=0, keepdims=True), 1e-30)
                imp = imp + p[0:half] + p[half:nc]
                oc[(c, stack)] = _tn(p.astype(BF), vc)
        o_cmp.append(oc)
        imps.append(imp)

    ow = value_stage(win_scored, win_chains)
    sel_b = [select(imp).astype(BF) for imp in imps]

    gates = []
    for g in groups:
        ga = ga_ref[:, g * LANE:(g + 1) * LANE]
        hi = ga.astype(BF)
        lo = (ga - hi.astype(F32)).astype(BF)
        gates.append(_nn(jnp.concatenate([hi, lo], axis=1), gx_ref[...]))

    def gate_of(g, br, c):
        j = br * 2 + c
        return gates[g][:, j * LANE:(j + 1) * LANE]

    part = [[gate_of(g, 0, c) * jnp.where(lo64, o_cmp[g][(c, 0)], o_cmp[g][(c, 1)])
             + gate_of(g, 2, c) * jnp.where(lo64, ow[2 * g][c * QB:(c + 1) * QB], ow[2 * g + 1][c * QB:(c + 1) * QB])
             for c in range(2)] for g in groups]


    def slc_variant(nk):
        tq_r = i * QB + lax.broadcasted_iota(jnp.int32, (QB, _KT), 0)
        causal = (nk - _KT + lax.broadcasted_iota(jnp.int32, (QB, _KT), 1)) <= tq_r
        chains = []
        for g in groups:
            bias = _tn(sel_b[g], e_ref[:, 0:nk]) - MASK_BIG
            tail = jnp.where(causal, bias[:, nk - _KT:], -MASK_BIG)
            bias = tail if nk == _KT else jnp.concatenate([bias[:, :nk - _KT], tail], axis=1)
            kv = kvb_ref[0, g, 0, 0:nk, :]
            vk = kvb_ref[0, g, 1, 0:nk, :]
            chains.append((qr[g][0], kv, vk, bias))
            chains.append((qr[g][1], vk, kv, bias))
        for c, o in enumerate(masked_attend(chains)):
            os_scr[c // 2, c % 2] = o

    for k in range(1, kvb_ref.shape[3] // _KT + 1):
        pl.when(n_tiles == k)(functools.partial(slc_variant, k * _KT))

    for g in groups:
        os_e = os_scr[g, 0]
        os_o = os_scr[g, 1]
        for c in range(2):
            rows = slice(c * QB, (c + 1) * QB)
            acc = part[g][c] + gate_of(g, 1, c) * jnp.where(lo64, os_e[rows], os_o[rows])
            o_ref[:, (2 * g + c) * LANE:(2 * g + c + 1) * LANE] = acc.astype(BF)


def _gate_expand():
    gx = np.zeros((LANE, 3 * 2 * LANE), np.float32)
    for br in range(3):
        for c in range(2):
            j = br * 2 + c
            gx[br * HPG + 2 * c, j * LANE:j * LANE + DH] = 1.0
            gx[br * HPG + 2 * c + 1, j * LANE + DH:(j + 1) * LANE] = 1.0
    return jnp.asarray(np.concatenate([gx, gx], axis=0), BF)


def _sel_expand(ns, nkeys):
    e = (np.arange(nkeys)[None, :] // L_SEL) == np.arange(ns)[:, None]
    return jnp.asarray(e * MASK_BIG, BF)


def _nsa_prompt(q, qr, kvb, kc2, vc2, ga, B, T):
    nqb = T // Q_BLOCK
    nc, ns = T // L_CMP, T // L_SEL
    assert T >= WINDOW + Q_BLOCK and T % _KT == 0
    qspec = pl.BlockSpec((Q_BLOCK, 512), lambda b, i: (b * nqb + i, 0))
    cspec = pl.BlockSpec((None, G_A, nc, LANE), lambda b, i: (b, 0, 0, 0))
    return pl.pallas_call(
        functools.partial(_nsa_prompt_body, nc=nc, ns=ns),
        grid=(B, nqb),
        in_specs=[qspec, qspec,
                  pl.BlockSpec((2, G_A, 2, T, LANE), lambda b, i: (0, 0, 0, b, 0)),
                  cspec, cspec,
                  pl.BlockSpec((Q_BLOCK, G_A * LANE), lambda b, i: (b * nqb + i, 0)),
                  pl.BlockSpec((ns, T), lambda b, i: (0, 0)),
                  pl.BlockSpec((2 * LANE, 6 * LANE), lambda b, i: (0, 0))],
        out_specs=qspec,
        out_shape=jax.ShapeDtypeStruct((B * T, 512), BF),
        scratch_shapes=[pltpu.VMEM((G_A, 2, 2 * Q_BLOCK, LANE), F32)],
        compiler_params=_cparams(("arbitrary", "arbitrary")),
        name="nsa_prompt",
    )(q, qr, kvb, kc2, vc2, ga, _sel_expand(ns, T), _gate_expand())


def _ret_body(qkv_ref, gr_ref, s0_ref, dm_ref, dq_ref, wk_ref, gc_ref, gnw_ref, z_ref, sout_ref, s_scr, *, C, sb):
    c = pl.program_id(1)

    @pl.when(c == 0)
    def _():
        s_scr[...] = s0_ref[...]

    units = [(j, h, slice(j * C, (j + 1) * C), slice(h * LANE, (h + 1) * LANE))
             for j in range(sb) for h in range(H_R)]
    qkv = [tuple(qkv_ref[t, rows, sl] for t in range(3)) for _, _, rows, sl in units]
    inner = [_nt(q, k) * dm_ref[h] for (q, k, _), (_, h, _, _) in zip(qkv, units)]
    cross = [_nn(q, s_scr[j, h].astype(BF)) * dq_ref[h] for (q, _, _), (j, h, _, _) in zip(qkv, units)]
    outs = [_nn(a.astype(BF), v) + c for a, c, (_, _, v) in zip(inner, cross, qkv)]
    for (_, k, v), (j, h, _, _) in zip(qkv, units):
        kw = (k.astype(F32) * wk_ref[h]).astype(BF)
        s_new = gc_ref[h, 0:1, :] * s_scr[j, h] + _tn(kw, v)
        s_scr[j, h] = s_new
        sout_ref[j, h] = s_new
    for o, (_, _, rows, sl) in zip(outs, units):
        mu = jnp.mean(o, axis=-1, keepdims=True)
        d = o - mu
        var = jnp.mean(d * d, axis=-1, keepdims=True)
        on = d * lax.rsqrt(var + EPS) * gnw_ref[:, sl]
        z_ref[rows, sl] = (gr_ref[rows, sl].astype(F32) * on).astype(BF)


def _retention(ret3, gr, s0, gnw, nseq, rows_per_seq, C, c_true, sb):
    nC = rows_per_seq // C
    assert sb == 1 or nC == 1
    rows = nseq * rows_per_seq
    dm, dq, wk, gc = _ret_decay(C, c_true)
    full = lambda a: pl.BlockSpec(a.shape, lambda b, c: (0,) * a.ndim)
    return pl.pallas_call(
        functools.partial(_ret_body, C=C, sb=sb),
        grid=(nseq // sb, nC),
        in_specs=[pl.BlockSpec((3, sb * C, H_R * LANE), lambda b, c: (0, b * nC + c, 0)),
                  pl.BlockSpec((sb * C, H_R * LANE), lambda b, c: (b * nC + c, 0)),
                  pl.BlockSpec((sb, H_R, DK_R, DV_R), lambda b, c: (b, 0, 0, 0)),
                  full(dm), full(dq), full(wk), full(gc), full(gnw)],
        out_specs=[pl.BlockSpec((sb * C, H_R * LANE), lambda b, c: (b * nC + c, 0)),
                   pl.BlockSpec((sb, H_R, DK_R, DV_R), lambda b, c: (b, 0, 0, 0))],
        out_shape=[jax.ShapeDtypeStruct((rows, 512), BF),
                   jax.ShapeDtypeStruct((nseq, H_R, DK_R, DV_R), F32)],
        scratch_shapes=[pltpu.VMEM((sb, H_R, DK_R, DV_R), F32)],
        compiler_params=_cparams(("arbitrary", "arbitrary")),
        name="retention",
    )(ret3, gr, s0, dm, dq, wk, gc, gnw)


def _mix_ffn_body(x_ref, oa_ref, zr_ref, gm_ref, gt1_ref, wa_ref, wr_ref, wo_ref,
                  nw_ref, sh_ref, sc_ref, gt_ref, wu_ref, cw_ref, cb_ref, wd_ref, nf_ref, p1_ref, p2_ref,
                  y_ref, a_ref, carry, *, blocks_per_seq, seq_rows):
    i = pl.program_id(0)
    tm = x_ref.shape[0]
    parts = [slice(k * _ROW_PART, (k + 1) * _ROW_PART) for k in range(tm // _ROW_PART)] if tm > _ROW_PART else [slice(0, tm)]

    def mod(ref, r):
        return ref[...] if ref.shape[0] == 1 else ref[r, :]

    ya = [_nn(oa_ref[r, :], wa_ref[...]) for r in parts]
    yr = [_nn(zr_ref[r, :], wr_ref[...]) for r in parts]
    xs = []
    for r, ya_k, yr_k in zip(parts, ya, yr):
        gm = gm_ref[r, :].astype(F32)
        merged = gm[:, 0:D_MODEL] * ya_k + gm[:, D_MODEL:2 * D_MODEL] * yr_k
        xs.append(x_ref[r, :] + mod(gt1_ref, r) * _nn(merged.astype(BF), wo_ref[...]))
    hs = []
    for r, x in zip(parts, xs):
        ms = jnp.mean(x * x, axis=-1, keepdims=True)
        h = (x * lax.rsqrt(ms + EPS)) * nw_ref[...]
        hs.append((h * (1.0 + mod(sc_ref, r)) + mod(sh_ref, r)).astype(BF))
    a_parts = [_nn(h, wu_ref[:, 0:D_FF]) for h in hs]
    b_parts = [_nn(h, wu_ref[:, D_FF:2 * D_FF]) for h in hs]
    a = a_parts[0] if len(parts) == 1 else jnp.concatenate(a_parts, axis=0)
    a_ref[...] = a[tm - a_ref.shape[0]:tm, :]
    rid = lax.broadcasted_iota(jnp.int32, (tm, D_FF), 0)
    if seq_rows is None:
        first = (i % blocks_per_seq) == 0
        prev = jnp.where(first, p1_ref[...], carry[...])
        carry[...] = a[tm - 8:tm, :]
        am1 = jnp.where(rid == 0, prev[7:8, :], pltpu.roll(a, 1, 0))
        am2 = jnp.where(rid == 0, prev[6:7, :], jnp.where(rid == 1, prev[7:8, :], pltpu.roll(a, 2, 0)))
    else:
        s = rid % seq_rows
        am1 = jnp.where(s == 0, p1_ref[...], pltpu.roll(a, 1, 0))
        am2 = jnp.where(s <= 1, p2_ref[...], pltpu.roll(a, 2, 0))
    u = cb_ref[...] + am2 * cw_ref[0:1, :] + am1 * cw_ref[1:2, :] + a * cw_ref[2:3, :]
    ffs = [_nn((_gelu(u[r]) * b_k).astype(BF), wd_ref[...]) for r, b_k in zip(parts, b_parts)]
    for r, x, ff in zip(parts, xs, ffs):
        x2 = x + mod(gt_ref, r) * ff
        ms2 = jnp.mean(x2 * x2, axis=-1, keepdims=True)
        y_ref[r, :] = (x2 * lax.rsqrt(ms2 + EPS)) * nf_ref[...]


def _mix_ffn(x2d, oa, zr, gm, gt1, wa, wr, wo, nw, sh, sc, gt, wu, cw, cb, wd, nf, p1, p2,
             tm, blocks_per_seq, per_row_mods, seq_rows):
    rows = x2d.shape[0]
    a_rows = tm if seq_rows is not None else 8
    if per_row_mods:
        mod_spec = pl.BlockSpec((tm, D_MODEL), lambda i: (i, 0))
    else:
        mod_spec = pl.BlockSpec((None, 1, D_MODEL), lambda i: (i // blocks_per_seq, 0, 0))
    row = lambda w: pl.BlockSpec((tm, w), lambda i: (i, 0))
    full = lambda a, b: pl.BlockSpec((a, b), lambda i: (0, 0), pipeline_mode=pl.Buffered(1))
    vec = lambda w: pl.BlockSpec((1, w), lambda i: (0, 0))
    pspec = pl.BlockSpec(p1.shape, lambda i: (0, 0))
    return pl.pallas_call(
        functools.partial(_mix_ffn_body, blocks_per_seq=blocks_per_seq, seq_rows=seq_rows),
        grid=(rows // tm,),
        in_specs=[row(D_MODEL), row(512), row(512), row(2048), mod_spec,
                  full(512, D_MODEL), full(512, D_MODEL), full(D_MODEL, D_MODEL),
                  vec(D_MODEL), mod_spec, mod_spec, mod_spec,
                  full(D_MODEL, 2 * D_FF), pl.BlockSpec((CONV_W, D_FF), lambda i: (0, 0)), vec(D_FF),
                  full(D_FF, D_MODEL), vec(D_MODEL), pspec, pspec],
        out_specs=[row(D_MODEL), pl.BlockSpec((a_rows, D_FF), lambda i: (i, 0))],
        out_shape=[jax.ShapeDtypeStruct((rows, D_MODEL), F32),
                   jax.ShapeDtypeStruct((rows // tm * a_rows, D_FF), F32)],
        scratch_shapes=[pltpu.VMEM((8, D_FF), F32)],
        compiler_params=_cparams(("arbitrary",)),
        name="mix_ffn",
    )(x2d, oa, zr, gm, gt1, wa, wr, wo, nw, sh, sc, gt, wu, cw, cb, wd, nf, p1, p2)


def _prompt_path(x_prompt, mods_p, W):
    B, T, _ = x_prompt.shape
    rows = B * T
    x2d = x_prompt.reshape(rows, D_MODEL)
    pos = np.arange(T)
    tm = 2 * _ROW_PART if T % (2 * _ROW_PART) == 0 else _ROW_PART
    sh1, sc1, gt1, sh2, sc2, gt2 = [mods_p[:, None, j * D_MODEL:(j + 1) * D_MODEL] for j in range(6)]
    (q, qr, kvc, kvs, kvw, kvb, ret3, gr, gm, ga) = _inproj(
        x2d, W["norm1"], sh1, sc1, W["w_in"], _rope_tab(pos, Q_SCALE), _rope_tab(pos, 1.0), _ret_tab(pos),
        tm, T // tm, False, True)
    nc, n_pages = T // L_CMP, T // _PAGE
    comp = _compress_t(kvc, W["phi_posT"], W["phi_w1p"], W["phi_w2p"])
    comp = comp.reshape(2, G_A, B, n_pages, 2, 2, DH).transpose(0, 2, 1, 5, 3, 4, 6).reshape(2, B, G_A, nc, DH)
    comp2 = jnp.concatenate([comp, comp], axis=-1).astype(BF)
    oa = _nsa_prompt(q, qr, kvb, comp2[0], comp2[1], ga, B, T)
    C = 256 if T % 256 == 0 else T
    zr, ret_new = _retention(ret3, gr, jnp.zeros((B, H_R, DK_R, DV_R), F32), W["gnw"], B, T, C, C, 1)
    zeros8 = jnp.zeros((8, D_FF), F32)
    tf = tm
    y, a_up = _mix_ffn(x2d, oa, zr, gm, gt1, W["w_up_a"], W["w_up_r"], W["w_out"],
                       W["norm2"], sh2, sc2, gt2, W["w_ffn_up"], W["conv_w"], W["conv_b"], W["w_ffn_down"],
                       W["normf"], zeros8, zeros8, tf, T // tf, False, None)
    wsz = min(WINDOW, T)
    rows_major = lambda t: t.reshape(B, 2, G_A, DH, t.shape[-1]).transpose(0, 4, 1, 2, 3)[None]
    outs = dict(
        y=y.reshape(B, T, D_MODEL),
        cmp=rows_major(kvc), slc=rows_major(kvs), win=rows_major(kvw[:, :, T - wsz:]),
        ret=ret_new[None],
        conv=a_up.reshape(B, T // tf, 8, D_FF)[None, :, T // tf - 1, 8 - (CONV_W - 1):],
    )
    return outs


_QC = 32


def _cmp_paged_body(pt_ref, cache_ref, q_ref, pos_ref, w1_ref, w2_ref, ocmp_ref, topi_ref, xbuf, sem,
                    *, n_pages, n_pick, n_q):
    b = pl.program_id(0)
    nb = pl.num_programs(0)
    slot = b % 2

    def copy(bb, sl, p):
        return pltpu.make_async_copy(cache_ref.at[pt_ref[bb, p]], xbuf.at[sl, :, p, :], sem.at[sl])

    def issue(bb, sl):
        for p in range(n_pages):
            copy(bb, sl, p).start()

    @pl.when(b == 0)
    def _():
        issue(b, slot)

    @pl.when(b + 1 < nb)
    def _():
        issue(b + 1, 1 - slot)

    for p in range(n_pages):
        copy(b, slot, p).wait()

    comp = _compress_slab_rows(lambda r: xbuf[slot, r], n_pages, pos_ref, w1_ref, w2_ref)

    def lane_groups(x, op):
        r = x
        for j in range(1, _BPP):
            r = op(r, pltpu.roll(x, j * _QC, 1))
        return r

    grp = range(G_A)
    sts = [_nn(comp[0][g * n_pages:(g + 1) * n_pages].astype(BF), q_ref[g]) for g in grp]
    ps = []
    for g in grp:
        m = lane_groups(jnp.max(sts[g], axis=0, keepdims=True), jnp.maximum)
        p = jnp.exp2(sts[g] - m)
        ps.append(p / lane_groups(jnp.sum(p, axis=0, keepdims=True), jnp.add))
    for g in grp:
        r_full = _tn(ps[g].astype(BF), comp[1][g * n_pages:(g + 1) * n_pages].astype(BF))
        o = r_full[0:_QC, 0:DH]
        for j in range(1, _BPP):
            o = o + r_full[j * _QC:(j + 1) * _QC, j * DH:(j + 1) * DH]
        ocmp_ref[g] = o
    imp = None
    for g in grp:
        pair = ps[g] + pltpu.roll(ps[g], LANE - _QC, 1)
        t = pair
        for r in range(1, HPG):
            t = t + pltpu.roll(pair, LANE - r * n_q, 1)
        if g:
            t = pltpu.roll(t, g * n_q, 1)
        lane_g = lax.broadcasted_iota(jnp.int32, t.shape, 1) % (2 * _QC)
        t = jnp.where((lane_g >= g * n_q) & (lane_g < (g + 1) * n_q), t, 0.0)
        imp = t if imp is None else imp + t
    sc = jnp.concatenate([imp, pltpu.roll(imp, 2 * _QC, 1)], axis=0)
    row = lax.broadcasted_iota(jnp.int32, (2 * n_pages, LANE), 0)
    blk = jnp.where(row < n_pages, 2 * row, 2 * (row - n_pages) + 1)
    score = sc + jnp.where(blk == 0, FORCE_BONUS, 0.0)
    topi_ref[...] = jnp.zeros((8, LANE), jnp.int32)
    for k in range(n_pick):
        mx = jnp.max(score, axis=0, keepdims=True)
        idx = jnp.min(jnp.where(score == mx, blk, 2 * n_pages), axis=0, keepdims=True)
        topi_ref[k:k + 1, :] = idx
        score = jnp.where(blk == idx, -jnp.inf, score)


def _cmp_paged(page_table, cache_t, q_bd, pos, w1, w2, n_q):
    DB, n_pages = page_table.shape
    SL = cache_t.shape[1]
    page_rows = cache_t.shape[2]
    kern = functools.partial(_cmp_paged_body, n_pages=n_pages, n_pick=N_SEL - 1, n_q=n_q)
    one = dict(pipeline_mode=pl.Buffered(1))
    return pl.pallas_call(
        kern,
        grid_spec=pltpu.PrefetchScalarGridSpec(
            num_scalar_prefetch=1, grid=(DB,),
            in_specs=[pl.BlockSpec(memory_space=pl.ANY),
                      pl.BlockSpec((None, G_A, _BPP * DH, LANE), lambda b, pt: (b, 0, 0, 0)),
                      pl.BlockSpec(pos.shape, lambda b, pt: (0, 0, 0), **one),
                      pl.BlockSpec(w1.shape, lambda b, pt: (0, 0, 0, 0), **one),
                      pl.BlockSpec((2, _BPP * DH, _BPP * DH), lambda b, pt: (0, 0, 0), **one)],
            out_specs=[pl.BlockSpec((None, G_A, _QC, DH), lambda b, pt: (b, 0, 0, 0)),
                       pl.BlockSpec((None, 8, LANE), lambda b, pt: (b, 0, 0))],
            scratch_shapes=[pltpu.VMEM((2, SL, n_pages, page_rows), F32),
                            pltpu.SemaphoreType.DMA((2,))]),
        out_shape=[jax.ShapeDtypeStruct((DB, G_A, _QC, DH), F32),
                   jax.ShapeDtypeStruct((DB, 8, LANE), jnp.int32)],
        compiler_params=_cparams(("arbitrary",)),
        name="cmp_paged",
    )(page_table, cache_t, q_bd, pos, w1, w2)


def _slc_win_paged_body(pt_ref, ti_ref, cslc_ref, win_ref, q_ref, tiv_ref, ns_ref, nw_ref, ex_ref,
                        oslc_ref, owin_ref, kbuf, sem, *, n_q, n_pick, page_rows):
    b = pl.program_id(0)
    nb = pl.num_programs(0)
    slot = b % 2
    n_slab = n_q * n_pick
    bpp = page_rows // L_SEL
    wb = win_ref.shape[1]

    def copies(bb, sl):
        cps = []
        for g in range(G_A):
            for j in range(n_slab):
                page = pt_ref[bb, ti_ref[bb, g * n_slab + j] // bpp]
                for kv in range(2):
                    cps.append(pltpu.make_async_copy(
                        cslc_ref.at[page, pl.ds((kv * G_A + g) * DH, DH), :],
                        kbuf.at[sl, g, kv, :, pl.ds(j * page_rows, page_rows)], sem.at[sl]))
        return cps

    @pl.when(b == 0)
    def _():
        for cp in copies(b, slot):
            cp.start()

    @pl.when(b + 1 < nb)
    def _():
        for cp in copies(b + 1, 1 - slot):
            cp.start()

    for cp in copies(b, slot):
        cp.wait()

    nq_rows = q_ref.shape[1]
    nk = n_slab * page_rows
    row_q = lax.broadcasted_iota(jnp.int32, (nq_rows, nk), 0) % n_q
    col = lax.broadcasted_iota(jnp.int32, (nq_rows, nk), 1)
    own = row_q == col // (n_pick * page_rows)
    half = ((col % page_rows) // L_SEL).astype(F32)
    nnew = ns_ref.shape[2]
    new_ok = (lax.broadcasted_iota(jnp.int32, (nq_rows, nnew), 1)
              <= lax.broadcasted_iota(jnp.int32, (nq_rows, nnew), 0) % n_q)
    dwin = (wb + lax.broadcasted_iota(jnp.int32, (nq_rows, wb), 0) % n_q
            - lax.broadcasted_iota(jnp.int32, (nq_rows, wb), 1))
    win_ok = (dwin >= 0) & (dwin < WINDOW)

    chains = []
    for g in range(G_A):
        q = q_ref[g]
        par = (tiv_ref[g] % bpp).astype(F32).astype(BF)
        want = _nn(par, ex_ref[...])[0:1, :]
        chains.append((q, kbuf[slot, g, 0].astype(BF), kbuf[slot, g, 1].astype(BF), own & (half == want),
                       ns_ref[g, 0].astype(BF), ns_ref[g, 1].astype(BF), oslc_ref, g))
        kw = win_ref[pl.ds((0 * G_A + g) * DH, DH), :].astype(BF)
        vw = win_ref[pl.ds((1 * G_A + g) * DH, DH), :].astype(BF)
        chains.append((q, kw, vw, win_ok, nw_ref[g, 0].astype(BF), nw_ref[g, 1].astype(BF), owin_ref, g))
    scores = [(jnp.where(ok_old, _nn(q, kt_old), NEG), jnp.where(new_ok, _nt(q, k_new), NEG))
              for q, kt_old, _, ok_old, k_new, _, _, _ in chains]
    probs = []
    for s_o, s_n in scores:
        m = jnp.maximum(jnp.max(s_o, axis=1, keepdims=True), jnp.max(s_n, axis=1, keepdims=True))
        p_o = jnp.exp2(s_o - m)
        p_n = jnp.exp2(s_n - m)
        probs.append((p_o, p_n, jnp.sum(p_o, axis=1, keepdims=True) + jnp.sum(p_n, axis=1, keepdims=True)))
    for (p_o, p_n, den), (_, _, vt_old, _, _, v_new, out_ref, g) in zip(probs, chains):
        out_ref[g] = (_nt(p_o.astype(BF), vt_old) + _nn(p_n.astype(BF), v_new)) / den


def _slc_win_paged(page_table, topi_flat, topi_vec, cache_t, win_t, q_rot, new_slc, new_win, n_q):
    DB = page_table.shape[0]
    n_pick = N_SEL - 1
    page_rows = cache_t.shape[2]
    wb = win_t.shape[2]
    nq_rows = q_rot.shape[2]
    n_slab = n_q * n_pick
    ex = (np.arange(n_slab * page_rows)[None, :] // page_rows) == np.arange(LANE)[:, None]
    kern = functools.partial(_slc_win_paged_body, n_q=n_q, n_pick=n_pick, page_rows=page_rows)
    bspec = lambda shp: pl.BlockSpec((None,) + shp, lambda b, pt, ti: (b,) + (0,) * len(shp))
    return pl.pallas_call(
        kern,
        grid_spec=pltpu.PrefetchScalarGridSpec(
            num_scalar_prefetch=2, grid=(DB,),
            in_specs=[pl.BlockSpec(memory_space=pl.ANY), bspec((2 * G_A * DH, wb)),
                      bspec((G_A, nq_rows, DH)), bspec((G_A, 16, LANE)),
                      bspec((G_A, 2, 16, DH)), bspec((G_A, 2, 16, DH)),
                      pl.BlockSpec((LANE, n_slab * page_rows), lambda b, pt, ti: (0, 0))],
            out_specs=[bspec((G_A, nq_rows, DH)), bspec((G_A, nq_rows, DH))],
            scratch_shapes=[pltpu.VMEM((2, G_A, 2, DH, n_slab * page_rows), F32),
                            pltpu.SemaphoreType.DMA((2,))]),
        out_shape=[jax.ShapeDtypeStruct((DB, G_A, nq_rows, DH), F32)] * 2,
        compiler_params=_cparams(("arbitrary",)),
        name="slc_win_paged",
    )(page_table, topi_flat, cache_t, win_t, q_rot, topi_vec, new_slc, new_win, jnp.asarray(ex, BF))


def _gate_sample_body(oc_ref, os_ref, ow_ref, ga_ref, o_ref):
    rows = o_ref.shape[0]
    lo64 = lax.broadcasted_iota(jnp.int32, (rows, LANE), 1) < DH
    for c4 in range(H_A // 2):
        g, c = c4 // 2, c4 % 2
        sl = slice(c4 * LANE, (c4 + 1) * LANE)
        acc = jnp.zeros((rows, LANE), F32)
        for br, ref in enumerate((oc_ref, os_ref, ow_ref)):
            col = g * LANE + br * HPG + 2 * c
            gate = jnp.where(lo64, ga_ref[:, col:col + 1], ga_ref[:, col + 1:col + 2])
            acc = acc + gate * ref[:, sl]
        o_ref[:, sl] = acc.astype(BF)


def _gate_sample(oc, osl, ow, ga):
    rows = oc.shape[0]
    full = lambda w: pl.BlockSpec((rows, w), lambda i: (0, 0))
    return pl.pallas_call(
        _gate_sample_body, grid=(1,),
        in_specs=[full(512), full(512), full(512), full(256)],
        out_specs=full(512),
        out_shape=jax.ShapeDtypeStruct((rows, 512), BF),
        compiler_params=_cparams(("arbitrary",)),
        name="gate_sample",
    )(oc, osl, ow, ga)


def _sample_path(x_sample, mods_s, W, cache_cmp, cache_slc, state_win, state_ret, state_conv, page_table):
    DB, S, _ = x_sample.shape
    rows = DB * S
    page_rows = cache_cmp.shape[1]
    P = page_table.shape[1] * page_rows
    wb = state_win.shape[1]
    assert P % L_SEL == 0 and S < L_CMP and S <= 8 and wb == WINDOW and page_rows % L_SEL == 0
    assert P // L_SEL >= N_SEL and CONV_W == 3 and S >= CONV_W - 1
    pos = P + np.arange(S)
    pos_rows = np.tile(pos, DB)
    x2d = x_sample.reshape(rows, D_MODEL)
    modr = jnp.repeat(mods_s, S, axis=0)
    sh1, sc1, gt1, sh2, sc2, gt2 = [modr[:, j * D_MODEL:(j + 1) * D_MODEL] for j in range(6)]
    (q, qr, kvc, kvs, kvw, _, ret3, gr, gm, ga) = _inproj(
        x2d, W["norm1"], sh1, sc1, W["w_in"], _rope_tab(pos_rows, Q_SCALE), _rope_tab(pos_rows, 1.0),
        _ret_tab(pos_rows), rows, 1, True, False)

    def to_heads(t):
        return t.reshape(DB, S, G_A, HPG, DH).transpose(0, 2, 3, 1, 4).reshape(DB, G_A, HPG * S, DH)

    def from_heads(t):
        return t.reshape(DB, G_A, HPG, S, DH).transpose(0, 3, 1, 2, 4).reshape(rows, H_A * DH)

    assert page_rows == _BPP * L_CMP and HPG * S <= _QC and S * (N_SEL - 1) <= LANE
    slab = lambda t: t.transpose(0, 2, 3, 4, 1).reshape(t.shape[0], 2 * G_A * DH, t.shape[1])
    qt = jnp.pad(to_heads(q).transpose(0, 1, 3, 2), ((0, 0), (0, 0), (0, 0), (0, _QC - HPG * S)))
    q_bd = _blockdiag(qt)
    o_cmp, topi = _cmp_paged(page_table, slab(cache_cmp), q_bd, W["phi_posT"], W["phi_w1p"], W["phi_w2p"], S)
    n_pick = N_SEL - 1
    topi = topi[:, :n_pick, :G_A * S].transpose(0, 2, 1).reshape(DB, G_A, S * n_pick)
    topi_vec = jnp.broadcast_to(jnp.pad(topi, ((0, 0), (0, 0), (0, LANE - S * n_pick)))[:, :, None, :],
                                (DB, G_A, 16, LANE))

    def new_rows(t):
        t = t.reshape(DB, S, 2, G_A, DH).transpose(0, 3, 2, 1, 4)
        return jnp.pad(t, ((0, 0), (0, 0), (0, 0), (0, 16 - S), (0, 0)))

    o_slc, o_win = _slc_win_paged(page_table, topi.reshape(DB, G_A * S * n_pick), topi_vec, slab(cache_slc),
                                  slab(state_win), to_heads(qr), new_rows(kvs), new_rows(kvw), S)
    oa = _gate_sample(from_heads(o_cmp[:, :, :HPG * S]), from_heads(o_slc), from_heads(o_win), ga)

    RP = 16
    padr = lambda t: jnp.pad(t.reshape(t.shape[:-2] + (DB, S, 512)),
                             ((0, 0),) * (t.ndim - 1) + ((0, RP - S), (0, 0))).reshape(t.shape[:-2] + (DB * RP, 512))
    zr, ret_new = _retention(padr(ret3), padr(gr), state_ret, W["gnw"], DB, RP, RP, S, 8 if DB % 8 == 0 else 1)
    zr = zr.reshape(DB, RP, 512)[:, :S].reshape(rows, 512)
    zs = lambda n: jnp.zeros((DB, n, D_FF), F32)
    p1 = jnp.concatenate([state_conv[:, 1:2], zs(S - 1)], axis=1).reshape(rows, D_FF)
    p2 = jnp.concatenate([state_conv[:, 0:2], zs(S - 2)], axis=1).reshape(rows, D_FF)
    y, a_up = _mix_ffn(x2d, oa, zr, gm, gt1, W["w_up_a"], W["w_up_r"], W["w_out"],
                       W["norm2"], sh2, sc2, gt2, W["w_ffn_up"], W["conv_w"], W["conv_b"], W["w_ffn_down"],
                       W["normf"], p1, p2, rows, 1, True, S)
    shp = (1, DB, S, 2, G_A, DH)
    return dict(
        y=y.reshape(DB, S, D_MODEL),
        cmp=kvc.reshape(shp), slc=kvs.reshape(shp),
        win=jnp.concatenate([state_win[:, S:], kvw.reshape(DB, S, 2, G_A, DH)], axis=1)[None],
        ret=ret_new[None],
        conv=a_up.reshape(DB, S, D_FF)[None, :, S - (CONV_W - 1):],
    )


def kernel(x_prompt, x_sample, cache_cmp_kv, cache_slc_kv, state_win_kv, state_ret, state_conv, page_table,
           c_prompt, c_sample, norm1_w, norm2_w, w_ada, b_ada, w_in, phi_pos_k, phi_k1, phi_k2, phi_pos_v,
           phi_v1, phi_v2, w_up_a, ret_gn_w, w_up_r, w_out, w_ffn_up, ffn_conv_w, ffn_conv_b, w_ffn_down, normf_w):
    B = x_prompt.shape[0]
    l = 0
    W = dict(
        norm1=norm1_w[l].reshape(1, D_MODEL), norm2=norm2_w[l].reshape(1, D_MODEL), normf=normf_w.reshape(1, D_MODEL),
        w_in=_pack_w_in(w_in[l]),
        w_up_a=w_up_a[l].astype(BF), w_up_r=w_up_r[l].astype(BF), w_out=w_out[l].astype(BF),
        gnw=ret_gn_w[l].reshape(1, H_R * DV_R),
        w_ffn_up=w_ffn_up[l].astype(BF), conv_w=ffn_conv_w[l], conv_b=ffn_conv_b[l].reshape(1, D_FF),
        w_ffn_down=w_ffn_down[l].astype(BF),
    )
    W["phi_posT"], W["phi_w1p"], W["phi_w2p"] = _pack_phi_paged(
        phi_pos_k[l], phi_k1[l], phi_k2[l], phi_pos_v[l], phi_v1[l], phi_v2[l])
    mods = _mods(jnp.concatenate([c_prompt, c_sample], axis=0), w_ada[l], b_ada[l])
    p = _prompt_path(x_prompt, mods[:B], W)
    s = _sample_path(x_sample, mods[B:], W, cache_cmp_kv[l], cache_slc_kv[l], state_win_kv[l], state_ret[l],
                     state_conv[l], page_table)
    return (p["y"], s["y"], p["cmp"], s["cmp"], p["slc"], s["slc"], p["win"], s["win"],
            p["ret"], s["ret"], p["conv"], s["conv"])
```

```python
import functools

import numpy as np
import jax
import jax.numpy as jnp
from jax import lax
from jax.experimental import pallas as pl
from jax.experimental.pallas import tpu as pltpu

BF = jnp.bfloat16
F32 = jnp.float32

D_MODEL = 1024
H_A, G_A, DH = 8, 2, 64
HPG = H_A // G_A
ROPE_DIM = DH // 4
ROPE_THETA = 500000.0
L_CMP, L_SEL, N_SEL = 32, 64, 8
WINDOW = 512
Q_BLOCK = 128
FORCE_BONUS = 1e4
H_R, DK_R, DV_R = 4, 128, 128
RET_THETA = 10000.0
D_FF = 2816
CONV_W = 3
EPS = 1e-6
NEG = -1e30
MASK_BIG = 2.0 ** 100
Q_SCALE = DH ** -0.5 * 1.4426950408889634
LANE = 128
_ROW_PART = 256
VMEM_LIMIT = 56 * 1024 * 1024


def _cparams(sem):
    return pltpu.CompilerParams(dimension_semantics=sem, vmem_limit_bytes=VMEM_LIMIT)


def _sigmoid(x):
    return 1.0 / (1.0 + jnp.exp(-x))


def _gelu(x):
    return 0.5 * x * (1.0 + jnp.tanh(0.7978845608028654 * (x + 0.044715 * (x * x * x))))


def _nt(a, b):
    return lax.dot_general(a, b, (((1,), (1,)), ((), ())), preferred_element_type=F32)


def _tn(a, b):
    return lax.dot_general(a, b, (((0,), (0,)), ((), ())), preferred_element_type=F32)


def _nn(a, b):
    return jnp.dot(a, b, preferred_element_type=F32)


def _rope_tab(pos, scale):
    half = ROPE_DIM // 2
    inv = ROPE_THETA ** (-np.arange(half, dtype=np.float64) * (2.0 / ROPE_DIM))
    ang = pos.astype(np.float64)[:, None] * inv
    cos, sin = np.cos(ang), np.sin(ang)
    n = pos.shape[0]
    c = np.ones((n, DH)); s_lo = np.zeros((n, DH)); s_hi = np.zeros((n, DH))
    c[:, :half] = cos; c[:, half:ROPE_DIM] = cos
    s_lo[:, half:ROPE_DIM] = sin
    s_hi[:, :half] = -sin
    tab = np.concatenate([np.tile(t, (1, 2)) for t in (c, s_lo, s_hi)], axis=1) * scale
    return jnp.asarray(tab, F32)


def _ret_tab(pos):
    half = DK_R // 2
    inv = RET_THETA ** (-np.arange(half, dtype=np.float64) * (2.0 / DK_R))
    ang = pos.astype(np.float64)[:, None] * inv
    cos, sin = np.cos(ang), np.sin(ang)
    c = np.concatenate([cos, cos], axis=1)
    s = np.concatenate([-sin, sin], axis=1)
    ks = DK_R ** -0.5
    return jnp.asarray(np.concatenate([c, s, c * ks, s * ks], axis=1), F32)


def _ret_decay(C, c_true):
    h = np.arange(H_R, dtype=np.float64)
    log_g = np.log1p(-np.exp2(-5.0 - h))
    i = np.arange(C, dtype=np.float64)
    diff = i[:, None] - i[None, :]
    dm = np.where(diff >= 0, np.exp(log_g[:, None, None] * np.maximum(diff, 0.0)), 0.0)
    dq = np.exp(log_g[:, None] * (i + 1.0))[:, :, None] * np.ones((1, 1, LANE))
    wk = np.exp(log_g[:, None] * (c_true - 1.0 - i))[:, :, None] * np.ones((1, 1, LANE))
    wk = np.where(i[None, :, None] < c_true, wk, 0.0)
    gc = np.exp(log_g * c_true)[:, None, None] * np.ones((1, 8, LANE))
    return (jnp.asarray(dm, F32), jnp.asarray(dq, F32), jnp.asarray(wk, F32), jnp.asarray(gc, F32))


def _mods_body(c_ref, w_ref, b_ref, o_ref):
    c = c_ref[...]
    s = c * _sigmoid(c)
    o_ref[...] = _nn(s.astype(BF), w_ref[...].astype(BF)) + b_ref[...]


def _mods(c_all, w_ada, b_ada):
    n = c_all.shape[0]
    nout = w_ada.shape[1]
    tn = 1024
    return pl.pallas_call(
        _mods_body,
        grid=(nout // tn,),
        in_specs=[pl.BlockSpec((n, D_MODEL), lambda j: (0, 0)),
                  pl.BlockSpec((D_MODEL, tn), lambda j: (0, j)),
                  pl.BlockSpec((1, tn), lambda j: (0, j))],
        out_specs=pl.BlockSpec((n, tn), lambda j: (0, j)),
        out_shape=jax.ShapeDtypeStruct((n, nout), F32),
        compiler_params=_cparams(("arbitrary",)),
        name="mods",
    )(c_all, w_ada, b_ada.reshape(1, nout))


_C_Q, _C_KV, _C_QR, _C_KR, _C_VR, _C_GR, _C_GM, _C_GA, _C_END = (
    0, 512, 1280, 1792, 2304, 2816, 3328, 5376, 5632)


def _pack_w_in(w_in):
    wt = w_in.T
    o = np.cumsum((0, 512, 768, 24, 512, 512, 512, 512, 2048))
    q, kv, ga, qr, kr, vr, gr, gm = [wt[o[i]:o[i + 1]] for i in range(8)]
    ga = ga.reshape(3, G_A, HPG, D_MODEL).transpose(1, 0, 2, 3).reshape(G_A, 3 * HPG, D_MODEL)
    ga = jnp.pad(ga, ((0, 0), (0, LANE - 3 * HPG), (0, 0))).reshape(G_A * LANE, D_MODEL)
    return jnp.concatenate([q, kv, qr, kr, vr, gr, gm, ga], axis=0).astype(BF)


def _inproj_body(x_ref, nw_ref, sh_ref, sc_ref, w_ref, rq_ref, rk_ref, rr_ref,
                 q_ref, qr_ref, kvc_ref, kvs_ref, kvw_ref, kvb_ref, ret_ref, gr_ref, gm_ref, ga_ref, *, kv_t):
    tm = x_ref.shape[0]
    np_ = tm // _ROW_PART if tm > _ROW_PART else 1
    pr = tm // np_
    parts = [slice(k * pr, (k + 1) * pr) for k in range(np_)]
    lo64 = lax.broadcasted_iota(jnp.int32, (pr, LANE), 1) < DH

    def put_kv(out_ref, r, k, v):
        if kv_t:
            out_ref[0:LANE, r] = k.T
            out_ref[LANE:2 * LANE, r] = v.T
        else:
            out_ref[r, 0:LANE] = k
            out_ref[r, LANE:2 * LANE] = v

    def mod(ref, r):
        return ref[...] if ref.shape[0] == 1 else ref[r, :]

    hbs = []
    for r in parts:
        x = x_ref[r, :]
        ms = jnp.mean(x * x, axis=-1, keepdims=True)
        h = (x * lax.rsqrt(ms + EPS)) * nw_ref[...]
        hbs.append((h * (1.0 + mod(sc_ref, r)) + mod(sh_ref, r)).astype(BF))

    def mm(k, lo, hi):
        return _nt(hbs[k], w_ref[lo:hi, :])

    def rope(xc, tab_ref, r):
        return (xc * tab_ref[r, 0:LANE] + pltpu.roll(xc, 8, 1) * tab_ref[r, LANE:2 * LANE]
                + pltpu.roll(xc, LANE - 8, 1) * tab_ref[r, 2 * LANE:3 * LANE])

    gm_half = (_C_GA - _C_GM) // 2
    for k, r in enumerate(parts):
        gm_ref[r, 0:gm_half] = _sigmoid(mm(k, _C_GM, _C_GM + gm_half)).astype(BF)
    for k, r in enumerate(parts):
        g = mm(k, _C_GR, _C_GM)
        gr_ref[r, :] = (g * _sigmoid(g)).astype(BF)
    for k, r in enumerate(parts):
        gm_ref[r, gm_half:2 * gm_half] = _sigmoid(mm(k, _C_GM + gm_half, _C_GA)).astype(BF)

    for k, r in enumerate(parts):
        qa = mm(k, _C_Q, _C_KV)
        q_ref[r, :] = (qa * Q_SCALE).astype(BF)
        for c in range(4):
            qr_ref[r, c * LANE:(c + 1) * LANE] = rope(qa[:, c * LANE:(c + 1) * LANE], rq_ref, r).astype(BF)

    for k, r in enumerate(parts):
        kv = mm(k, _C_KV, _C_QR)
        put_kv(kvc_ref, r, kv[:, 0:LANE], kv[:, LANE:2 * LANE])
        for kind, out_ref in ((0, kvs_ref), (1, kvw_ref)):
            base = 256 + kind * 256
            kk = rope(kv[:, base:base + LANE], rk_ref, r)
            v = kv[:, base + LANE:base + 2 * LANE]
            put_kv(out_ref, r, kk, v)
            kr_ = pltpu.roll(kk, DH, 1)
            vr_ = pltpu.roll(v, DH, 1)
            kvb_ref[kind, 0, 0, r, :] = jnp.where(lo64, kk, vr_).astype(BF)
            kvb_ref[kind, 0, 1, r, :] = jnp.where(lo64, v, kr_).astype(BF)
            kvb_ref[kind, 1, 0, r, :] = jnp.where(lo64, kr_, v).astype(BF)
            kvb_ref[kind, 1, 1, r, :] = jnp.where(lo64, vr_, kk).astype(BF)

    for k, r in enumerate(parts):
        qr = mm(k, _C_QR, _C_KR)
        kr = mm(k, _C_KR, _C_VR)
        for hh in range(H_R):
            sl = slice(hh * LANE, (hh + 1) * LANE)
            xq = qr[:, sl]
            ret_ref[0, r, sl] = (xq * rr_ref[r, 0:LANE]
                                 + pltpu.roll(xq, DK_R // 2, 1) * rr_ref[r, LANE:2 * LANE]).astype(BF)
            xk = kr[:, sl]
            ret_ref[1, r, sl] = (xk * rr_ref[r, 2 * LANE:3 * LANE]
                                 + pltpu.roll(xk, DK_R // 2, 1) * rr_ref[r, 3 * LANE:4 * LANE]).astype(BF)
    for k, r in enumerate(parts):
        ga_ref[r, :] = _sigmoid(mm(k, _C_GA, _C_END))
    for k, r in enumerate(parts):
        ret_ref[2, r, :] = mm(k, _C_VR, _C_GR).astype(BF)


def _inproj(x2d, nw, sh, sc, w_pack, rq, rk, rr, tm, tab_blocks, per_row_mods, kv_t):
    rows = x2d.shape[0]
    nb = rows // tm
    if kv_t:
        kv_shape = jax.ShapeDtypeStruct((nb // tab_blocks, 256, tab_blocks * tm), F32)
        kv_spec = pl.BlockSpec((None, 256, tm), lambda i: (i // tab_blocks, 0, i % tab_blocks))
    else:
        kv_shape = jax.ShapeDtypeStruct((rows, 256), F32)
        kv_spec = pl.BlockSpec((tm, 256), lambda i: (i, 0))
    if per_row_mods:
        mod_spec = pl.BlockSpec((tm, D_MODEL), lambda i: (i, 0))
    else:
        mod_spec = pl.BlockSpec((None, 1, D_MODEL), lambda i: (i // tab_blocks, 0, 0))
    tab = lambda w: pl.BlockSpec((tm, w), lambda i: (i % tab_blocks, 0))
    row = lambda w: pl.BlockSpec((tm, w), lambda i: (i, 0))
    out_shapes = [
        jax.ShapeDtypeStruct((rows, 512), BF),
        jax.ShapeDtypeStruct((rows, 512), BF),
        kv_shape,
        kv_shape,
        kv_shape,
        jax.ShapeDtypeStruct((2, G_A, 2, rows, LANE), BF),
        jax.ShapeDtypeStruct((3, rows, 512), BF),
        jax.ShapeDtypeStruct((rows, 512), BF),
        jax.ShapeDtypeStruct((rows, 2048), BF),
        jax.ShapeDtypeStruct((rows, 256), F32),
    ]
    out_specs = [row(512), row(512), kv_spec, kv_spec, kv_spec,
                 pl.BlockSpec((2, G_A, 2, tm, LANE), lambda i: (0, 0, 0, i, 0)),
                 pl.BlockSpec((3, tm, 512), lambda i: (0, i, 0)),
                 row(512), row(2048), row(256)]
    return pl.pallas_call(
        functools.partial(_inproj_body, kv_t=kv_t),
        grid=(nb,),
        in_specs=[row(D_MODEL),
                  pl.BlockSpec((1, D_MODEL), lambda i: (0, 0)),
                  mod_spec, mod_spec,
                  pl.BlockSpec((_C_END, D_MODEL), lambda i: (0, 0), pipeline_mode=pl.Buffered(1)),
                  tab(384), tab(384), tab(512)],
        out_specs=out_specs,
        out_shape=out_shapes,
        compiler_params=_cparams(("arbitrary",)),
        name="inproj",
    )(x2d, nw, sh, sc, w_pack, rq, rk, rr)


_BPP = 4
_PAGE = _BPP * L_CMP


def _blockdiag(w):
    z = jnp.zeros_like(w)
    return jnp.concatenate(
        [jnp.concatenate([w if j == i else z for j in range(_BPP)], axis=-1) for i in range(_BPP)], axis=-2)


def _pack_phi_paged(phi_pos_k, phi_k1, phi_k2, phi_pos_v, phi_v1, phi_v2):
    w1, w2, pos = [], [], []
    for p_, a, b_ in ((phi_pos_k, phi_k1, phi_k2), (phi_pos_v, phi_v1, phi_v2)):
        w1.append(_blockdiag(a.astype(BF).reshape(L_CMP, DH, DH).transpose(1, 0, 2)))
        w2.append(_blockdiag(b_.astype(BF)))
        pos.append(jnp.tile(p_.T, (1, _BPP)))
    pos = jnp.stack(pos).reshape(2, DH // 2, 2 * _PAGE)
    w1 = jnp.stack(w1).reshape(2, DH // 2, 2 * _PAGE, _BPP * DH)
    return pos, w1, jnp.stack(w2)


def _compress_slab_rows(load, n_rows, pos_ref, w1_ref, w2_ref):
    acc = [jnp.zeros((G_A * n_rows, _BPP * DH), F32) for _ in range(2)]
    for dp in range(DH // 2):
        for kv in range(2):
            x = jnp.concatenate(
                [jnp.concatenate([load((kv * G_A + g) * DH + 2 * dp + j) for j in range(2)], axis=1)
                 for g in range(G_A)], axis=0)
            acc[kv] = acc[kv] + _nn((x + pos_ref[kv, dp:dp + 1, :]).astype(BF), w1_ref[kv, dp])
    return [_nn(_gelu(acc[kv]).astype(BF), w2_ref[kv]) for kv in range(2)]


def _compress_t_body(src_ref, pos_ref, w1_ref, w2_ref, o_ref, xbuf, sem, *, nseq, n_pages):
    def copy(t):
        b, p = t // n_pages, t % n_pages
        return pltpu.make_async_copy(src_ref.at[b, :, pl.ds(pl.multiple_of(p * _PAGE, _PAGE), _PAGE)],
                                     xbuf.at[:, t, :], sem)

    def start(t, c):
        copy(t).start()
        return c

    def wait(t, c):
        copy(t).wait()
        return c
    lax.fori_loop(0, nseq * n_pages, start, 0)
    lax.fori_loop(0, nseq * n_pages, wait, 0)
    kc, vc = _compress_slab_rows(lambda r: xbuf[r], nseq * n_pages, pos_ref, w1_ref, w2_ref)
    o_ref[0] = kc
    o_ref[1] = vc


def _compress_t(kv_t, pos, w1, w2):
    nseq, SL, T = kv_t.shape
    n_pages = T // _PAGE
    one = lambda a: pl.BlockSpec(a.shape, lambda i: (0,) * a.ndim, pipeline_mode=pl.Buffered(1))
    n_out = G_A * nseq * n_pages
    return pl.pallas_call(
        functools.partial(_compress_t_body, nseq=nseq, n_pages=n_pages),
        grid=(1,),
        in_specs=[pl.BlockSpec(memory_space=pl.ANY), one(pos), one(w1), one(w2)],
        out_specs=pl.BlockSpec((2, n_out, _BPP * DH), lambda i: (0, 0, 0)),
        out_shape=jax.ShapeDtypeStruct((2, n_out, _BPP * DH), F32),
        scratch_shapes=[pltpu.VMEM((SL, nseq * n_pages, _PAGE), F32), pltpu.SemaphoreType.DMA(())],
        compiler_params=_cparams(("arbitrary",)),
        name="compress_t",
    )(kv_t, pos, w1, w2)


_KT = 256


def _nsa_prompt_body(q_ref, qr_ref, kvb_ref, kc_ref, vc_ref, ga_ref, e_ref, gx_ref, o_ref, os_scr, *, nc, ns):
    i = pl.program_id(1)
    QB = Q_BLOCK
    lane = lax.broadcasted_iota(jnp.int32, (QB, LANE), 1)
    lo64 = lane < DH
    zero_b = jnp.zeros((QB, LANE), BF)
    groups = range(G_A)

    def split_heads(ref, g):
        ev, od = [], []
        for c in range(2):
            xc = ref[:, (2 * g + c) * LANE:(2 * g + c + 1) * LANE]
            ev.append(jnp.where(lo64, xc, zero_b))
            od.append(jnp.where(lo64, zero_b, xc))
        return jnp.concatenate(ev, axis=0), jnp.concatenate(od, axis=0)

    tq_l = i * QB + lax.broadcasted_iota(jnp.int32, (nc, QB), 1)
    r_c = lax.broadcasted_iota(jnp.int32, (nc, QB), 0)
    half = nc // 2
    blk_c = jnp.where(r_c < half, 2 * r_c, 2 * (r_c - half) + 1)
    cmask = (blk_c * L_CMP + (L_CMP - 1)) <= tq_l
    cmask_f = cmask.astype(F32)
    tq_s = i * QB + lax.broadcasted_iota(jnp.int32, (ns, QB), 1)
    blk_s = lax.broadcasted_iota(jnp.int32, (ns, QB), 0)
    valid = (blk_s * L_SEL) <= tq_s
    forced = (blk_s == 0) | (blk_s == tq_s // L_SEL)
    n_top = min(N_SEL, ns)

    def select(imp):
        score = jnp.where(valid, imp + jnp.where(forced, FORCE_BONUS, 0.0), NEG)
        rank = jnp.zeros((ns, QB), F32)
        for b2 in range(ns):
            row = score[b2:b2 + 1, :]
            rank = rank + jnp.where(blk_s > b2, jnp.where(row >= score, 1.0, 0.0), jnp.where(row > score, 1.0, 0.0))
        return jnp.where((rank < n_top) & (score > 0.5 * NEG), 1.0, 0.0)

    def score_stage(chains):
        scored = []
        for qs, kmat, _, bias in chains:
            s = _nt(qs, kmat)
            s = (s.reshape(2, QB, s.shape[1]) + bias[None]).reshape(s.shape)
            scored.append((s, jnp.max(s, axis=1, keepdims=True)))
        return scored

    def value_stage(scored, chains):
        outs = []
        for (s, m), (_, _, vmat, _) in zip(scored, chains):
            p = jnp.exp2(s - m)
            outs.append(_nn(p.astype(BF), vmat) / jnp.sum(p, axis=1, keepdims=True))
        return outs

    def masked_attend(chains):
        return value_stage(score_stage(chains), chains)


    qr = [split_heads(qr_ref, g) for g in groups]
    n_tiles = (i * QB + QB + _KT - 1) // _KT

    cmp_scores = [[_nt(kc_ref[g], qs) for qs in split_heads(q_ref, g)] for g in groups]

    WK = WINDOW + QB
    start = pl.multiple_of(jnp.maximum(i * QB - WINDOW, 0), QB)
    diff = (i * QB + lax.broadcasted_iota(jnp.int32, (QB, WK), 0)) - (start + lax.broadcasted_iota(jnp.int32, (QB, WK), 1))
    wb = jnp.where((diff >= 0) & (diff < WINDOW), 0.0, NEG)
    win_chains = []
    for g in groups:
        wkv = kvb_ref[1, g, 0, pl.ds(start, WK), :]
        wvk = kvb_ref[1, g, 1, pl.ds(start, WK), :]
        win_chains += [(qr[g][0], wkv, wvk, wb), (qr[g][1], wvk, wkv, wb)]
    win_scored = score_stage(win_chains)

    imps, o_cmp = [], []
    for g in groups:
        vc = vc_ref[g]
        imp = jnp.zeros((ns, QB), F32)
        oc = {}
        for stack in range(2):
            st = cmp_scores[g][stack]
            for c in range(2):
                s = jnp.where(cmask, st[:, c * QB:(c + 1) * QB], NEG)
                p = jnp.exp2(s - jnp.max(s, axis=0, keepdims=True)) * cmask_f
                p = p / jnp.maximum(jnp.sum(p, axis=0, keepdims=True), 1e-30)
                imp = imp + p[0:half] + p[half:nc]
                oc[(c, stack)] = _tn(p.astype(BF), vc)
        o_cmp.append(oc)
        imps.append(imp)

    ow = value_stage(win_scored, win_chains)
    sel_b = [select(imp).astype(BF) for imp in imps]

    gates = []
    for g in groups:
        ga = ga_ref[:, g * LANE:(g + 1) * LANE]
        hi = ga.astype(BF)
        lo = (ga - hi.astype(F32)).astype(BF)
        gates.append(_nn(jnp.concatenate([hi, lo], axis=1), gx_ref[...]))

    def gate_of(g, br, c):
        j = br * 2 + c
        return gates[g][:, j * LANE:(j + 1) * LANE]

    part = [[gate_of(g, 0, c) * jnp.where(lo64, o_cmp[g][(c, 0)], o_cmp[g][(c, 1)])
             + gate_of(g, 2, c) * jnp.where(lo64, ow[2 * g][c * QB:(c + 1) * QB], ow[2 * g + 1][c * QB:(c + 1) * QB])
             for c in range(2)] for g in groups]


    def slc_variant(nk):
        tq_r = i * QB + lax.broadcasted_iota(jnp.int32, (QB, _KT), 0)
        causal = (nk - _KT + lax.broadcasted_iota(jnp.int32, (QB, _KT), 1)) <= tq_r
        chains = []
        for g in groups:
            bias = _tn(sel_b[g], e_ref[:, 0:nk]) - MASK_BIG
            tail = jnp.where(causal, bias[:, nk - _KT:], -MASK_BIG)
            bias = tail if nk == _KT else jnp.concatenate([bias[:, :nk - _KT], tail], axis=1)
            kv = kvb_ref[0, g, 0, 0:nk, :]
            vk = kvb_ref[0, g, 1, 0:nk, :]
            chains.append((qr[g][0], kv, vk, bias))
            chains.append((qr[g][1], vk, kv, bias))
        for c, o in enumerate(masked_attend(chains)):
            os_scr[c // 2, c % 2] = o

    for k in range(1, kvb_ref.shape[3] // _KT + 1):
        pl.when(n_tiles == k)(functools.partial(slc_variant, k * _KT))

    for g in groups:
        os_e = os_scr[g, 0]
        os_o = os_scr[g, 1]
        for c in range(2):
            rows = slice(c * QB, (c + 1) * QB)
            acc = part[g][c] + gate_of(g, 1, c) * jnp.where(lo64, os_e[rows], os_o[rows])
            o_ref[:, (2 * g + c) * LANE:(2 * g + c + 1) * LANE] = acc.astype(BF)


def _gate_expand():
    gx = np.zeros((LANE, 3 * 2 * LANE), np.float32)
    for br in range(3):
        for c in range(2):
            j = br * 2 + c
            gx[br * HPG + 2 * c, j * LANE:j * LANE + DH] = 1.0
            gx[br * HPG + 2 * c + 1, j * LANE + DH:(j + 1) * LANE] = 1.0
    return jnp.asarray(np.concatenate([gx, gx], axis=0), BF)


def _sel_expand(ns, nkeys):
    e = (np.arange(nkeys)[None, :] // L_SEL) == np.arange(ns)[:, None]
    return jnp.asarray(e * MASK_BIG, BF)


def _nsa_prompt(q, qr, kvb, kc2, vc2, ga, B, T):
    nqb = T // Q_BLOCK
    nc, ns = T // L_CMP, T // L_SEL
    assert T >= WINDOW + Q_BLOCK and T % _KT == 0
    qspec = pl.BlockSpec((Q_BLOCK, 512), lambda b, i: (b * nqb + i, 0))
    cspec = pl.BlockSpec((None, G_A, nc, LANE), lambda b, i: (b, 0, 0, 0))
    return pl.pallas_call(
        functools.partial(_nsa_prompt_body, nc=nc, ns=ns),
        grid=(B, nqb),
        in_specs=[qspec, qspec,
                  pl.BlockSpec((2, G_A, 2, T, LANE), lambda b, i: (0, 0, 0, b, 0)),
                  cspec, cspec,
                  pl.BlockSpec((Q_BLOCK, G_A * LANE), lambda b, i: (b * nqb + i, 0)),
                  pl.BlockSpec((ns, T), lambda b, i: (0, 0)),
                  pl.BlockSpec((2 * LANE, 6 * LANE), lambda b, i: (0, 0))],
        out_specs=qspec,
        out_shape=jax.ShapeDtypeStruct((B * T, 512), BF),
        scratch_shapes=[pltpu.VMEM((G_A, 2, 2 * Q_BLOCK, LANE), F32)],
        compiler_params=_cparams(("arbitrary", "arbitrary")),
        name="nsa_prompt",
    )(q, qr, kvb, kc2, vc2, ga, _sel_expand(ns, T), _gate_expand())


def _ret_body(qkv_ref, gr_ref, s0_ref, dm_ref, dq_ref, wk_ref, gc_ref, gnw_ref, z_ref, sout_ref, s_scr, *, C, sb):
    c = pl.program_id(1)

    @pl.when(c == 0)
    def _():
        s_scr[...] = s0_ref[...]

    units = [(j, h, slice(j * C, (j + 1) * C), slice(h * LANE, (h + 1) * LANE))
             for j in range(sb) for h in range(H_R)]
    qkv = [tuple(qkv_ref[t, rows, sl] for t in range(3)) for _, _, rows, sl in units]
    inner = [_nt(q, k) * dm_ref[h] for (q, k, _), (_, h, _, _) in zip(qkv, units)]
    cross = [_nn(q, s_scr[j, h].astype(BF)) * dq_ref[h] for (q, _, _), (j, h, _, _) in zip(qkv, units)]
    outs = [_nn(a.astype(BF), v) + c for a, c, (_, _, v) in zip(inner, cross, qkv)]
    for (_, k, v), (j, h, _, _) in zip(qkv, units):
        kw = (k.astype(F32) * wk_ref[h]).astype(BF)
        s_new = gc_ref[h, 0:1, :] * s_scr[j, h] + _tn(kw, v)
        s_scr[j, h] = s_new
        sout_ref[j, h] = s_new
    for o, (_, _, rows, sl) in zip(outs, units):
        mu = jnp.mean(o, axis=-1, keepdims=True)
        d = o - mu
        var = jnp.mean(d * d, axis=-1, keepdims=True)
        on = d * lax.rsqrt(var + EPS) * gnw_ref[:, sl]
        z_ref[rows, sl] = (gr_ref[rows, sl].astype(F32) * on).astype(BF)


def _retention(ret3, gr, s0, gnw, nseq, rows_per_seq, C, c_true, sb):
    nC = rows_per_seq // C
    assert sb == 1 or nC == 1
    rows = nseq * rows_per_seq
    dm, dq, wk, gc = _ret_decay(C, c_true)
    full = lambda a: pl.BlockSpec(a.shape, lambda b, c: (0,) * a.ndim)
    return pl.pallas_call(
        functools.partial(_ret_body, C=C, sb=sb),
        grid=(nseq // sb, nC),
        in_specs=[pl.BlockSpec((3, sb * C, H_R * LANE), lambda b, c: (0, b * nC + c, 0)),
                  pl.BlockSpec((sb * C, H_R * LANE), lambda b, c: (b * nC + c, 0)),
                  pl.BlockSpec((sb, H_R, DK_R, DV_R), lambda b, c: (b, 0, 0, 0)),
                  full(dm), full(dq), full(wk), full(gc), full(gnw)],
        out_specs=[pl.BlockSpec((sb * C, H_R * LANE), lambda b, c: (b * nC + c, 0)),
                   pl.BlockSpec((sb, H_R, DK_R, DV_R), lambda b, c: (b, 0, 0, 0))],
        out_shape=[jax.ShapeDtypeStruct((rows, 512), BF),
                   jax.ShapeDtypeStruct((nseq, H_R, DK_R, DV_R), F32)],
        scratch_shapes=[pltpu.VMEM((sb, H_R, DK_R, DV_R), F32)],
        compiler_params=_cparams(("arbitrary", "arbitrary")),
        name="retention",
    )(ret3, gr, s0, dm, dq, wk, gc, gnw)


def _mix_ffn_body(x_ref, oa_ref, zr_ref, gm_ref, gt1_ref, wa_ref, wr_ref, wo_ref,
                  nw_ref, sh_ref, sc_ref, gt_ref, wu_ref, cw_ref, cb_ref, wd_ref, nf_ref, p1_ref, p2_ref,
                  y_ref, a_ref, carry, *, blocks_per_seq, seq_rows):
    i = pl.program_id(0)
    tm = x_ref.shape[0]
    parts = [slice(k * _ROW_PART, (k + 1) * _ROW_PART) for k in range(tm // _ROW_PART)] if tm > _ROW_PART else [slice(0, tm)]

    def mod(ref, r):
        return ref[...] if ref.shape[0] == 1 else ref[r, :]

    ya = [_nn(oa_ref[r, :], wa_ref[...]) for r in parts]
    yr = [_nn(zr_ref[r, :], wr_ref[...]) for r in parts]
    xs = []
    for r, ya_k, yr_k in zip(parts, ya, yr):
        gm = gm_ref[r, :].astype(F32)
        merged = gm[:, 0:D_MODEL] * ya_k + gm[:, D_MODEL:2 * D_MODEL] * yr_k
        xs.append(x_ref[r, :] + mod(gt1_ref, r) * _nn(merged.astype(BF), wo_ref[...]))
    hs = []
    for r, x in zip(parts, xs):
        ms = jnp.mean(x * x, axis=-1, keepdims=True)
        h = (x * lax.rsqrt(ms + EPS)) * nw_ref[...]
        hs.append((h * (1.0 + mod(sc_ref, r)) + mod(sh_ref, r)).astype(BF))
    a_parts = [_nn(h, wu_ref[:, 0:D_FF]) for h in hs]
    b_parts = [_nn(h, wu_ref[:, D_FF:2 * D_FF]) for h in hs]
    a = a_parts[0] if len(parts) == 1 else jnp.concatenate(a_parts, axis=0)
    a_ref[...] = a[tm - a_ref.shape[0]:tm, :]
    rid = lax.broadcasted_iota(jnp.int32, (tm, D_FF), 0)
    if seq_rows is None:
        first = (i % blocks_per_seq) == 0
        prev = jnp.where(first, p1_ref[...], carry[...])
        carry[...] = a[tm - 8:tm, :]
        am1 = jnp.where(rid == 0, prev[7:8, :], pltpu.roll(a, 1, 0))
        am2 = jnp.where(rid == 0, prev[6:7, :], jnp.where(rid == 1, prev[7:8, :], pltpu.roll(a, 2, 0)))
    else:
        s = rid % seq_rows
        am1 = jnp.where(s == 0, p1_ref[...], pltpu.roll(a, 1, 0))
        am2 = jnp.where(s <= 1, p2_ref[...], pltpu.roll(a, 2, 0))
    u = cb_ref[...] + am2 * cw_ref[0:1, :] + am1 * cw_ref[1:2, :] + a * cw_ref[2:3, :]
    ffs = [_nn((_gelu(u[r]) * b_k).astype(BF), wd_ref[...]) for r, b_k in zip(parts, b_parts)]
    for r, x, ff in zip(parts, xs, ffs):
        x2 = x + mod(gt_ref, r) * ff
        ms2 = jnp.mean(x2 * x2, axis=-1, keepdims=True)
        y_ref[r, :] = (x2 * lax.rsqrt(ms2 + EPS)) * nf_ref[...]


def _mix_ffn(x2d, oa, zr, gm, gt1, wa, wr, wo, nw, sh, sc, gt, wu, cw, cb, wd, nf, p1, p2,
             tm, blocks_per_seq, per_row_mods, seq_rows):
    rows = x2d.shape[0]
    a_rows = tm if seq_rows is not None else 8
    if per_row_mods:
        mod_spec = pl.BlockSpec((tm, D_MODEL), lambda i: (i, 0))
    else:
        mod_spec = pl.BlockSpec((None, 1, D_MODEL), lambda i: (i // blocks_per_seq, 0, 0))
    row = lambda w: pl.BlockSpec((tm, w), lambda i: (i, 0))
    full = lambda a, b: pl.BlockSpec((a, b), lambda i: (0, 0), pipeline_mode=pl.Buffered(1))
    vec = lambda w: pl.BlockSpec((1, w), lambda i: (0, 0))
    pspec = pl.BlockSpec(p1.shape, lambda i: (0, 0))
    return pl.pallas_call(
        functools.partial(_mix_ffn_body, blocks_per_seq=blocks_per_seq, seq_rows=seq_rows),
        grid=(rows // tm,),
        in_specs=[row(D_MODEL), row(512), row(512), row(2048), mod_spec,
                  full(512, D_MODEL), full(512, D_MODEL), full(D_MODEL, D_MODEL),
                  vec(D_MODEL), mod_spec, mod_spec, mod_spec,
                  full(D_MODEL, 2 * D_FF), pl.BlockSpec((CONV_W, D_FF), lambda i: (0, 0)), vec(D_FF),
                  full(D_FF, D_MODEL), vec(D_MODEL), pspec, pspec],
        out_specs=[row(D_MODEL), pl.BlockSpec((a_rows, D_FF), lambda i: (i, 0))],
        out_shape=[jax.ShapeDtypeStruct((rows, D_MODEL), F32),
                   jax.ShapeDtypeStruct((rows // tm * a_rows, D_FF), F32)],
        scratch_shapes=[pltpu.VMEM((8, D_FF), F32)],
        compiler_params=_cparams(("arbitrary",)),
        name="mix_ffn",
    )(x2d, oa, zr, gm, gt1, wa, wr, wo, nw, sh, sc, gt, wu, cw, cb, wd, nf, p1, p2)


def _prompt_path(x_prompt, mods_p, W):
    B, T, _ = x_prompt.shape
    rows = B * T
    x2d = x_prompt.reshape(rows, D_MODEL)
    pos = np.arange(T)
    tm = 2 * _ROW_PART if T % (2 * _ROW_PART) == 0 else _ROW_PART
    sh1, sc1, gt1, sh2, sc2, gt2 = [mods_p[:, None, j * D_MODEL:(j + 1) * D_MODEL] for j in range(6)]
    (q, qr, kvc, kvs, kvw, kvb, ret3, gr, gm, ga) = _inproj(
        x2d, W["norm1"], sh1, sc1, W["w_in"], _rope_tab(pos, Q_SCALE), _rope_tab(pos, 1.0), _ret_tab(pos),
        tm, T // tm, False, True)
    nc, n_pages = T // L_CMP, T // _PAGE
    comp = _compress_t(kvc, W["phi_posT"], W["phi_w1p"], W["phi_w2p"])
    comp = comp.reshape(2, G_A, B, n_pages, 2, 2, DH).transpose(0, 2, 1, 5, 3, 4, 6).reshape(2, B, G_A, nc, DH)
    comp2 = jnp.concatenate([comp, comp], axis=-1).astype(BF)
    oa = _nsa_prompt(q, qr, kvb, comp2[0], comp2[1], ga, B, T)
    C = 256 if T % 256 == 0 else T
    zr, ret_new = _retention(ret3, gr, jnp.zeros((B, H_R, DK_R, DV_R), F32), W["gnw"], B, T, C, C, 1)
    zeros8 = jnp.zeros((8, D_FF), F32)
    tf = tm
    y, a_up = _mix_ffn(x2d, oa, zr, gm, gt1, W["w_up_a"], W["w_up_r"], W["w_out"],
                       W["norm2"], sh2, sc2, gt2, W["w_ffn_up"], W["conv_w"], W["conv_b"], W["w_ffn_down"],
                       W["normf"], zeros8, zeros8, tf, T // tf, False, None)
    wsz = min(WINDOW, T)
    rows_major = lambda t: t.reshape(B, 2, G_A, DH, t.shape[-1]).transpose(0, 4, 1, 2, 3)[None]
    outs = dict(
        y=y.reshape(B, T, D_MODEL),
        cmp=rows_major(kvc), slc=rows_major(kvs), win=rows_major(kvw[:, :, T - wsz:]),
        ret=ret_new[None],
        conv=a_up.reshape(B, T // tf, 8, D_FF)[None, :, T // tf - 1, 8 - (CONV_W - 1):],
    )
    return outs


_QC = 32


def _cmp_paged_body(pt_ref, cache_ref, q_ref, pos_ref, w1_ref, w2_ref, ocmp_ref, topi_ref, xbuf, sem,
                    *, n_pages, n_pick, n_q):
    b = pl.program_id(0)
    nb = pl.num_programs(0)
    slot = b % 2

    def copy(bb, sl, p):
        return pltpu.make_async_copy(cache_ref.at[pt_ref[bb, p]], xbuf.at[sl, :, p, :], sem.at[sl])

    def issue(bb, sl):
        for p in range(n_pages):
            copy(bb, sl, p).start()

    @pl.when(b == 0)
    def _():
        issue(b, slot)

    @pl.when(b + 1 < nb)
    def _():
        issue(b + 1, 1 - slot)

    for p in range(n_pages):
        copy(b, slot, p).wait()

    comp = _compress_slab_rows(lambda r: xbuf[slot, r], n_pages, pos_ref, w1_ref, w2_ref)

    def lane_groups(x, op):
        r = x
        for j in range(1, _BPP):
            r = op(r, pltpu.roll(x, j * _QC, 1))
        return r

    grp = range(G_A)
    sts = [_nn(comp[0][g * n_pages:(g + 1) * n_pages].astype(BF), q_ref[g]) for g in grp]
    ps = []
    for g in grp:
        m = lane_groups(jnp.max(sts[g], axis=0, keepdims=True), jnp.maximum)
        p = jnp.exp2(sts[g] - m)
        ps.append(p / lane_groups(jnp.sum(p, axis=0, keepdims=True), jnp.add))
    for g in grp:
        r_full = _tn(ps[g].astype(BF), comp[1][g * n_pages:(g + 1) * n_pages].astype(BF))
        o = r_full[0:_QC, 0:DH]
        for j in range(1, _BPP):
            o = o + r_full[j * _QC:(j + 1) * _QC, j * DH:(j + 1) * DH]
        ocmp_ref[g] = o
    imp = None
    for g in grp:
        pair = ps[g] + pltpu.roll(ps[g], LANE - _QC, 1)
        t = pair
        for r in range(1, HPG):
            t = t + pltpu.roll(pair, LANE - r * n_q, 1)
        if g:
            t = pltpu.roll(t, g * n_q, 1)
        lane_g = lax.broadcasted_iota(jnp.int32, t.shape, 1) % (2 * _QC)
        t = jnp.where((lane_g >= g * n_q) & (lane_g < (g + 1) * n_q), t, 0.0)
        imp = t if imp is None else imp + t
    sc = jnp.concatenate([imp, pltpu.roll(imp, 2 * _QC, 1)], axis=0)
    row = lax.broadcasted_iota(jnp.int32, (2 * n_pages, LANE), 0)
    blk = jnp.where(row < n_pages, 2 * row, 2 * (row - n_pages) + 1)
    score = sc + jnp.where(blk == 0, FORCE_BONUS, 0.0)
    topi_ref[...] = jnp.zeros((8, LANE), jnp.int32)
    for k in range(n_pick):
        mx = jnp.max(score, axis=0, keepdims=True)
        idx = jnp.min(jnp.where(score == mx, blk, 2 * n_pages), axis=0, keepdims=True)
        topi_ref[k:k + 1, :] = idx
        score = jnp.where(blk == idx, -jnp.inf, score)


def _cmp_paged(page_table, cache_t, q_bd, pos, w1, w2, n_q):
    DB, n_pages = page_table.shape
    SL = cache_t.shape[1]
    page_rows = cache_t.shape[2]
    kern = functools.partial(_cmp_paged_body, n_pages=n_pages, n_pick=N_SEL - 1, n_q=n_q)
    one = dict(pipeline_mode=pl.Buffered(1))
    return pl.pallas_call(
        kern,
        grid_spec=pltpu.PrefetchScalarGridSpec(
            num_scalar_prefetch=1, grid=(DB,),
            in_specs=[pl.BlockSpec(memory_space=pl.ANY),
                      pl.BlockSpec((None, G_A, _BPP * DH, LANE), lambda b, pt: (b, 0, 0, 0)),
                      pl.BlockSpec(pos.shape, lambda b, pt: (0, 0, 0), **one),
                      pl.BlockSpec(w1.shape, lambda b, pt: (0, 0, 0, 0), **one),
                      pl.BlockSpec((2, _BPP * DH, _BPP * DH), lambda b, pt: (0, 0, 0), **one)],
            out_specs=[pl.BlockSpec((None, G_A, _QC, DH), lambda b, pt: (b, 0, 0, 0)),
                       pl.BlockSpec((None, 8, LANE), lambda b, pt: (b, 0, 0))],
            scratch_shapes=[pltpu.VMEM((2, SL, n_pages, page_rows), F32),
                            pltpu.SemaphoreType.DMA((2,))]),
        out_shape=[jax.ShapeDtypeStruct((DB, G_A, _QC, DH), F32),
                   jax.ShapeDtypeStruct((DB, 8, LANE), jnp.int32)],
        compiler_params=_cparams(("arbitrary",)),
        name="cmp_paged",
    )(page_table, cache_t, q_bd, pos, w1, w2)


def _slc_win_paged_body(pt_ref, ti_ref, cslc_ref, win_ref, q_ref, tiv_ref, ns_ref, nw_ref, ex_ref,
                        oslc_ref, owin_ref, kbuf, sem, *, n_q, n_pick, page_rows):
    b = pl.program_id(0)
    nb = pl.num_programs(0)
    slot = b % 2
    n_slab = n_q * n_pick
    bpp = page_rows // L_SEL
    wb = win_ref.shape[1]

    def copies(bb, sl):
        cps = []
        for g in range(G_A):
            for j in range(n_slab):
                page = pt_ref[bb, ti_ref[bb, g * n_slab + j] // bpp]
                for kv in range(2):
                    cps.append(pltpu.make_async_copy(
                        cslc_ref.at[page, pl.ds((kv * G_A + g) * DH, DH), :],
                        kbuf.at[sl, g, kv, :, pl.ds(j * page_rows, page_rows)], sem.at[sl]))
        return cps

    @pl.when(b == 0)
    def _():
        for cp in copies(b, slot):
            cp.start()

    @pl.when(b + 1 < nb)
    def _():
        for cp in copies(b + 1, 1 - slot):
            cp.start()

    for cp in copies(b, slot):
        cp.wait()

    nq_rows = q_ref.shape[1]
    nk = n_slab * page_rows
    row_q = lax.broadcasted_iota(jnp.int32, (nq_rows, nk), 0) % n_q
    col = lax.broadcasted_iota(jnp.int32, (nq_rows, nk), 1)
    own = row_q == col // (n_pick * page_rows)
    half = ((col % page_rows) // L_SEL).astype(F32)
    nnew = ns_ref.shape[2]
    new_ok = (lax.broadcasted_iota(jnp.int32, (nq_rows, nnew), 1)
              <= lax.broadcasted_iota(jnp.int32, (nq_rows, nnew), 0) % n_q)
    dwin = (wb + lax.broadcasted_iota(jnp.int32, (nq_rows, wb), 0) % n_q
            - lax.broadcasted_iota(jnp.int32, (nq_rows, wb), 1))
    win_ok = (dwin >= 0) & (dwin < WINDOW)

    chains = []
    for g in range(G_A):
        q = q_ref[g]
        par = (tiv_ref[g] % bpp).astype(F32).astype(BF)
        want = _nn(par, ex_ref[...])[0:1, :]
        chains.append((q, kbuf[slot, g, 0].astype(BF), kbuf[slot, g, 1].astype(BF), own & (half == want),
                       ns_ref[g, 0].astype(BF), ns_ref[g, 1].astype(BF), oslc_ref, g))
        kw = win_ref[pl.ds((0 * G_A + g) * DH, DH), :].astype(BF)
        vw = win_ref[pl.ds((1 * G_A + g) * DH, DH), :].astype(BF)
        chains.append((q, kw, vw, win_ok, nw_ref[g, 0].astype(BF), nw_ref[g, 1].astype(BF), owin_ref, g))
    scores = [(jnp.where(ok_old, _nn(q, kt_old), NEG), jnp.where(new_ok, _nt(q, k_new), NEG))
              for q, kt_old, _, ok_old, k_new, _, _, _ in chains]
    probs = []
    for s_o, s_n in scores:
        m = jnp.maximum(jnp.max(s_o, axis=1, keepdims=True), jnp.max(s_n, axis=1, keepdims=True))
        p_o = jnp.exp2(s_o - m)
        p_n = jnp.exp2(s_n - m)
        probs.append((p_o, p_n, jnp.sum(p_o, axis=1, keepdims=True) + jnp.sum(p_n, axis=1, keepdims=True)))
    for (p_o, p_n, den), (_, _, vt_old, _, _, v_new, out_ref, g) in zip(probs, chains):
        out_ref[g] = (_nt(p_o.astype(BF), vt_old) + _nn(p_n.astype(BF), v_new)) / den


def _slc_win_paged(page_table, topi_flat, topi_vec, cache_t, win_t, q_rot, new_slc, new_win, n_q):
    DB = page_table.shape[0]
    n_pick = N_SEL - 1
    page_rows = cache_t.shape[2]
    wb = win_t.shape[2]
    nq_rows = q_rot.shape[2]
    n_slab = n_q * n_pick
    ex = (np.arange(n_slab * page_rows)[None, :] // page_rows) == np.arange(LANE)[:, None]
    kern = functools.partial(_slc_win_paged_body, n_q=n_q, n_pick=n_pick, page_rows=page_rows)
    bspec = lambda shp: pl.BlockSpec((None,) + shp, lambda b, pt, ti: (b,) + (0,) * len(shp))
    return pl.pallas_call(
        kern,
        grid_spec=pltpu.PrefetchScalarGridSpec(
            num_scalar_prefetch=2, grid=(DB,),
            in_specs=[pl.BlockSpec(memory_space=pl.ANY), bspec((2 * G_A * DH, wb)),
                      bspec((G_A, nq_rows, DH)), bspec((G_A, 16, LANE)),
                      bspec((G_A, 2, 16, DH)), bspec((G_A, 2, 16, DH)),
                      pl.BlockSpec((LANE, n_slab * page_rows), lambda b, pt, ti: (0, 0))],
            out_specs=[bspec((G_A, nq_rows, DH)), bspec((G_A, nq_rows, DH))],
            scratch_shapes=[pltpu.VMEM((2, G_A, 2, DH, n_slab * page_rows), F32),
                            pltpu.SemaphoreType.DMA((2,))]),
        out_shape=[jax.ShapeDtypeStruct((DB, G_A, nq_rows, DH), F32)] * 2,
        compiler_params=_cparams(("arbitrary",)),
        name="slc_win_paged",
    )(page_table, topi_flat, cache_t, win_t, q_rot, topi_vec, new_slc, new_win, jnp.asarray(ex, BF))


def _gate_sample_body(oc_ref, os_ref, ow_ref, ga_ref, o_ref):
    rows = o_ref.shape[0]
    lo64 = lax.broadcasted_iota(jnp.int32, (rows, LANE), 1) < DH
    for c4 in range(H_A // 2):
        g, c = c4 // 2, c4 % 2
        sl = slice(c4 * LANE, (c4 + 1) * LANE)
        acc = jnp.zeros((rows, LANE), F32)
        for br, ref in enumerate((oc_ref, os_ref, ow_ref)):
            col = g * LANE + br * HPG + 2 * c
            gate = jnp.where(lo64, ga_ref[:, col:col + 1], ga_ref[:, col + 1:col + 2])
            acc = acc + gate * ref[:, sl]
        o_ref[:, sl] = acc.astype(BF)


def _gate_sample(oc, osl, ow, ga):
    rows = oc.shape[0]
    full = lambda w: pl.BlockSpec((rows, w), lambda i: (0, 0))
    return pl.pallas_call(
        _gate_sample_body, grid=(1,),
        in_specs=[full(512), full(512), full(512), full(256)],
        out_specs=full(512),
        out_shape=jax.ShapeDtypeStruct((rows, 512), BF),
        compiler_params=_cparams(("arbitrary",)),
        name="gate_sample",
    )(oc, osl, ow, ga)


def _sample_path(x_sample, mods_s, W, cache_cmp, cache_slc, state_win, state_ret, state_conv, page_table):
    DB, S, _ = x_sample.shape
    rows = DB * S
    page_rows = cache_cmp.shape[1]
    P = page_table.shape[1] * page_rows
    wb = state_win.shape[1]
    assert P % L_SEL == 0 and S < L_CMP and S <= 8 and wb == WINDOW and page_rows % L_SEL == 0
    assert P // L_SEL >= N_SEL and CONV_W == 3 and S >= CONV_W - 1
    pos = P + np.arange(S)
    pos_rows = np.tile(pos, DB)
    x2d = x_sample.reshape(rows, D_MODEL)
    modr = jnp.repeat(mods_s, S, axis=0)
    sh1, sc1, gt1, sh2, sc2, gt2 = [modr[:, j * D_MODEL:(j + 1) * D_MODEL] for j in range(6)]
    (q, qr, kvc, kvs, kvw, _, ret3, gr, gm, ga) = _inproj(
        x2d, W["norm1"], sh1, sc1, W["w_in"], _rope_tab(pos_rows, Q_SCALE), _rope_tab(pos_rows, 1.0),
        _ret_tab(pos_rows), rows, 1, True, False)

    def to_heads(t):
        return t.reshape(DB, S, G_A, HPG, DH).transpose(0, 2, 3, 1, 4).reshape(DB, G_A, HPG * S, DH)

    def from_heads(t):
        return t.reshape(DB, G_A, HPG, S, DH).transpose(0, 3, 1, 2, 4).reshape(rows, H_A * DH)

    assert page_rows == _BPP * L_CMP and HPG * S <= _QC and S * (N_SEL - 1) <= LANE
    slab = lambda t: t.transpose(0, 2, 3, 4, 1).reshape(t.shape[0], 2 * G_A * DH, t.shape[1])
    qt = jnp.pad(to_heads(q).transpose(0, 1, 3, 2), ((0, 0), (0, 0), (0, 0), (0, _QC - HPG * S)))
    q_bd = _blockdiag(qt)
    o_cmp, topi = _cmp_paged(page_table, slab(cache_cmp), q_bd, W["phi_posT"], W["phi_w1p"], W["phi_w2p"], S)
    n_pick = N_SEL - 1
    topi = topi[:, :n_pick, :G_A * S].transpose(0, 2, 1).reshape(DB, G_A, S * n_pick)
    topi_vec = jnp.broadcast_to(jnp.pad(topi, ((0, 0), (0, 0), (0, LANE - S * n_pick)))[:, :, None, :],
                                (DB, G_A, 16, LANE))

    def new_rows(t):
        t = t.reshape(DB, S, 2, G_A, DH).transpose(0, 3, 2, 1, 4)
        return jnp.pad(t, ((0, 0), (0, 0), (0, 0), (0, 16 - S), (0, 0)))

    o_slc, o_win = _slc_win_paged(page_table, topi.reshape(DB, G_A * S * n_pick), topi_vec, slab(cache_slc),
                                  slab(state_win), to_heads(qr), new_rows(kvs), new_rows(kvw), S)
    oa = _gate_sample(from_heads(o_cmp[:, :, :HPG * S]), from_heads(o_slc), from_heads(o_win), ga)

    RP = 16
    padr = lambda t: jnp.pad(t.reshape(t.shape[:-2] + (DB, S, 512)),
                             ((0, 0),) * (t.ndim - 1) + ((0, RP - S), (0, 0))).reshape(t.shape[:-2] + (DB * RP, 512))
    zr, ret_new = _retention(padr(ret3), padr(gr), state_ret, W["gnw"], DB, RP, RP, S, 8 if DB % 8 == 0 else 1)
    zr = zr.reshape(DB, RP, 512)[:, :S].reshape(rows, 512)
    zs = lambda n: jnp.zeros((DB, n, D_FF), F32)
    p1 = jnp.concatenate([state_conv[:, 1:2], zs(S - 1)], axis=1).reshape(rows, D_FF)
    p2 = jnp.concatenate([state_conv[:, 0:2], zs(S - 2)], axis=1).reshape(rows, D_FF)
    y, a_up = _mix_ffn(x2d, oa, zr, gm, gt1, W["w_up_a"], W["w_up_r"], W["w_out"],
                       W["norm2"], sh2, sc2, gt2, W["w_ffn_up"], W["conv_w"], W["conv_b"], W["w_ffn_down"],
                       W["normf"], p1, p2, rows, 1, True, S)
    shp = (1, DB, S, 2, G_A, DH)
    return dict(
        y=y.reshape(DB, S, D_MODEL),
        cmp=kvc.reshape(shp), slc=kvs.reshape(shp),
        win=jnp.concatenate([state_win[:, S:], kvw.reshape(DB, S, 2, G_A, DH)], axis=1)[None],
        ret=ret_new[None],
        conv=a_up.reshape(DB, S, D_FF)[None, :, S - (CONV_W - 1):],
    )


def kernel(x_prompt, x_sample, cache_cmp_kv, cache_slc_kv, state_win_kv, state_ret, state_conv, page_table,
           c_prompt, c_sample, norm1_w, norm2_w, w_ada, b_ada, w_in, phi_pos_k, phi_k1, phi_k2, phi_pos_v,
           phi_v1, phi_v2, w_up_a, ret_gn_w, w_up_r, w_out, w_ffn_up, ffn_conv_w, ffn_conv_b, w_ffn_down, normf_w):
    B = x_prompt.shape[0]
    l = 0
    W = dict(
        norm1=norm1_w[l].reshape(1, D_MODEL), norm2=norm2_w[l].reshape(1, D_MODEL), normf=normf_w.reshape(1, D_MODEL),
        w_in=_pack_w_in(w_in[l]),
        w_up_a=w_up_a[l].astype(BF), w_up_r=w_up_r[l].astype(BF), w_out=w_out[l].astype(BF),
        gnw=ret_gn_w[l].reshape(1, H_R * DV_R),
        w_ffn_up=w_ffn_up[l].astype(BF), conv_w=ffn_conv_w[l], conv_b=ffn_conv_b[l].reshape(1, D_FF),
        w_ffn_down=w_ffn_down[l].astype(BF),
    )
    W["phi_posT"], W["phi_w1p"], W["phi_w2p"] = _pack_phi_paged(
        phi_pos_k[l], phi_k1[l], phi_k2[l], phi_pos_v[l], phi_v1[l], phi_v2[l])
    mods = _mods(jnp.concatenate([c_prompt, c_sample], axis=0), w_ada[l], b_ada[l])
    p = _prompt_path(x_prompt, mods[:B], W)
    s = _sample_path(x_sample, mods[B:], W, cache_cmp_kv[l], cache_slc_kv[l], state_win_kv[l], state_ret[l],
                     state_conv[l], page_table)
    return (p["y"], s["y"], p["cmp"], s["cmp"], p["slc"], s["slc"], p["win"], s["win"],
            p["ret"], s["ret"], p["conv"], s["conv"])
```

```python
import functools

import numpy as np
import jax
import jax.numpy as jnp
from jax import lax
from jax.experimental import pallas as pl
from jax.experimental.pallas import tpu as pltpu

BF = jnp.bfloat16
F32 = jnp.float32

D_MODEL = 1024
H_A, G_A, DH = 8, 2, 64
HPG = H_A // G_A
ROPE_DIM = DH // 4
ROPE_THETA = 500000.0
L_CMP, L_SEL, N_SEL = 32, 64, 8
WINDOW = 512
Q_BLOCK = 128
FORCE_BONUS = 1e4
H_R, DK_R, DV_R = 4, 128, 128
RET_THETA = 10000.0
D_FF = 2816
CONV_W = 3
EPS = 1e-6
NEG = -1e30
MASK_BIG = 2.0 ** 100
Q_SCALE = DH ** -0.5 * 1.4426950408889634
LANE = 128
_ROW_PART = 256
VMEM_LIMIT = 56 * 1024 * 1024


def _cparams(sem):
    return pltpu.CompilerParams(dimension_semantics=sem, vmem_limit_bytes=VMEM_LIMIT)


def _sigmoid(x):
    return 1.0 / (1.0 + jnp.exp(-x))


def _gelu(x):
    return 0.5 * x * (1.0 + jnp.tanh(0.7978845608028654 * (x + 0.044715 * (x * x * x))))


def _nt(a, b):
    return lax.dot_general(a, b, (((1,), (1,)), ((), ())), preferred_element_type=F32)


def _tn(a, b):
    return lax.dot_general(a, b, (((0,), (0,)), ((), ())), preferred_element_type=F32)


def _nn(a, b):
    return jnp.dot(a, b, preferred_element_type=F32)


def _rope_tab(pos, scale):
    half = ROPE_DIM // 2
    inv = ROPE_THETA ** (-np.arange(half, dtype=np.float64) * (2.0 / ROPE_DIM))
    ang = pos.astype(np.float64)[:, None] * inv
    cos, sin = np.cos(ang), np.sin(ang)
    n = pos.shape[0]
    c = np.ones((n, DH)); s_lo = np.zeros((n, DH)); s_hi = np.zeros((n, DH))
    c[:, :half] = cos; c[:, half:ROPE_DIM] = cos
    s_lo[:, half:ROPE_DIM] = sin
    s_hi[:, :half] = -sin
    tab = np.concatenate([np.tile(t, (1, 2)) for t in (c, s_lo, s_hi)], axis=1) * scale
    return jnp.asarray(tab, F32)


def _ret_tab(pos):
    half = DK_R // 2
    inv = RET_THETA ** (-np.arange(half, dtype=np.float64) * (2.0 / DK_R))
    ang = pos.astype(np.float64)[:, None] * inv
    cos, sin = np.cos(ang), np.sin(ang)
    c = np.concatenate([cos, cos], axis=1)
    s = np.concatenate([-sin, sin], axis=1)
    ks = DK_R ** -0.5
    return jnp.asarray(np.concatenate([c, s, c * ks, s * ks], axis=1), F32)


def _ret_decay(C, c_true):
    h = np.arange(H_R, dtype=np.float64)
    log_g = np.log1p(-np.exp2(-5.0 - h))
    i = np.arange(C, dtype=np.float64)
    diff = i[:, None] - i[None, :]
    dm = np.where(diff >= 0, np.exp(log_g[:, None, None] * np.maximum(diff, 0.0)), 0.0)
    dq = np.exp(log_g[:, None] * (i + 1.0))[:, :, None] * np.ones((1, 1, LANE))
    wk = np.exp(log_g[:, None] * (c_true - 1.0 - i))[:, :, None] * np.ones((1, 1, LANE))
    wk = np.where(i[None, :, None] < c_true, wk, 0.0)
    gc = np.exp(log_g * c_true)[:, None, None] * np.ones((1, 8, LANE))
    return (jnp.asarray(dm, F32), jnp.asarray(dq, F32), jnp.asarray(wk, F32), jnp.asarray(gc, F32))


def _mods_body(c_ref, w_ref, b_ref, o_ref):
    c = c_ref[...]
    s = c * _sigmoid(c)
    o_ref[...] = _nn(s.astype(BF), w_ref[...].astype(BF)) + b_ref[...]


def _mods(c_all, w_ada, b_ada):
    n = c_all.shape[0]
    nout = w_ada.shape[1]
    tn = 1024
    return pl.pallas_call(
        _mods_body,
        grid=(nout // tn,),
        in_specs=[pl.BlockSpec((n, D_MODEL), lambda j: (0, 0)),
                  pl.BlockSpec((D_MODEL, tn), lambda j: (0, j)),
                  pl.BlockSpec((1, tn), lambda j: (0, j))],
        out_specs=pl.BlockSpec((n, tn), lambda j: (0, j)),
        out_shape=jax.ShapeDtypeStruct((n, nout), F32),
        compiler_params=_cparams(("arbitrary",)),
        name="mods",
    )(c_all, w_ada, b_ada.reshape(1, nout))


_C_Q, _C_KV, _C_QR, _C_KR, _C_VR, _C_GR, _C_GM, _C_GA, _C_END = (
    0, 512, 1280, 1792, 2304, 2816, 3328, 5376, 5632)


def _pack_w_in(w_in):
    wt = w_in.T
    o = np.cumsum((0, 512, 768, 24, 512, 512, 512, 512, 2048))
    q, kv, ga, qr, kr, vr, gr, gm = [wt[o[i]:o[i + 1]] for i in range(8)]
    ga = ga.reshape(3, G_A, HPG, D_MODEL).transpose(1, 0, 2, 3).reshape(G_A, 3 * HPG, D_MODEL)
    ga = jnp.pad(ga, ((0, 0), (0, LANE - 3 * HPG), (0, 0))).reshape(G_A * LANE, D_MODEL)
    return jnp.concatenate([q, kv, qr, kr, vr, gr, gm, ga], axis=0).astype(BF)


def _inproj_body(x_ref, nw_ref, sh_ref, sc_ref, w_ref, rq_ref, rk_ref, rr_ref,
                 q_ref, qr_ref, kvc_ref, kvs_ref, kvw_ref, kvb_ref, ret_ref, gr_ref, gm_ref, ga_ref, *, kv_t):
    tm = x_ref.shape[0]
    np_ = tm // _ROW_PART if tm > _ROW_PART else 1
    pr = tm // np_
    parts = [slice(k * pr, (k + 1) * pr) for k in range(np_)]
    lo64 = lax.broadcasted_iota(jnp.int32, (pr, LANE), 1) < DH

    def put_kv(out_ref, r, k, v):
        if kv_t:
            out_ref[0:LANE, r] = k.T
            out_ref[LANE:2 * LANE, r] = v.T
        else:
            out_ref[r, 0:LANE] = k
            out_ref[r, LANE:2 * LANE] = v

    def mod(ref, r):
        return ref[...] if ref.shape[0] == 1 else ref[r, :]

    hbs = []
    for r in parts:
        x = x_ref[r, :]
        ms = jnp.mean(x * x, axis=-1, keepdims=True)
        h = (x * lax.rsqrt(ms + EPS)) * nw_ref[...]
        hbs.append((h * (1.0 + mod(sc_ref, r)) + mod(sh_ref, r)).astype(BF))

    def mm(k, lo, hi):
        return _nt(hbs[k], w_ref[lo:hi, :])

    def rope(xc, tab_ref, r):
        return (xc * tab_ref[r, 0:LANE] + pltpu.roll(xc, 8, 1) * tab_ref[r, LANE:2 * LANE]
                + pltpu.roll(xc, LANE - 8, 1) * tab_ref[r, 2 * LANE:3 * LANE])

    gm_half = (_C_GA - _C_GM) // 2
    for k, r in enumerate(parts):
        gm_ref[r, 0:gm_half] = _sigmoid(mm(k, _C_GM, _C_GM + gm_half)).astype(BF)
    for k, r in enumerate(parts):
        g = mm(k, _C_GR, _C_GM)
        gr_ref[r, :] = (g * _sigmoid(g)).astype(BF)
    for k, r in enumerate(parts):
        gm_ref[r, gm_half:2 * gm_half] = _sigmoid(mm(k, _C_GM + gm_half, _C_GA)).astype(BF)

    for k, r in enumerate(parts):
        qa = mm(k, _C_Q, _C_KV)
        q_ref[r, :] = (qa * Q_SCALE).astype(BF)
        for c in range(4):
            qr_ref[r, c * LANE:(c + 1) * LANE] = rope(qa[:, c * LANE:(c + 1) * LANE], rq_ref, r).astype(BF)

    for k, r in enumerate(parts):
        kv = mm(k, _C_KV, _C_QR)
        put_kv(kvc_ref, r, kv[:, 0:LANE], kv[:, LANE:2 * LANE])
        for kind, out_ref in ((0, kvs_ref), (1, kvw_ref)):
            base = 256 + kind * 256
            kk = rope(kv[:, base:base + LANE], rk_ref, r)
            v = kv[:, base + LANE:base + 2 * LANE]
            put_kv(out_ref, r, kk, v)
            kr_ = pltpu.roll(kk, DH, 1)
            vr_ = pltpu.roll(v, DH, 1)
            kvb_ref[kind, 0, 0, r, :] = jnp.where(lo64, kk, vr_).astype(BF)
            kvb_ref[kind, 0, 1, r, :] = jnp.where(lo64, v, kr_).astype(BF)
            kvb_ref[kind, 1, 0, r, :] = jnp.where(lo64, kr_, v).astype(BF)
            kvb_ref[kind, 1, 1, r, :] = jnp.where(lo64, vr_, kk).astype(BF)

    for k, r in enumerate(parts):
        qr = mm(k, _C_QR, _C_KR)
        kr = mm(k, _C_KR, _C_VR)
        for hh in range(H_R):
            sl = slice(hh * LANE, (hh + 1) * LANE)
            xq = qr[:, sl]
            ret_ref[0, r, sl] = (xq * rr_ref[r, 0:LANE]
                                 + pltpu.roll(xq, DK_R // 2, 1) * rr_ref[r, LANE:2 * LANE]).astype(BF)
            xk = kr[:, sl]
            ret_ref[1, r, sl] = (xk * rr_ref[r, 2 * LANE:3 * LANE]
                                 + pltpu.roll(xk, DK_R // 2, 1) * rr_ref[r, 3 * LANE:4 * LANE]).astype(BF)
    for k, r in enumerate(parts):
        ga_ref[r, :] = _sigmoid(mm(k, _C_GA, _C_END))
    for k, r in enumerate(parts):
        ret_ref[2, r, :] = mm(k, _C_VR, _C_GR).astype(BF)


def _inproj(x2d, nw, sh, sc, w_pack, rq, rk, rr, tm, tab_blocks, per_row_mods, kv_t):
    rows = x2d.shape[0]
    nb = rows // tm
    if kv_t:
        kv_shape = jax.ShapeDtypeStruct((nb // tab_blocks, 256, tab_blocks * tm), F32)
        kv_spec = pl.BlockSpec((None, 256, tm), lambda i: (i // tab_blocks, 0, i % tab_blocks))
    else:
        kv_shape = jax.ShapeDtypeStruct((rows, 256), F32)
        kv_spec = pl.BlockSpec((tm, 256), lambda i: (i, 0))
    if per_row_mods:
        mod_spec = pl.BlockSpec((tm, D_MODEL), lambda i: (i, 0))
    else:
        mod_spec = pl.BlockSpec((None, 1, D_MODEL), lambda i: (i // tab_blocks, 0, 0))
    tab = lambda w: pl.BlockSpec((tm, w), lambda i: (i % tab_blocks, 0))
    row = lambda w: pl.BlockSpec((tm, w), lambda i: (i, 0))
    out_shapes = [
        jax.ShapeDtypeStruct((rows, 512), BF),
        jax.ShapeDtypeStruct((rows, 512), BF),
        kv_shape,
        kv_shape,
        kv_shape,
        jax.ShapeDtypeStruct((2, G_A, 2, rows, LANE), BF),
        jax.ShapeDtypeStruct((3, rows, 512), BF),
        jax.ShapeDtypeStruct((rows, 512), BF),
        jax.ShapeDtypeStruct((rows, 2048), BF),
        jax.ShapeDtypeStruct((rows, 256), F32),
    ]
    out_specs = [row(512), row(512), kv_spec, kv_spec, kv_spec,
                 pl.BlockSpec((2, G_A, 2, tm, LANE), lambda i: (0, 0, 0, i, 0)),
                 pl.BlockSpec((3, tm, 512), lambda i: (0, i, 0)),
                 row(512), row(2048), row(256)]
    return pl.pallas_call(
        functools.partial(_inproj_body, kv_t=kv_t),
        grid=(nb,),
        in_specs=[row(D_MODEL),
                  pl.BlockSpec((1, D_MODEL), lambda i: (0, 0)),
                  mod_spec, mod_spec,
                  pl.BlockSpec((_C_END, D_MODEL), lambda i: (0, 0), pipeline_mode=pl.Buffered(1)),
                  tab(384), tab(384), tab(512)],
        out_specs=out_specs,
        out_shape=out_shapes,
        compiler_params=_cparams(("arbitrary",)),
        name="inproj",
    )(x2d, nw, sh, sc, w_pack, rq, rk, rr)


_BPP = 4
_PAGE = _BPP * L_CMP


def _blockdiag(w):
    z = jnp.zeros_like(w)
    return jnp.concatenate(
        [jnp.concatenate([w if j == i else z for j in range(_BPP)], axis=-1) for i in range(_BPP)], axis=-2)


def _pack_phi_paged(phi_pos_k, phi_k1, phi_k2, phi_pos_v, phi_v1, phi_v2):
    w1, w2, pos = [], [], []
    for p_, a, b_ in ((phi_pos_k, phi_k1, phi_k2), (phi_pos_v, phi_v1, phi_v2)):
        w1.append(_blockdiag(a.astype(BF).reshape(L_CMP, DH, DH).transpose(1, 0, 2)))
        w2.append(_blockdiag(b_.astype(BF)))
        pos.append(jnp.tile(p_.T, (1, _BPP)))
    pos = jnp.stack(pos).reshape(2, DH // 2, 2 * _PAGE)
    w1 = jnp.stack(w1).reshape(2, DH // 2, 2 * _PAGE, _BPP * DH)
    return pos, w1, jnp.stack(w2)


def _compress_slab_rows(load, n_rows, pos_ref, w1_ref, w2_ref):
    acc = [jnp.zeros((G_A * n_rows, _BPP * DH), F32) for _ in range(2)]
    for dp in range(DH // 2):
        for kv in range(2):
            x = jnp.concatenate(
                [jnp.concatenate([load((kv * G_A + g) * DH + 2 * dp + j) for j in range(2)], axis=1)
                 for g in range(G_A)], axis=0)
            acc[kv] = acc[kv] + _nn((x + pos_ref[kv, dp:dp + 1, :]).astype(BF), w1_ref[kv, dp])
    return [_nn(_gelu(acc[kv]).astype(BF), w2_ref[kv]) for kv in range(2)]


def _compress_t_body(src_ref, pos_ref, w1_ref, w2_ref, o_ref, xbuf, sem, *, nseq, n_pages):
    def copy(t):
        b, p = t // n_pages, t % n_pages
        return pltpu.make_async_copy(src_ref.at[b, :, pl.ds(pl.multiple_of(p * _PAGE, _PAGE), _PAGE)],
                                     xbuf.at[:, t, :], sem)

    def start(t, c):
        copy(t).start()
        return c

    def wait(t, c):
        copy(t).wait()
        return c
    lax.fori_loop(0, nseq * n_pages, start, 0)
    lax.fori_loop(0, nseq * n_pages, wait, 0)
    kc, vc = _compress_slab_rows(lambda r: xbuf[r], nseq * n_pages, pos_ref, w1_ref, w2_ref)
    o_ref[0] = kc
    o_ref[1] = vc


def _compress_t(kv_t, pos, w1, w2):
    nseq, SL, T = kv_t.shape
    n_pages = T // _PAGE
    one = lambda a: pl.BlockSpec(a.shape, lambda i: (0,) * a.ndim, pipeline_mode=pl.Buffered(1))
    n_out = G_A * nseq * n_pages
    return pl.pallas_call(
        functools.partial(_compress_t_body, nseq=nseq, n_pages=n_pages),
        grid=(1,),
        in_specs=[pl.BlockSpec(memory_space=pl.ANY), one(pos), one(w1), one(w2)],
        out_specs=pl.BlockSpec((2, n_out, _BPP * DH), lambda i: (0, 0, 0)),
        out_shape=jax.ShapeDtypeStruct((2, n_out, _BPP * DH), F32),
        scratch_shapes=[pltpu.VMEM((SL, nseq * n_pages, _PAGE), F32), pltpu.SemaphoreType.DMA(())],
        compiler_params=_cparams(("arbitrary",)),
        name="compress_t",
    )(kv_t, pos, w1, w2)


_KT = 256


def _nsa_prompt_body(q_ref, qr_ref, kvb_ref, kc_ref, vc_ref, ga_ref, e_ref, gx_ref, o_ref, os_scr, *, nc, ns):
    i = pl.program_id(1)
    QB = Q_BLOCK
    lane = lax.broadcasted_iota(jnp.int32, (QB, LANE), 1)
    lo64 = lane < DH
    zero_b = jnp.zeros((QB, LANE), BF)
    groups = range(G_A)

    def split_heads(ref, g):
        ev, od = [], []
        for c in range(2):
            xc = ref[:, (2 * g + c) * LANE:(2 * g + c + 1) * LANE]
            ev.append(jnp.where(lo64, xc, zero_b))
            od.append(jnp.where(lo64, zero_b, xc))
        return jnp.concatenate(ev, axis=0), jnp.concatenate(od, axis=0)

    tq_l = i * QB + lax.broadcasted_iota(jnp.int32, (nc, QB), 1)
    r_c = lax.broadcasted_iota(jnp.int32, (nc, QB), 0)
    half = nc // 2
    blk_c = jnp.where(r_c < half, 2 * r_c, 2 * (r_c - half) + 1)
    cmask = (blk_c * L_CMP + (L_CMP - 1)) <= tq_l
    cmask_f = cmask.astype(F32)
    tq_s = i * QB + lax.broadcasted_iota(jnp.int32, (ns, QB), 1)
    blk_s = lax.broadcasted_iota(jnp.int32, (ns, QB), 0)
    valid = (blk_s * L_SEL) <= tq_s
    forced = (blk_s == 0) | (blk_s == tq_s // L_SEL)
    n_top = min(N_SEL, ns)

    def select(imp):
        score = jnp.where(valid, imp + jnp.where(forced, FORCE_BONUS, 0.0), NEG)
        rank = jnp.zeros((ns, QB), F32)
        for b2 in range(ns):
            row = score[b2:b2 + 1, :]
            rank = rank + jnp.where(blk_s > b2, jnp.where(row >= score, 1.0, 0.0), jnp.where(row > score, 1.0, 0.0))
        return jnp.where((rank < n_top) & (score > 0.5 * NEG), 1.0, 0.0)

    def score_stage(chains):
        scored = []
        for qs, kmat, _, bias in chains:
            s = _nt(qs, kmat)
            s = (s.reshape(2, QB, s.shape[1]) + bias[None]).reshape(s.shape)
            scored.append((s, jnp.max(s, axis=1, keepdims=True)))
        return scored

    def value_stage(scored, chains):
        outs = []
        for (s, m), (_, _, vmat, _) in zip(scored, chains):
            p = jnp.exp2(s - m)
            outs.append(_nn(p.astype(BF), vmat) / jnp.sum(p, axis=1, keepdims=True))
        return outs

    def masked_attend(chains):
        return value_stage(score_stage(chains), chains)


    qr = [split_heads(qr_ref, g) for g in groups]
    n_tiles = (i * QB + QB + _KT - 1) // _KT

    cmp_scores = [[_nt(kc_ref[g], qs) for qs in split_heads(q_ref, g)] for g in groups]

    WK = WINDOW + QB
    start = pl.multiple_of(jnp.maximum(i * QB - WINDOW, 0), QB)
    diff = (i * QB + lax.broadcasted_iota(jnp.int32, (QB, WK), 0)) - (start + lax.broadcasted_iota(jnp.int32, (QB, WK), 1))
    wb = jnp.where((diff >= 0) & (diff < WINDOW), 0.0, NEG)
    win_chains = []
    for g in groups:
        wkv = kvb_ref[1, g, 0, pl.ds(start, WK), :]
        wvk = kvb_ref[1, g, 1, pl.ds(start, WK), :]
        win_chains += [(qr[g][0], wkv, wvk, wb), (qr[g][1], wvk, wkv, wb)]
    win_scored = score_stage(win_chains)

    imps, o_cmp = [], []
    for g in groups:
        vc = vc_ref[g]
        imp = jnp.zeros((ns, QB), F32)
        oc = {}
        for stack in range(2):
            st = cmp_scores[g][stack]
            for c in range(2):
                s = jnp.where(cmask, st[:, c * QB:(c + 1) * QB], NEG)
                p = jnp.exp2(s - jnp.max(s, axis=0, keepdims=True)) * cmask_f
                p = p / jnp.maximum(jnp.sum(p, axis=0, keepdims=True), 1e-30)
                imp = imp + p[0:half] + p[half:nc]
                oc[(c, stack)] = _tn(p.astype(BF), vc)
        o_cmp.append(oc)
        imps.append(imp)

    ow = value_stage(win_scored, win_chains)
    sel_b = [select(imp).astype(BF) for imp in imps]

    gates = []
    for g in groups:
        ga = ga_ref[:, g * LANE:(g + 1) * LANE]
        hi = ga.astype(BF)
        lo = (ga - hi.astype(F32)).astype(BF)
        gates.append(_nn(jnp.concatenate([hi, lo], axis=1), gx_ref[...]))

    def gate_of(g, br, c):
        j = br * 2 + c
        return gates[g][:, j * LANE:(j + 1) * LANE]

    part = [[gate_of(g, 0, c) * jnp.where(lo64, o_cmp[g][(c, 0)], o_cmp[g][(c, 1)])
             + gate_of(g, 2, c) * jnp.where(lo64, ow[2 * g][c * QB:(c + 1) * QB], ow[2 * g + 1][c * QB:(c + 1) * QB])
             for c in range(2)] for g in groups]


    def slc_variant(nk):
        tq_r = i * QB + lax.broadcasted_iota(jnp.int32, (QB, _KT), 0)
        causal = (nk - _KT + lax.broadcasted_iota(jnp.int32, (QB, _KT), 1)) <= tq_r
        chains = []
        for g in groups:
            bias = _tn(sel_b[g], e_ref[:, 0:nk]) - MASK_BIG
            tail = jnp.where(causal, bias[:, nk - _KT:], -MASK_BIG)
            bias = tail if nk == _KT else jnp.concatenate([bias[:, :nk - _KT], tail], axis=1)
            kv = kvb_ref[0, g, 0, 0:nk, :]
            vk = kvb_ref[0, g, 1, 0:nk, :]
            chains.append((qr[g][0], kv, vk, bias))
            chains.append((qr[g][1], vk, kv, bias))
        for c, o in enumerate(masked_attend(chains)):
            os_scr[c // 2, c % 2] = o

    for k in range(1, kvb_ref.shape[3] // _KT + 1):
        pl.when(n_tiles == k)(functools.partial(slc_variant, k * _KT))

    for g in groups:
        os_e = os_scr[g, 0]
        os_o = os_scr[g, 1]
        for c in range(2):
            rows = slice(c * QB, (c + 1) * QB)
            acc = part[g][c] + gate_of(g, 1, c) * jnp.where(lo64, os_e[rows], os_o[rows])
            o_ref[:, (2 * g + c) * LANE:(2 * g + c + 1) * LANE] = acc.astype(BF)


def _gate_expand():
    gx = np.zeros((LANE, 3 * 2 * LANE), np.float32)
    for br in range(3):
        for c in range(2):
            j = br * 2 + c
            gx[br * HPG + 2 * c, j * LANE:j * LANE + DH] = 1.0
            gx[br * HPG + 2 * c + 1, j * LANE + DH:(j + 1) * LANE] = 1.0
    return jnp.asarray(np.concatenate([gx, gx], axis=0), BF)


def _sel_expand(ns, nkeys):
    e = (np.arange(nkeys)[None, :] // L_SEL) == np.arange(ns)[:, None]
    return jnp.asarray(e * MASK_BIG, BF)


def _nsa_prompt(q, qr, kvb, kc2, vc2, ga, B, T):
    nqb = T // Q_BLOCK
    nc, ns = T // L_CMP, T // L_SEL
    assert T >= WINDOW + Q_BLOCK and T % _KT == 0
    qspec = pl.BlockSpec((Q_BLOCK, 512), lambda b, i: (b * nqb + i, 0))
    cspec = pl.BlockSpec((None, G_A, nc, LANE), lambda b, i: (b, 0, 0, 0))
    return pl.pallas_call(
        functools.partial(_nsa_prompt_body, nc=nc, ns=ns),
        grid=(B, nqb),
        in_specs=[qspec, qspec,
                  pl.BlockSpec((2, G_A, 2, T, LANE), lambda b, i: (0, 0, 0, b, 0)),
                  cspec, cspec,
                  pl.BlockSpec((Q_BLOCK, G_A * LANE), lambda b, i: (b * nqb + i, 0)),
                  pl.BlockSpec((ns, T), lambda b, i: (0, 0)),
                  pl.BlockSpec((2 * LANE, 6 * LANE), lambda b, i: (0, 0))],
        out_specs=qspec,
        out_shape=jax.ShapeDtypeStruct((B * T, 512), BF),
        scratch_shapes=[pltpu.VMEM((G_A, 2, 2 * Q_BLOCK, LANE), F32)],
        compiler_params=_cparams(("arbitrary", "arbitrary")),
        name="nsa_prompt",
    )(q, qr, kvb, kc2, vc2, ga, _sel_expand(ns, T), _gate_expand())


def _ret_body(qkv_ref, gr_ref, s0_ref, dm_ref, dq_ref, wk_ref, gc_ref, gnw_ref, z_ref, sout_ref, s_scr, *, C, sb):
    c = pl.program_id(1)

    @pl.when(c == 0)
    def _():
        s_scr[...] = s0_ref[...]

    units = [(j, h, j, slice(h * LANE, (h + 1) * LANE)) for j in range(sb) for h in range(H_R)]
    qkv = [tuple(qkv_ref[t, rows, :, sl] for t in range(3)) for _, _, rows, sl in units]
    inner = [_nt(q, k) * dm_ref[h] for (q, k, _), (_, h, _, _) in zip(qkv, units)]
    cross = [_nn(q, s_scr[j, h].astype(BF)) * dq_ref[h] for (q, _, _), (j, h, _, _) in zip(qkv, units)]
    outs = [_nn(a.astype(BF), v) + c for a, c, (_, _, v) in zip(inner, cross, qkv)]
    for (_, k, v), (j, h, _, _) in zip(qkv, units):
        kw = (k.astype(F32) * wk_ref[h]).astype(BF)
        s_new = gc_ref[h, 0:1, :] * s_scr[j, h] + _tn(kw, v)
        s_scr[j, h] = s_new
        sout_ref[j, h] = s_new
    for o, (_, _, rows, sl) in zip(outs, units):
        mu = jnp.mean(o, axis=-1, keepdims=True)
        d = o - mu
        var = jnp.mean(d * d, axis=-1, keepdims=True)
        on = d * lax.rsqrt(var + EPS) * gnw_ref[:, sl]
        z_ref[rows, :, sl] = (gr_ref[rows, :, sl].astype(F32) * on).astype(BF)


def _retention(ret3, gr, s0, gnw, nseq, rows_per_seq, C, c_true, sb):
    nC = rows_per_seq // C
    rows = nseq * rows_per_seq
    width = H_R * LANE
    dm, dq, wk, gc = _ret_decay(C, c_true)
    full = lambda a: pl.BlockSpec(a.shape, lambda b, c: (0,) * a.ndim)
    z, s_new = pl.pallas_call(
        functools.partial(_ret_body, C=C, sb=sb),
        grid=(nseq // sb, nC),
        in_specs=[pl.BlockSpec((3, sb, C, width), lambda b, c: (0, b, c, 0)),
                  pl.BlockSpec((sb, C, width), lambda b, c: (b, c, 0)),
                  pl.BlockSpec((sb, H_R, DK_R, DV_R), lambda b, c: (b, 0, 0, 0)),
                  full(dm), full(dq), full(wk), full(gc), full(gnw)],
        out_specs=[pl.BlockSpec((sb, C, width), lambda b, c: (b, c, 0)),
                   pl.BlockSpec((sb, H_R, DK_R, DV_R), lambda b, c: (b, 0, 0, 0))],
        out_shape=[jax.ShapeDtypeStruct((nseq, rows_per_seq, width), BF),
                   jax.ShapeDtypeStruct((nseq, H_R, DK_R, DV_R), F32)],
        scratch_shapes=[pltpu.VMEM((sb, H_R, DK_R, DV_R), F32)],
        compiler_params=_cparams(("arbitrary", "arbitrary")),
        name="retention",
    )(ret3.reshape(3, nseq, rows_per_seq, width), gr.reshape(nseq, rows_per_seq, width), s0, dm, dq, wk, gc, gnw)
    return z.reshape(rows, width), s_new


def _mix_ffn_body(x_ref, oa_ref, zr_ref, gm_ref, gt1_ref, wa_ref, wr_ref, wo_ref,
                  nw_ref, sh_ref, sc_ref, gt_ref, wu_ref, cw_ref, cb_ref, wd_ref, nf_ref, p1_ref, p2_ref,
                  y_ref, a_ref, carry, *, blocks_per_seq, seq_rows):
    i = pl.program_id(0)
    tm = x_ref.shape[0]
    parts = [slice(k * _ROW_PART, (k + 1) * _ROW_PART) for k in range(tm // _ROW_PART)] if tm > _ROW_PART else [slice(0, tm)]

    def mod(ref, r):
        return ref[...] if ref.shape[0] == 1 else ref[r, :]

    ya = [_nn(oa_ref[r, :], wa_ref[...]) for r in parts]
    yr = [_nn(zr_ref[r, :], wr_ref[...]) for r in parts]
    xs = []
    for r, ya_k, yr_k in zip(parts, ya, yr):
        gm = gm_ref[r, :].astype(F32)
        merged = gm[:, 0:D_MODEL] * ya_k + gm[:, D_MODEL:2 * D_MODEL] * yr_k
        xs.append(x_ref[r, :] + mod(gt1_ref, r) * _nn(merged.astype(BF), wo_ref[...]))
    hs = []
    for r, x in zip(parts, xs):
        ms = jnp.mean(x * x, axis=-1, keepdims=True)
        h = (x * lax.rsqrt(ms + EPS)) * nw_ref[...]
        hs.append((h * (1.0 + mod(sc_ref, r)) + mod(sh_ref, r)).astype(BF))
    a_parts = [_nn(h, wu_ref[:, 0:D_FF]) for h in hs]
    b_parts = [_nn(h, wu_ref[:, D_FF:2 * D_FF]) for h in hs]
    a = a_parts[0] if len(parts) == 1 else jnp.concatenate(a_parts, axis=0)
    a_ref[...] = a[tm - a_ref.shape[0]:tm, :]
    rid = lax.broadcasted_iota(jnp.int32, (tm, D_FF), 0)
    if seq_rows is None:
        first = (i % blocks_per_seq) == 0
        prev = jnp.where(first, p1_ref[...], carry[...])
        carry[...] = a[tm - 8:tm, :]
        am1 = jnp.where(rid == 0, prev[7:8, :], pltpu.roll(a, 1, 0))
        am2 = jnp.where(rid == 0, prev[6:7, :], jnp.where(rid == 1, prev[7:8, :], pltpu.roll(a, 2, 0)))
    else:
        s = rid % seq_rows
        am1 = jnp.where(s == 0, p1_ref[...], pltpu.roll(a, 1, 0))
        am2 = jnp.where(s <= 1, p2_ref[...], pltpu.roll(a, 2, 0))
    u = cb_ref[...] + am2 * cw_ref[0:1, :] + am1 * cw_ref[1:2, :] + a * cw_ref[2:3, :]
    ffs = [_nn((_gelu(u[r]) * b_k).astype(BF), wd_ref[...]) for r, b_k in zip(parts, b_parts)]
    for r, x, ff in zip(parts, xs, ffs):
        x2 = x + mod(gt_ref, r) * ff
        ms2 = jnp.mean(x2 * x2, axis=-1, keepdims=True)
        y_ref[r, :] = (x2 * lax.rsqrt(ms2 + EPS)) * nf_ref[...]


def _mix_ffn(x2d, oa, zr, gm, gt1, wa, wr, wo, nw, sh, sc, gt, wu, cw, cb, wd, nf, p1, p2,
             tm, blocks_per_seq, per_row_mods, seq_rows):
    rows = x2d.shape[0]
    a_rows = tm if seq_rows is not None else 8
    if per_row_mods:
        mod_spec = pl.BlockSpec((tm, D_MODEL), lambda i: (i, 0))
    else:
        mod_spec = pl.BlockSpec((None, 1, D_MODEL), lambda i: (i // blocks_per_seq, 0, 0))
    row = lambda w: pl.BlockSpec((tm, w), lambda i: (i, 0))
    full = lambda a, b: pl.BlockSpec((a, b), lambda i: (0, 0), pipeline_mode=pl.Buffered(1))
    vec = lambda w: pl.BlockSpec((1, w), lambda i: (0, 0))
    pspec = pl.BlockSpec(p1.shape, lambda i: (0, 0))
    return pl.pallas_call(
        functools.partial(_mix_ffn_body, blocks_per_seq=blocks_per_seq, seq_rows=seq_rows),
        grid=(rows // tm,),
        in_specs=[row(D_MODEL), row(512), row(512), row(2048), mod_spec,
                  full(512, D_MODEL), full(512, D_MODEL), full(D_MODEL, D_MODEL),
                  vec(D_MODEL), mod_spec, mod_spec, mod_spec,
                  full(D_MODEL, 2 * D_FF), pl.BlockSpec((CONV_W, D_FF), lambda i: (0, 0)), vec(D_FF),
                  full(D_FF, D_MODEL), vec(D_MODEL), pspec, pspec],
        out_specs=[row(D_MODEL), pl.BlockSpec((a_rows, D_FF), lambda i: (i, 0))],
        out_shape=[jax.ShapeDtypeStruct((rows, D_MODEL), F32),
                   jax.ShapeDtypeStruct((rows // tm * a_rows, D_FF), F32)],
        scratch_shapes=[pltpu.VMEM((8, D_FF), F32)],
        compiler_params=_cparams(("arbitrary",)),
        name="mix_ffn",
    )(x2d, oa, zr, gm, gt1, wa, wr, wo, nw, sh, sc, gt, wu, cw, cb, wd, nf, p1, p2)


def _prompt_path(x_prompt, mods_p, W):
    B, T, _ = x_prompt.shape
    rows = B * T
    x2d = x_prompt.reshape(rows, D_MODEL)
    pos = np.arange(T)
    tm = 2 * _ROW_PART if T % (2 * _ROW_PART) == 0 else _ROW_PART
    sh1, sc1, gt1, sh2, sc2, gt2 = [mods_p[:, None, j * D_MODEL:(j + 1) * D_MODEL] for j in range(6)]
    (q, qr, kvc, kvs, kvw, kvb, ret3, gr, gm, ga) = _inproj(
        x2d, W["norm1"], sh1, sc1, W["w_in"], _rope_tab(pos, Q_SCALE), _rope_tab(pos, 1.0), _ret_tab(pos),
        tm, T // tm, False, True)
    nc, n_pages = T // L_CMP, T // _PAGE
    comp = _compress_t(kvc, W["phi_posT"], W["phi_w1p"], W["phi_w2p"])
    comp = comp.reshape(2, G_A, B, n_pages, 2, 2, DH).transpose(0, 2, 1, 5, 3, 4, 6).reshape(2, B, G_A, nc, DH)
    comp2 = jnp.concatenate([comp, comp], axis=-1).astype(BF)
    oa = _nsa_prompt(q, qr, kvb, comp2[0], comp2[1], ga, B, T)
    C = 256 if T % 256 == 0 else T
    zr, ret_new = _retention(ret3, gr, jnp.zeros((B, H_R, DK_R, DV_R), F32), W["gnw"], B, T, C, C,
                             4 if B % 4 == 0 else 1)
    zeros8 = jnp.zeros((8, D_FF), F32)
    tf = tm
    y, a_up = _mix_ffn(x2d, oa, zr, gm, gt1, W["w_up_a"], W["w_up_r"], W["w_out"],
                       W["norm2"], sh2, sc2, gt2, W["w_ffn_up"], W["conv_w"], W["conv_b"], W["w_ffn_down"],
                       W["normf"], zeros8, zeros8, tf, T // tf, False, None)
    wsz = min(WINDOW, T)
    rows_major = lambda t: t.reshape(B, 2, G_A, DH, t.shape[-1]).transpose(0, 4, 1, 2, 3)[None]
    outs = dict(
        y=y.reshape(B, T, D_MODEL),
        cmp=rows_major(kvc), slc=rows_major(kvs), win=rows_major(kvw[:, :, T - wsz:]),
        ret=ret_new[None],
        conv=a_up.reshape(B, T // tf, 8, D_FF)[None, :, T // tf - 1, 8 - (CONV_W - 1):],
    )
    return outs


_QC = 32


def _cmp_paged_body(pt_ref, cache_ref, q_ref, pos_ref, w1_ref, w2_ref, ocmp_ref, topi_ref, xbuf, sem,
                    *, n_pages, n_pick, n_q):
    b = pl.program_id(0)
    nb = pl.num_programs(0)
    slot = b % 2

    def copy(bb, sl, p):
        return pltpu.make_async_copy(cache_ref.at[pt_ref[bb, p]], xbuf.at[sl, :, p, :], sem.at[sl])

    def issue(bb, sl):
        for p in range(n_pages):
            copy(bb, sl, p).start()

    @pl.when(b == 0)
    def _():
        issue(b, slot)

    @pl.when(b + 1 < nb)
    def _():
        issue(b + 1, 1 - slot)

    for p in range(n_pages):
        copy(b, slot, p).wait()

    comp = _compress_slab_rows(lambda r: xbuf[slot, r], n_pages, pos_ref, w1_ref, w2_ref)

    def lane_groups(x, op):
        r = x
        for j in range(1, _BPP):
            r = op(r, pltpu.roll(x, j * _QC, 1))
        return r

    grp = range(G_A)
    sts = [_nn(comp[0][g * n_pages:(g + 1) * n_pages].astype(BF), q_ref[g]) for g in grp]
    ps = []
    for g in grp:
        m = lane_groups(jnp.max(sts[g], axis=0, keepdims=True), jnp.maximum)
        p = jnp.exp2(sts[g] - m)
        ps.append(p / lane_groups(jnp.sum(p, axis=0, keepdims=True), jnp.add))
    for g in grp:
        r_full = _tn(ps[g].astype(BF), comp[1][g * n_pages:(g + 1) * n_pages].astype(BF))
        o = r_full[0:_QC, 0:DH]
        for j in range(1, _BPP):
            o = o + r_full[j * _QC:(j + 1) * _QC, j * DH:(j + 1) * DH]
        ocmp_ref[g] = o
    imp = None
    for g in grp:
        pair = ps[g] + pltpu.roll(ps[g], LANE - _QC, 1)
        t = pair
        for r in range(1, HPG):
            t = t + pltpu.roll(pair, LANE - r * n_q, 1)
        if g:
            t = pltpu.roll(t, g * n_q, 1)
        lane_g = lax.broadcasted_iota(jnp.int32, t.shape, 1) % (2 * _QC)
        t = jnp.where((lane_g >= g * n_q) & (lane_g < (g + 1) * n_q), t, 0.0)
        imp = t if imp is None else imp + t
    sc = jnp.concatenate([imp, pltpu.roll(imp, 2 * _QC, 1)], axis=0)
    row = lax.broadcasted_iota(jnp.int32, (2 * n_pages, LANE), 0)
    blk = jnp.where(row < n_pages, 2 * row, 2 * (row - n_pages) + 1)
    score = sc + jnp.where(blk == 0, FORCE_BONUS, 0.0)
    topi_ref[...] = jnp.zeros((8, LANE), jnp.int32)
    for k in range(n_pick):
        mx = jnp.max(score, axis=0, keepdims=True)
        idx = jnp.min(jnp.where(score == mx, blk, 2 * n_pages), axis=0, keepdims=True)
        topi_ref[k:k + 1, :] = idx
        score = jnp.where(blk == idx, -jnp.inf, score)


def _cmp_paged(page_table, cache_t, q_bd, pos, w1, w2, n_q):
    DB, n_pages = page_table.shape
    SL = cache_t.shape[1]
    page_rows = cache_t.shape[2]
    kern = functools.partial(_cmp_paged_body, n_pages=n_pages, n_pick=N_SEL - 1, n_q=n_q)
    one = dict(pipeline_mode=pl.Buffered(1))
    return pl.pallas_call(
        kern,
        grid_spec=pltpu.PrefetchScalarGridSpec(
            num_scalar_prefetch=1, grid=(DB,),
            in_specs=[pl.BlockSpec(memory_space=pl.ANY),
                      pl.BlockSpec((None, G_A, _BPP * DH, LANE), lambda b, pt: (b, 0, 0, 0)),
                      pl.BlockSpec(pos.shape, lambda b, pt: (0, 0, 0), **one),
                      pl.BlockSpec(w1.shape, lambda b, pt: (0, 0, 0, 0), **one),
                      pl.BlockSpec((2, _BPP * DH, _BPP * DH), lambda b, pt: (0, 0, 0), **one)],
            out_specs=[pl.BlockSpec((None, G_A, _QC, DH), lambda b, pt: (b, 0, 0, 0)),
                       pl.BlockSpec((None, 8, LANE), lambda b, pt: (b, 0, 0))],
            scratch_shapes=[pltpu.VMEM((2, SL, n_pages, page_rows), F32),
                            pltpu.SemaphoreType.DMA((2,))]),
        out_shape=[jax.ShapeDtypeStruct((DB, G_A, _QC, DH), F32),
                   jax.ShapeDtypeStruct((DB, 8, LANE), jnp.int32)],
        compiler_params=_cparams(("arbitrary",)),
        name="cmp_paged",
    )(page_table, cache_t, q_bd, pos, w1, w2)


def _slc_win_paged_body(pt_ref, ti_ref, cslc_ref, win_ref, q_ref, tiv_ref, ns_ref, nw_ref, ex_ref,
                        oslc_ref, owin_ref, kbuf, sem, *, n_q, n_pick, page_rows):
    b = pl.program_id(0)
    nb = pl.num_programs(0)
    slot = b % 2
    n_slab = n_q * n_pick
    bpp = page_rows // L_SEL
    wb = win_ref.shape[1]

    def copies(bb, sl):
        cps = []
        for g in range(G_A):
            for j in range(n_slab):
                page = pt_ref[bb, ti_ref[bb, g * n_slab + j] // bpp]
                for kv in range(2):
                    cps.append(pltpu.make_async_copy(
                        cslc_ref.at[page, pl.ds((kv * G_A + g) * DH, DH), :],
                        kbuf.at[sl, g, kv, :, pl.ds(j * page_rows, page_rows)], sem.at[sl]))
        return cps

    @pl.when(b == 0)
    def _():
        for cp in copies(b, slot):
            cp.start()

    @pl.when(b + 1 < nb)
    def _():
        for cp in copies(b + 1, 1 - slot):
            cp.start()

    for cp in copies(b, slot):
        cp.wait()

    nq_rows = q_ref.shape[1]
    nk = n_slab * page_rows
    row_q = lax.broadcasted_iota(jnp.int32, (nq_rows, nk), 0) % n_q
    col = lax.broadcasted_iota(jnp.int32, (nq_rows, nk), 1)
    own = row_q == col // (n_pick * page_rows)
    half = ((col % page_rows) // L_SEL).astype(F32)
    nnew = ns_ref.shape[2]
    new_ok = (lax.broadcasted_iota(jnp.int32, (nq_rows, nnew), 1)
              <= lax.broadcasted_iota(jnp.int32, (nq_rows, nnew), 0) % n_q)
    dwin = (wb + lax.broadcasted_iota(jnp.int32, (nq_rows, wb), 0) % n_q
            - lax.broadcasted_iota(jnp.int32, (nq_rows, wb), 1))
    win_ok = (dwin >= 0) & (dwin < WINDOW)

    chains = []
    for g in range(G_A):
        q = q_ref[g]
        par = (tiv_ref[g] % bpp).astype(F32).astype(BF)
        want = _nn(par, ex_ref[...])[0:1, :]
        chains.append((q, kbuf[slot, g, 0].astype(BF), kbuf[slot, g, 1].astype(BF), own & (half == want),
                       ns_ref[g, 0].astype(BF), ns_ref[g, 1].astype(BF), oslc_ref, g))
        kw = win_ref[pl.ds((0 * G_A + g) * DH, DH), :].astype(BF)
        vw = win_ref[pl.ds((1 * G_A + g) * DH, DH), :].astype(BF)
        chains.append((q, kw, vw, win_ok, nw_ref[g, 0].astype(BF), nw_ref[g, 1].astype(BF), owin_ref, g))
    scores = [(jnp.where(ok_old, _nn(q, kt_old), NEG), jnp.where(new_ok, _nt(q, k_new), NEG))
              for q, kt_old, _, ok_old, k_new, _, _, _ in chains]
    probs = []
    for s_o, s_n in scores:
        m = jnp.maximum(jnp.max(s_o, axis=1, keepdims=True), jnp.max(s_n, axis=1, keepdims=True))
        p_o = jnp.exp2(s_o - m)
        p_n = jnp.exp2(s_n - m)
        probs.append((p_o, p_n, jnp.sum(p_o, axis=1, keepdims=True) + jnp.sum(p_n, axis=1, keepdims=True)))
    for (p_o, p_n, den), (_, _, vt_old, _, _, v_new, out_ref, g) in zip(probs, chains):
        out_ref[g] = (_nt(p_o.astype(BF), vt_old) + _nn(p_n.astype(BF), v_new)) / den


def _slc_win_paged(page_table, topi_flat, topi_vec, cache_t, win_t, q_rot, new_slc, new_win, n_q):
    DB = page_table.shape[0]
    n_pick = N_SEL - 1
    page_rows = cache_t.shape[2]
    wb = win_t.shape[2]
    nq_rows = q_rot.shape[2]
    n_slab = n_q * n_pick
    ex = (np.arange(n_slab * page_rows)[None, :] // page_rows) == np.arange(LANE)[:, None]
    kern = functools.partial(_slc_win_paged_body, n_q=n_q, n_pick=n_pick, page_rows=page_rows)
    bspec = lambda shp: pl.BlockSpec((None,) + shp, lambda b, pt, ti: (b,) + (0,) * len(shp))
    return pl.pallas_call(
        kern,
        grid_spec=pltpu.PrefetchScalarGridSpec(
            num_scalar_prefetch=2, grid=(DB,),
            in_specs=[pl.BlockSpec(memory_space=pl.ANY), bspec((2 * G_A * DH, wb)),
                      bspec((G_A, nq_rows, DH)), bspec((G_A, 16, LANE)),
                      bspec((G_A, 2, 16, DH)), bspec((G_A, 2, 16, DH)),
                      pl.BlockSpec((LANE, n_slab * page_rows), lambda b, pt, ti: (0, 0))],
            out_specs=[bspec((G_A, nq_rows, DH)), bspec((G_A, nq_rows, DH))],
            scratch_shapes=[pltpu.VMEM((2, G_A, 2, DH, n_slab * page_rows), F32),
                            pltpu.SemaphoreType.DMA((2,))]),
        out_shape=[jax.ShapeDtypeStruct((DB, G_A, nq_rows, DH), F32)] * 2,
        compiler_params=_cparams(("arbitrary",)),
        name="slc_win_paged",
    )(page_table, topi_flat, cache_t, win_t, q_rot, topi_vec, new_slc, new_win, jnp.asarray(ex, BF))


def _gate_sample_body(oc_ref, os_ref, ow_ref, ga_ref, o_ref):
    rows = o_ref.shape[0]
    lo64 = lax.broadcasted_iota(jnp.int32, (rows, LANE), 1) < DH
    for c4 in range(H_A // 2):
        g, c = c4 // 2, c4 % 2
        sl = slice(c4 * LANE, (c4 + 1) * LANE)
        acc = jnp.zeros((rows, LANE), F32)
        for br, ref in enumerate((oc_ref, os_ref, ow_ref)):
            col = g * LANE + br * HPG + 2 * c
            gate = jnp.where(lo64, ga_ref[:, col:col + 1], ga_ref[:, col + 1:col + 2])
            acc = acc + gate * ref[:, sl]
        o_ref[:, sl] = acc.astype(BF)


def _gate_sample(oc, osl, ow, ga):
    rows = oc.shape[0]
    full = lambda w: pl.BlockSpec((rows, w), lambda i: (0, 0))
    return pl.pallas_call(
        _gate_sample_body, grid=(1,),
        in_specs=[full(512), full(512), full(512), full(256)],
        out_specs=full(512),
        out_shape=jax.ShapeDtypeStruct((rows, 512), BF),
        compiler_params=_cparams(("arbitrary",)),
        name="gate_sample",
    )(oc, osl, ow, ga)


def _sample_path(x_sample, mods_s, W, cache_cmp, cache_slc, state_win, state_ret, state_conv, page_table):
    DB, S, _ = x_sample.shape
    rows = DB * S
    page_rows = cache_cmp.shape[1]
    P = page_table.shape[1] * page_rows
    wb = state_win.shape[1]
    assert P % L_SEL == 0 and S < L_CMP and S <= 8 and wb == WINDOW and page_rows % L_SEL == 0
    assert P // L_SEL >= N_SEL and CONV_W == 3 and S >= CONV_W - 1
    pos = P + np.arange(S)
    pos_rows = np.tile(pos, DB)
    x2d = x_sample.reshape(rows, D_MODEL)
    modr = jnp.repeat(mods_s, S, axis=0)
    sh1, sc1, gt1, sh2, sc2, gt2 = [modr[:, j * D_MODEL:(j + 1) * D_MODEL] for j in range(6)]
    (q, qr, kvc, kvs, kvw, _, ret3, gr, gm, ga) = _inproj(
        x2d, W["norm1"], sh1, sc1, W["w_in"], _rope_tab(pos_rows, Q_SCALE), _rope_tab(pos_rows, 1.0),
        _ret_tab(pos_rows), rows, 1, True, False)

    def to_heads(t):
        return t.reshape(DB, S, G_A, HPG, DH).transpose(0, 2, 3, 1, 4).reshape(DB, G_A, HPG * S, DH)

    def from_heads(t):
        return t.reshape(DB, G_A, HPG, S, DH).transpose(0, 3, 1, 2, 4).reshape(rows, H_A * DH)

    assert page_rows == _BPP * L_CMP and HPG * S <= _QC and S * (N_SEL - 1) <= LANE
    slab = lambda t: t.transpose(0, 2, 3, 4, 1).reshape(t.shape[0], 2 * G_A * DH, t.shape[1])
    qt = jnp.pad(to_heads(q).transpose(0, 1, 3, 2), ((0, 0), (0, 0), (0, 0), (0, _QC - HPG * S)))
    q_bd = _blockdiag(qt)
    o_cmp, topi = _cmp_paged(page_table, slab(cache_cmp), q_bd, W["phi_posT"], W["phi_w1p"], W["phi_w2p"], S)
    n_pick = N_SEL - 1
    topi = topi[:, :n_pick, :G_A * S].transpose(0, 2, 1).reshape(DB, G_A, S * n_pick)
    topi_vec = jnp.broadcast_to(jnp.pad(topi, ((0, 0), (0, 0), (0, LANE - S * n_pick)))[:, :, None, :],
                                (DB, G_A, 16, LANE))

    def new_rows(t):
        t = t.reshape(DB, S, 2, G_A, DH).transpose(0, 3, 2, 1, 4)
        return jnp.pad(t, ((0, 0), (0, 0), (0, 0), (0, 16 - S), (0, 0)))

    o_slc, o_win = _slc_win_paged(page_table, topi.reshape(DB, G_A * S * n_pick), topi_vec, slab(cache_slc),
                                  slab(state_win), to_heads(qr), new_rows(kvs), new_rows(kvw), S)
    oa = _gate_sample(from_heads(o_cmp[:, :, :HPG * S]), from_heads(o_slc), from_heads(o_win), ga)

    RP = 16
    padr = lambda t: jnp.pad(t.reshape(t.shape[:-2] + (DB, S, 512)),
                             ((0, 0),) * (t.ndim - 1) + ((0, RP - S), (0, 0))).reshape(t.shape[:-2] + (DB * RP, 512))
    zr, ret_new = _retention(padr(ret3), padr(gr), state_ret, W["gnw"], DB, RP, RP, S, 8 if DB % 8 == 0 else 1)
    zr = zr.reshape(DB, RP, 512)[:, :S].reshape(rows, 512)
    zs = lambda n: jnp.zeros((DB, n, D_FF), F32)
    p1 = jnp.concatenate([state_conv[:, 1:2], zs(S - 1)], axis=1).reshape(rows, D_FF)
    p2 = jnp.concatenate([state_conv[:, 0:2], zs(S - 2)], axis=1).reshape(rows, D_FF)
    y, a_up = _mix_ffn(x2d, oa, zr, gm, gt1, W["w_up_a"], W["w_up_r"], W["w_out"],
                       W["norm2"], sh2, sc2, gt2, W["w_ffn_up"], W["conv_w"], W["conv_b"], W["w_ffn_down"],
                       W["normf"], p1, p2, rows, 1, True, S)
    shp = (1, DB, S, 2, G_A, DH)
    return dict(
        y=y.reshape(DB, S, D_MODEL),
        cmp=kvc.reshape(shp), slc=kvs.reshape(shp),
        win=jnp.concatenate([state_win[:, S:], kvw.reshape(DB, S, 2, G_A, DH)], axis=1)[None],
        ret=ret_new[None],
        conv=a_up.reshape(DB, S, D_FF)[None, :, S - (CONV_W - 1):],
    )


def kernel(x_prompt, x_sample, cache_cmp_kv, cache_slc_kv, state_win_kv, state_ret, state_conv, page_table,
           c_prompt, c_sample, norm1_w, norm2_w, w_ada, b_ada, w_in, phi_pos_k, phi_k1, phi_k2, phi_pos_v,
           phi_v1, phi_v2, w_up_a, ret_gn_w, w_up_r, w_out, w_ffn_up, ffn_conv_w, ffn_conv_b, w_ffn_down, normf_w):
    B = x_prompt.shape[0]
    l = 0
    W = dict(
        norm1=norm1_w[l].reshape(1, D_MODEL), norm2=norm2_w[l].reshape(1, D_MODEL), normf=normf_w.reshape(1, D_MODEL),
        w_in=_pack_w_in(w_in[l]),
        w_up_a=w_up_a[l].astype(BF), w_up_r=w_up_r[l].astype(BF), w_out=w_out[l].astype(BF),
        gnw=ret_gn_w[l].reshape(1, H_R * DV_R),
        w_ffn_up=w_ffn_up[l].astype(BF), conv_w=ffn_conv_w[l], conv_b=ffn_conv_b[l].reshape(1, D_FF),
        w_ffn_down=w_ffn_down[l].astype(BF),
    )
    W["phi_posT"], W["phi_w1p"], W["phi_w2p"] = _pack_phi_paged(
        phi_pos_k[l], phi_k1[l], phi_k2[l], phi_pos_v[l], phi_v1[l], phi_v2[l])
    mods = _mods(jnp.concatenate([c_prompt, c_sample], axis=0), w_ada[l], b_ada[l])
    p = _prompt_path(x_prompt, mods[:B], W)
    s = _sample_path(x_sample, mods[B:], W, cache_cmp_kv[l], cache_slc_kv[l], state_win_kv[l], state_ret[l],
                     state_conv[l], page_table)
    return (p["y"], s["y"], p["cmp"], s["cmp"], p["slc"], s["slc"], p["win"], s["win"],
            p["ret"], s["ret"], p["conv"], s["conv"])
```

```python
import functools

import numpy as np
import jax
import jax.numpy as jnp
from jax import lax
from jax.experimental import pallas as pl
from jax.experimental.pallas import tpu as pltpu

BF = jnp.bfloat16
F32 = jnp.float32

D_MODEL = 1024
H_A, G_A, DH = 8, 2, 64
HPG = H_A // G_A
ROPE_DIM = DH // 4
ROPE_THETA = 500000.0
L_CMP, L_SEL, N_SEL = 32, 64, 8
WINDOW = 512
Q_BLOCK = 128
FORCE_BONUS = 1e4
H_R, DK_R, DV_R = 4, 128, 128
RET_THETA = 10000.0
D_FF = 2816
CONV_W = 3
EPS = 1e-6
NEG = -1e30
MASK_BIG = 2.0 ** 100
Q_SCALE = DH ** -0.5 * 1.4426950408889634
LANE = 128
_ROW_PART = 256
VMEM_LIMIT = 56 * 1024 * 1024


def _cparams(sem):
    return pltpu.CompilerParams(dimension_semantics=sem, vmem_limit_bytes=VMEM_LIMIT)


def _sigmoid(x):
    return 1.0 / (1.0 + jnp.exp(-x))


def _gelu(x):
    return 0.5 * x * (1.0 + jnp.tanh(0.7978845608028654 * (x + 0.044715 * (x * x * x))))


def _nt(a, b):
    return lax.dot_general(a, b, (((1,), (1,)), ((), ())), preferred_element_type=F32)


def _tn(a, b):
    return lax.dot_general(a, b, (((0,), (0,)), ((), ())), preferred_element_type=F32)


def _nn(a, b):
    return jnp.dot(a, b, preferred_element_type=F32)


def _rope_tab(pos, scale):
    half = ROPE_DIM // 2
    inv = ROPE_THETA ** (-np.arange(half, dtype=np.float64) * (2.0 / ROPE_DIM))
    ang = pos.astype(np.float64)[:, None] * inv
    cos, sin = np.cos(ang), np.sin(ang)
    n = pos.shape[0]
    c = np.ones((n, DH)); s_lo = np.zeros((n, DH)); s_hi = np.zeros((n, DH))
    c[:, :half] = cos; c[:, half:ROPE_DIM] = cos
    s_lo[:, half:ROPE_DIM] = sin
    s_hi[:, :half] = -sin
    tab = np.concatenate([np.tile(t, (1, 2)) for t in (c, s_lo, s_hi)], axis=1) * scale
    return jnp.asarray(tab, F32)


def _ret_tab(pos):
    half = DK_R // 2
    inv = RET_THETA ** (-np.arange(half, dtype=np.float64) * (2.0 / DK_R))
    ang = pos.astype(np.float64)[:, None] * inv
    cos, sin = np.cos(ang), np.sin(ang)
    c = np.concatenate([cos, cos], axis=1)
    s = np.concatenate([-sin, sin], axis=1)
    ks = DK_R ** -0.5
    return jnp.asarray(np.concatenate([c, s, c * ks, s * ks], axis=1), F32)


def _ret_decay(C, c_true):
    h = np.arange(H_R, dtype=np.float64)
    log_g = np.log1p(-np.exp2(-5.0 - h))
    i = np.arange(C, dtype=np.float64)
    diff = i[:, None] - i[None, :]
    dm = np.where(diff >= 0, np.exp(log_g[:, None, None] * np.maximum(diff, 0.0)), 0.0)
    dq = np.exp(log_g[:, None] * (i + 1.0))[:, :, None] * np.ones((1, 1, LANE))
    wk = np.exp(log_g[:, None] * (c_true - 1.0 - i))[:, :, None] * np.ones((1, 1, LANE))
    wk = np.where(i[None, :, None] < c_true, wk, 0.0)
    gc = np.exp(log_g * c_true)[:, None, None] * np.ones((1, 8, LANE))
    return (jnp.asarray(dm, F32), jnp.asarray(dq, F32), jnp.asarray(wk, F32), jnp.asarray(gc, F32))


def _mods_body(c_ref, w_ref, b_ref, o_ref):
    c = c_ref[...]
    s = c * _sigmoid(c)
    o_ref[...] = _nn(s.astype(BF), w_ref[...].astype(BF)) + b_ref[...]


def _mods(c_all, w_ada, b_ada):
    n = c_all.shape[0]
    nout = w_ada.shape[1]
    tn = 1024
    return pl.pallas_call(
        _mods_body,
        grid=(nout // tn,),
        in_specs=[pl.BlockSpec((n, D_MODEL), lambda j: (0, 0)),
                  pl.BlockSpec((D_MODEL, tn), lambda j: (0, j)),
                  pl.BlockSpec((1, tn), lambda j: (0, j))],
        out_specs=pl.BlockSpec((n, tn), lambda j: (0, j)),
        out_shape=jax.ShapeDtypeStruct((n, nout), F32),
        compiler_params=_cparams(("arbitrary",)),
        name="mods",
    )(c_all, w_ada, b_ada.reshape(1, nout))


_C_Q, _C_KV, _C_QR, _C_KR, _C_VR, _C_GR, _C_GM, _C_GA, _C_END = (
    0, 512, 1280, 1792, 2304, 2816, 3328, 5376, 5632)


def _pack_w_in(w_in):
    wt = w_in.T
    o = np.cumsum((0, 512, 768, 24, 512, 512, 512, 512, 2048))
    q, kv, ga, qr, kr, vr, gr, gm = [wt[o[i]:o[i + 1]] for i in range(8)]
    ga = ga.reshape(3, G_A, HPG, D_MODEL).transpose(1, 0, 2, 3).reshape(G_A, 3 * HPG, D_MODEL)
    ga = jnp.pad(ga, ((0, 0), (0, LANE - 3 * HPG), (0, 0))).reshape(G_A * LANE, D_MODEL)
    return (wt[0:o[2]].astype(BF), wt[o[3]:o[8]].astype(BF), ga.astype(BF))


def _inproj_body(x_ref, nw_ref, sh_ref, sc_ref, wa_ref, wb_ref, wg_ref, rq_ref, rk_ref, rr_ref,
                 q_ref, qr_ref, kvc_ref, kvs_ref, kvw_ref, kvb_ref, ret_ref, gr_ref, gm_ref, ga_ref, *, kv_t):
    tm = x_ref.shape[0]
    np_ = tm // _ROW_PART if tm > _ROW_PART else 1
    pr = tm // np_
    parts = [slice(k * pr, (k + 1) * pr) for k in range(np_)]
    lo64 = lax.broadcasted_iota(jnp.int32, (pr, LANE), 1) < DH

    def put_kv(out_ref, r, k, v):
        if kv_t:
            out_ref[0:LANE, r] = k.T
            out_ref[LANE:2 * LANE, r] = v.T
        else:
            out_ref[r, 0:LANE] = k
            out_ref[r, LANE:2 * LANE] = v

    def mod(ref, r):
        return ref[...] if ref.shape[0] == 1 else ref[r, :]

    hbs = []
    for r in parts:
        x = x_ref[r, :]
        ms = jnp.mean(x * x, axis=-1, keepdims=True)
        h = (x * lax.rsqrt(ms + EPS)) * nw_ref[...]
        hbs.append((h * (1.0 + mod(sc_ref, r)) + mod(sh_ref, r)).astype(BF))

    def mm(k, lo, hi):
        if hi <= _C_QR:
            w = wa_ref[lo:hi, :]
        elif hi <= _C_GA:
            w = wb_ref[lo - _C_QR:hi - _C_QR, :]
        else:
            w = wg_ref[lo - _C_GA:hi - _C_GA, :]
        return _nt(hbs[k], w)

    def rope(xc, tab_ref, r):
        return (xc * tab_ref[r, 0:LANE] + pltpu.roll(xc, 8, 1) * tab_ref[r, LANE:2 * LANE]
                + pltpu.roll(xc, LANE - 8, 1) * tab_ref[r, 2 * LANE:3 * LANE])

    gm_half = (_C_GA - _C_GM) // 2
    for k, r in enumerate(parts):
        gm_ref[r, 0:gm_half] = _sigmoid(mm(k, _C_GM, _C_GM + gm_half)).astype(BF)
    for k, r in enumerate(parts):
        g = mm(k, _C_GR, _C_GM)
        gr_ref[r, :] = (g * _sigmoid(g)).astype(BF)
    for k, r in enumerate(parts):
        gm_ref[r, gm_half:2 * gm_half] = _sigmoid(mm(k, _C_GM + gm_half, _C_GA)).astype(BF)

    for k, r in enumerate(parts):
        qa = mm(k, _C_Q, _C_KV)
        q_ref[r, :] = (qa * Q_SCALE).astype(BF)
        for c in range(4):
            qr_ref[r, c * LANE:(c + 1) * LANE] = rope(qa[:, c * LANE:(c + 1) * LANE], rq_ref, r).astype(BF)

    for k, r in enumerate(parts):
        kv = mm(k, _C_KV, _C_QR)
        put_kv(kvc_ref, r, kv[:, 0:LANE], kv[:, LANE:2 * LANE])
        for kind, out_ref in ((0, kvs_ref), (1, kvw_ref)):
            base = 256 + kind * 256
            kk = rope(kv[:, base:base + LANE], rk_ref, r)
            v = kv[:, base + LANE:base + 2 * LANE]
            put_kv(out_ref, r, kk, v)
            kr_ = pltpu.roll(kk, DH, 1)
            vr_ = pltpu.roll(v, DH, 1)
            kvb_ref[kind, 0, 0, r, :] = jnp.where(lo64, kk, vr_).astype(BF)
            kvb_ref[kind, 0, 1, r, :] = jnp.where(lo64, v, kr_).astype(BF)
            kvb_ref[kind, 1, 0, r, :] = jnp.where(lo64, kr_, v).astype(BF)
            kvb_ref[kind, 1, 1, r, :] = jnp.where(lo64, vr_, kk).astype(BF)

    for k, r in enumerate(parts):
        qr = mm(k, _C_QR, _C_KR)
        kr = mm(k, _C_KR, _C_VR)
        for hh in range(H_R):
            sl = slice(hh * LANE, (hh + 1) * LANE)
            xq = qr[:, sl]
            ret_ref[0, r, sl] = (xq * rr_ref[r, 0:LANE]
                                 + pltpu.roll(xq, DK_R // 2, 1) * rr_ref[r, LANE:2 * LANE]).astype(BF)
            xk = kr[:, sl]
            ret_ref[1, r, sl] = (xk * rr_ref[r, 2 * LANE:3 * LANE]
                                 + pltpu.roll(xk, DK_R // 2, 1) * rr_ref[r, 3 * LANE:4 * LANE]).astype(BF)
    for k, r in enumerate(parts):
        ga_ref[r, :] = _sigmoid(mm(k, _C_GA, _C_END))
    for k, r in enumerate(parts):
        ret_ref[2, r, :] = mm(k, _C_VR, _C_GR).astype(BF)


def _inproj(x2d, nw, sh, sc, w_pack, rq, rk, rr, tm, tab_blocks, per_row_mods, kv_t):
    rows = x2d.shape[0]
    nb = rows // tm
    if kv_t:
        kv_shape = jax.ShapeDtypeStruct((nb // tab_blocks, 256, tab_blocks * tm), F32)
        kv_spec = pl.BlockSpec((None, 256, tm), lambda i: (i // tab_blocks, 0, i % tab_blocks))
    else:
        kv_shape = jax.ShapeDtypeStruct((rows, 256), F32)
        kv_spec = pl.BlockSpec((tm, 256), lambda i: (i, 0))
    if per_row_mods:
        mod_spec = pl.BlockSpec((tm, D_MODEL), lambda i: (i, 0))
    else:
        mod_spec = pl.BlockSpec((None, 1, D_MODEL), lambda i: (i // tab_blocks, 0, 0))
    tab = lambda w: pl.BlockSpec((tm, w), lambda i: (i % tab_blocks, 0))
    row = lambda w: pl.BlockSpec((tm, w), lambda i: (i, 0))
    out_shapes = [
        jax.ShapeDtypeStruct((rows, 512), BF),
        jax.ShapeDtypeStruct((rows, 512), BF),
        kv_shape,
        kv_shape,
        kv_shape,
        jax.ShapeDtypeStruct((2, G_A, 2, rows, LANE), BF),
        jax.ShapeDtypeStruct((3, rows, 512), BF),
        jax.ShapeDtypeStruct((rows, 512), BF),
        jax.ShapeDtypeStruct((rows, 2048), BF),
        jax.ShapeDtypeStruct((rows, 256), F32),
    ]
    out_specs = [row(512), row(512), kv_spec, kv_spec, kv_spec,
                 pl.BlockSpec((2, G_A, 2, tm, LANE), lambda i: (0, 0, 0, i, 0)),
                 pl.BlockSpec((3, tm, 512), lambda i: (0, i, 0)),
                 row(512), row(2048), row(256)]
    return pl.pallas_call(
        functools.partial(_inproj_body, kv_t=kv_t),
        grid=(nb,),
        in_specs=[row(D_MODEL),
                  pl.BlockSpec((1, D_MODEL), lambda i: (0, 0)),
                  mod_spec, mod_spec,
                  *[pl.BlockSpec(w.shape, lambda i: (0, 0), pipeline_mode=pl.Buffered(1)) for w in w_pack],
                  tab(384), tab(384), tab(512)],
        out_specs=out_specs,
        out_shape=out_shapes,
        compiler_params=_cparams(("arbitrary",)),
        name="inproj",
    )(x2d, nw, sh, sc, *w_pack, rq, rk, rr)


_BPP = 4
_PAGE = _BPP * L_CMP


def _blockdiag(w):
    z = jnp.zeros_like(w)
    return jnp.concatenate(
        [jnp.concatenate([w if j == i else z for j in range(_BPP)], axis=-1) for i in range(_BPP)], axis=-2)


def _pack_phi_paged(phi_pos_k, phi_k1, phi_k2, phi_pos_v, phi_v1, phi_v2):
    w1, w2, pos = [], [], []
    for p_, a, b_ in ((phi_pos_k, phi_k1, phi_k2), (phi_pos_v, phi_v1, phi_v2)):
        w1.append(_blockdiag(a.astype(BF).reshape(L_CMP, DH, DH).transpose(1, 0, 2)))
        w2.append(_blockdiag(b_.astype(BF)))
        pos.append(jnp.tile(p_.T, (1, _BPP)))
    pos = jnp.stack(pos).reshape(2, DH // 2, 2 * _PAGE)
    w1 = jnp.stack(w1).reshape(2, DH // 2, 2 * _PAGE, _BPP * DH)
    return pos, w1, jnp.stack(w2)


def _compress_slab_rows(load, n_rows, pos_ref, w1_ref, w2_ref):
    acc = [jnp.zeros((G_A * n_rows, _BPP * DH), F32) for _ in range(2)]
    for dp in range(DH // 2):
        for kv in range(2):
            x = jnp.concatenate(
                [jnp.concatenate([load((kv * G_A + g) * DH + 2 * dp + j) for j in range(2)], axis=1)
                 for g in range(G_A)], axis=0)
            acc[kv] = acc[kv] + _nn((x + pos_ref[kv, dp:dp + 1, :]).astype(BF), w1_ref[kv, dp])
    return [_nn(_gelu(acc[kv]).astype(BF), w2_ref[kv]) for kv in range(2)]


def _compress_t_body(src_ref, pos_ref, w1_ref, w2_ref, o_ref, xbuf, sem, *, nseq, n_pages):
    def copy(t):
        b, p = t // n_pages, t % n_pages
        return pltpu.make_async_copy(src_ref.at[b, :, pl.ds(pl.multiple_of(p * _PAGE, _PAGE), _PAGE)],
                                     xbuf.at[:, t, :], sem)

    def start(t, c):
        copy(t).start()
        return c

    def wait(t, c):
        copy(t).wait()
        return c
    lax.fori_loop(0, nseq * n_pages, start, 0)
    lax.fori_loop(0, nseq * n_pages, wait, 0)
    kc, vc = _compress_slab_rows(lambda r: xbuf[r], nseq * n_pages, pos_ref, w1_ref, w2_ref)
    o_ref[0] = kc
    o_ref[1] = vc


def _compress_t(kv_t, pos, w1, w2):
    nseq, SL, T = kv_t.shape
    n_pages = T // _PAGE
    one = lambda a: pl.BlockSpec(a.shape, lambda i: (0,) * a.ndim, pipeline_mode=pl.Buffered(1))
    n_out = G_A * nseq * n_pages
    return pl.pallas_call(
        functools.partial(_compress_t_body, nseq=nseq, n_pages=n_pages),
        grid=(1,),
        in_specs=[pl.BlockSpec(memory_space=pl.ANY), one(pos), one(w1), one(w2)],
        out_specs=pl.BlockSpec((2, n_out, _BPP * DH), lambda i: (0, 0, 0)),
        out_shape=jax.ShapeDtypeStruct((2, n_out, _BPP * DH), F32),
        scratch_shapes=[pltpu.VMEM((SL, nseq * n_pages, _PAGE), F32), pltpu.SemaphoreType.DMA(())],
        compiler_params=_cparams(("arbitrary",)),
        name="compress_t",
    )(kv_t, pos, w1, w2)


_KT = 256


def _nsa_prompt_body(q_ref, qr_ref, kvb_ref, kc_ref, vc_ref, ga_ref, e_ref, gx_ref, o_ref, os_scr, *, nc, ns):
    i = pl.program_id(1)
    QB = Q_BLOCK
    lane = lax.broadcasted_iota(jnp.int32, (QB, LANE), 1)
    lo64 = lane < DH
    zero_b = jnp.zeros((QB, LANE), BF)
    groups = range(G_A)

    def split_heads(ref, g):
        ev, od = [], []
        for c in range(2):
            xc = ref[:, (2 * g + c) * LANE:(2 * g + c + 1) * LANE]
            ev.append(jnp.where(lo64, xc, zero_b))
            od.append(jnp.where(lo64, zero_b, xc))
        return jnp.concatenate(ev, axis=0), jnp.concatenate(od, axis=0)

    tq_l = i * QB + lax.broadcasted_iota(jnp.int32, (nc, QB), 1)
    r_c = lax.broadcasted_iota(jnp.int32, (nc, QB), 0)
    half = nc // 2
    blk_c = jnp.where(r_c < half, 2 * r_c, 2 * (r_c - half) + 1)
    cmask = (blk_c * L_CMP + (L_CMP - 1)) <= tq_l
    cmask_f = cmask.astype(F32)
    tq_s = i * QB + lax.broadcasted_iota(jnp.int32, (ns, QB), 1)
    blk_s = lax.broadcasted_iota(jnp.int32, (ns, QB), 0)
    valid = (blk_s * L_SEL) <= tq_s
    forced = (blk_s == 0) | (blk_s == tq_s // L_SEL)
    n_top = min(N_SEL, ns)

    def select(imp):
        score = jnp.where(valid, imp + jnp.where(forced, FORCE_BONUS, 0.0), NEG)
        rank = jnp.zeros((ns, QB), F32)
        for b2 in range(ns):
            row = score[b2:b2 + 1, :]
            rank = rank + jnp.where(blk_s > b2, jnp.where(row >= score, 1.0, 0.0), jnp.where(row > score, 1.0, 0.0))
        return jnp.where((rank < n_top) & (score > 0.5 * NEG), 1.0, 0.0)

    def score_stage(chains):
        scored = []
        for qs, kmat, _, bias in chains:
            s = _nt(qs, kmat)
            s = (s.reshape(2, QB, s.shape[1]) + bias[None]).reshape(s.shape)
            scored.append((s, jnp.max(s, axis=1, keepdims=True)))
        return scored

    def value_stage(scored, chains):
        outs = []
        for (s, m), (_, _, vmat, _) in zip(scored, chains):
            p = jnp.exp2(s - m)
            outs.append(_nn(p.astype(BF), vmat) / jnp.sum(p, axis=1, keepdims=True))
        return outs

    def masked_attend(chains):
        return value_stage(score_stage(chains), chains)


    qr = [split_heads(qr_ref, g) for g in groups]
    n_tiles = (i * QB + QB + _KT - 1) // _KT

    cmp_scores = [[_nt(kc_ref[g], qs) for qs in split_heads(q_ref, g)] for g in groups]

    WK = WINDOW + QB
    start = pl.multiple_of(jnp.maximum(i * QB - WINDOW, 0), QB)
    diff = (i * QB + lax.broadcasted_iota(jnp.int32, (QB, WK), 0)) - (start + lax.broadcasted_iota(jnp.int32, (QB, WK), 1))
    wb = jnp.where((diff >= 0) & (diff < WINDOW), 0.0, NEG)
    win_chains = []
    for g in groups:
        wkv = kvb_ref[1, g, 0, pl.ds(start, WK), :]
        wvk = kvb_ref[1, g, 1, pl.ds(start, WK), :]
        win_chains += [(qr[g][0], wkv, wvk, wb), (qr[g][1], wvk, wkv, wb)]
    win_scored = score_stage(win_chains)

    imps, o_cmp = [], []
    for g in groups:
        vc = vc_ref[g]
        imp = jnp.zeros((ns, QB), F32)
        oc = {}
        for stack in range(2):
            st = cmp_scores[g][stack]
            for c in range(2):
                s = jnp.where(cmask, st[:, c * QB:(c + 1) * QB], NEG)
                p = jnp.exp2(s - jnp.max(s, axis=0, keepdims=True)) * cmask_f
                p = p / jnp.maximum(jnp.sum(p, axis=0, keepdims=True), 1e-30)
                imp = imp + p[0:half] + p[half:nc]
                oc[(c, stack)] = _tn(p.astype(BF), vc)
        o_cmp.append(oc)
        imps.append(imp)

    ow = value_stage(win_scored, win_chains)
    sel_b = [select(imp).astype(BF) for imp in imps]

    gates = []
    for g in groups:
        ga = ga_ref[:, g * LANE:(g + 1) * LANE]
        hi = ga.astype(BF)
        lo = (ga - hi.astype(F32)).astype(BF)
        gates.append(_nn(jnp.concatenate([hi, lo], axis=1), gx_ref[...]))

    def gate_of(g, br, c):
        j = br * 2 + c
        return gates[g][:, j * LANE:(j + 1) * LANE]

    part = [[gate_of(g, 0, c) * jnp.where(lo64, o_cmp[g][(c, 0)], o_cmp[g][(c, 1)])
             + gate_of(g, 2, c) * jnp.where(lo64, ow[2 * g][c * QB:(c + 1) * QB], ow[2 * g + 1][c * QB:(c + 1) * QB])
             for c in range(2)] for g in groups]


    def slc_variant(nk):
        tq_r = i * QB + lax.broadcasted_iota(jnp.int32, (QB, _KT), 0)
        causal = (nk - _KT + lax.broadcasted_iota(jnp.int32, (QB, _KT), 1)) <= tq_r
        chains = []
        for g in groups:
            bias = _tn(sel_b[g], e_ref[:, 0:nk]) - MASK_BIG
            tail = jnp.where(causal, bias[:, nk - _KT:], -MASK_BIG)
            bias = tail if nk == _KT else jnp.concatenate([bias[:, :nk - _KT], tail], axis=1)
            kv = kvb_ref[0, g, 0, 0:nk, :]
            vk = kvb_ref[0, g, 1, 0:nk, :]
            chains.append((qr[g][0], kv, vk, bias))
            chains.append((qr[g][1], vk, kv, bias))
        for c, o in enumerate(masked_attend(chains)):
            os_scr[c // 2, c % 2] = o

    for k in range(1, kvb_ref.shape[3] // _KT + 1):
        pl.when(n_tiles == k)(functools.partial(slc_variant, k * _KT))

    for g in groups:
        os_e = os_scr[g, 0]
        os_o = os_scr[g, 1]
        for c in range(2):
            rows = slice(c * QB, (c + 1) * QB)
            acc = part[g][c] + gate_of(g, 1, c) * jnp.where(lo64, os_e[rows], os_o[rows])
            o_ref[:, (2 * g + c) * LANE:(2 * g + c + 1) * LANE] = acc.astype(BF)


def _gate_expand():
    gx = np.zeros((LANE, 3 * 2 * LANE), np.float32)
    for br in range(3):
        for c in range(2):
            j = br * 2 + c
            gx[br * HPG + 2 * c, j * LANE:j * LANE + DH] = 1.0
            gx[br * HPG + 2 * c + 1, j * LANE + DH:(j + 1) * LANE] = 1.0
    return jnp.asarray(np.concatenate([gx, gx], axis=0), BF)


def _sel_expand(ns, nkeys):
    e = (np.arange(nkeys)[None, :] // L_SEL) == np.arange(ns)[:, None]
    return jnp.asarray(e * MASK_BIG, BF)


def _nsa_prompt(q, qr, kvb, kc2, vc2, ga, B, T):
    nqb = T // Q_BLOCK
    nc, ns = T // L_CMP, T // L_SEL
    assert T >= WINDOW + Q_BLOCK and T % _KT == 0
    qspec = pl.BlockSpec((Q_BLOCK, 512), lambda b, i: (b * nqb + i, 0))
    cspec = pl.BlockSpec((None, G_A, nc, LANE), lambda b, i: (b, 0, 0, 0))
    return pl.pallas_call(
        functools.partial(_nsa_prompt_body, nc=nc, ns=ns),
        grid=(B, nqb),
        in_specs=[qspec, qspec,
                  pl.BlockSpec((2, G_A, 2, T, LANE), lambda b, i: (0, 0, 0, b, 0)),
                  cspec, cspec,
                  pl.BlockSpec((Q_BLOCK, G_A * LANE), lambda b, i: (b * nqb + i, 0)),
                  pl.BlockSpec((ns, T), lambda b, i: (0, 0)),
                  pl.BlockSpec((2 * LANE, 6 * LANE), lambda b, i: (0, 0))],
        out_specs=qspec,
        out_shape=jax.ShapeDtypeStruct((B * T, 512), BF),
        scratch_shapes=[pltpu.VMEM((G_A, 2, 2 * Q_BLOCK, LANE), F32)],
        compiler_params=_cparams(("arbitrary", "arbitrary")),
        name="nsa_prompt",
    )(q, qr, kvb, kc2, vc2, ga, _sel_expand(ns, T), _gate_expand())


def _ret_body(qkv_ref, gr_ref, s0_ref, dm_ref, dq_ref, wk_ref, gc_ref, gnw_ref, z_ref, sout_ref, s_scr, *, C, sb):
    c = pl.program_id(1)

    @pl.when(c == 0)
    def _():
        s_scr[...] = s0_ref[...]

    units = [(j, h, j, slice(h * LANE, (h + 1) * LANE)) for j in range(sb) for h in range(H_R)]
    qkv = [tuple(qkv_ref[t, rows, :, sl] for t in range(3)) for _, _, rows, sl in units]
    inner = [_nt(q, k) * dm_ref[h] for (q, k, _), (_, h, _, _) in zip(qkv, units)]
    cross = [_nn(q, s_scr[j, h].astype(BF)) * dq_ref[h] for (q, _, _), (j, h, _, _) in zip(qkv, units)]
    outs = [_nn(a.astype(BF), v) + c for a, c, (_, _, v) in zip(inner, cross, qkv)]
    for (_, k, v), (j, h, _, _) in zip(qkv, units):
        kw = (k.astype(F32) * wk_ref[h]).astype(BF)
        s_new = gc_ref[h, 0:1, :] * s_scr[j, h] + _tn(kw, v)
        s_scr[j, h] = s_new
        sout_ref[j, h] = s_new
    for o, (_, _, rows, sl) in zip(outs, units):
        mu = jnp.mean(o, axis=-1, keepdims=True)
        d = o - mu
        var = jnp.mean(d * d, axis=-1, keepdims=True)
        on = d * lax.rsqrt(var + EPS) * gnw_ref[:, sl]
        z_ref[rows, :, sl] = (gr_ref[rows, :, sl].astype(F32) * on).astype(BF)


def _retention(ret3, gr, s0, gnw, nseq, rows_per_seq, C, c_true, sb):
    nC = rows_per_seq // C
    rows = nseq * rows_per_seq
    width = H_R * LANE
    dm, dq, wk, gc = _ret_decay(C, c_true)
    full = lambda a: pl.BlockSpec(a.shape, lambda b, c: (0,) * a.ndim)
    z, s_new = pl.pallas_call(
        functools.partial(_ret_body, C=C, sb=sb),
        grid=(nseq // sb, nC),
        in_specs=[pl.BlockSpec((3, sb, C, width), lambda b, c: (0, b, c, 0)),
                  pl.BlockSpec((sb, C, width), lambda b, c: (b, c, 0)),
                  pl.BlockSpec((sb, H_R, DK_R, DV_R), lambda b, c: (b, 0, 0, 0)),
                  full(dm), full(dq), full(wk), full(gc), full(gnw)],
        out_specs=[pl.BlockSpec((sb, C, width), lambda b, c: (b, c, 0)),
                   pl.BlockSpec((sb, H_R, DK_R, DV_R), lambda b, c: (b, 0, 0, 0))],
        out_shape=[jax.ShapeDtypeStruct((nseq, rows_per_seq, width), BF),
                   jax.ShapeDtypeStruct((nseq, H_R, DK_R, DV_R), F32)],
        scratch_shapes=[pltpu.VMEM((sb, H_R, DK_R, DV_R), F32)],
        compiler_params=_cparams(("arbitrary", "arbitrary")),
        name="retention",
    )(ret3.reshape(3, nseq, rows_per_seq, width), gr.reshape(nseq, rows_per_seq, width), s0, dm, dq, wk, gc, gnw)
    return z.reshape(rows, width), s_new


def _mix_ffn_body(x_ref, oa_ref, zr_ref, gm_ref, gt1_ref, wa_ref, wr_ref, wo_ref,
                  nw_ref, sh_ref, sc_ref, gt_ref, wu_ref, cw_ref, cb_ref, wd_ref, nf_ref, p1_ref, p2_ref,
                  y_ref, a_ref, carry, *, blocks_per_seq, seq_rows):
    i = pl.program_id(0)
    tm = x_ref.shape[0]
    parts = [slice(k * _ROW_PART, (k + 1) * _ROW_PART) for k in range(tm // _ROW_PART)] if tm > _ROW_PART else [slice(0, tm)]

    def mod(ref, r):
        return ref[...] if ref.shape[0] == 1 else ref[r, :]

    ya = [_nn(oa_ref[r, :], wa_ref[...]) for r in parts]
    yr = [_nn(zr_ref[r, :], wr_ref[...]) for r in parts]
    xs = []
    for r, ya_k, yr_k in zip(parts, ya, yr):
        gm = gm_ref[r, :].astype(F32)
        merged = gm[:, 0:D_MODEL] * ya_k + gm[:, D_MODEL:2 * D_MODEL] * yr_k
        xs.append(x_ref[r, :] + mod(gt1_ref, r) * _nn(merged.astype(BF), wo_ref[...]))
    hs = []
    for r, x in zip(parts, xs):
        ms = jnp.mean(x * x, axis=-1, keepdims=True)
        h = (x * lax.rsqrt(ms + EPS)) * nw_ref[...]
        hs.append((h * (1.0 + mod(sc_ref, r)) + mod(sh_ref, r)).astype(BF))
    a_parts = [_nn(h, wu_ref[:, 0:D_FF]) for h in hs]
    b_parts = [_nn(h, wu_ref[:, D_FF:2 * D_FF]) for h in hs]
    a = a_parts[0] if len(parts) == 1 else jnp.concatenate(a_parts, axis=0)
    a_ref[...] = a[tm - a_ref.shape[0]:tm, :]
    rid = lax.broadcasted_iota(jnp.int32, (tm, D_FF), 0)
    if seq_rows is None:
        first = (i % blocks_per_seq) == 0
        prev = jnp.where(first, p1_ref[...], carry[...])
        carry[...] = a[tm - 8:tm, :]
        am1 = jnp.where(rid == 0, prev[7:8, :], pltpu.roll(a, 1, 0))
        am2 = jnp.where(rid == 0, prev[6:7, :], jnp.where(rid == 1, prev[7:8, :], pltpu.roll(a, 2, 0)))
    else:
        s = rid % seq_rows
        am1 = jnp.where(s == 0, p1_ref[...], pltpu.roll(a, 1, 0))
        am2 = jnp.where(s <= 1, p2_ref[...], pltpu.roll(a, 2, 0))
    u = cb_ref[...] + am2 * cw_ref[0:1, :] + am1 * cw_ref[1:2, :] + a * cw_ref[2:3, :]
    ffs = [_nn((_gelu(u[r]) * b_k).astype(BF), wd_ref[...]) for r, b_k in zip(parts, b_parts)]
    for r, x, ff in zip(parts, xs, ffs):
        x2 = x + mod(gt_ref, r) * ff
        ms2 = jnp.mean(x2 * x2, axis=-1, keepdims=True)
        y_ref[r, :] = (x2 * lax.rsqrt(ms2 + EPS)) * nf_ref[...]


def _mix_ffn(x2d, oa, zr, gm, gt1, wa, wr, wo, nw, sh, sc, gt, wu, cw, cb, wd, nf, p1, p2,
             tm, blocks_per_seq, per_row_mods, seq_rows):
    rows = x2d.shape[0]
    a_rows = tm if seq_rows is not None else 8
    if per_row_mods:
        mod_spec = pl.BlockSpec((tm, D_MODEL), lambda i: (i, 0))
    else:
        mod_spec = pl.BlockSpec((None, 1, D_MODEL), lambda i: (i // blocks_per_seq, 0, 0))
    row = lambda w: pl.BlockSpec((tm, w), lambda i: (i, 0))
    full = lambda a, b: pl.BlockSpec((a, b), lambda i: (0, 0), pipeline_mode=pl.Buffered(1))
    vec = lambda w: pl.BlockSpec((1, w), lambda i: (0, 0))
    pspec = pl.BlockSpec(p1.shape, lambda i: (0, 0))
    return pl.pallas_call(
        functools.partial(_mix_ffn_body, blocks_per_seq=blocks_per_seq, seq_rows=seq_rows),
        grid=(rows // tm,),
        in_specs=[row(D_MODEL), row(512), row(512), row(2048), mod_spec,
                  full(512, D_MODEL), full(512, D_MODEL), full(D_MODEL, D_MODEL),
                  vec(D_MODEL), mod_spec, mod_spec, mod_spec,
                  full(D_MODEL, 2 * D_FF), pl.BlockSpec((CONV_W, D_FF), lambda i: (0, 0)), vec(D_FF),
                  full(D_FF, D_MODEL), vec(D_MODEL), pspec, pspec],
        out_specs=[row(D_MODEL), pl.BlockSpec((a_rows, D_FF), lambda i: (i, 0))],
        out_shape=[jax.ShapeDtypeStruct((rows, D_MODEL), F32),
                   jax.ShapeDtypeStruct((rows // tm * a_rows, D_FF), F32)],
        scratch_shapes=[pltpu.VMEM((8, D_FF), F32)],
        compiler_params=_cparams(("arbitrary",)),
        name="mix_ffn",
    )(x2d, oa, zr, gm, gt1, wa, wr, wo, nw, sh, sc, gt, wu, cw, cb, wd, nf, p1, p2)


def _prompt_path(x_prompt, mods_p, W):
    B, T, _ = x_prompt.shape
    rows = B * T
    x2d = x_prompt.reshape(rows, D_MODEL)
    pos = np.arange(T)
    tm = 2 * _ROW_PART if T % (2 * _ROW_PART) == 0 else _ROW_PART
    sh1, sc1, gt1, sh2, sc2, gt2 = [mods_p[:, None, j * D_MODEL:(j + 1) * D_MODEL] for j in range(6)]
    (q, qr, kvc, kvs, kvw, kvb, ret3, gr, gm, ga) = _inproj(
        x2d, W["norm1"], sh1, sc1, W["w_in"], _rope_tab(pos, Q_SCALE), _rope_tab(pos, 1.0), _ret_tab(pos),
        tm, T // tm, False, True)
    nc, n_pages = T // L_CMP, T // _PAGE
    comp = _compress_t(kvc, W["phi_posT"], W["phi_w1p"], W["phi_w2p"])
    comp = comp.reshape(2, G_A, B, n_pages, 2, 2, DH).transpose(0, 2, 1, 5, 3, 4, 6).reshape(2, B, G_A, nc, DH)
    comp2 = jnp.concatenate([comp, comp], axis=-1).astype(BF)
    oa = _nsa_prompt(q, qr, kvb, comp2[0], comp2[1], ga, B, T)
    C = 256 if T % 256 == 0 else T
    zr, ret_new = _retention(ret3, gr, jnp.zeros((B, H_R, DK_R, DV_R), F32), W["gnw"], B, T, C, C,
                             4 if B % 4 == 0 else 1)
    zeros8 = jnp.zeros((8, D_FF), F32)
    tf = tm
    y, a_up = _mix_ffn(x2d, oa, zr, gm, gt1, W["w_up_a"], W["w_up_r"], W["w_out"],
                       W["norm2"], sh2, sc2, gt2, W["w_ffn_up"], W["conv_w"], W["conv_b"], W["w_ffn_down"],
                       W["normf"], zeros8, zeros8, tf, T // tf, False, None)
    wsz = min(WINDOW, T)
    rows_major = lambda t: t.reshape(B, 2, G_A, DH, t.shape[-1]).transpose(0, 4, 1, 2, 3)[None]
    outs = dict(
        y=y.reshape(B, T, D_MODEL),
        cmp=rows_major(kvc), slc=rows_major(kvs), win=rows_major(kvw[:, :, T - wsz:]),
        ret=ret_new[None],
        conv=a_up.reshape(B, T // tf, 8, D_FF)[None, :, T // tf - 1, 8 - (CONV_W - 1):],
    )
    return outs


_QC = 32


def _cmp_paged_body(pt_ref, cache_ref, q_ref, pos_ref, w1_ref, w2_ref, ocmp_ref, topi_ref, xbuf, sem,
                    *, n_pages, n_pick, n_q):
    b = pl.program_id(0)
    nb = pl.num_programs(0)
    slot = b % 2

    def copy(bb, sl, p):
        return pltpu.make_async_copy(cache_ref.at[pt_ref[bb, p]], xbuf.at[sl, :, p, :], sem.at[sl])

    def issue(bb, sl):
        for p in range(n_pages):
            copy(bb, sl, p).start()

    @pl.when(b == 0)
    def _():
        issue(b, slot)

    @pl.when(b + 1 < nb)
    def _():
        issue(b + 1, 1 - slot)

    for p in range(n_pages):
        copy(b, slot, p).wait()

    comp = _compress_slab_rows(lambda r: xbuf[slot, r], n_pages, pos_ref, w1_ref, w2_ref)

    def lane_groups(x, op):
        r = x
        for j in range(1, _BPP):
            r = op(r, pltpu.roll(x, j * _QC, 1))
        return r

    grp = range(G_A)
    sts = [_nn(comp[0][g * n_pages:(g + 1) * n_pages].astype(BF), q_ref[g]) for g in grp]
    ps = []
    for g in grp:
        m = lane_groups(jnp.max(sts[g], axis=0, keepdims=True), jnp.maximum)
        p = jnp.exp2(sts[g] - m)
        ps.append(p / lane_groups(jnp.sum(p, axis=0, keepdims=True), jnp.add))
    for g in grp:
        r_full = _tn(ps[g].astype(BF), comp[1][g * n_pages:(g + 1) * n_pages].astype(BF))
        o = r_full[0:_QC, 0:DH]
        for j in range(1, _BPP):
            o = o + r_full[j * _QC:(j + 1) * _QC, j * DH:(j + 1) * DH]
        ocmp_ref[g] = o
    imp = None
    for g in grp:
        pair = ps[g] + pltpu.roll(ps[g], LANE - _QC, 1)
        t = pair
        for r in range(1, HPG):
            t = t + pltpu.roll(pair, LANE - r * n_q, 1)
        if g:
            t = pltpu.roll(t, g * n_q, 1)
        lane_g = lax.broadcasted_iota(jnp.int32, t.shape, 1) % (2 * _QC)
        t = jnp.where((lane_g >= g * n_q) & (lane_g < (g + 1) * n_q), t, 0.0)
        imp = t if imp is None else imp + t
    sc = jnp.concatenate([imp, pltpu.roll(imp, 2 * _QC, 1)], axis=0)
    row = lax.broadcasted_iota(jnp.int32, (2 * n_pages, LANE), 0)
    blk = jnp.where(row < n_pages, 2 * row, 2 * (row - n_pages) + 1)
    score = sc + jnp.where(blk == 0, FORCE_BONUS, 0.0)
    topi_ref[...] = jnp.zeros((8, LANE), jnp.int32)
    for k in range(n_pick):
        mx = jnp.max(score, axis=0, keepdims=True)
        idx = jnp.min(jnp.where(score == mx, blk, 2 * n_pages), axis=0, keepdims=True)
        topi_ref[k:k + 1, :] = idx
        score = jnp.where(blk == idx, -jnp.inf, score)


def _cmp_paged(page_table, cache_t, q_bd, pos, w1, w2, n_q):
    DB, n_pages = page_table.shape
    SL = cache_t.shape[1]
    page_rows = cache_t.shape[2]
    kern = functools.partial(_cmp_paged_body, n_pages=n_pages, n_pick=N_SEL - 1, n_q=n_q)
    one = dict(pipeline_mode=pl.Buffered(1))
    return pl.pallas_call(
        kern,
        grid_spec=pltpu.PrefetchScalarGridSpec(
            num_scalar_prefetch=1, grid=(DB,),
            in_specs=[pl.BlockSpec(memory_space=pl.ANY),
                      pl.BlockSpec((None, G_A, _BPP * DH, LANE), lambda b, pt: (b, 0, 0, 0)),
                      pl.BlockSpec(pos.shape, lambda b, pt: (0, 0, 0), **one),
                      pl.BlockSpec(w1.shape, lambda b, pt: (0, 0, 0, 0), **one),
                      pl.BlockSpec((2, _BPP * DH, _BPP * DH), lambda b, pt: (0, 0, 0), **one)],
            out_specs=[pl.BlockSpec((None, G_A, _QC, DH), lambda b, pt: (b, 0, 0, 0)),
                       pl.BlockSpec((None, 8, LANE), lambda b, pt: (b, 0, 0))],
            scratch_shapes=[pltpu.VMEM((2, SL, n_pages, page_rows), F32),
                            pltpu.SemaphoreType.DMA((2,))]),
        out_shape=[jax.ShapeDtypeStruct((DB, G_A, _QC, DH), F32),
                   jax.ShapeDtypeStruct((DB, 8, LANE), jnp.int32)],
        compiler_params=_cparams(("arbitrary",)),
        name="cmp_paged",
    )(page_table, cache_t, q_bd, pos, w1, w2)


def _slc_win_paged_body(pt_ref, ti_ref, cslc_ref, win_ref, q_ref, tiv_ref, ns_ref, nw_ref, ex_ref,
                        oslc_ref, owin_ref, kbuf, sem, *, n_q, n_pick, page_rows):
    b = pl.program_id(0)
    nb = pl.num_programs(0)
    slot = b % 2
    n_slab = n_q * n_pick
    bpp = page_rows // L_SEL
    wb = win_ref.shape[1]

    def copies(bb, sl):
        cps = []
        for g in range(G_A):
            for j in range(n_slab):
                page = pt_ref[bb, ti_ref[bb, g * n_slab + j] // bpp]
                for kv in range(2):
                    cps.append(pltpu.make_async_copy(
                        cslc_ref.at[page, pl.ds((kv * G_A + g) * DH, DH), :],
                        kbuf.at[sl, g, kv, :, pl.ds(j * page_rows, page_rows)], sem.at[sl]))
        return cps

    @pl.when(b == 0)
    def _():
        for cp in copies(b, slot):
            cp.start()

    @pl.when(b + 1 < nb)
    def _():
        for cp in copies(b + 1, 1 - slot):
            cp.start()

    for cp in copies(b, slot):
        cp.wait()

    nq_rows = q_ref.shape[1]
    nk = n_slab * page_rows
    row_q = lax.broadcasted_iota(jnp.int32, (nq_rows, nk), 0) % n_q
    col = lax.broadcasted_iota(jnp.int32, (nq_rows, nk), 1)
    own = row_q == col // (n_pick * page_rows)
    half = ((col % page_rows) // L_SEL).astype(F32)
    nnew = ns_ref.shape[2]
    new_ok = (lax.broadcasted_iota(jnp.int32, (nq_rows, nnew), 1)
              <= lax.broadcasted_iota(jnp.int32, (nq_rows, nnew), 0) % n_q)
    dwin = (wb + lax.broadcasted_iota(jnp.int32, (nq_rows, wb), 0) % n_q
            - lax.broadcasted_iota(jnp.int32, (nq_rows, wb), 1))
    win_ok = (dwin >= 0) & (dwin < WINDOW)

    chains = []
    for g in range(G_A):
        q = q_ref[g]
        par = (tiv_ref[g] % bpp).astype(F32).astype(BF)
        want = _nn(par, ex_ref[...])[0:1, :]
        chains.append((q, kbuf[slot, g, 0].astype(BF), kbuf[slot, g, 1].astype(BF), own & (half == want),
                       ns_ref[g, 0].astype(BF), ns_ref[g, 1].astype(BF), oslc_ref, g))
        kw = win_ref[pl.ds((0 * G_A + g) * DH, DH), :].astype(BF)
        vw = win_ref[pl.ds((1 * G_A + g) * DH, DH), :].astype(BF)
        chains.append((q, kw, vw, win_ok, nw_ref[g, 0].astype(BF), nw_ref[g, 1].astype(BF), owin_ref, g))
    scores = [(jnp.where(ok_old, _nn(q, kt_old), NEG), jnp.where(new_ok, _nt(q, k_new), NEG))
              for q, kt_old, _, ok_old, k_new, _, _, _ in chains]
    probs = []
    for s_o, s_n in scores:
        m = jnp.maximum(jnp.max(s_o, axis=1, keepdims=True), jnp.max(s_n, axis=1, keepdims=True))
        p_o = jnp.exp2(s_o - m)
        p_n = jnp.exp2(s_n - m)
        probs.append((p_o, p_n, jnp.sum(p_o, axis=1, keepdims=True) + jnp.sum(p_n, axis=1, keepdims=True)))
    for (p_o, p_n, den), (_, _, vt_old, _, _, v_new, out_ref, g) in zip(probs, chains):
        out_ref[g] = (_nt(p_o.astype(BF), vt_old) + _nn(p_n.astype(BF), v_new)) / den


def _slc_win_paged(page_table, topi_flat, topi_vec, cache_t, win_t, q_rot, new_slc, new_win, n_q):
    DB = page_table.shape[0]
    n_pick = N_SEL - 1
    page_rows = cache_t.shape[2]
    wb = win_t.shape[2]
    nq_rows = q_rot.shape[2]
    n_slab = n_q * n_pick
    ex = (np.arange(n_slab * page_rows)[None, :] // page_rows) == np.arange(LANE)[:, None]
    kern = functools.partial(_slc_win_paged_body, n_q=n_q, n_pick=n_pick, page_rows=page_rows)
    bspec = lambda shp: pl.BlockSpec((None,) + shp, lambda b, pt, ti: (b,) + (0,) * len(shp))
    return pl.pallas_call(
        kern,
        grid_spec=pltpu.PrefetchScalarGridSpec(
            num_scalar_prefetch=2, grid=(DB,),
            in_specs=[pl.BlockSpec(memory_space=pl.ANY), bspec((2 * G_A * DH, wb)),
                      bspec((G_A, nq_rows, DH)), bspec((G_A, 16, LANE)),
                      bspec((G_A, 2, 16, DH)), bspec((G_A, 2, 16, DH)),
                      pl.BlockSpec((LANE, n_slab * page_rows), lambda b, pt, ti: (0, 0))],
            out_specs=[bspec((G_A, nq_rows, DH)), bspec((G_A, nq_rows, DH))],
            scratch_shapes=[pltpu.VMEM((2, G_A, 2, DH, n_slab * page_rows), F32),
                            pltpu.SemaphoreType.DMA((2,))]),
        out_shape=[jax.ShapeDtypeStruct((DB, G_A, nq_rows, DH), F32)] * 2,
        compiler_params=_cparams(("arbitrary",)),
        name="slc_win_paged",
    )(page_table, topi_flat, cache_t, win_t, q_rot, topi_vec, new_slc, new_win, jnp.asarray(ex, BF))


def _gate_sample_body(oc_ref, os_ref, ow_ref, ga_ref, o_ref):
    rows = o_ref.shape[0]
    lo64 = lax.broadcasted_iota(jnp.int32, (rows, LANE), 1) < DH
    for c4 in range(H_A // 2):
        g, c = c4 // 2, c4 % 2
        sl = slice(c4 * LANE, (c4 + 1) * LANE)
        acc = jnp.zeros((rows, LANE), F32)
        for br, ref in enumerate((oc_ref, os_ref, ow_ref)):
            col = g * LANE + br * HPG + 2 * c
            gate = jnp.where(lo64, ga_ref[:, col:col + 1], ga_ref[:, col + 1:col + 2])
            acc = acc + gate * ref[:, sl]
        o_ref[:, sl] = acc.astype(BF)


def _gate_sample(oc, osl, ow, ga):
    rows = oc.shape[0]
    full = lambda w: pl.BlockSpec((rows, w), lambda i: (0, 0))
    return pl.pallas_call(
        _gate_sample_body, grid=(1,),
        in_specs=[full(512), full(512), full(512), full(256)],
        out_specs=full(512),
        out_shape=jax.ShapeDtypeStruct((rows, 512), BF),
        compiler_params=_cparams(("arbitrary",)),
        name="gate_sample",
    )(oc, osl, ow, ga)


def _sample_path(x_sample, mods_s, W, cache_cmp, cache_slc, state_win, state_ret, state_conv, page_table):
    DB, S, _ = x_sample.shape
    rows = DB * S
    page_rows = cache_cmp.shape[1]
    P = page_table.shape[1] * page_rows
    wb = state_win.shape[1]
    assert P % L_SEL == 0 and S < L_CMP and S <= 8 and wb == WINDOW and page_rows % L_SEL == 0
    assert P // L_SEL >= N_SEL and CONV_W == 3 and S >= CONV_W - 1
    pos = P + np.arange(S)
    pos_rows = np.tile(pos, DB)
    x2d = x_sample.reshape(rows, D_MODEL)
    modr = jnp.repeat(mods_s, S, axis=0)
    sh1, sc1, gt1, sh2, sc2, gt2 = [modr[:, j * D_MODEL:(j + 1) * D_MODEL] for j in range(6)]
    (q, qr, kvc, kvs, kvw, _, ret3, gr, gm, ga) = _inproj(
        x2d, W["norm1"], sh1, sc1, W["w_in"], _rope_tab(pos_rows, Q_SCALE), _rope_tab(pos_rows, 1.0),
        _ret_tab(pos_rows), rows, 1, True, False)

    def to_heads(t):
        return t.reshape(DB, S, G_A, HPG, DH).transpose(0, 2, 3, 1, 4).reshape(DB, G_A, HPG * S, DH)

    def from_heads(t):
        return t.reshape(DB, G_A, HPG, S, DH).transpose(0, 3, 1, 2, 4).reshape(rows, H_A * DH)

    assert page_rows == _BPP * L_CMP and HPG * S <= _QC and S * (N_SEL - 1) <= LANE
    slab = lambda t: t.transpose(0, 2, 3, 4, 1).reshape(t.shape[0], 2 * G_A * DH, t.shape[1])
    qt = jnp.pad(to_heads(q).transpose(0, 1, 3, 2), ((0, 0), (0, 0), (0, 0), (0, _QC - HPG * S)))
    q_bd = _blockdiag(qt)
    o_cmp, topi = _cmp_paged(page_table, slab(cache_cmp), q_bd, W["phi_posT"], W["phi_w1p"], W["phi_w2p"], S)
    n_pick = N_SEL - 1
    topi = topi[:, :n_pick, :G_A * S].transpose(0, 2, 1).reshape(DB, G_A, S * n_pick)
    topi_vec = jnp.broadcast_to(jnp.pad(topi, ((0, 0), (0, 0), (0, LANE - S * n_pick)))[:, :, None, :],
                                (DB, G_A, 16, LANE))

    def new_rows(t):
        t = t.reshape(DB, S, 2, G_A, DH).transpose(0, 3, 2, 1, 4)
        return jnp.pad(t, ((0, 0), (0, 0), (0, 0), (0, 16 - S), (0, 0)))

    o_slc, o_win = _slc_win_paged(page_table, topi.reshape(DB, G_A * S * n_pick), topi_vec, slab(cache_slc),
                                  slab(state_win), to_heads(qr), new_rows(kvs), new_rows(kvw), S)
    oa = _gate_sample(from_heads(o_cmp[:, :, :HPG * S]), from_heads(o_slc), from_heads(o_win), ga)

    RP = 16
    padr = lambda t: jnp.pad(t.reshape(t.shape[:-2] + (DB, S, 512)),
                             ((0, 0),) * (t.ndim - 1) + ((0, RP - S), (0, 0))).reshape(t.shape[:-2] + (DB * RP, 512))
    zr, ret_new = _retention(padr(ret3), padr(gr), state_ret, W["gnw"], DB, RP, RP, S, 8 if DB % 8 == 0 else 1)
    zr = zr.reshape(DB, RP, 512)[:, :S].reshape(rows, 512)
    zs = lambda n: jnp.zeros((DB, n, D_FF), F32)
    p1 = jnp.concatenate([state_conv[:, 1:2], zs(S - 1)], axis=1).reshape(rows, D_FF)
    p2 = jnp.concatenate([state_conv[:, 0:2], zs(S - 2)], axis=1).reshape(rows, D_FF)
    y, a_up = _mix_ffn(x2d, oa, zr, gm, gt1, W["w_up_a"], W["w_up_r"], W["w_out"],
                       W["norm2"], sh2, sc2, gt2, W["w_ffn_up"], W["conv_w"], W["conv_b"], W["w_ffn_down"],
                       W["normf"], p1, p2, rows, 1, True, S)
    shp = (1, DB, S, 2, G_A, DH)
    return dict(
        y=y.reshape(DB, S, D_MODEL),
        cmp=kvc.reshape(shp), slc=kvs.reshape(shp),
        win=jnp.concatenate([state_win[:, S:], kvw.reshape(DB, S, 2, G_A, DH)], axis=1)[None],
        ret=ret_new[None],
        conv=a_up.reshape(DB, S, D_FF)[None, :, S - (CONV_W - 1):],
    )


def kernel(x_prompt, x_sample, cache_cmp_kv, cache_slc_kv, state_win_kv, state_ret, state_conv, page_table,
           c_prompt, c_sample, norm1_w, norm2_w, w_ada, b_ada, w_in, phi_pos_k, phi_k1, phi_k2, phi_pos_v,
           phi_v1, phi_v2, w_up_a, ret_gn_w, w_up_r, w_out, w_ffn_up, ffn_conv_w, ffn_conv_b, w_ffn_down, normf_w):
    B = x_prompt.shape[0]
    l = 0
    W = dict(
        norm1=norm1_w[l].reshape(1, D_MODEL), norm2=norm2_w[l].reshape(1, D_MODEL), normf=normf_w.reshape(1, D_MODEL),
        w_in=_pack_w_in(w_in[l]),
        w_up_a=w_up_a[l].astype(BF), w_up_r=w_up_r[l].astype(BF), w_out=w_out[l].astype(BF),
        gnw=ret_gn_w[l].reshape(1, H_R * DV_R),
        w_ffn_up=w_ffn_up[l].astype(BF), conv_w=ffn_conv_w[l], conv_b=ffn_conv_b[l].reshape(1, D_FF),
        w_ffn_down=w_ffn_down[l].astype(BF),
    )
    W["phi_posT"], W["phi_w1p"], W["phi_w2p"] = _pack_phi_paged(
        phi_pos_k[l], phi_k1[l], phi_k2[l], phi_pos_v[l], phi_v1[l], phi_v2[l])
    mods = _mods(jnp.concatenate([c_prompt, c_sample], axis=0), w_ada[l], b_ada[l])
    p = _prompt_path(x_prompt, mods[:B], W)
    s = _sample_path(x_sample, mods[B:], W, cache_cmp_kv[l], cache_slc_kv[l], state_win_kv[l], state_ret[l],
                     state_conv[l], page_table)
    return (p["y"], s["y"], p["cmp"], s["cmp"], p["slc"], s["slc"], p["win"], s["win"],
            p["ret"], s["ret"], p["conv"], s["conv"])
```

```python
import functools

import numpy as np
import jax
import jax.numpy as jnp
from jax import lax
from jax.experimental import pallas as pl
from jax.experimental.pallas import tpu as pltpu

BF = jnp.bfloat16
F32 = jnp.float32

D_MODEL = 1024
H_A, G_A, DH = 8, 2, 64
HPG = H_A // G_A
ROPE_DIM = DH // 4
ROPE_THETA = 500000.0
L_CMP, L_SEL, N_SEL = 32, 64, 8
WINDOW = 512
Q_BLOCK = 128
FORCE_BONUS = 1e4
H_R, DK_R, DV_R = 4, 128, 128
RET_THETA = 10000.0
D_FF = 2816
CONV_W = 3
EPS = 1e-6
NEG = -1e30
MASK_BIG = 2.0 ** 100
Q_SCALE = DH ** -0.5 * 1.4426950408889634
LANE = 128
_ROW_PART = 256
VMEM_LIMIT = 56 * 1024 * 1024


def _cparams(sem):
    return pltpu.CompilerParams(dimension_semantics=sem, vmem_limit_bytes=VMEM_LIMIT)


def _sigmoid(x):
    return 1.0 / (1.0 + jnp.exp(-x))


def _gelu(x):
    return 0.5 * x * (1.0 + jnp.tanh(0.7978845608028654 * (x + 0.044715 * (x * x * x))))


def _nt(a, b):
    return lax.dot_general(a, b, (((1,), (1,)), ((), ())), preferred_element_type=F32)


def _tn(a, b):
    return lax.dot_general(a, b, (((0,), (0,)), ((), ())), preferred_element_type=F32)


def _nn(a, b):
    return jnp.dot(a, b, preferred_element_type=F32)


def _rope_tab(pos, scale):
    half = ROPE_DIM // 2
    inv = ROPE_THETA ** (-np.arange(half, dtype=np.float64) * (2.0 / ROPE_DIM))
    ang = pos.astype(np.float64)[:, None] * inv
    cos, sin = np.cos(ang), np.sin(ang)
    n = pos.shape[0]
    c = np.ones((n, DH)); s_lo = np.zeros((n, DH)); s_hi = np.zeros((n, DH))
    c[:, :half] = cos; c[:, half:ROPE_DIM] = cos
    s_lo[:, half:ROPE_DIM] = sin
    s_hi[:, :half] = -sin
    tab = np.concatenate([np.tile(t, (1, 2)) for t in (c, s_lo, s_hi)], axis=1) * scale
    return jnp.asarray(tab, F32)


def _ret_tab(pos):
    half = DK_R // 2
    inv = RET_THETA ** (-np.arange(half, dtype=np.float64) * (2.0 / DK_R))
    ang = pos.astype(np.float64)[:, None] * inv
    cos, sin = np.cos(ang), np.sin(ang)
    c = np.concatenate([cos, cos], axis=1)
    s = np.concatenate([-sin, sin], axis=1)
    ks = DK_R ** -0.5
    return jnp.asarray(np.concatenate([c, s, c * ks, s * ks], axis=1), F32)


def _ret_decay(C, c_true):
    h = np.arange(H_R, dtype=np.float64)
    log_g = np.log1p(-np.exp2(-5.0 - h))
    i = np.arange(C, dtype=np.float64)
    diff = i[:, None] - i[None, :]
    dm = np.where(diff >= 0, np.exp(log_g[:, None, None] * np.maximum(diff, 0.0)), 0.0)
    dq = np.exp(log_g[:, None] * (i + 1.0))[:, :, None] * np.ones((1, 1, LANE))
    wk = np.exp(log_g[:, None] * (c_true - 1.0 - i))[:, :, None] * np.ones((1, 1, LANE))
    wk = np.where(i[None, :, None] < c_true, wk, 0.0)
    gc = np.exp(log_g * c_true)[:, None, None] * np.ones((1, 8, LANE))
    return (jnp.asarray(dm, F32), jnp.asarray(dq, F32), jnp.asarray(wk, F32), jnp.asarray(gc, F32))


def _mods_body(c_ref, w_ref, b_ref, o_ref):
    c = c_ref[...]
    s = c * _sigmoid(c)
    o_ref[...] = _nn(s.astype(BF), w_ref[...].astype(BF)) + b_ref[...]


def _mods(c_all, w_ada, b_ada):
    n = c_all.shape[0]
    nout = w_ada.shape[1]
    tn = 1024
    return pl.pallas_call(
        _mods_body,
        grid=(nout // tn,),
        in_specs=[pl.BlockSpec((n, D_MODEL), lambda j: (0, 0)),
                  pl.BlockSpec((D_MODEL, tn), lambda j: (0, j)),
                  pl.BlockSpec((1, tn), lambda j: (0, j))],
        out_specs=pl.BlockSpec((n, tn), lambda j: (0, j)),
        out_shape=jax.ShapeDtypeStruct((n, nout), F32),
        compiler_params=_cparams(("arbitrary",)),
        name="mods",
    )(c_all, w_ada, b_ada.reshape(1, nout))


_C_Q, _C_KV, _C_QR, _C_KR, _C_VR, _C_GR, _C_GM, _C_GA, _C_END = (
    0, 512, 1280, 1792, 2304, 2816, 3328, 5376, 5632)


def _pack_w_in(w_in):
    wt = w_in.T
    o = np.cumsum((0, 512, 768, 24, 512, 512, 512, 512, 2048))
    q, kv, ga, qr, kr, vr, gr, gm = [wt[o[i]:o[i + 1]] for i in range(8)]
    ga = ga.reshape(3, G_A, HPG, D_MODEL).transpose(1, 0, 2, 3).reshape(G_A, 3 * HPG, D_MODEL)
    ga = jnp.pad(ga, ((0, 0), (0, LANE - 3 * HPG), (0, 0))).reshape(G_A * LANE, D_MODEL)
    return (wt[0:o[2]].astype(BF), wt[o[3]:o[8]].astype(BF), ga.astype(BF))


def _inproj_body(x_ref, nw_ref, sh_ref, sc_ref, wa_ref, wb_ref, wg_ref, rq_ref, rk_ref, rr_ref,
                 q_ref, qr_ref, kvc_ref, kvs_ref, kvw_ref, kvb_ref, ret_ref, gr_ref, gm_ref, ga_ref, *, kv_t):
    tm = x_ref.shape[0]
    np_ = tm // _ROW_PART if tm > _ROW_PART else 1
    pr = tm // np_
    parts = [slice(k * pr, (k + 1) * pr) for k in range(np_)]
    lo64 = lax.broadcasted_iota(jnp.int32, (pr, LANE), 1) < DH

    def put_kv(out_ref, r, k, v):
        if kv_t:
            out_ref[0:LANE, r] = k.T
            out_ref[LANE:2 * LANE, r] = v.T
        else:
            out_ref[r, 0:LANE] = k
            out_ref[r, LANE:2 * LANE] = v

    def mod(ref, r):
        return ref[...] if ref.shape[0] == 1 else ref[r, :]

    hbs = []
    for r in parts:
        x = x_ref[r, :]
        ms = jnp.mean(x * x, axis=-1, keepdims=True)
        h = (x * lax.rsqrt(ms + EPS)) * nw_ref[...]
        hbs.append((h * (1.0 + mod(sc_ref, r)) + mod(sh_ref, r)).astype(BF))

    def mm(k, lo, hi):
        if hi <= _C_QR:
            w = wa_ref[lo:hi, :]
        elif hi <= _C_GA:
            w = wb_ref[lo - _C_QR:hi - _C_QR, :]
        else:
            w = wg_ref[lo - _C_GA:hi - _C_GA, :]
        return _nt(hbs[k], w)

    def rope(xc, tab_ref, r):
        return (xc * tab_ref[r, 0:LANE] + pltpu.roll(xc, 8, 1) * tab_ref[r, LANE:2 * LANE]
                + pltpu.roll(xc, LANE - 8, 1) * tab_ref[r, 2 * LANE:3 * LANE])

    gm_half = (_C_GA - _C_GM) // 2
    for k, r in enumerate(parts):
        gm_ref[r, 0:gm_half] = _sigmoid(mm(k, _C_GM, _C_GM + gm_half)).astype(BF)
    for k, r in enumerate(parts):
        g = mm(k, _C_GR, _C_GM)
        gr_ref[r, :] = (g * _sigmoid(g)).astype(BF)
    for k, r in enumerate(parts):
        gm_ref[r, gm_half:2 * gm_half] = _sigmoid(mm(k, _C_GM + gm_half, _C_GA)).astype(BF)

    for k, r in enumerate(parts):
        qa = mm(k, _C_Q, _C_KV)
        q_ref[r, :] = (qa * Q_SCALE).astype(BF)
        for c in range(4):
            qr_ref[r, c * LANE:(c + 1) * LANE] = rope(qa[:, c * LANE:(c + 1) * LANE], rq_ref, r).astype(BF)

    for k, r in enumerate(parts):
        kv = mm(k, _C_KV, _C_QR)
        put_kv(kvc_ref, r, kv[:, 0:LANE], kv[:, LANE:2 * LANE])
        for kind, out_ref in ((0, kvs_ref), (1, kvw_ref)):
            base = 256 + kind * 256
            kk = rope(kv[:, base:base + LANE], rk_ref, r)
            v = kv[:, base + LANE:base + 2 * LANE]
            put_kv(out_ref, r, kk, v)
            kr_ = pltpu.roll(kk, DH, 1)
            vr_ = pltpu.roll(v, DH, 1)
            kvb_ref[kind, 0, 0, r, :] = jnp.where(lo64, kk, vr_).astype(BF)
            kvb_ref[kind, 0, 1, r, :] = jnp.where(lo64, v, kr_).astype(BF)
            kvb_ref[kind, 1, 0, r, :] = jnp.where(lo64, kr_, v).astype(BF)
            kvb_ref[kind, 1, 1, r, :] = jnp.where(lo64, vr_, kk).astype(BF)

    for k, r in enumerate(parts):
        qr = mm(k, _C_QR, _C_KR)
        kr = mm(k, _C_KR, _C_VR)
        for hh in range(H_R):
            sl = slice(hh * LANE, (hh + 1) * LANE)
            xq = qr[:, sl]
            ret_ref[0, r, sl] = (xq * rr_ref[r, 0:LANE]
                                 + pltpu.roll(xq, DK_R // 2, 1) * rr_ref[r, LANE:2 * LANE]).astype(BF)
            xk = kr[:, sl]
            ret_ref[1, r, sl] = (xk * rr_ref[r, 2 * LANE:3 * LANE]
                                 + pltpu.roll(xk, DK_R // 2, 1) * rr_ref[r, 3 * LANE:4 * LANE]).astype(BF)
    for k, r in enumerate(parts):
        ga_ref[r, :] = _sigmoid(mm(k, _C_GA, _C_END))
    for k, r in enumerate(parts):
        ret_ref[2, r, :] = mm(k, _C_VR, _C_GR).astype(BF)


def _inproj(x2d, nw, sh, sc, w_pack, rq, rk, rr, tm, tab_blocks, per_row_mods, kv_t):
    rows = x2d.shape[0]
    nb = rows // tm
    if kv_t:
        kv_shape = jax.ShapeDtypeStruct((nb // tab_blocks, 256, tab_blocks * tm), F32)
        kv_spec = pl.BlockSpec((None, 256, tm), lambda i: (i // tab_blocks, 0, i % tab_blocks))
    else:
        kv_shape = jax.ShapeDtypeStruct((rows, 256), F32)
        kv_spec = pl.BlockSpec((tm, 256), lambda i: (i, 0))
    if per_row_mods:
        mod_spec = pl.BlockSpec((tm, D_MODEL), lambda i: (i, 0))
    else:
        mod_spec = pl.BlockSpec((None, 1, D_MODEL), lambda i: (i // tab_blocks, 0, 0))
    tab = lambda w: pl.BlockSpec((tm, w), lambda i: (i % tab_blocks, 0))
    row = lambda w: pl.BlockSpec((tm, w), lambda i: (i, 0))
    out_shapes = [
        jax.ShapeDtypeStruct((rows, 512), BF),
        jax.ShapeDtypeStruct((rows, 512), BF),
        kv_shape,
        kv_shape,
        kv_shape,
        jax.ShapeDtypeStruct((2, G_A, 2, rows, LANE), BF),
        jax.ShapeDtypeStruct((3, rows, 512), BF),
        jax.ShapeDtypeStruct((rows, 512), BF),
        jax.ShapeDtypeStruct((rows, 2048), BF),
        jax.ShapeDtypeStruct((rows, 256), F32),
    ]
    out_specs = [row(512), row(512), kv_spec, kv_spec, kv_spec,
                 pl.BlockSpec((2, G_A, 2, tm, LANE), lambda i: (0, 0, 0, i, 0)),
                 pl.BlockSpec((3, tm, 512), lambda i: (0, i, 0)),
                 row(512), row(2048), row(256)]
    return pl.pallas_call(
        functools.partial(_inproj_body, kv_t=kv_t),
        grid=(nb,),
        in_specs=[row(D_MODEL),
                  pl.BlockSpec((1, D_MODEL), lambda i: (0, 0)),
                  mod_spec, mod_spec,
                  *[pl.BlockSpec(w.shape, lambda i: (0, 0), pipeline_mode=pl.Buffered(1)) for w in w_pack],
                  tab(384), tab(384), tab(512)],
        out_specs=out_specs,
        out_shape=out_shapes,
        compiler_params=_cparams(("arbitrary",)),
        name="inproj",
    )(x2d, nw, sh, sc, *w_pack, rq, rk, rr)


_BPP = 4
_PAGE = _BPP * L_CMP


def _blockdiag(w):
    z = jnp.zeros_like(w)
    return jnp.concatenate(
        [jnp.concatenate([w if j == i else z for j in range(_BPP)], axis=-1) for i in range(_BPP)], axis=-2)


def _pack_phi_paged(phi_pos_k, phi_k1, phi_k2, phi_pos_v, phi_v1, phi_v2):
    w1, w2, pos = [], [], []
    for p_, a, b_ in ((phi_pos_k, phi_k1, phi_k2), (phi_pos_v, phi_v1, phi_v2)):
        w1.append(_blockdiag(a.astype(BF).reshape(L_CMP, DH, DH).transpose(1, 0, 2)))
        w2.append(_blockdiag(b_.astype(BF)))
        pos.append(jnp.tile(p_.T, (1, _BPP)))
    pos = jnp.stack(pos).reshape(2, DH // 2, 2 * _PAGE)
    w1 = jnp.stack(w1).reshape(2, DH // 2, 2 * _PAGE, _BPP * DH)
    return pos, w1, jnp.stack(w2)


def _compress_slab_rows(load, n_rows, pos_ref, w1_ref, w2_ref):
    acc = [jnp.zeros((G_A * n_rows, _BPP * DH), F32) for _ in range(2)]
    for dp in range(DH // 2):
        for kv in range(2):
            x = jnp.concatenate(
                [jnp.concatenate([load((kv * G_A + g) * DH + 2 * dp + j) for j in range(2)], axis=1)
                 for g in range(G_A)], axis=0)
            acc[kv] = acc[kv] + _nn((x + pos_ref[kv, dp:dp + 1, :]).astype(BF), w1_ref[kv, dp])
    return [_nn(_gelu(acc[kv]).astype(BF), w2_ref[kv]) for kv in range(2)]


def _compress_t_body(src_ref, pos_ref, w1_ref, w2_ref, o_ref, xbuf, sem, *, nseq, n_pages):
    def copy(t):
        b, p = t // n_pages, t % n_pages
        return pltpu.make_async_copy(src_ref.at[b, :, pl.ds(pl.multiple_of(p * _PAGE, _PAGE), _PAGE)],
                                     xbuf.at[:, t, :], sem)

    def start(t, c):
        copy(t).start()
        return c

    def wait(t, c):
        copy(t).wait()
        return c
    lax.fori_loop(0, nseq * n_pages, start, 0)
    lax.fori_loop(0, nseq * n_pages, wait, 0)
    kc, vc = _compress_slab_rows(lambda r: xbuf[r], nseq * n_pages, pos_ref, w1_ref, w2_ref)
    o_ref[0] = kc
    o_ref[1] = vc


def _compress_t(kv_t, pos, w1, w2):
    nseq, SL, T = kv_t.shape
    n_pages = T // _PAGE
    one = lambda a: pl.BlockSpec(a.shape, lambda i: (0,) * a.ndim, pipeline_mode=pl.Buffered(1))
    n_out = G_A * nseq * n_pages
    return pl.pallas_call(
        functools.partial(_compress_t_body, nseq=nseq, n_pages=n_pages),
        grid=(1,),
        in_specs=[pl.BlockSpec(memory_space=pl.ANY), one(pos), one(w1), one(w2)],
        out_specs=pl.BlockSpec((2, n_out, _BPP * DH), lambda i: (0, 0, 0)),
        out_shape=jax.ShapeDtypeStruct((2, n_out, _BPP * DH), F32),
        scratch_shapes=[pltpu.VMEM((SL, nseq * n_pages, _PAGE), F32), pltpu.SemaphoreType.DMA(())],
        compiler_params=_cparams(("arbitrary",)),
        name="compress_t",
    )(kv_t, pos, w1, w2)


_KT = 256


def _nsa_prompt_body(q_ref, qr_ref, kvb_ref, kc_ref, vc_ref, ga_ref, e_ref, gx_ref, o_ref, os_scr, *, nc, ns):
    i = pl.program_id(1)
    QB = Q_BLOCK
    lane = lax.broadcasted_iota(jnp.int32, (QB, LANE), 1)
    lo64 = lane < DH
    zero_b = jnp.zeros((QB, LANE), BF)
    groups = range(G_A)

    def split_heads(ref, g):
        ev, od = [], []
        for c in range(2):
            xc = ref[:, (2 * g + c) * LANE:(2 * g + c + 1) * LANE]
            ev.append(jnp.where(lo64, xc, zero_b))
            od.append(jnp.where(lo64, zero_b, xc))
        return jnp.concatenate(ev, axis=0), jnp.concatenate(od, axis=0)

    tq_l = i * QB + lax.broadcasted_iota(jnp.int32, (nc, QB), 1)
    r_c = lax.broadcasted_iota(jnp.int32, (nc, QB), 0)
    half = nc // 2
    blk_c = jnp.where(r_c < half, 2 * r_c, 2 * (r_c - half) + 1)
    cmask = (blk_c * L_CMP + (L_CMP - 1)) <= tq_l
    cmask_f = cmask.astype(F32)
    tq_s = i * QB + lax.broadcasted_iota(jnp.int32, (ns, QB), 1)
    blk_s = lax.broadcasted_iota(jnp.int32, (ns, QB), 0)
    valid = (blk_s * L_SEL) <= tq_s
    forced = (blk_s == 0) | (blk_s == tq_s // L_SEL)
    n_top = min(N_SEL, ns)

    def select(imp):
        score = jnp.where(valid, imp + jnp.where(forced, FORCE_BONUS, 0.0), NEG)
        rank = jnp.zeros((ns, QB), F32)
        for b2 in range(ns):
            row = score[b2:b2 + 1, :]
            rank = rank + jnp.where(blk_s > b2, jnp.where(row >= score, 1.0, 0.0), jnp.where(row > score, 1.0, 0.0))
        return jnp.where((rank < n_top) & (score > 0.5 * NEG), 1.0, 0.0)

    def score_stage(chains):
        scored = []
        for qs, kmat, _, bias in chains:
            s = _nt(qs, kmat)
            s = (s.reshape(2, QB, s.shape[1]) + bias[None]).reshape(s.shape)
            scored.append((s, jnp.max(s, axis=1, keepdims=True)))
        return scored

    def value_stage(scored, chains):
        outs = []
        for (s, m), (_, _, vmat, _) in zip(scored, chains):
            p = jnp.exp2(s - m)
            outs.append(_nn(p.astype(BF), vmat) / jnp.sum(p, axis=1, keepdims=True))
        return outs

    def masked_attend(chains):
        return value_stage(score_stage(chains), chains)


    qr = [split_heads(qr_ref, g) for g in groups]
    n_tiles = (i * QB + QB + _KT - 1) // _KT

    cmp_scores = [[_nt(kc_ref[g], qs) for qs in split_heads(q_ref, g)] for g in groups]

    WK = WINDOW + QB
    start = pl.multiple_of(jnp.maximum(i * QB - WINDOW, 0), QB)
    diff = (i * QB + lax.broadcasted_iota(jnp.int32, (QB, WK), 0)) - (start + lax.broadcasted_iota(jnp.int32, (QB, WK), 1))
    wb = jnp.where((diff >= 0) & (diff < WINDOW), 0.0, NEG)
    win_chains = []
    for g in groups:
        wkv = kvb_ref[1, g, 0, pl.ds(start, WK), :]
        wvk = kvb_ref[1, g, 1, pl.ds(start, WK), :]
        win_chains += [(qr[g][0], wkv, wvk, wb), (qr[g][1], wvk, wkv, wb)]
    win_scored = score_stage(win_chains)

    imps, o_cmp = [], []
    for g in groups:
        vc = vc_ref[g]
        imp = jnp.zeros((ns, QB), F32)
        oc = {}
        for stack in range(2):
            st = cmp_scores[g][stack]
            for c in range(2):
                s = jnp.where(cmask, st[:, c * QB:(c + 1) * QB], NEG)
                p = jnp.exp2(s - jnp.max(s, axis=0, keepdims=True)) * cmask_f
                p = p / jnp.maximum(jnp.sum(p, axis=0, keepdims=True), 1e-30)
                imp = imp + p[0:half] + p[half:nc]
                oc[(c, stack)] = _tn(p.astype(BF), vc)
        o_cmp.append(oc)
        imps.append(imp)

    ow = value_stage(win_scored, win_chains)
    sel_b = [select(imp).astype(BF) for imp in imps]

    gates = []
    for g in groups:
        ga = ga_ref[:, g * LANE:(g + 1) * LANE]
        hi = ga.astype(BF)
        lo = (ga - hi.astype(F32)).astype(BF)
        gates.append(_nn(jnp.concatenate([hi, lo], axis=1), gx_ref[...]))

    def gate_of(g, br, c):
        j = br * 2 + c
        return gates[g][:, j * LANE:(j + 1) * LANE]

    part = [[gate_of(g, 0, c) * jnp.where(lo64, o_cmp[g][(c, 0)], o_cmp[g][(c, 1)])
             + gate_of(g, 2, c) * jnp.where(lo64, ow[2 * g][c * QB:(c + 1) * QB], ow[2 * g + 1][c * QB:(c + 1) * QB])
             for c in range(2)] for g in groups]


    def slc_variant(nk):
        tq_r = i * QB + lax.broadcasted_iota(jnp.int32, (QB, _KT), 0)
        causal = (nk - _KT + lax.broadcasted_iota(jnp.int32, (QB, _KT), 1)) <= tq_r
        chains = []
        for g in groups:
            bias = _tn(sel_b[g], e_ref[:, 0:nk]) - MASK_BIG
            tail = jnp.where(causal, bias[:, nk - _KT:], -MASK_BIG)
            bias = tail if nk == _KT else jnp.concatenate([bias[:, :nk - _KT], tail], axis=1)
            kv = kvb_ref[0, g, 0, 0:nk, :]
            vk = kvb_ref[0, g, 1, 0:nk, :]
            chains.append((qr[g][0], kv, vk, bias))
            chains.append((qr[g][1], vk, kv, bias))
        for c, o in enumerate(masked_attend(chains)):
            os_scr[c // 2, c % 2] = o

    for k in range(1, kvb_ref.shape[3] // _KT + 1):
        pl.when(n_tiles == k)(functools.partial(slc_variant, k * _KT))

    for g in groups:
        os_e = os_scr[g, 0]
        os_o = os_scr[g, 1]
        for c in range(2):
            rows = slice(c * QB, (c + 1) * QB)
            acc = part[g][c] + gate_of(g, 1, c) * jnp.where(lo64, os_e[rows], os_o[rows])
            o_ref[:, (2 * g + c) * LANE:(2 * g + c + 1) * LANE] = acc.astype(BF)


def _gate_expand():
    gx = np.zeros((LANE, 3 * 2 * LANE), np.float32)
    for br in range(3):
        for c in range(2):
            j = br * 2 + c
            gx[br * HPG + 2 * c, j * LANE:j * LANE + DH] = 1.0
            gx[br * HPG + 2 * c + 1, j * LANE + DH:(j + 1) * LANE] = 1.0
    return jnp.asarray(np.concatenate([gx, gx], axis=0), BF)


def _sel_expand(ns, nkeys):
    e = (np.arange(nkeys)[None, :] // L_SEL) == np.arange(ns)[:, None]
    return jnp.asarray(e * MASK_BIG, BF)


def _nsa_prompt(q, qr, kvb, kc2, vc2, ga, B, T):
    nqb = T // Q_BLOCK
    nc, ns = T // L_CMP, T // L_SEL
    assert T >= WINDOW + Q_BLOCK and T % _KT == 0
    qspec = pl.BlockSpec((Q_BLOCK, 512), lambda b, i: (b * nqb + i, 0))
    cspec = pl.BlockSpec((None, G_A, nc, LANE), lambda b, i: (b, 0, 0, 0))
    return pl.pallas_call(
        functools.partial(_nsa_prompt_body, nc=nc, ns=ns),
        grid=(B, nqb),
        in_specs=[qspec, qspec,
                  pl.BlockSpec((2, G_A, 2, T, LANE), lambda b, i: (0, 0, 0, b, 0)),
                  cspec, cspec,
                  pl.BlockSpec((Q_BLOCK, G_A * LANE), lambda b, i: (b * nqb + i, 0)),
                  pl.BlockSpec((ns, T), lambda b, i: (0, 0)),
                  pl.BlockSpec((2 * LANE, 6 * LANE), lambda b, i: (0, 0))],
        out_specs=qspec,
        out_shape=jax.ShapeDtypeStruct((B * T, 512), BF),
        scratch_shapes=[pltpu.VMEM((G_A, 2, 2 * Q_BLOCK, LANE), F32)],
        compiler_params=_cparams(("arbitrary", "arbitrary")),
        name="nsa_prompt",
    )(q, qr, kvb, kc2, vc2, ga, _sel_expand(ns, T), _gate_expand())


def _ret_body(qkv_ref, gr_ref, s0_ref, dm_ref, dq_ref, wk_ref, gc_ref, gnw_ref, z_ref, sout_ref, s_scr, *, C, sb):
    c = pl.program_id(1)

    @pl.when(c == 0)
    def _():
        s_scr[...] = s0_ref[...]

    units = [(j, h, j, slice(h * LANE, (h + 1) * LANE)) for j in range(sb) for h in range(H_R)]
    qkv = [tuple(qkv_ref[t, rows, :, sl] for t in range(3)) for _, _, rows, sl in units]
    inner = [_nt(q, k) * dm_ref[h] for (q, k, _), (_, h, _, _) in zip(qkv, units)]
    cross = [_nn(q, s_scr[j, h].astype(BF)) * dq_ref[h] for (q, _, _), (j, h, _, _) in zip(qkv, units)]
    outs = [_nn(a.astype(BF), v) + c for a, c, (_, _, v) in zip(inner, cross, qkv)]
    for (_, k, v), (j, h, _, _) in zip(qkv, units):
        kw = (k.astype(F32) * wk_ref[h]).astype(BF)
        s_new = gc_ref[h, 0:1, :] * s_scr[j, h] + _tn(kw, v)
        s_scr[j, h] = s_new
        sout_ref[j, h] = s_new
    for o, (_, _, rows, sl) in zip(outs, units):
        mu = jnp.mean(o, axis=-1, keepdims=True)
        d = o - mu
        var = jnp.mean(d * d, axis=-1, keepdims=True)
        on = d * lax.rsqrt(var + EPS) * gnw_ref[:, sl]
        z_ref[rows, :, sl] = (gr_ref[rows, :, sl].astype(F32) * on).astype(BF)


def _retention(ret3, gr, s0, gnw, nseq, rows_per_seq, C, c_true, sb):
    nC = rows_per_seq // C
    rows = nseq * rows_per_seq
    width = H_R * LANE
    dm, dq, wk, gc = _ret_decay(C, c_true)
    full = lambda a: pl.BlockSpec(a.shape, lambda b, c: (0,) * a.ndim)
    z, s_new = pl.pallas_call(
        functools.partial(_ret_body, C=C, sb=sb),
        grid=(nseq // sb, nC),
        in_specs=[pl.BlockSpec((3, sb, C, width), lambda b, c: (0, b, c, 0)),
                  pl.BlockSpec((sb, C, width), lambda b, c: (b, c, 0)),
                  pl.BlockSpec((sb, H_R, DK_R, DV_R), lambda b, c: (b, 0, 0, 0)),
                  full(dm), full(dq), full(wk), full(gc), full(gnw)],
        out_specs=[pl.BlockSpec((sb, C, width), lambda b, c: (b, c, 0)),
                   pl.BlockSpec((sb, H_R, DK_R, DV_R), lambda b, c: (b, 0, 0, 0))],
        out_shape=[jax.ShapeDtypeStruct((nseq, rows_per_seq, width), BF),
                   jax.ShapeDtypeStruct((nseq, H_R, DK_R, DV_R), F32)],
        scratch_shapes=[pltpu.VMEM((sb, H_R, DK_R, DV_R), F32)],
        compiler_params=_cparams(("arbitrary", "arbitrary")),
        name="retention",
    )(ret3.reshape(3, nseq, rows_per_seq, width), gr.reshape(nseq, rows_per_seq, width), s0, dm, dq, wk, gc, gnw)
    return z.reshape(rows, width), s_new


def _mix_ffn_body(x_ref, oa_ref, zr_ref, gm_ref, gt1_ref, wa_ref, wr_ref, wo_ref,
                  nw_ref, sh_ref, sc_ref, gt_ref, wu_ref, cw_ref, cb_ref, wd_ref, nf_ref, p1_ref, p2_ref,
                  y_ref, a_ref, carry, *, blocks_per_seq, seq_rows):
    i = pl.program_id(0)
    tm = x_ref.shape[0]
    parts = [slice(k * _ROW_PART, (k + 1) * _ROW_PART) for k in range(tm // _ROW_PART)] if tm > _ROW_PART else [slice(0, tm)]

    def mod(ref, r):
        return ref[...] if ref.shape[0] == 1 else ref[r, :]

    ya = [_nn(oa_ref[r, :], wa_ref[...]) for r in parts]
    yr = [_nn(zr_ref[r, :], wr_ref[...]) for r in parts]
    xs = []
    for r, ya_k, yr_k in zip(parts, ya, yr):
        gm = gm_ref[r, :].astype(F32)
        merged = gm[:, 0:D_MODEL] * ya_k + gm[:, D_MODEL:2 * D_MODEL] * yr_k
        xs.append(x_ref[r, :] + mod(gt1_ref, r) * _nn(merged.astype(BF), wo_ref[...]))
    hs = []
    for r, x in zip(parts, xs):
        ms = jnp.mean(x * x, axis=-1, keepdims=True)
        h = (x * lax.rsqrt(ms + EPS)) * nw_ref[...]
        hs.append((h * (1.0 + mod(sc_ref, r)) + mod(sh_ref, r)).astype(BF))
    a_parts = [_nn(h, wu_ref[:, 0:D_FF]) for h in hs]
    b_parts = [_nn(h, wu_ref[:, D_FF:2 * D_FF]) for h in hs]
    a = a_parts[0] if len(parts) == 1 else jnp.concatenate(a_parts, axis=0)
    a_ref[...] = a[tm - a_ref.shape[0]:tm, :]
    rid = lax.broadcasted_iota(jnp.int32, (tm, D_FF), 0)
    if seq_rows is None:
        first = (i % blocks_per_seq) == 0
        prev = jnp.where(first, p1_ref[...], carry[...])
        carry[...] = a[tm - 8:tm, :]
        am1 = jnp.where(rid == 0, prev[7:8, :], pltpu.roll(a, 1, 0))
        am2 = jnp.where(rid == 0, prev[6:7, :], jnp.where(rid == 1, prev[7:8, :], pltpu.roll(a, 2, 0)))
    else:
        s = rid % seq_rows
        am1 = jnp.where(s == 0, p1_ref[...], pltpu.roll(a, 1, 0))
        am2 = jnp.where(s <= 1, p2_ref[...], pltpu.roll(a, 2, 0))
    u = cb_ref[...] + am2 * cw_ref[0:1, :] + am1 * cw_ref[1:2, :] + a * cw_ref[2:3, :]
    ffs = [_nn((_gelu(u[r]) * b_k).astype(BF), wd_ref[...]) for r, b_k in zip(parts, b_parts)]
    for r, x, ff in zip(parts, xs, ffs):
        x2 = x + mod(gt_ref, r) * ff
        ms2 = jnp.mean(x2 * x2, axis=-1, keepdims=True)
        y_ref[r, :] = (x2 * lax.rsqrt(ms2 + EPS)) * nf_ref[...]


def _mix_ffn(x2d, oa, zr, gm, gt1, wa, wr, wo, nw, sh, sc, gt, wu, cw, cb, wd, nf, p1, p2,
             tm, blocks_per_seq, per_row_mods, seq_rows):
    rows = x2d.shape[0]
    a_rows = tm if seq_rows is not None else 8
    if per_row_mods:
        mod_spec = pl.BlockSpec((tm, D_MODEL), lambda i: (i, 0))
    else:
        mod_spec = pl.BlockSpec((None, 1, D_MODEL), lambda i: (i // blocks_per_seq, 0, 0))
    row = lambda w: pl.BlockSpec((tm, w), lambda i: (i, 0))
    full = lambda a, b: pl.BlockSpec((a, b), lambda i: (0, 0), pipeline_mode=pl.Buffered(1))
    vec = lambda w: pl.BlockSpec((1, w), lambda i: (0, 0))
    pspec = pl.BlockSpec(p1.shape, lambda i: (0, 0))
    return pl.pallas_call(
        functools.partial(_mix_ffn_body, blocks_per_seq=blocks_per_seq, seq_rows=seq_rows),
        grid=(rows // tm,),
        in_specs=[row(D_MODEL), row(512), row(512), row(2048), mod_spec,
                  full(512, D_MODEL), full(512, D_MODEL), full(D_MODEL, D_MODEL),
                  vec(D_MODEL), mod_spec, mod_spec, mod_spec,
                  full(D_MODEL, 2 * D_FF), pl.BlockSpec((CONV_W, D_FF), lambda i: (0, 0)), vec(D_FF),
                  full(D_FF, D_MODEL), vec(D_MODEL), pspec, pspec],
        out_specs=[row(D_MODEL), pl.BlockSpec((a_rows, D_FF), lambda i: (i, 0))],
        out_shape=[jax.ShapeDtypeStruct((rows, D_MODEL), F32),
                   jax.ShapeDtypeStruct((rows // tm * a_rows, D_FF), F32)],
        scratch_shapes=[pltpu.VMEM((8, D_FF), F32)],
        compiler_params=_cparams(("arbitrary",)),
        name="mix_ffn",
    )(x2d, oa, zr, gm, gt1, wa, wr, wo, nw, sh, sc, gt, wu, cw, cb, wd, nf, p1, p2)


def _prompt_path(x_prompt, mods_p, W):
    B, T, _ = x_prompt.shape
    rows = B * T
    x2d = x_prompt.reshape(rows, D_MODEL)
    pos = np.arange(T)
    tm = 2 * _ROW_PART if T % (2 * _ROW_PART) == 0 else _ROW_PART
    sh1, sc1, gt1, sh2, sc2, gt2 = [mods_p[:, None, j * D_MODEL:(j + 1) * D_MODEL] for j in range(6)]
    (q, qr, kvc, kvs, kvw, kvb, ret3, gr, gm, ga) = _inproj(
        x2d, W["norm1"], sh1, sc1, W["w_in"], _rope_tab(pos, Q_SCALE), _rope_tab(pos, 1.0), _ret_tab(pos),
        tm, T // tm, False, True)
    nc, n_pages = T // L_CMP, T // _PAGE
    comp = _compress_t(kvc, W["phi_posT"], W["phi_w1p"], W["phi_w2p"])
    comp = comp.reshape(2, G_A, B, n_pages, 2, 2, DH).transpose(0, 2, 1, 5, 3, 4, 6).reshape(2, B, G_A, nc, DH)
    comp2 = jnp.concatenate([comp, comp], axis=-1).astype(BF)
    oa = _nsa_prompt(q, qr, kvb, comp2[0], comp2[1], ga, B, T)
    C = 256 if T % 256 == 0 else T
    zr, ret_new = _retention(ret3, gr, jnp.zeros((B, H_R, DK_R, DV_R), F32), W["gnw"], B, T, C, C,
                             4 if B % 4 == 0 else 1)
    zeros8 = jnp.zeros((8, D_FF), F32)
    tf = tm
    y, a_up = _mix_ffn(x2d, oa, zr, gm, gt1, W["w_up_a"], W["w_up_r"], W["w_out"],
                       W["norm2"], sh2, sc2, gt2, W["w_ffn_up"], W["conv_w"], W["conv_b"], W["w_ffn_down"],
                       W["normf"], zeros8, zeros8, tf, T // tf, False, None)
    wsz = min(WINDOW, T)
    rows_major = lambda t: t.reshape(B, 2, G_A, DH, t.shape[-1]).transpose(0, 4, 1, 2, 3)[None]
    outs = dict(
        y=y.reshape(B, T, D_MODEL),
        cmp=rows_major(kvc), slc=rows_major(kvs), win=rows_major(kvw[:, :, T - wsz:]),
        ret=ret_new[None],
        conv=a_up.reshape(B, T // tf, 8, D_FF)[None, :, T // tf - 1, 8 - (CONV_W - 1):],
    )
    return outs


_QC = 32


def _cmp_paged_body(pt_ref, cache_ref, q_ref, pos_ref, w1_ref, w2_ref, ocmp_ref, topi_ref, xbuf, sem,
                    *, n_pages, n_pick, n_q):
    b = pl.program_id(0)
    nb = pl.num_programs(0)
    slot = b % 2

    def copy(bb, sl, p):
        return pltpu.make_async_copy(cache_ref.at[pt_ref[bb, p]], xbuf.at[sl, :, p, :], sem.at[sl])

    def issue(bb, sl):
        for p in range(n_pages):
            copy(bb, sl, p).start(priority=p % 2)

    @pl.when(b == 0)
    def _():
        issue(b, slot)

    @pl.when(b + 1 < nb)
    def _():
        issue(b + 1, 1 - slot)

    for p in range(n_pages):
        copy(b, slot, p).wait()

    comp = _compress_slab_rows(lambda r: xbuf[slot, r], n_pages, pos_ref, w1_ref, w2_ref)

    def lane_groups(x, op):
        r = x
        for j in range(1, _BPP):
            r = op(r, pltpu.roll(x, j * _QC, 1))
        return r

    grp = range(G_A)
    sts = [_nn(comp[0][g * n_pages:(g + 1) * n_pages].astype(BF), q_ref[g]) for g in grp]
    ps = []
    for g in grp:
        m = lane_groups(jnp.max(sts[g], axis=0, keepdims=True), jnp.maximum)
        p = jnp.exp2(sts[g] - m)
        ps.append(p / lane_groups(jnp.sum(p, axis=0, keepdims=True), jnp.add))
    for g in grp:
        r_full = _tn(ps[g].astype(BF), comp[1][g * n_pages:(g + 1) * n_pages].astype(BF))
        o = r_full[0:_QC, 0:DH]
        for j in range(1, _BPP):
            o = o + r_full[j * _QC:(j + 1) * _QC, j * DH:(j + 1) * DH]
        ocmp_ref[g] = o
    imp = None
    for g in grp:
        pair = ps[g] + pltpu.roll(ps[g], LANE - _QC, 1)
        t = pair
        for r in range(1, HPG):
            t = t + pltpu.roll(pair, LANE - r * n_q, 1)
        if g:
            t = pltpu.roll(t, g * n_q, 1)
        lane_g = lax.broadcasted_iota(jnp.int32, t.shape, 1) % (2 * _QC)
        t = jnp.where((lane_g >= g * n_q) & (lane_g < (g + 1) * n_q), t, 0.0)
        imp = t if imp is None else imp + t
    sc = jnp.concatenate([imp, pltpu.roll(imp, 2 * _QC, 1)], axis=0)
    row = lax.broadcasted_iota(jnp.int32, (2 * n_pages, LANE), 0)
    blk = jnp.where(row < n_pages, 2 * row, 2 * (row - n_pages) + 1)
    score = sc + jnp.where(blk == 0, FORCE_BONUS, 0.0)
    topi_ref[...] = jnp.zeros((8, LANE), jnp.int32)
    for k in range(n_pick):
        mx = jnp.max(score, axis=0, keepdims=True)
        idx = jnp.min(jnp.where(score == mx, blk, 2 * n_pages), axis=0, keepdims=True)
        topi_ref[k:k + 1, :] = idx
        score = jnp.where(blk == idx, -jnp.inf, score)


def _cmp_paged(page_table, cache_t, q_bd, pos, w1, w2, n_q):
    DB, n_pages = page_table.shape
    SL = cache_t.shape[1]
    page_rows = cache_t.shape[2]
    kern = functools.partial(_cmp_paged_body, n_pages=n_pages, n_pick=N_SEL - 1, n_q=n_q)
    one = dict(pipeline_mode=pl.Buffered(1))
    return pl.pallas_call(
        kern,
        grid_spec=pltpu.PrefetchScalarGridSpec(
            num_scalar_prefetch=1, grid=(DB,),
            in_specs=[pl.BlockSpec(memory_space=pl.ANY),
                      pl.BlockSpec((None, G_A, _BPP * DH, LANE), lambda b, pt: (b, 0, 0, 0)),
                      pl.BlockSpec(pos.shape, lambda b, pt: (0, 0, 0), **one),
                      pl.BlockSpec(w1.shape, lambda b, pt: (0, 0, 0, 0), **one),
                      pl.BlockSpec((2, _BPP * DH, _BPP * DH), lambda b, pt: (0, 0, 0), **one)],
            out_specs=[pl.BlockSpec((None, G_A, _QC, DH), lambda b, pt: (b, 0, 0, 0)),
                       pl.BlockSpec((None, 8, LANE), lambda b, pt: (b, 0, 0))],
            scratch_shapes=[pltpu.VMEM((2, SL, n_pages, page_rows), F32),
                            pltpu.SemaphoreType.DMA((2,))]),
        out_shape=[jax.ShapeDtypeStruct((DB, G_A, _QC, DH), F32),
                   jax.ShapeDtypeStruct((DB, 8, LANE), jnp.int32)],
        compiler_params=_cparams(("arbitrary",)),
        name="cmp_paged",
    )(page_table, cache_t, q_bd, pos, w1, w2)


def _slc_win_paged_body(pt_ref, ti_ref, cslc_ref, win_ref, q_ref, tiv_ref, ns_ref, nw_ref, ex_ref,
                        oslc_ref, owin_ref, kbuf, sem, *, n_q, n_pick, page_rows):
    b = pl.program_id(0)
    nb = pl.num_programs(0)
    slot = b % 2
    n_slab = n_q * n_pick
    bpp = page_rows // L_SEL
    wb = win_ref.shape[1]

    def copies(bb, sl):
        cps = []
        for g in range(G_A):
            for j in range(n_slab):
                page = pt_ref[bb, ti_ref[bb, g * n_slab + j] // bpp]
                for kv in range(2):
                    cps.append(pltpu.make_async_copy(
                        cslc_ref.at[page, pl.ds((kv * G_A + g) * DH, DH), :],
                        kbuf.at[sl, g, kv, :, pl.ds(j * page_rows, page_rows)], sem.at[sl]))
        return cps

    def start_all(cps):
        for n, cp in enumerate(cps):
            cp.start(priority=n % 2)

    @pl.when(b == 0)
    def _():
        start_all(copies(b, slot))

    @pl.when(b + 1 < nb)
    def _():
        start_all(copies(b + 1, 1 - slot))

    for cp in copies(b, slot):
        cp.wait()

    nq_rows = q_ref.shape[1]
    nk = n_slab * page_rows
    row_q = lax.broadcasted_iota(jnp.int32, (nq_rows, nk), 0) % n_q
    col = lax.broadcasted_iota(jnp.int32, (nq_rows, nk), 1)
    own = row_q == col // (n_pick * page_rows)
    half = ((col % page_rows) // L_SEL).astype(F32)
    nnew = ns_ref.shape[2]
    new_ok = (lax.broadcasted_iota(jnp.int32, (nq_rows, nnew), 1)
              <= lax.broadcasted_iota(jnp.int32, (nq_rows, nnew), 0) % n_q)
    dwin = (wb + lax.broadcasted_iota(jnp.int32, (nq_rows, wb), 0) % n_q
            - lax.broadcasted_iota(jnp.int32, (nq_rows, wb), 1))
    win_ok = (dwin >= 0) & (dwin < WINDOW)

    chains = []
    for g in range(G_A):
        q = q_ref[g]
        par = (tiv_ref[g] % bpp).astype(F32).astype(BF)
        want = _nn(par, ex_ref[...])[0:1, :]
        chains.append((q, kbuf[slot, g, 0].astype(BF), kbuf[slot, g, 1].astype(BF), own & (half == want),
                       ns_ref[g, 0].astype(BF), ns_ref[g, 1].astype(BF), oslc_ref, g))
        kw = win_ref[pl.ds((0 * G_A + g) * DH, DH), :].astype(BF)
        vw = win_ref[pl.ds((1 * G_A + g) * DH, DH), :].astype(BF)
        chains.append((q, kw, vw, win_ok, nw_ref[g, 0].astype(BF), nw_ref[g, 1].astype(BF), owin_ref, g))
    scores = [(jnp.where(ok_old, _nn(q, kt_old), NEG), jnp.where(new_ok, _nt(q, k_new), NEG))
              for q, kt_old, _, ok_old, k_new, _, _, _ in chains]
    probs = []
    for s_o, s_n in scores:
        m = jnp.maximum(jnp.max(s_o, axis=1, keepdims=True), jnp.max(s_n, axis=1, keepdims=True))
        p_o = jnp.exp2(s_o - m)
        p_n = jnp.exp2(s_n - m)
        probs.append((p_o, p_n, jnp.sum(p_o, axis=1, keepdims=True) + jnp.sum(p_n, axis=1, keepdims=True)))
    for (p_o, p_n, den), (_, _, vt_old, _, _, v_new, out_ref, g) in zip(probs, chains):
        out_ref[g] = (_nt(p_o.astype(BF), vt_old) + _nn(p_n.astype(BF), v_new)) / den


def _slc_win_paged(page_table, topi_flat, topi_vec, cache_t, win_t, q_rot, new_slc, new_win, n_q):
    DB = page_table.shape[0]
    n_pick = N_SEL - 1
    page_rows = cache_t.shape[2]
    wb = win_t.shape[2]
    nq_rows = q_rot.shape[2]
    n_slab = n_q * n_pick
    ex = (np.arange(n_slab * page_rows)[None, :] // page_rows) == np.arange(LANE)[:, None]
    kern = functools.partial(_slc_win_paged_body, n_q=n_q, n_pick=n_pick, page_rows=page_rows)
    bspec = lambda shp: pl.BlockSpec((None,) + shp, lambda b, pt, ti: (b,) + (0,) * len(shp))
    return pl.pallas_call(
        kern,
        grid_spec=pltpu.PrefetchScalarGridSpec(
            num_scalar_prefetch=2, grid=(DB,),
            in_specs=[pl.BlockSpec(memory_space=pl.ANY), bspec((2 * G_A * DH, wb)),
                      bspec((G_A, nq_rows, DH)), bspec((G_A, 16, LANE)),
                      bspec((G_A, 2, 16, DH)), bspec((G_A, 2, 16, DH)),
                      pl.BlockSpec((LANE, n_slab * page_rows), lambda b, pt, ti: (0, 0))],
            out_specs=[bspec((G_A, nq_rows, DH)), bspec((G_A, nq_rows, DH))],
            scratch_shapes=[pltpu.VMEM((2, G_A, 2, DH, n_slab * page_rows), F32),
                            pltpu.SemaphoreType.DMA((2,))]),
        out_shape=[jax.ShapeDtypeStruct((DB, G_A, nq_rows, DH), F32)] * 2,
        compiler_params=_cparams(("arbitrary",)),
        name="slc_win_paged",
    )(page_table, topi_flat, cache_t, win_t, q_rot, topi_vec, new_slc, new_win, jnp.asarray(ex, BF))


def _gate_sample_body(oc_ref, os_ref, ow_ref, ga_ref, o_ref):
    rows = o_ref.shape[0]
    lo64 = lax.broadcasted_iota(jnp.int32, (rows, LANE), 1) < DH
    for c4 in range(H_A // 2):
        g, c = c4 // 2, c4 % 2
        sl = slice(c4 * LANE, (c4 + 1) * LANE)
        acc = jnp.zeros((rows, LANE), F32)
        for br, ref in enumerate((oc_ref, os_ref, ow_ref)):
            col = g * LANE + br * HPG + 2 * c
            gate = jnp.where(lo64, ga_ref[:, col:col + 1], ga_ref[:, col + 1:col + 2])
            acc = acc + gate * ref[:, sl]
        o_ref[:, sl] = acc.astype(BF)


def _gate_sample(oc, osl, ow, ga):
    rows = oc.shape[0]
    full = lambda w: pl.BlockSpec((rows, w), lambda i: (0, 0))
    return pl.pallas_call(
        _gate_sample_body, grid=(1,),
        in_specs=[full(512), full(512), full(512), full(256)],
        out_specs=full(512),
        out_shape=jax.ShapeDtypeStruct((rows, 512), BF),
        compiler_params=_cparams(("arbitrary",)),
        name="gate_sample",
    )(oc, osl, ow, ga)


def _sample_path(x_sample, mods_s, W, cache_cmp, cache_slc, state_win, state_ret, state_conv, page_table):
    DB, S, _ = x_sample.shape
    rows = DB * S
    page_rows = cache_cmp.shape[1]
    P = page_table.shape[1] * page_rows
    wb = state_win.shape[1]
    assert P % L_SEL == 0 and S < L_CMP and S <= 8 and wb == WINDOW and page_rows % L_SEL == 0
    assert P // L_SEL >= N_SEL and CONV_W == 3 and S >= CONV_W - 1
    pos = P + np.arange(S)
    pos_rows = np.tile(pos, DB)
    x2d = x_sample.reshape(rows, D_MODEL)
    modr = jnp.repeat(mods_s, S, axis=0)
    sh1, sc1, gt1, sh2, sc2, gt2 = [modr[:, j * D_MODEL:(j + 1) * D_MODEL] for j in range(6)]
    (q, qr, kvc, kvs, kvw, _, ret3, gr, gm, ga) = _inproj(
        x2d, W["norm1"], sh1, sc1, W["w_in"], _rope_tab(pos_rows, Q_SCALE), _rope_tab(pos_rows, 1.0),
        _ret_tab(pos_rows), rows, 1, True, False)

    def to_heads(t):
        return t.reshape(DB, S, G_A, HPG, DH).transpose(0, 2, 3, 1, 4).reshape(DB, G_A, HPG * S, DH)

    def from_heads(t):
        return t.reshape(DB, G_A, HPG, S, DH).transpose(0, 3, 1, 2, 4).reshape(rows, H_A * DH)

    assert page_rows == _BPP * L_CMP and HPG * S <= _QC and S * (N_SEL - 1) <= LANE
    slab = lambda t: t.transpose(0, 2, 3, 4, 1).reshape(t.shape[0], 2 * G_A * DH, t.shape[1])
    qt = jnp.pad(to_heads(q).transpose(0, 1, 3, 2), ((0, 0), (0, 0), (0, 0), (0, _QC - HPG * S)))
    q_bd = _blockdiag(qt)
    o_cmp, topi = _cmp_paged(page_table, slab(cache_cmp), q_bd, W["phi_posT"], W["phi_w1p"], W["phi_w2p"], S)
    n_pick = N_SEL - 1
    topi = topi[:, :n_pick, :G_A * S].transpose(0, 2, 1).reshape(DB, G_A, S * n_pick)
    topi_vec = jnp.broadcast_to(jnp.pad(topi, ((0, 0), (0, 0), (0, LANE - S * n_pick)))[:, :, None, :],
                                (DB, G_A, 16, LANE))

    def new_rows(t):
        t = t.reshape(DB, S, 2, G_A, DH).transpose(0, 3, 2, 1, 4)
        return jnp.pad(t, ((0, 0), (0, 0), (0, 0), (0, 16 - S), (0, 0)))

    o_slc, o_win = _slc_win_paged(page_table, topi.reshape(DB, G_A * S * n_pick), topi_vec, slab(cache_slc),
                                  slab(state_win), to_heads(qr), new_rows(kvs), new_rows(kvw), S)
    oa = _gate_sample(from_heads(o_cmp[:, :, :HPG * S]), from_heads(o_slc), from_heads(o_win), ga)

    RP = 16
    padr = lambda t: jnp.pad(t.reshape(t.shape[:-2] + (DB, S, 512)),
                             ((0, 0),) * (t.ndim - 1) + ((0, RP - S), (0, 0))).reshape(t.shape[:-2] + (DB * RP, 512))
    zr, ret_new = _retention(padr(ret3), padr(gr), state_ret, W["gnw"], DB, RP, RP, S, 8 if DB % 8 == 0 else 1)
    zr = zr.reshape(DB, RP, 512)[:, :S].reshape(rows, 512)
    zs = lambda n: jnp.zeros((DB, n, D_FF), F32)
    p1 = jnp.concatenate([state_conv[:, 1:2], zs(S - 1)], axis=1).reshape(rows, D_FF)
    p2 = jnp.concatenate([state_conv[:, 0:2], zs(S - 2)], axis=1).reshape(rows, D_FF)
    y, a_up = _mix_ffn(x2d, oa, zr, gm, gt1, W["w_up_a"], W["w_up_r"], W["w_out"],
                       W["norm2"], sh2, sc2, gt2, W["w_ffn_up"], W["conv_w"], W["conv_b"], W["w_ffn_down"],
                       W["normf"], p1, p2, rows, 1, True, S)
    shp = (1, DB, S, 2, G_A, DH)
    return dict(
        y=y.reshape(DB, S, D_MODEL),
        cmp=kvc.reshape(shp), slc=kvs.reshape(shp),
        win=jnp.concatenate([state_win[:, S:], kvw.reshape(DB, S, 2, G_A, DH)], axis=1)[None],
        ret=ret_new[None],
        conv=a_up.reshape(DB, S, D_FF)[None, :, S - (CONV_W - 1):],
    )


def kernel(x_prompt, x_sample, cache_cmp_kv, cache_slc_kv, state_win_kv, state_ret, state_conv, page_table,
           c_prompt, c_sample, norm1_w, norm2_w, w_ada, b_ada, w_in, phi_pos_k, phi_k1, phi_k2, phi_pos_v,
           phi_v1, phi_v2, w_up_a, ret_gn_w, w_up_r, w_out, w_ffn_up, ffn_conv_w, ffn_conv_b, w_ffn_down, normf_w):
    B = x_prompt.shape[0]
    l = 0
    W = dict(
        norm1=norm1_w[l].reshape(1, D_MODEL), norm2=norm2_w[l].reshape(1, D_MODEL), normf=normf_w.reshape(1, D_MODEL),
        w_in=_pack_w_in(w_in[l]),
        w_up_a=w_up_a[l].astype(BF), w_up_r=w_up_r[l].astype(BF), w_out=w_out[l].astype(BF),
        gnw=ret_gn_w[l].reshape(1, H_R * DV_R),
        w_ffn_up=w_ffn_up[l].astype(BF), conv_w=ffn_conv_w[l], conv_b=ffn_conv_b[l].reshape(1, D_FF),
        w_ffn_down=w_ffn_down[l].astype(BF),
    )
    W["phi_posT"], W["phi_w1p"], W["phi_w2p"] = _pack_phi_paged(
        phi_pos_k[l], phi_k1[l], phi_k2[l], phi_pos_v[l], phi_v1[l], phi_v2[l])
    mods = _mods(jnp.concatenate([c_prompt, c_sample], axis=0), w_ada[l], b_ada[l])
    p = _prompt_path(x_prompt, mods[:B], W)
    s = _sample_path(x_sample, mods[B:], W, cache_cmp_kv[l], cache_slc_kv[l], state_win_kv[l], state_ret[l],
                     state_conv[l], page_table)
    return (p["y"], s["y"], p["cmp"], s["cmp"], p["slc"], s["slc"], p["win"], s["win"],
            p["ret"], s["ret"], p["conv"], s["conv"])
```
